```python
import functools
import jax, jax.numpy as jnp
from jax import lax
import numpy as np

D_MODEL = 1024
BATCH = 32
SEQ = 256
DEPTH = 4
DEC_BATCH = 8
DEC_SEQ = 4096
PAST_LEN = 512

GRID_W = 64
MIX_W = D_MODEL
N_GROUPS = 4
GW = MIX_W // N_GROUPS
CONV_K = 31
GLA_HEADS = 4
GLA_DK = GW // GLA_HEADS
GLA_DV = GW // GLA_HEADS
GLA_LR = 16
GLA_TAU = 16.0
GDN_HEADS = 4
GDN_DK = GW // GDN_HEADS
GDN_DV = GW // GDN_HEADS
GDN_CONV_K = 3
SC_K = 3
CHUNK = 64
D_FF = 2816
N_EXPERTS = 8
TOP_K = 2
D_FF_EXPERT = 1408
N_DENSE = (DEPTH + 1) // 2
N_MOE = DEPTH // 2
ALPHA = (2 * DEPTH) ** 0.25
BETA_INIT = (8 * DEPTH) ** -0.25
EPS = 1e-5
IN_SIZES = (GW, GW,
            GW, GW, GW, GW,
            2 * GLA_LR,
            GW, GW, GW, GW,
            GDN_HEADS,
            2 * GDN_HEADS,
            GW, GW, GW)
N_IN = 13 * GW + 2 * GLA_LR + 3 * GDN_HEADS

kernel_name = "hybrid_bidir_diffusion_trunk_step"


def _layer_norm(x, g, b):
    xf = x.astype(jnp.float32)
    xc = xf - jnp.mean(xf, -1, keepdims=True)
    var = jnp.mean(xc * xc, -1, keepdims=True)
    return (xc * lax.rsqrt(var + EPS)).astype(x.dtype) * g + b


def _rms(x):
    return x * lax.rsqrt(jnp.mean(x * x, -1, keepdims=True) + EPS)


def _l2norm(x):
    return x * lax.rsqrt(jnp.sum(x * x, -1, keepdims=True) + 1e-6)


def _heads(x, n):
    b, t, _ = x.shape
    return x.reshape(b, t, n, -1).transpose(0, 2, 1, 3)


def _unheads(x):
    b, n, t, d = x.shape
    return x.transpose(0, 2, 1, 3).reshape(b, t, n * d)


def _to_cols(x):
    b, t, ch = x.shape
    rows = t // GRID_W
    return x.reshape(b, rows, GRID_W, ch).transpose(0, 2, 1, 3).reshape(b, t, ch)


def _from_cols(x):
    b, t, ch = x.shape
    rows = t // GRID_W
    return x.reshape(b, GRID_W, rows, ch).transpose(0, 2, 1, 3).reshape(b, t, ch)


def _flip(a):
    return jnp.flip(a, axis=2)


def _dwconv(x, w):
    k = w.shape[0]
    return lax.conv_general_dilated(x, w[:, None, :].astype(x.dtype), (1,), [(k // 2, k // 2)],
                                    dimension_numbers=('NWC', 'WIO', 'NWC'),
                                    feature_group_count=x.shape[-1])


def _gla_chunk(q, k, v, log_a, s0):
    b, h, t, _ = q.shape
    n = t // CHUNK
    ch = lambda a: a.reshape(b, h, n, CHUNK, a.shape[-1])
    q, k, v, log_a = ch(q), ch(k), ch(v), ch(log_a)
    cum = jnp.cumsum(log_a, axis=3)
    tot = cum[:, :, :, -1]
    qe = q * jnp.exp(cum)
    ke = k * jnp.exp(-cum)
    kd = k * jnp.exp(tot[:, :, :, None] - cum)
    mask = jnp.tril(jnp.ones((CHUNK, CHUNK), bool))
    att = jnp.where(mask, jnp.einsum('bhncd,bhnsd->bhncs', qe, ke), 0.0)
    o = jnp.einsum('bhncs,bhnse->bhnce', att, v)
    upd = jnp.einsum('bhncd,bhnce->bhnde', kd, v)

    def step(s, inp):
        dec, u = inp
        return dec[..., None] * s + u, s

    s_fin, s_in = lax.scan(step, s0, (jnp.moveaxis(jnp.exp(tot), 2, 0), jnp.moveaxis(upd, 2, 0)))
    o = o + jnp.einsum('bhncd,nbhde->bhnce', qe, s_in)
    return o.reshape(b, h, t, -1), s_fin


def _gdn_chunk(q, k, v, beta, log_a, s0):
    b, h, t, _ = q.shape
    n = t // CHUNK
    ch = lambda a: a.reshape(b, h, n, CHUNK, a.shape[-1])
    q, k, v = ch(q), ch(k), ch(v)
    beta = beta.reshape(b, h, n, CHUNK)
    g = jnp.cumsum(log_a.reshape(b, h, n, CHUNK), axis=-1)
    incl = jnp.tril(jnp.ones((CHUNK, CHUNK), bool))
    strict = jnp.tril(jnp.ones((CHUNK, CHUNK), bool), -1)
    diff = g[..., :, None] - g[..., None, :]
    decay = jnp.where(incl, jnp.exp(jnp.where(incl, diff, 0.0)), 0.0)
    kb = k * beta[..., None]
    eye = jnp.eye(CHUNK, dtype=jnp.float32)
    lmat = jnp.where(strict, jnp.einsum('bhncd,bhnsd->bhncs', kb, k) * decay, 0.0)
    tinv = lax.linalg.triangular_solve(eye + lmat, jnp.broadcast_to(eye, lmat.shape),
                                       left_side=True, lower=True)
    u = jnp.einsum('bhncs,bhnse->bhnce', tinv, v * beta[..., None])
    w = jnp.einsum('bhncs,bhnsd->bhncd', tinv, kb * jnp.exp(g)[..., None])
    qk = jnp.einsum('bhncd,bhnsd->bhncs', q, k) * decay
    qg = q * jnp.exp(g)[..., None]
    kd = k * jnp.exp(g[..., -1:] - g)[..., None]
    gl = jnp.exp(g[..., -1])

    def step(s, inp):
        u_n, w_n, qk_n, qg_n, kd_n, gl_n = inp
        v_new = u_n - jnp.einsum('bhcd,bhde->bhce', w_n, s)
        o_n = jnp.einsum('bhcd,bhde->bhce', qg_n, s) + jnp.einsum('bhcs,bhse->bhce', qk_n, v_new)
        s = gl_n[..., None, None] * s + jnp.einsum('bhcd,bhce->bhde', kd_n, v_new)
        return s, o_n

    xs = (jnp.moveaxis(u, 2, 0), jnp.moveaxis(w, 2, 0), jnp.moveaxis(qk, 2, 0),
          jnp.moveaxis(qg, 2, 0), jnp.moveaxis(kd, 2, 0), jnp.moveaxis(gl, 2, 0))
    s_fin, o = lax.scan(step, s0, xs)
    return jnp.moveaxis(o, 0, 2).reshape(b, h, t, -1), s_fin


def _gla_mixer(q, k, v, gate, lr, w_lr, b_lr, norm_g, s0):
    b, t, _ = q.shape
    f32 = jnp.float32
    qh = _heads(q, GLA_HEADS).astype(f32) * GLA_DK ** -0.5
    kh = _heads(k, GLA_HEADS).astype(f32)
    vh = _heads(v, GLA_HEADS).astype(f32)
    z = jnp.einsum('btzr,zrc->btzc', lr.reshape(b, t, 2, GLA_LR), w_lr) + b_lr
    la = jax.nn.log_sigmoid(z.astype(f32)) / GLA_TAU
    la_f = _heads(la[:, :, 0], GLA_HEADS)
    la_b = _heads(la[:, :, 1], GLA_HEADS)
    o_f, s_f = _gla_chunk(qh, kh, vh, la_f, s0[:, 0])
    o_b, s_b = _gla_chunk(_flip(qh), _flip(kh), _flip(vh), _flip(la_b), s0[:, 1])
    o = _unheads(_rms(o_f + _flip(o_b))).astype(q.dtype)
    return o * norm_g * jax.nn.silu(gate), jnp.stack([s_f, s_b], axis=1)


def _gdn_mixer(q, k, v, gate, beta, dec, conv_w, a_log, dt_bias, norm_g, s0, latent):
    f32 = jnp.float32
    qkv = jnp.concatenate([q, k, v], axis=-1)
    bd = jnp.concatenate([beta, dec], axis=-1)
    if latent:
        qkv, bd = _to_cols(qkv), _to_cols(bd)
    b, t, _ = qkv.shape
    qkv = jax.nn.silu(_dwconv(qkv, conv_w))
    q, k, v = jnp.split(qkv, 3, axis=-1)
    qh = _l2norm(_heads(q, GDN_HEADS).astype(f32)) * GDN_DK ** -0.5
    kh = _l2norm(_heads(k, GDN_HEADS).astype(f32))
    vh = _heads(v, GDN_HEADS).astype(f32)
    bd = bd.astype(f32)
    bt = jax.nn.sigmoid(bd[..., :GDN_HEADS]).transpose(0, 2, 1)
    dd = bd[..., GDN_HEADS:].reshape(b, t, 2, GDN_HEADS)
    la = -jnp.exp(a_log.astype(f32)) * jax.nn.softplus(dd + dt_bias.astype(f32))
    la_f = la[:, :, 0].transpose(0, 2, 1)
    la_b = la[:, :, 1].transpose(0, 2, 1)
    o_f, s_f = _gdn_chunk(qh, kh, vh, bt, la_f, s0[:, 0])
    o_b, s_b = _gdn_chunk(_flip(qh), _flip(kh), _flip(vh), _flip(bt), _flip(la_b), s0[:, 1])
    o = _unheads(_rms(o_f + _flip(o_b))).astype(gate.dtype)
    if latent:
        o = _from_cols(o)
    return o * norm_g * jax.nn.silu(gate), jnp.stack([s_f, s_b], axis=1)


def _mixer(h, p, s0_gla, s0_gdn, latent):
    u = h @ p['w_in']
    idx = np.cumsum(IN_SIZES)[:-1].tolist()
    (a_val, a_gate, gq, gk, gv, gg, glr, dq, dk, dv, dgate, dbeta, ddec, sb, sc, sx) = jnp.split(u, idx, axis=-1)
    a = a_val * jax.nn.sigmoid(a_gate)
    a = _dwconv(a, p['conv_w']) + p['conv_b']
    a = jax.nn.silu(_layer_norm(a, p['conv_ln_g'], p['conv_ln_b'])) @ p['conv_pw']
    y_gla, st_gla = _gla_mixer(gq, gk, gv, gg, glr, p['gla_w_lr'], p['gla_b_lr'], p['gla_norm_g'], s0_gla)
    y_gdn, st_gdn = _gdn_mixer(dq, dk, dv, dgate, dbeta, ddec, p['gdn_conv_w'], p['gdn_a_log'],
                               p['gdn_dt_bias'], p['gdn_norm_g'], s0_gdn, latent)
    y_sc = sb * _dwconv(sc * sx, p['sc_conv_w'])
    y = jnp.concatenate([a, y_gla, y_gdn, y_sc], axis=-1) @ p['w_out']
    return y, st_gla, st_gdn


def _swiglu(h, w_gate, w_up, w_down):
    return (jax.nn.silu(h @ w_gate) * (h @ w_up)) @ w_down


def _moe(h, router, w_gate, w_up, w_down):
    b, t, d = h.shape
    xt = h.reshape(b * t, d)
    logits = (xt @ router).astype(jnp.float32)
    top_v, top_i = lax.top_k(logits, TOP_K)
    probs = jax.nn.softmax(top_v, axis=-1)
    combine = jnp.einsum('nk,nke->ne', probs,
                         jax.nn.one_hot(top_i, N_EXPERTS, dtype=jnp.float32)).astype(h.dtype)
    out = jnp.zeros_like(xt)
    for e in range(N_EXPERTS):
        out = out + combine[:, e:e + 1] * _swiglu(xt, w_gate[e], w_up[e], w_down[e])
    return out.reshape(b, t, d)


def _layer(x, mod, p, ffn, s0_gla, s0_gdn, latent):
    sh1, sc1, g1, sh2, sc2, g2 = jnp.split(mod, 6, axis=-1)
    m, st_gla, st_gdn = _mixer(x * (1 + sc1) + sh1, p, s0_gla, s0_gdn, latent)
    x = _layer_norm(ALPHA * x + g1 * m, p['ln1_g'], p['ln1_b'])
    f = ffn(x * (1 + sc2) + sh2)
    x = _layer_norm(ALPHA * x + g2 * f, p['ln2_g'], p['ln2_b'])
    return x, st_gla, st_gdn


def setup_inputs(seed: int = 0) -> dict:
    key = jax.random.key(seed)
    ks = list(jax.random.split(key, 40))
    f32 = jnp.float32
    nrm = lambda i, shape, s: jax.random.normal(ks[i], shape, f32) * s
    D = D_MODEL
    dt = jnp.exp(jax.random.uniform(ks[30], (DEPTH, 2, GDN_HEADS), f32, np.log(1e-3), np.log(1e-1)))
    return {
        'x_prompt': nrm(0, (BATCH, SEQ, D), 1.0),
        'x_sample': nrm(1, (DEC_BATCH, DEC_SEQ, D), 1.0),
        'c': nrm(2, (DEC_BATCH, D), 1.0),
        'state_gla': nrm(3, (DEC_BATCH, DEPTH, 2, GLA_HEADS, GLA_DK, GLA_DV), 1.0),
        'state_gdn': nrm(4, (DEC_BATCH, DEPTH, 2, GDN_HEADS, GDN_DK, GDN_DV), 0.5),
        'c_ctx': nrm(5, (D,), 1.0),
        'w_ada': nrm(6, (DEPTH, D, 6 * D), 0.5 * D ** -0.5),
        'b_ada': nrm(7, (DEPTH, 6 * D), 0.02),
        'w_in': nrm(8, (DEPTH, D, N_IN), D ** -0.5),
        'w_out': nrm(9, (DEPTH, MIX_W, D), BETA_INIT * MIX_W ** -0.5),
        'conv_w': nrm(10, (DEPTH, CONV_K, GW), CONV_K ** -0.5),
        'conv_b': nrm(11, (DEPTH, GW), 0.02),
        'conv_ln_g': 1.0 + nrm(12, (DEPTH, GW), 0.02),
        'conv_ln_b': nrm(13, (DEPTH, GW), 0.02),
        'conv_pw': nrm(14, (DEPTH, GW, GW), GW ** -0.5),
        'gla_w_lr': nrm(15, (DEPTH, 2, GLA_LR, GW), GLA_LR ** -0.5),
        'gla_b_lr': nrm(16, (DEPTH, 2, GW), 0.1),
        'gla_norm_g': 1.0 + nrm(17, (DEPTH, GW), 0.02),
        'gdn_conv_w': nrm(18, (DEPTH, GDN_CONV_K, 3 * GW), GDN_CONV_K ** -0.5),
        'gdn_a_log': jnp.log(jax.random.uniform(ks[19], (DEPTH, 2, GDN_HEADS), f32, 1.0, 16.0)),
        'gdn_dt_bias': jnp.log(jnp.expm1(dt)),
        'gdn_norm_g': 1.0 + nrm(20, (DEPTH, GW), 0.02),
        'sc_conv_w': nrm(21, (DEPTH, SC_K, GW), SC_K ** -0.5),
        'ln1_g': 1.0 + nrm(22, (DEPTH, D), 0.02),
        'ln1_b': nrm(23, (DEPTH, D), 0.02),
        'ln2_g': 1.0 + nrm(24, (DEPTH, D), 0.02),
        'ln2_b': nrm(25, (DEPTH, D), 0.02),
        'ffn_w_gate': nrm(26, (N_DENSE, D, D_FF), D ** -0.5),
        'ffn_w_up': nrm(27, (N_DENSE, D, D_FF), D ** -0.5),
        'ffn_w_down': nrm(28, (N_DENSE, D_FF, D), BETA_INIT * D_FF ** -0.5),
        'moe_router': nrm(29, (N_MOE, D, N_EXPERTS), D ** -0.5),
        'moe_w_gate': nrm(31, (N_MOE, N_EXPERTS, D, D_FF_EXPERT), D ** -0.5),
        'moe_w_up': nrm(32, (N_MOE, N_EXPERTS, D, D_FF_EXPERT), D ** -0.5),
        'moe_w_down': nrm(33, (N_MOE, N_EXPERTS, D_FF_EXPERT, D), BETA_INIT * D_FF_EXPERT ** -0.5),
    }


def reference(x_prompt, x_sample, c, state_gla, state_gdn, c_ctx, w_ada, b_ada, w_in, w_out,
              conv_w, conv_b, conv_ln_g, conv_ln_b, conv_pw, gla_w_lr, gla_b_lr, gla_norm_g,
              gdn_conv_w, gdn_a_log, gdn_dt_bias, gdn_norm_g, sc_conv_w, ln1_g, ln1_b, ln2_g, ln2_b,
              ffn_w_gate, ffn_w_up, ffn_w_down, moe_router, moe_w_gate, moe_w_up, moe_w_down):
    f32 = jnp.float32
    nb = x_prompt.shape[0]
    zero_gla = jnp.zeros((nb, 2, GLA_HEADS, GLA_DK, GLA_DV), f32)
    zero_gdn = jnp.zeros((nb, 2, GDN_HEADS, GDN_DK, GDN_DV), f32)
    y_p, y_s = x_prompt, x_sample
    gla_states, gdn_states = [], []
    for l in range(DEPTH):
        p = dict(w_in=w_in[l], w_out=w_out[l], conv_w=conv_w[l], conv_b=conv_b[l],
                 conv_ln_g=conv_ln_g[l], conv_ln_b=conv_ln_b[l], conv_pw=conv_pw[l],
                 gla_w_lr=gla_w_lr[l], gla_b_lr=gla_b_lr[l], gla_norm_g=gla_norm_g[l],
                 gdn_conv_w=gdn_conv_w[l], gdn_a_log=gdn_a_log[l], gdn_dt_bias=gdn_dt_bias[l],
                 gdn_norm_g=gdn_norm_g[l], sc_conv_w=sc_conv_w[l],
                 ln1_g=ln1_g[l], ln1_b=ln1_b[l], ln2_g=ln2_g[l], ln2_b=ln2_b[l])
        j = l // 2
        if l % 2 == 0:
            ffn = functools.partial(_swiglu, w_gate=ffn_w_gate[j], w_up=ffn_w_up[j], w_down=ffn_w_down[j])
        else:
            ffn = functools.partial(_moe, router=moe_router[j], w_gate=moe_w_gate[j],
                                    w_up=moe_w_up[j], w_down=moe_w_down[j])
        mod_ctx = (jax.nn.silu(c_ctx) @ w_ada[l] + b_ada[l])[None, None, :]
        mod_lat = (jax.nn.silu(c) @ w_ada[l] + b_ada[l])[:, None, :]
        y_p, s_gla, s_gdn = _layer(y_p, mod_ctx, p, ffn, zero_gla, zero_gdn, False)
        gla_states.append(s_gla.astype(x_prompt.dtype))
        gdn_states.append(s_gdn.astype(x_prompt.dtype))
        y_s, _, _ = _layer(y_s, mod_lat, p, ffn, state_gla[:, l].astype(f32), state_gdn[:, l].astype(f32), True)
    new_state_gla = jnp.stack(gla_states, axis=1)
    new_state_gdn = jnp.stack(gdn_states, axis=1)
    return (y_p, y_s, new_state_gla, new_state_gdn)
```

```python
import functools

import numpy as np
import jax
import jax.numpy as jnp
from jax import lax
from jax.experimental import pallas as pl
from jax.experimental.pallas import tpu as pltpu

f32 = jnp.float32
bf16 = jnp.bfloat16

D = 1024
GW = 256
NH = 4
DH = 64
CH = 64
GRID_W = 64
CONV_K = 31
GLA_LR = 16
GLA_TAU = 16.0
DEPTH = 4
N_EXPERTS = 8
D_FF = 2816
D_FF_EXPERT = 1408
ALPHA = (2 * DEPTH) ** 0.25
EPS = 1e-5
NU = 13 * GW + 128
SMALL_BLK = 13 * GW // 128
LANE = 128
C_AVAL, C_AGATE, C_GQ, C_GK, C_GV, C_GG, C_DQ, C_DK, C_DV, C_DGATE, C_SB, C_SC, C_SX = range(13)
L_BETA = 2 * GLA_LR
L_DEC = L_BETA + NH

VMEM_LIMIT = 56 * 1024 * 1024


def _cparams(sem):
    return pltpu.CompilerParams(dimension_semantics=sem, vmem_limit_bytes=VMEM_LIMIT)


def _dot(a, b):
    return jnp.dot(a.astype(bf16), b.astype(bf16), preferred_element_type=f32)


def _dot_nt(a, b):
    return lax.dot_general(a.astype(bf16), b.astype(bf16), (((1,), (1,)), ((), ())),
                           preferred_element_type=f32)


def _dot_tn(a, b):
    return lax.dot_general(a.astype(bf16), b.astype(bf16), (((0,), (0,)), ((), ())),
                           preferred_element_type=f32)


def _split3(x):
    hi = x.astype(bf16)
    r = x - hi.astype(f32)
    mid = r.astype(bf16)
    lo = (r - mid.astype(f32)).astype(bf16)
    return hi, mid, lo


def _dot01_left(m01, x):
    hi, mid, lo = _split3(x)
    d = lambda t: jnp.dot(m01, t, preferred_element_type=f32)
    return d(hi) + d(mid) + d(lo)


def _dot01_right(x, m01):
    hi, mid, lo = _split3(x)
    d = lambda t: jnp.dot(t, m01, preferred_element_type=f32)
    return d(hi) + d(mid) + d(lo)


def _bd(x, mask):
    xb = x.astype(bf16)
    return jnp.concatenate([xb, xb, xb, xb], axis=0) * mask


def _dot3(a, b, mask):
    ah = a.astype(bf16)
    al = (a - ah.astype(f32)).astype(bf16)
    bh = b.astype(bf16)
    bdh = _bd(bh, mask)
    bdl = _bd(b - bh.astype(f32), mask)
    d = lambda x, y: jnp.dot(x, y, preferred_element_type=f32)
    return d(ah, bdh) + d(al, bdh) + d(ah, bdl)


def _silu(x):
    return x * jax.nn.sigmoid(x)


def _ln(x, g, b):
    xc = x - jnp.mean(x, axis=-1, keepdims=True)
    var = jnp.mean(xc * xc, axis=-1, keepdims=True)
    return xc * lax.rsqrt(var + EPS) * g + b


def _head_iotas():
    rowi = lax.broadcasted_iota(jnp.int32, (CH, GW), 0)
    coli = lax.broadcasted_iota(jnp.int32, (CH, GW), 1) & (DH - 1)
    return rowi, coli


def _mod_kernel(c_ref, w_ref, b_ref, o_ref):
    o_ref[...] = jnp.dot(_silu(c_ref[...]), w_ref[...], precision=lax.Precision.HIGHEST,
                         preferred_element_type=f32) + b_ref[...]


def _mod_call(cond, w_ada, b_ada):
    nl = w_ada.shape[0]
    rows = cond.shape[0]
    return pl.pallas_call(
        _mod_kernel, grid=(nl, 6),
        in_specs=[pl.BlockSpec((rows, D), lambda l, n: (0, 0)),
                  pl.BlockSpec((None, D, D), lambda l, n: (l, 0, n)),
                  pl.BlockSpec((None, 1, D), lambda l, n: (l, 0, n))],
        out_specs=pl.BlockSpec((None, rows, D), lambda l, n: (l, 0, n)),
        out_shape=jax.ShapeDtypeStruct((nl, rows, 6 * D), f32),
        compiler_params=_cparams(("arbitrary", "arbitrary")), name="adaln_mod",
    )(cond, w_ada, b_ada.reshape(nl, 1, 6 * D))


def _in_kernel(x_ref, mod_ref, w_ref, u_ref):
    m = mod_ref[...]
    h = x_ref[...] * (1.0 + m[:, D:2 * D]) + m[:, 0:D]
    u_ref[...] = jnp.dot(h.astype(bf16), w_ref[...], preferred_element_type=f32)


def _in_call(x, mod, w, tiles_per_mod, tm=512):
    n = x.shape[0]
    return pl.pallas_call(
        _in_kernel, grid=(n // tm,),
        in_specs=[pl.BlockSpec((tm, D), lambda i: (i, 0)),
                  pl.BlockSpec((None, 1, 6 * D), lambda i: (i // tiles_per_mod, 0, 0)),
                  pl.BlockSpec((D, NU), lambda i: (0, 0))],
        out_specs=pl.BlockSpec((tm, NU), lambda i: (i, 0)),
        out_shape=jax.ShapeDtypeStruct((n, NU), f32),
        compiler_params=_cparams(("arbitrary",)), name="in_proj",
    )(x, mod, w)


CT = 256
HALO = 16
SHALO = 8


def _conv_kernel(av, ag, avp, agp, avn, agn, sb, sc, sx, scp, sxp, scn, sxn,
                 cw, cb, lng, lnb, pw, scw, a_out, ysc_out, pad_a, pad_s, *, tiles_per_seq):
    i = pl.program_id(0)
    j = i % tiles_per_seq
    first = j == 0
    last = j == tiles_per_seq - 1
    glu = lambda v, g: v * jax.nn.sigmoid(g)
    pad_a[0:HALO, :] = jnp.where(first, 0.0, glu(avp[...], agp[...]))
    pad_a[HALO:HALO + CT, :] = glu(av[...], ag[...])
    pad_a[HALO + CT:2 * HALO + CT, :] = jnp.where(last, 0.0, glu(avn[...], agn[...]))
    acc = jnp.zeros((CT, GW), f32)
    for k in range(CONV_K):
        o = HALO - CONV_K // 2 + k
        acc = acc + pad_a[o:o + CT, :] * cw[k:k + 1, :]
    a = _ln(acc + cb[...], lng[...], lnb[...])
    a_out[...] = _dot(_silu(a), pw[...]).astype(bf16)

    pad_s[0:SHALO, :] = jnp.where(first, 0.0, scp[...] * sxp[...])
    pad_s[SHALO:SHALO + CT, :] = sc[...] * sx[...]
    pad_s[SHALO + CT:2 * SHALO + CT, :] = jnp.where(last, 0.0, scn[...] * sxn[...])
    acc = jnp.zeros((CT, GW), f32)
    for k in range(3):
        o = SHALO - 1 + k
        acc = acc + pad_s[o:o + CT, :] * scw[k:k + 1, :]
    ysc_out[...] = (sb[...] * acc).astype(bf16)


def _conv_call(u, seq_len, p):
    n = u.shape[0]
    tps = seq_len // CT
    nt = n // CT
    main = lambda c: pl.BlockSpec((CT, GW), lambda i: (i, c))

    def halo(c, rows, nxt):
        per = CT // rows
        if nxt:
            return pl.BlockSpec((rows, GW), lambda i: (jnp.minimum((i + 1) * per, nt * per - 1), c))
        return pl.BlockSpec((rows, GW), lambda i: (jnp.maximum(i * per - 1, 0), c))

    full = lambda a: pl.BlockSpec(a.shape, lambda i: (0,) * a.ndim)
    consts = (p['conv_w'], p['conv_b'], p['conv_ln_g'], p['conv_ln_b'], p['conv_pw'], p['sc_conv_w'])
    in_specs = [main(C_AVAL), main(C_AGATE), halo(C_AVAL, HALO, 0), halo(C_AGATE, HALO, 0),
                halo(C_AVAL, HALO, 1), halo(C_AGATE, HALO, 1),
                main(C_SB), main(C_SC), main(C_SX), halo(C_SC, SHALO, 0), halo(C_SX, SHALO, 0),
                halo(C_SC, SHALO, 1), halo(C_SX, SHALO, 1)] + [full(a) for a in consts]
    return pl.pallas_call(
        functools.partial(_conv_kernel, tiles_per_seq=tps), grid=(nt,),
        in_specs=in_specs,
        out_specs=[pl.BlockSpec((CT, GW), lambda i: (i, 0))] * 2,
        out_shape=[jax.ShapeDtypeStruct((n, GW), bf16)] * 2,
        scratch_shapes=[pltpu.VMEM((CT + 2 * HALO, GW), f32), pltpu.VMEM((CT + 2 * SHALO, GW), f32)],
        compiler_params=_cparams(("arbitrary",)), name="conv_mixers",
    )(*([u] * 13), *consts)


def _gla_kernel(q_ref, k_ref, v_ref, g_ref, sm_ref, s0_ref, wlr_ref, blr_ref, ng_ref,
                mbd_ref, mbdf_ref, tril_ref, triu_ref, y_ref, sfin_ref, st_scr, of_scr, *, tt, ntiles):
    p = pl.program_id(1)
    nch = tt // CH
    rowi, coli = _head_iotas()

    @pl.when(p == 0)
    def _():
        st_scr[...] = s0_ref[0]

    @pl.when(p == ntiles)
    def _():
        sfin_ref[0] = st_scr[...]
        st_scr[...] = s0_ref[1]

    def run(dirn):
        tri = tril_ref[...] if dirn == 0 else triu_ref[...]
        keep = (rowi >= coli) if dirn == 0 else (rowi <= coli)
        wlr = wlr_ref[dirn]
        blr = blr_ref[dirn]
        tile = p if dirn == 0 else 2 * ntiles - 1 - p

        def body(c, carry):
            ci = c if dirn == 0 else nch - 1 - c
            rows = pl.ds(pl.multiple_of(ci * CH, CH), CH)
            grows = pl.ds(pl.multiple_of(tile * tt + ci * CH, CH), CH)
            z = jnp.dot(sm_ref[rows, :].astype(bf16), wlr, preferred_element_type=f32) + blr
            la = jax.nn.log_sigmoid(z) * (1.0 / GLA_TAU)
            cum = _dot01_left(tri, la)
            tot = cum[CH - 1:CH, :] if dirn == 0 else cum[0:1, :]
            q = q_ref[rows, :] * (DH ** -0.5)
            k = k_ref[rows, :]
            v = v_ref[rows, :]
            qe = q * jnp.exp(cum)
            ke = k * jnp.exp(-cum)
            kd = k * jnp.exp(tot - cum)
            mbd = mbd_ref[...]
            att = jnp.where(keep, _dot_nt(qe, _bd(ke, mbd)), 0.0)
            st = st_scr[...]
            o = _dot(att, _bd(v, mbd)) + _dot_nt(qe, st)
            st_scr[...] = st * jnp.exp(tot) + mbdf_ref[...] * _dot_tn(v, kd)
            if dirn == 0:
                of_scr[grows, :] = o
            else:
                o = of_scr[grows, :] + o
                ms = _dot(o * o, mbd) * (1.0 / DH)
                y = o * lax.rsqrt(ms + EPS) * ng_ref[...] * _silu(g_ref[rows, :])
                y_ref[rows, :] = y.astype(bf16)
            return carry

        lax.fori_loop(0, nch, body, 0)

    @pl.when(p < ntiles)
    def _():
        run(0)

    @pl.when(p >= ntiles)
    def _():
        run(1)

    @pl.when(p == 2 * ntiles - 1)
    def _():
        sfin_ref[1] = st_scr[...]


def _gla_call(u, nseq, seq_len, s0, p, cst):
    tt = min(seq_len, 512)
    ntiles = seq_len // tt
    tmap = lambda q: jnp.where(q < ntiles, q, 2 * ntiles - 1 - q)
    main = lambda c: pl.BlockSpec((tt, GW), lambda s, q: (s * ntiles + tmap(q), c))
    full = lambda a: pl.BlockSpec(a.shape, lambda s, q: (0,) * a.ndim)
    consts = (p['gla_wlr'], p['gla_blr'], p['gla_norm_g'], cst['mbd'], cst['mbdf'], cst['tril'], cst['triu'])
    return pl.pallas_call(
        functools.partial(_gla_kernel, tt=tt, ntiles=ntiles), grid=(nseq, 2 * ntiles),
        in_specs=[main(C_GQ), main(C_GK), main(C_GV), main(C_GG),
                  pl.BlockSpec((tt, LANE), lambda s, q: (s * ntiles + tmap(q), SMALL_BLK)),
                  pl.BlockSpec((None, 2, GW, GW), lambda s, q: (s, 0, 0, 0))] + [full(a) for a in consts],
        out_specs=[pl.BlockSpec((tt, GW), lambda s, q: (s * ntiles + jnp.where(q < ntiles, ntiles - 1, 2 * ntiles - 1 - q), 0)),
                   pl.BlockSpec((None, 2, GW, GW), lambda s, q: (s, 0, 0, 0))],
        out_shape=[jax.ShapeDtypeStruct((nseq * seq_len, GW), bf16),
                   jax.ShapeDtypeStruct((nseq, 2, GW, GW), f32)],
        scratch_shapes=[pltpu.VMEM((GW, GW), f32), pltpu.VMEM((seq_len, GW), f32)],
        compiler_params=_cparams(("arbitrary", "arbitrary")), name="gla",
    )(u, u, u, u, u, s0, *consts)


def _gdn_kernel(q_ref, k_ref, v_ref, g_ref, sm_ref, s0_ref, cw_ref, alog_ref, dtb_ref, ng_ref, e_ref,
                mbd_ref, mbdf_ref, tril_ref, triu_ref, y_ref, sfin_ref, sf_scr, sb_scr, osum_scr, *, n, colmajor):
    rowi, coli = _head_iotas()
    row_first = rowi == 0
    row_last = rowi == CH - 1
    icat = jnp.where(rowi == coli, 1.0, 0.0)

    if colmajor:
        get = lambda ref, j: ref[:, j, :]
        last_row = lambda ref, j: ref[CH - 1, pl.ds(j, 1), :]
        first_row = lambda ref, j: ref[0, pl.ds(j, 1), :]
    else:
        get = lambda ref, j: ref[j]
        last_row = lambda ref, j: ref[j, CH - 1:CH, :]
        first_row = lambda ref, j: ref[j, 0:1, :]

    def chunk(j, dirn, s_scr):
        has_prev = j > 0
        has_next = j < n - 1
        jm = jnp.maximum(j - 1, 0)
        jp = jnp.minimum(j + 1, n - 1)

        def conv(ref, c0):
            x = get(ref, j)
            pr = jnp.where(has_prev, last_row(ref, jm), 0.0)
            nx = jnp.where(has_next, first_row(ref, jp), 0.0)
            xd = jnp.where(row_first, pr, pltpu.roll(x, 1, 0))
            xu = jnp.where(row_last, nx, pltpu.roll(x, CH - 1, 0))
            return _silu(xd * cw_ref[0:1, c0:c0 + GW] + x * cw_ref[1:2, c0:c0 + GW] + xu * cw_ref[2:3, c0:c0 + GW])

        mbd = mbd_ref[...]
        q = conv(q_ref, 0)
        k = conv(k_ref, GW)
        v = conv(v_ref, 2 * GW)
        q = q * lax.rsqrt(_dot(q * q, mbd) + 1e-6) * (DH ** -0.5)
        k = k * lax.rsqrt(_dot(k * k, mbd) + 1e-6)
        ex = _dot01_right(get(sm_ref, j), e_ref[dirn])
        beta = jax.nn.sigmoid(ex[:, :GW])
        la = -jnp.exp(alog_ref[dirn]) * jax.nn.softplus(ex[:, GW:] + dtb_ref[dirn])
        if dirn == 0:
            tri, keep, strict, keep_t = tril_ref[...], rowi >= coli, rowi > coli, rowi <= coli
        else:
            tri, keep, strict, keep_t = triu_ref[...], rowi <= coli, rowi < coli, rowi >= coli
        gx = _dot01_left(tri, la)
        grow = jnp.sum(jnp.where(keep_t, la, 0.0), axis=0, keepdims=True)
        decay = jnp.where(keep, jnp.exp(jnp.where(keep, gx - grow, 0.0)), 0.0)
        glast = gx[CH - 1:CH, :] if dirn == 0 else gx[0:1, :]
        kb = k * beta
        vb = v * beta
        gm = _dot_nt(jnp.concatenate([kb, q], axis=0), _bd(k, mbd))
        nm = jnp.where(strict, -(gm[:CH] * decay), 0.0)
        qkd = gm[CH:] * decay
        pw = _dot3(nm, nm, mbd)
        tinv = icat + nm
        for _ in range(4):
            r = _dot3(jnp.concatenate([tinv, pw], axis=0), pw, mbd)
            tinv = tinv + r[:CH]
            pw = r[CH:]
        tinv = tinv + _dot3(tinv, pw, mbd)
        eg = jnp.exp(gx)
        u = _dot(tinv, _bd(vb, mbd))
        w = _dot(tinv, _bd(kb * eg, mbd))
        qg = q * eg
        kd = k * jnp.exp(glast - gx)
        s = s_scr[...]
        ws = _dot(jnp.concatenate([w, qg], axis=0), s)
        vn = u - ws[:CH]
        o = ws[CH:] + _dot(qkd, _bd(vn, mbd))
        s_scr[...] = s * jnp.exp(glast) + mbdf_ref[...] * _dot_tn(kd, vn)
        return o

    osum_scr[...] = jnp.zeros(osum_scr.shape, f32)
    sf_scr[...] = s0_ref[0]
    sb_scr[...] = s0_ref[1]

    def body(i, carry):
        jb = n - 1 - i
        osum_scr[i] = osum_scr[i] + chunk(i, 0, sf_scr)
        osum_scr[jb] = osum_scr[jb] + chunk(jb, 1, sb_scr)
        return carry

    lax.fori_loop(0, n, body, 0)
    sfin_ref[0] = sf_scr[...]
    sfin_ref[1] = sb_scr[...]

    def fin(j, carry):
        o = osum_scr[j]
        ms = _dot(o * o, mbd_ref[...]) * (1.0 / DH)
        y = o * lax.rsqrt(ms + EPS) * ng_ref[...] * _silu(get(g_ref, j))
        if colmajor:
            y_ref[:, j, :] = y
        else:
            y_ref[j] = y
        return carry

    lax.fori_loop(0, n, fin, 0)


def _gdn_call(u, nseq, seq_len, s0, p, cst, colmajor):
    n = seq_len // CH
    a = GRID_W if colmajor else n
    assert not colmajor or seq_len == GRID_W * CH
    u4 = u.reshape(nseq, a, CH, NU)
    main = lambda c: pl.BlockSpec((None, a, CH, GW), lambda s: (s, 0, 0, c))
    full = lambda x: pl.BlockSpec(x.shape, lambda s: (0,) * x.ndim)
    consts = (p['gdn_conv_w'], p['gdn_alog'], p['gdn_dtb'], p['gdn_norm_g'], cst['e_gdn'],
              cst['mbd'], cst['mbdf'], cst['tril'], cst['triu'])
    y, sfin = pl.pallas_call(
        functools.partial(_gdn_kernel, n=n, colmajor=colmajor), grid=(nseq,),
        in_specs=[main(C_DQ), main(C_DK), main(C_DV), main(C_DGATE),
                  pl.BlockSpec((None, a, CH, LANE), lambda s: (s, 0, 0, SMALL_BLK)),
                  pl.BlockSpec((None, 2, GW, GW), lambda s: (s, 0, 0, 0))] + [full(x) for x in consts],
        out_specs=[pl.BlockSpec((None, a, CH, GW), lambda s: (s, 0, 0, 0)),
                   pl.BlockSpec((None, 2, GW, GW), lambda s: (s, 0, 0, 0))],
        out_shape=[jax.ShapeDtypeStruct((nseq, a, CH, GW), f32),
                   jax.ShapeDtypeStruct((nseq, 2, GW, GW), f32)],
        scratch_shapes=[pltpu.VMEM((GW, GW), f32), pltpu.VMEM((GW, GW), f32), pltpu.VMEM((n, CH, GW), f32)],
        compiler_params=_cparams(("arbitrary",)), name="gdn",
    )(u4, u4, u4, u4, u4, s0, *consts)
    return y.reshape(nseq * seq_len, GW), sfin


def _out_kernel(a_ref, gl_ref, gd_ref, sc_ref, x_ref, mod_ref, w_ref, lg_ref, lb_ref, o_ref):
    m = mod_ref[...]
    mix_in = jnp.concatenate([a_ref[...], gl_ref[...], gd_ref[...].astype(bf16), sc_ref[...]], axis=1)
    mix = jnp.dot(mix_in, w_ref[...], preferred_element_type=f32)
    o_ref[...] = _ln(ALPHA * x_ref[...] + m[:, 2 * D:3 * D] * mix, lg_ref[...], lb_ref[...])


def _out_call(a, gl, gd, sc, x, mod, p, tiles_per_mod, tm=512):
    n = x.shape[0]
    part = pl.BlockSpec((tm, GW), lambda i: (i, 0))
    full = lambda t: pl.BlockSpec(t.shape, lambda i: (0,) * t.ndim)
    return pl.pallas_call(
        _out_kernel, grid=(n // tm,),
        in_specs=[part, part, part, part, pl.BlockSpec((tm, D), lambda i: (i, 0)),
                  pl.BlockSpec((None, 1, 6 * D), lambda i: (i // tiles_per_mod, 0, 0)),
                  full(p['w_out']), full(p['ln1_g']), full(p['ln1_b'])],
        out_specs=pl.BlockSpec((tm, D), lambda i: (i, 0)),
        out_shape=jax.ShapeDtypeStruct((n, D), f32),
        compiler_params=_cparams(("arbitrary",)), name="out_proj_ln",
    )(a, gl, gd, sc, x, mod, p['w_out'], p['ln1_g'], p['ln1_b'])


def _ffn_kernel(*refs, moe, ngroups):
    if moe:
        x_ref, mod_ref, wg_ref, wu_ref, wd_ref, lg_ref, lb_ref, rt_ref, o_ref, h_scr, acc_scr, comb_scr = refs
    else:
        x_ref, mod_ref, wg_ref, wu_ref, wd_ref, lg_ref, lb_ref, o_ref, h_scr, acc_scr = refs
    g = pl.program_id(1)

    @pl.when(g == 0)
    def _():
        m = mod_ref[...]
        h = x_ref[...] * (1.0 + m[:, 4 * D:5 * D]) + m[:, 3 * D:4 * D]
        h_scr[...] = h.astype(bf16)
        acc_scr[...] = jnp.zeros(acc_scr.shape, f32)
        if moe:
            logits = jnp.dot(h, rt_ref[...], precision=lax.Precision.HIGHEST, preferred_element_type=f32)
            lane = lax.broadcasted_iota(jnp.int32, logits.shape, 1).astype(f32)
            neg = jnp.float32(-jnp.inf)
            logits = jnp.where(lane < N_EXPERTS, logits, neg)
            m1 = jnp.max(logits, axis=-1, keepdims=True)
            i1 = jnp.min(jnp.where(logits == m1, lane, float(LANE)), axis=-1, keepdims=True)
            rest = jnp.where(lane == i1, neg, logits)
            m2 = jnp.max(rest, axis=-1, keepdims=True)
            i2 = jnp.min(jnp.where(rest == m2, lane, float(LANE)), axis=-1, keepdims=True)
            e2 = jnp.exp(m2 - m1)
            p1 = 1.0 / (1.0 + e2)
            p2 = e2 / (1.0 + e2)
            comb_scr[...] = jnp.where(lane == i1, p1, 0.0) + jnp.where(lane == i2, p2, 0.0)

    hb = h_scr[...]
    t = _silu(jnp.dot(hb, wg_ref[...], preferred_element_type=f32)) * jnp.dot(hb, wu_ref[...], preferred_element_type=f32)
    out = jnp.dot(t.astype(bf16), wd_ref[...], preferred_element_type=f32)
    if moe:
        comb = comb_scr[...]
        lane = lax.broadcasted_iota(jnp.int32, comb.shape, 1)
        out = out * jnp.sum(jnp.where(lane == g, comb, 0.0), axis=-1, keepdims=True)
    acc_scr[...] += out

    @pl.when(g == ngroups - 1)
    def _():
        m = mod_ref[...]
        o_ref[...] = _ln(ALPHA * x_ref[...] + m[:, 5 * D:6 * D] * acc_scr[...], lg_ref[...], lb_ref[...])


def _ffn_call(x, mod, p, tiles_per_mod, moe, tm=512):
    n = x.shape[0]
    fw = D_FF_EXPERT
    full = lambda t: pl.BlockSpec(t.shape, lambda i, g: (0,) * t.ndim)
    if moe:
        ngroups = N_EXPERTS
        wspecs = [pl.BlockSpec((None, D, fw), lambda i, g: (g, 0, 0)),
                  pl.BlockSpec((None, D, fw), lambda i, g: (g, 0, 0)),
                  pl.BlockSpec((None, fw, D), lambda i, g: (g, 0, 0))]
        extra, extra_specs = (p['router'],), [full(p['router'])]
        scratch = [pltpu.VMEM((tm, LANE), f32)]
    else:
        ngroups = D_FF // fw
        wspecs = [pl.BlockSpec((D, fw), lambda i, g: (0, g)),
                  pl.BlockSpec((D, fw), lambda i, g: (0, g)),
                  pl.BlockSpec((fw, D), lambda i, g: (g, 0))]
        extra, extra_specs, scratch = (), [], []
    return pl.pallas_call(
        functools.partial(_ffn_kernel, moe=moe, ngroups=ngroups), grid=(n // tm, ngroups),
        in_specs=[pl.BlockSpec((tm, D), lambda i, g: (i, 0)),
                  pl.BlockSpec((None, 1, 6 * D), lambda i, g: (i // tiles_per_mod, 0, 0))] + wspecs
                 + [full(p['ln2_g']), full(p['ln2_b'])] + extra_specs,
        out_specs=pl.BlockSpec((tm, D), lambda i, g: (i, 0)),
        out_shape=jax.ShapeDtypeStruct((n, D), f32),
        scratch_shapes=[pltpu.VMEM((tm, D), bf16), pltpu.VMEM((tm, D), f32)] + scratch,
        compiler_params=_cparams(("arbitrary", "arbitrary")), name="moe_ffn" if moe else "dense_ffn",
    )(x, mod, p['w_gate'], p['w_up'], p['w_down'], p['ln2_g'], p['ln2_b'], *extra)


def _constants():
    hb = np.arange(GW) // DH
    mbd = (hb[:, None] == hb[None, :]).astype(np.float32)
    r = np.arange(CH)
    e = np.zeros((2, LANE, 2 * GW), np.float32)
    for d in range(2):
        for h in range(NH):
            e[d, L_BETA + h, h * DH:(h + 1) * DH] = 1.0
            e[d, L_DEC + NH * d + h, GW + h * DH:GW + (h + 1) * DH] = 1.0
    return dict(mbd=jnp.asarray(mbd, bf16), mbdf=jnp.asarray(mbd, f32),
                tril=jnp.asarray(r[:, None] >= r[None, :], bf16), triu=jnp.asarray(r[:, None] <= r[None, :], bf16),
                e_gdn=jnp.asarray(e, bf16))


_IN_SIZES = (GW, GW, GW, GW, GW, GW, 2 * GLA_LR, GW, GW, GW, GW, NH, 2 * NH, GW, GW, GW)


def _pack_w_in(w_in):
    offs = np.concatenate([[0], np.cumsum(_IN_SIZES)])
    big = [i for i, s in enumerate(_IN_SIZES) if s == GW]
    small = [i for i, s in enumerate(_IN_SIZES) if s != GW]
    idx = np.concatenate([np.arange(offs[i], offs[i + 1]) for i in big + small])
    w = jnp.take(w_in, jnp.asarray(idx, jnp.int32), axis=-1)
    pad = NU - idx.shape[0]
    return jnp.pad(w, ((0, 0), (0, 0), (0, pad))).astype(bf16)


def _to_bd(s, transpose):
    if transpose:
        s = jnp.swapaxes(s, -1, -2)
    b = s.shape[0]
    z = jnp.zeros_like(s)
    rows = [jnp.concatenate([s[:, :, h] if g == h else z[:, :, h] for g in range(NH)], axis=-1) for h in range(NH)]
    return jnp.concatenate(rows, axis=-2).reshape(b, 2, GW, GW)


def _from_bd(s, transpose):
    out = jnp.stack([s[:, :, h * DH:(h + 1) * DH, h * DH:(h + 1) * DH] for h in range(NH)], axis=2)
    return jnp.swapaxes(out, -1, -2) if transpose else out


def _stream_layer(x, mod, nseq, seq_len, s0_gla, s0_gdn, p, cst, colmajor, moe):
    tpm_512 = max(seq_len // 512, 1) if mod.shape[0] > 1 else x.shape[0] // 512
    u = _in_call(x, mod, p['w_in'], tpm_512)
    a, ysc = _conv_call(u, seq_len, p)
    ygla, sgla = _gla_call(u, nseq, seq_len, s0_gla, p, cst)
    ygdn, sgdn = _gdn_call(u, nseq, seq_len, s0_gdn, p, cst, colmajor)
    x1 = _out_call(a, ygla, ygdn, ysc, x, mod, p, tpm_512)
    x2 = _ffn_call(x1, mod, p['ffn'], tpm_512, moe)
    return x2, sgla, sgdn


def kernel(x_prompt, x_sample, c, state_gla, state_gdn, c_ctx, w_ada, b_ada, w_in, w_out, conv_w, conv_b, conv_ln_g, conv_ln_b, conv_pw, gla_w_lr, gla_b_lr, gla_norm_g, gdn_conv_w, gdn_a_log, gdn_dt_bias, gdn_norm_g, sc_conv_w, ln1_g, ln1_b, ln2_g, ln2_b, ffn_w_gate, ffn_w_up, ffn_w_down, moe_router, moe_w_gate, moe_w_up, moe_w_down):
    nb, seq, _ = x_prompt.shape
    db, dseq, _ = x_sample.shape
    depth = w_in.shape[0]
    cst = _constants()

    rows = 16
    cond = jnp.concatenate([c_ctx[None, :], c, jnp.zeros((rows - 1 - db, D), f32)], axis=0)
    mod = _mod_call(cond, w_ada, b_ada)

    w_in_p = _pack_w_in(w_in)
    w_out_b = w_out.astype(bf16)
    conv_pw_b = conv_pw.astype(bf16)
    wlr = jnp.zeros((depth, 2, LANE, GW), f32)
    wlr = wlr.at[:, 0, 0:GLA_LR].set(gla_w_lr[:, 0]).at[:, 1, GLA_LR:2 * GLA_LR].set(gla_w_lr[:, 1]).astype(bf16)
    rep = lambda t: jnp.repeat(t, DH, axis=-1)[:, :, None, :]
    row = lambda t: t[:, None, :]
    ffn_g, ffn_u, ffn_d = ffn_w_gate.astype(bf16), ffn_w_up.astype(bf16), ffn_w_down.astype(bf16)
    moe_g, moe_u, moe_d = moe_w_gate.astype(bf16), moe_w_up.astype(bf16), moe_w_down.astype(bf16)
    router = jnp.pad(moe_router, ((0, 0), (0, 0), (0, LANE - N_EXPERTS)))

    zero_bd = jnp.zeros((nb, 2, GW, GW), f32)
    y_p = x_prompt.reshape(nb * seq, D)
    y_s = x_sample.reshape(db * dseq, D)
    gla_states, gdn_states = [], []
    for l in range(depth):
        j = l // 2
        moe = l % 2 == 1
        if moe:
            ffn = dict(w_gate=moe_g[j], w_up=moe_u[j], w_down=moe_d[j], router=router[j],
                       ln2_g=row(ln2_g)[l], ln2_b=row(ln2_b)[l])
        else:
            ffn = dict(w_gate=ffn_g[j], w_up=ffn_u[j], w_down=ffn_d[j], ln2_g=row(ln2_g)[l], ln2_b=row(ln2_b)[l])
        p = dict(w_in=w_in_p[l], w_out=w_out_b[l], conv_w=conv_w[l], conv_b=row(conv_b)[l],
                 conv_ln_g=row(conv_ln_g)[l], conv_ln_b=row(conv_ln_b)[l], conv_pw=conv_pw_b[l],
                 sc_conv_w=sc_conv_w[l], gla_wlr=wlr[l], gla_blr=gla_b_lr[l][:, None, :],
                 gla_norm_g=row(gla_norm_g)[l], gdn_conv_w=gdn_conv_w[l], gdn_alog=rep(gdn_a_log)[l],
                 gdn_dtb=rep(gdn_dt_bias)[l], gdn_norm_g=row(gdn_norm_g)[l],
                 ln1_g=row(ln1_g)[l], ln1_b=row(ln1_b)[l], ffn=ffn)
        mod_ctx = mod[l, 0:1][:, None, :]
        mod_lat = mod[l, 1:1 + db][:, None, :]
        y_p, s_gla, s_gdn = _stream_layer(y_p, mod_ctx, nb, seq, zero_bd, zero_bd, p, cst, False, moe)
        gla_states.append(_from_bd(s_gla, True))
        gdn_states.append(_from_bd(s_gdn, False))
        y_s, _, _ = _stream_layer(y_s, mod_lat, db, dseq, _to_bd(state_gla[:, l], True),
                                  _to_bd(state_gdn[:, l], False), p, cst, True, moe)
    return (y_p.reshape(nb, seq, D), y_s.reshape(db, dseq, D),
            jnp.stack(gla_states, axis=1), jnp.stack(gdn_states, axis=1))
```

```python
import functools
import math

import numpy as np
import jax
import jax.numpy as jnp
from jax import lax
from jax.experimental import pallas as pl
from jax.experimental.pallas import tpu as pltpu

f32 = jnp.float32
bf16 = jnp.bfloat16

D = 1024
GW = 256
NH = 4
DH = 64
CH = 64
GRID_W = 64
CONV_K = 31
GLA_LR = 16
GLA_TAU = 16.0
DEPTH = 4
N_EXPERTS = 8
D_FF = 2816
D_FF_EXPERT = 1408
ALPHA = (2 * DEPTH) ** 0.25
EPS = 1e-5
NU = 13 * GW + 128
SMALL_BLK = 13 * GW // 128
LANE = 128
C_AVAL, C_AGATE, C_GQ, C_GK, C_GV, C_GG, C_DQ, C_DK, C_DV, C_DGATE, C_SB, C_SC, C_SX = range(13)
L_BETA = 2 * GLA_LR
L_DEC = L_BETA + NH

VMEM_LIMIT = 56 * 1024 * 1024


def _cparams(sem):
    return pltpu.CompilerParams(dimension_semantics=sem, vmem_limit_bytes=VMEM_LIMIT)


def _dot(a, b):
    return jnp.dot(a.astype(bf16), b.astype(bf16), preferred_element_type=f32)


def _dot_nt(a, b):
    return lax.dot_general(a.astype(bf16), b.astype(bf16), (((1,), (1,)), ((), ())),
                           preferred_element_type=f32)


def _dot_tn(a, b):
    return lax.dot_general(a.astype(bf16), b.astype(bf16), (((0,), (0,)), ((), ())),
                           preferred_element_type=f32)


def _split3(x):
    hi = x.astype(bf16)
    r = x - hi.astype(f32)
    mid = r.astype(bf16)
    lo = (r - mid.astype(f32)).astype(bf16)
    return hi, mid, lo


def _dot01_left(m01, x):
    hi, mid, lo = _split3(x)
    d = lambda t: jnp.dot(m01, t, preferred_element_type=f32)
    return d(hi) + d(mid) + d(lo)


def _dot01_right(x, m01):
    hi, mid, lo = _split3(x)
    d = lambda t: jnp.dot(t, m01, preferred_element_type=f32)
    return d(hi) + d(mid) + d(lo)


def _bd(x, mask):
    xb = x.astype(bf16)
    return jnp.concatenate([xb, xb, xb, xb], axis=0) * mask


def _dot3(a, b, mask):
    ah = a.astype(bf16)
    al = (a - ah.astype(f32)).astype(bf16)
    bh = b.astype(bf16)
    bdh = _bd(bh, mask)
    bdl = _bd(b - bh.astype(f32), mask)
    d = lambda x, y: jnp.dot(x, y, preferred_element_type=f32)
    return d(ah, bdh) + d(al, bdh) + d(ah, bdl)


def _silu(x):
    return x * jax.nn.sigmoid(x)


def _ln(x, g, b):
    xc = x - jnp.mean(x, axis=-1, keepdims=True)
    var = jnp.mean(xc * xc, axis=-1, keepdims=True)
    return xc * lax.rsqrt(var + EPS) * g + b


def _head_iotas():
    rowi = lax.broadcasted_iota(jnp.int32, (CH, GW), 0)
    coli = lax.broadcasted_iota(jnp.int32, (CH, GW), 1) & (DH - 1)
    return rowi, coli


def _mod_kernel(c_ref, w_ref, b_ref, o_ref):
    o_ref[...] = jnp.dot(_silu(c_ref[...]), w_ref[...], precision=lax.Precision.HIGHEST,
                         preferred_element_type=f32) + b_ref[...]


def _mod_call(cond, w_ada, b_ada):
    nl = w_ada.shape[0]
    rows = cond.shape[0]
    return pl.pallas_call(
        _mod_kernel, grid=(nl, 6),
        in_specs=[pl.BlockSpec((rows, D), lambda l, n: (0, 0)),
                  pl.BlockSpec((None, D, D), lambda l, n: (l, 0, n)),
                  pl.BlockSpec((None, 1, D), lambda l, n: (l, 0, n))],
        out_specs=pl.BlockSpec((None, rows, D), lambda l, n: (l, 0, n)),
        out_shape=jax.ShapeDtypeStruct((nl, rows, 6 * D), f32),
        compiler_params=_cparams(("arbitrary", "arbitrary")), name="adaln_mod",
    )(cond, w_ada, b_ada.reshape(nl, 1, 6 * D))


def _in_kernel(x_ref, mod_ref, w_ref, u_ref):
    m = mod_ref[...]
    h = x_ref[...] * (1.0 + m[:, D:2 * D]) + m[:, 0:D]
    u_ref[...] = jnp.dot(h.astype(bf16), w_ref[...], preferred_element_type=f32)


def _in_call(x, mod, w, tiles_per_mod, tm=512):
    n = x.shape[0]
    return pl.pallas_call(
        _in_kernel, grid=(n // tm,),
        in_specs=[pl.BlockSpec((tm, D), lambda i: (i, 0)),
                  pl.BlockSpec((None, 1, 6 * D), lambda i: (i // tiles_per_mod, 0, 0)),
                  pl.BlockSpec((D, NU), lambda i: (0, 0))],
        out_specs=pl.BlockSpec((tm, NU), lambda i: (i, 0)),
        out_shape=jax.ShapeDtypeStruct((n, NU), f32),
        compiler_params=_cparams(("arbitrary",)), name="in_proj",
    )(x, mod, w)


CT = 256
HALO = 16
SHALO = 8


def _conv_kernel(av, ag, avp, agp, avn, agn, sb, sc, sx, scp, sxp, scn, sxn,
                 cw, cb, lng, lnb, pw, scw, a_out, ysc_out, pad_a, pad_s, *, tiles_per_seq):
    i = pl.program_id(0)
    j = i % tiles_per_seq
    first = j == 0
    last = j == tiles_per_seq - 1
    glu = lambda v, g: v * jax.nn.sigmoid(g)
    pad_a[0:HALO, :] = jnp.where(first, 0.0, glu(avp[...], agp[...]))
    pad_a[HALO:HALO + CT, :] = glu(av[...], ag[...])
    pad_a[HALO + CT:2 * HALO + CT, :] = jnp.where(last, 0.0, glu(avn[...], agn[...]))
    acc = jnp.zeros((CT, GW), f32)
    for k in range(CONV_K):
        o = HALO - CONV_K // 2 + k
        acc = acc + pad_a[o:o + CT, :] * cw[k:k + 1, :]
    a = _ln(acc + cb[...], lng[...], lnb[...])
    a_out[...] = _dot(_silu(a), pw[...]).astype(bf16)

    pad_s[0:SHALO, :] = jnp.where(first, 0.0, scp[...] * sxp[...])
    pad_s[SHALO:SHALO + CT, :] = sc[...] * sx[...]
    pad_s[SHALO + CT:2 * SHALO + CT, :] = jnp.where(last, 0.0, scn[...] * sxn[...])
    acc = jnp.zeros((CT, GW), f32)
    for k in range(3):
        o = SHALO - 1 + k
        acc = acc + pad_s[o:o + CT, :] * scw[k:k + 1, :]
    ysc_out[...] = (sb[...] * acc).astype(bf16)


def _conv_call(u, seq_len, p):
    n = u.shape[0]
    tps = seq_len // CT
    nt = n // CT
    main = lambda c: pl.BlockSpec((CT, GW), lambda i: (i, c))

    def halo(c, rows, nxt):
        per = CT // rows
        if nxt:
            return pl.BlockSpec((rows, GW), lambda i: (jnp.minimum((i + 1) * per, nt * per - 1), c))
        return pl.BlockSpec((rows, GW), lambda i: (jnp.maximum(i * per - 1, 0), c))

    full = lambda a: pl.BlockSpec(a.shape, lambda i: (0,) * a.ndim)
    consts = (p['conv_w'], p['conv_b'], p['conv_ln_g'], p['conv_ln_b'], p['conv_pw'], p['sc_conv_w'])
    in_specs = [main(C_AVAL), main(C_AGATE), halo(C_AVAL, HALO, 0), halo(C_AGATE, HALO, 0),
                halo(C_AVAL, HALO, 1), halo(C_AGATE, HALO, 1),
                main(C_SB), main(C_SC), main(C_SX), halo(C_SC, SHALO, 0), halo(C_SX, SHALO, 0),
                halo(C_SC, SHALO, 1), halo(C_SX, SHALO, 1)] + [full(a) for a in consts]
    return pl.pallas_call(
        functools.partial(_conv_kernel, tiles_per_seq=tps), grid=(nt,),
        in_specs=in_specs,
        out_specs=[pl.BlockSpec((CT, GW), lambda i: (i, 0))] * 2,
        out_shape=[jax.ShapeDtypeStruct((n, GW), bf16)] * 2,
        scratch_shapes=[pltpu.VMEM((CT + 2 * HALO, GW), f32), pltpu.VMEM((CT + 2 * SHALO, GW), f32)],
        compiler_params=_cparams(("arbitrary",)), name="conv_mixers",
    )(*([u] * 13), *consts)


GLA_LOCK = 4


def _gla_kernel(q_ref, k_ref, v_ref, g_ref, sm_ref, s0_ref, wlr_ref, blr_ref, ng_ref,
                mbd_ref, mbdf_ref, tril_ref, triu_ref, y_ref, sfin_ref, st_scr, of_scr, *, tt, ntiles):
    p = pl.program_id(1)
    nch = tt // CH
    rowi, coli = _head_iotas()

    @pl.when(p == 0)
    def _():
        st_scr[...] = s0_ref[0]

    @pl.when(p == ntiles)
    def _():
        sfin_ref[0] = st_scr[...]
        st_scr[...] = s0_ref[1]

    def run(dirn):
        tri = tril_ref[...] if dirn == 0 else triu_ref[...]
        keep = (rowi >= coli) if dirn == 0 else (rowi <= coli)
        wlr = wlr_ref[dirn]
        blr = blr_ref[dirn]
        tile = p if dirn == 0 else 2 * ntiles - 1 - p

        def body(it, carry):
            cis = [it * GLA_LOCK + c for c in range(GLA_LOCK)]
            if dirn == 1:
                cis = [nch - 1 - ci for ci in cis]
            rows = [pl.ds(pl.multiple_of(ci * CH, CH), CH) for ci in cis]
            grows = [pl.ds(pl.multiple_of(tile * tt + ci * CH, CH), CH) for ci in cis]
            mbd = mbd_ref[...]
            zs = [jnp.dot(sm_ref[r, :].astype(bf16), wlr, preferred_element_type=f32) + blr for r in rows]
            cums = [_dot01_left(tri, jax.nn.log_sigmoid(z) * (1.0 / GLA_TAU)) for z in zs]
            tots = [cum[CH - 1:CH, :] if dirn == 0 else cum[0:1, :] for cum in cums]
            vs = [v_ref[r, :] for r in rows]
            qes = [q_ref[r, :] * (DH ** -0.5) * jnp.exp(cum) for r, cum in zip(rows, cums)]
            kes = [k_ref[r, :] * jnp.exp(-cum) for r, cum in zip(rows, cums)]
            kds = [k_ref[r, :] * jnp.exp(tot - cum) for r, cum, tot in zip(rows, cums, tots)]
            atts = [jnp.where(keep, _dot_nt(qe, _bd(ke, mbd)), 0.0) for qe, ke in zip(qes, kes)]
            ols = [_dot(att, _bd(v, mbd)) for att, v in zip(atts, vs)]
            upds = [_dot_tn(v, kd) for v, kd in zip(vs, kds)]
            st = st_scr[...]
            for c in range(GLA_LOCK):
                o = ols[c] + _dot_nt(qes[c], st)
                st = st * jnp.exp(tots[c]) + mbdf_ref[...] * upds[c]
                if dirn == 0:
                    of_scr[grows[c], :] = o
                else:
                    o = of_scr[grows[c], :] + o
                    ms = _dot(o * o, mbd) * (1.0 / DH)
                    y = o * lax.rsqrt(ms + EPS) * ng_ref[...] * _silu(g_ref[rows[c], :])
                    y_ref[rows[c], :] = y.astype(bf16)
            st_scr[...] = st
            return carry

        lax.fori_loop(0, nch // GLA_LOCK, body, 0)

    @pl.when(p < ntiles)
    def _():
        run(0)

    @pl.when(p >= ntiles)
    def _():
        run(1)

    @pl.when(p == 2 * ntiles - 1)
    def _():
        sfin_ref[1] = st_scr[...]


def _gla_call(u, nseq, seq_len, s0, p, cst):
    tt = min(seq_len, 512)
    ntiles = seq_len // tt
    tmap = lambda q: jnp.where(q < ntiles, q, 2 * ntiles - 1 - q)
    main = lambda c: pl.BlockSpec((tt, GW), lambda s, q: (s * ntiles + tmap(q), c))
    full = lambda a: pl.BlockSpec(a.shape, lambda s, q: (0,) * a.ndim)
    consts = (p['gla_wlr'], p['gla_blr'], p['gla_norm_g'], cst['mbd'], cst['mbdf'], cst['tril'], cst['triu'])
    return pl.pallas_call(
        functools.partial(_gla_kernel, tt=tt, ntiles=ntiles), grid=(nseq, 2 * ntiles),
        in_specs=[main(C_GQ), main(C_GK), main(C_GV), main(C_GG),
                  pl.BlockSpec((tt, LANE), lambda s, q: (s * ntiles + tmap(q), SMALL_BLK)),
                  pl.BlockSpec((None, 2, GW, GW), lambda s, q: (s, 0, 0, 0))] + [full(a) for a in consts],
        out_specs=[pl.BlockSpec((tt, GW), lambda s, q: (s * ntiles + jnp.where(q < ntiles, ntiles - 1, 2 * ntiles - 1 - q), 0)),
                   pl.BlockSpec((None, 2, GW, GW), lambda s, q: (s, 0, 0, 0))],
        out_shape=[jax.ShapeDtypeStruct((nseq * seq_len, GW), bf16),
                   jax.ShapeDtypeStruct((nseq, 2, GW, GW), f32)],
        scratch_shapes=[pltpu.VMEM((GW, GW), f32), pltpu.VMEM((seq_len, GW), f32)],
        compiler_params=_cparams(("arbitrary", "arbitrary")), name="gla",
    )(u, u, u, u, u, s0, *consts)


GDN_G = 8
HR = 8
NLOCK = 2


def _gdn_prep_kernel(*refs, g, ngroups, colmajor):
    if ngroups > 1:
        (q_ref, k_ref, v_ref, sm_ref, qp, kp, vp, qn, kn, vnx, cw_ref, alog_ref, dtb_ref, e_ref, mbd_ref,
         tril_ref, triu_ref, u_out, wq_out, qkd_out, kd_out, egl_out) = refs
    else:
        (q_ref, k_ref, v_ref, sm_ref, cw_ref, alog_ref, dtb_ref, e_ref, mbd_ref,
         tril_ref, triu_ref, u_out, wq_out, qkd_out, kd_out, egl_out) = refs
        qp = kp = vp = qn = kn = vnx = None
    gi = pl.program_id(1)
    rowi, coli = _head_iotas()
    row_first = rowi == 0
    row_last = rowi == CH - 1
    icat = jnp.where(rowi == coli, 1.0, 0.0)

    if colmajor:
        get = lambda ref, j: ref[:, j, :]
        last_row = lambda ref, j: ref[CH - 1, pl.ds(j, 1), :]
        first_row = lambda ref, j: ref[0, pl.ds(j, 1), :]
    else:
        get = lambda ref, j: ref[j]
        last_row = lambda ref, j: ref[j, CH - 1:CH, :]
        first_row = lambda ref, j: ref[j, 0:1, :]

    def shared(jj):
        jm = jnp.maximum(jj - 1, 0)
        jp = jnp.minimum(jj + 1, g - 1)

        def conv(ref, pref, nref, c0):
            x = get(ref, jj)
            if pref is None:
                pr_out = 0.0
                nx_out = 0.0
            else:
                pr_out = jnp.where(gi > 0, pref[HR - 1, HR - 1:HR, :], 0.0)
                nx_out = jnp.where(gi < ngroups - 1, nref[0, 0:1, :], 0.0)
            pr = jnp.where(jj > 0, last_row(ref, jm), pr_out)
            nx = jnp.where(jj < g - 1, first_row(ref, jp), nx_out)
            xd = jnp.where(row_first, pr, pltpu.roll(x, 1, 0))
            xu = jnp.where(row_last, nx, pltpu.roll(x, CH - 1, 0))
            return _silu(xd * cw_ref[0:1, c0:c0 + GW] + x * cw_ref[1:2, c0:c0 + GW] + xu * cw_ref[2:3, c0:c0 + GW])

        mbd = mbd_ref[...]
        q = conv(q_ref, qp, qn, 0)
        k = conv(k_ref, kp, kn, GW)
        v = conv(v_ref, vp, vnx, 2 * GW)
        q = q * lax.rsqrt(_dot(q * q, mbd) + 1e-6) * (DH ** -0.5)
        k = k * lax.rsqrt(_dot(k * k, mbd) + 1e-6)
        ex = _dot01_right(get(sm_ref, jj), e_ref[...])
        beta = jax.nn.sigmoid(ex[:, :GW])
        kb = k * beta
        gm = _dot_nt(jnp.concatenate([kb, q], axis=0), _bd(k, mbd))
        return q, k, kb, v * beta, ex, gm

    def gates(sh, dirn):
        q, k, kb, vb, ex, gm = sh
        la = -jnp.exp(alog_ref[dirn]) * jax.nn.softplus(ex[:, (1 + dirn) * GW:(2 + dirn) * GW] + dtb_ref[dirn])
        if dirn == 0:
            tri, keep, strict, keep_t = tril_ref[...], rowi >= coli, rowi > coli, rowi <= coli
        else:
            tri, keep, strict, keep_t = triu_ref[...], rowi <= coli, rowi < coli, rowi >= coli
        gx = _dot01_left(tri, la)
        grow = jnp.sum(jnp.where(keep_t, la, 0.0), axis=0, keepdims=True)
        decay = jnp.where(keep, jnp.exp(jnp.where(keep, gx - grow, 0.0)), 0.0)
        glast = gx[CH - 1:CH, :] if dirn == 0 else gx[0:1, :]
        nm = jnp.where(strict, -(gm[:CH] * decay), 0.0)
        return gx, glast, nm, gm[CH:] * decay

    def body(it, carry):
        jjs = [it * NLOCK + c for c in range(NLOCK)]
        shs = [shared(jj) for jj in jjs]
        chains = [(c, dirn) for c in range(NLOCK) for dirn in range(2)]
        gs = [gates(shs[c], dirn) for c, dirn in chains]
        mbd = mbd_ref[...]
        nms = [gt[2] for gt in gs]
        pws = [_dot3(nm, nm, mbd) for nm in nms]
        tinvs = [icat + nm for nm in nms]
        for _ in range(4):
            rs = [_dot3(jnp.concatenate([ti, pw], axis=0), pw, mbd) for ti, pw in zip(tinvs, pws)]
            tinvs = [ti + r[:CH] for ti, r in zip(tinvs, rs)]
            pws = [r[CH:] for r in rs]
        tinvs = [ti + _dot3(ti, pw, mbd) for ti, pw in zip(tinvs, pws)]
        egs = [jnp.exp(gt[0]) for gt in gs]
        us = [_dot(ti, _bd(shs[c][3], mbd)) for ti, (c, dirn) in zip(tinvs, chains)]
        ws = [_dot(ti, _bd(shs[c][2] * eg, mbd)) for ti, eg, (c, dirn) in zip(tinvs, egs, chains)]
        for i, (c, dirn) in enumerate(chains):
            q, k = shs[c][0], shs[c][1]
            gx, glast, _, qkd = gs[i]
            jj = jjs[c]
            u_out[dirn, jj] = us[i]
            wq_out[dirn, jj] = jnp.concatenate([ws[i], q * egs[i]], axis=0).astype(bf16)
            qkd_out[dirn, jj] = qkd.astype(bf16)
            kd_out[dirn, jj] = (k * jnp.exp(glast - gx)).astype(bf16)
            egl_out[dirn, jj] = jnp.exp(glast)
        return carry

    lax.fori_loop(0, g // NLOCK, body, 0)


def _gdn_scan_kernel(uf, wqf, qkf, kdf, egf, ub, wqb, qkb, kdb, egb, s0_ref, mbd_ref, mbdf_ref,
                     of_out, ob_out, sfin_ref, s_scr, *, sg, cg, nt):
    t = pl.program_id(1)

    @pl.when(t == 0)
    def _():
        s_scr[...] = s0_ref[...]

    def body(c, carry):
        chains = [(sq, dirn) for sq in range(sg) for dirn in range(2)]
        src = lambda dirn: (uf, wqf, qkf, kdf, egf, of_out, c) if dirn == 0 else (ub, wqb, qkb, kdb, egb, ob_out, cg - 1 - c)
        mbd = mbd_ref[...]
        ss = [s_scr[sq, dirn] for sq, dirn in chains]
        wss = [jnp.dot(src(dirn)[1][sq, src(dirn)[6]], st.astype(bf16), preferred_element_type=f32)
               for st, (sq, dirn) in zip(ss, chains)]
        vns = [src(dirn)[0][sq, src(dirn)[6]] - ws[:CH] for ws, (sq, dirn) in zip(wss, chains)]
        os_ = [ws[CH:] + jnp.dot(src(dirn)[2][sq, src(dirn)[6]], _bd(vn, mbd), preferred_element_type=f32)
               for ws, vn, (sq, dirn) in zip(wss, vns, chains)]
        upd = [_dot_tn(src(dirn)[3][sq, src(dirn)[6]], vn) for vn, (sq, dirn) in zip(vns, chains)]
        for st, o, up, (sq, dirn) in zip(ss, os_, upd, chains):
            _, _, _, _, eg, o_out, cc = src(dirn)
            o_out[sq, cc] = o
            s_scr[sq, dirn] = st * eg[sq, cc] + mbdf_ref[...] * up
        return carry

    lax.fori_loop(0, cg, body, 0)

    @pl.when(t == nt - 1)
    def _():
        sfin_ref[...] = s_scr[...]


def _gdn_fin_kernel(of_ref, ob_ref, g_ref, ng_ref, mbd_ref, y_ref, *, g, colmajor):
    def body(jj, carry):
        o = of_ref[jj] + ob_ref[jj]
        ms = _dot(o * o, mbd_ref[...]) * (1.0 / DH)
        gate = g_ref[:, jj, :] if colmajor else g_ref[jj]
        y = o * lax.rsqrt(ms + EPS) * ng_ref[...] * _silu(gate)
        if colmajor:
            y_ref[:, jj, :] = y
        else:
            y_ref[jj] = y
        return carry

    lax.fori_loop(0, g, body, 0)


def _gdn_call(u, nseq, seq_len, s0, p, cst, colmajor):
    n = seq_len // CH
    if colmajor:
        assert seq_len == GRID_W * CH
        g, a = GDN_G, GRID_W
        blk = lambda w: (None, a, g, w)
        imap = lambda c: (lambda s, i: (s, 0, i, c))
    else:
        g, a = n, n
        blk = lambda w: (None, g, CH, w)
        imap = lambda c: (lambda s, i: (s, i, 0, c))
    ngroups = n // g
    u4 = u.reshape(nseq, a, CH, NU)
    main = lambda c: pl.BlockSpec(blk(GW), imap(c))
    full2 = lambda x: pl.BlockSpec(x.shape, lambda s, i: (0,) * x.ndim)
    halos, halo_specs = (), []
    if ngroups > 1:
        per = g // HR
        prev = lambda c: pl.BlockSpec((None, HR, HR, GW), lambda s, i: (s, a // HR - 1, jnp.maximum(i * per - 1, 0), c))
        nxt = lambda c: pl.BlockSpec((None, HR, HR, GW), lambda s, i: (s, 0, jnp.minimum((i + 1) * per, a // HR - 1), c))
        halo_specs = [prev(C_DQ), prev(C_DK), prev(C_DV), nxt(C_DQ), nxt(C_DK), nxt(C_DV)]
        halos = (u4,) * 6
    consts = (p['gdn_conv_w'], p['gdn_alog'], p['gdn_dtb'], cst['e_gdn'], cst['mbd'], cst['tril'], cst['triu'])
    per_chunk = lambda rows, dt: jax.ShapeDtypeStruct((nseq, 2, n, rows, GW), dt)
    per_chunk_spec = lambda rows: pl.BlockSpec((None, 2, g, rows, GW), lambda s, i: (s, 0, i, 0, 0))
    uu, wq, qkd, kd, egl = pl.pallas_call(
        functools.partial(_gdn_prep_kernel, g=g, ngroups=ngroups, colmajor=colmajor), grid=(nseq, ngroups),
        in_specs=[main(C_DQ), main(C_DK), main(C_DV), pl.BlockSpec(blk(LANE), imap(SMALL_BLK))] + halo_specs
                 + [full2(x) for x in consts],
        out_specs=[per_chunk_spec(CH), per_chunk_spec(2 * CH), per_chunk_spec(CH), per_chunk_spec(CH), per_chunk_spec(1)],
        out_shape=[per_chunk(CH, f32), per_chunk(2 * CH, bf16), per_chunk(CH, bf16), per_chunk(CH, bf16), per_chunk(1, f32)],
        compiler_params=_cparams(("arbitrary", "arbitrary")), name="gdn_prep",
    )(u4, u4, u4, u4, *halos, *consts)

    sg = math.gcd(nseq, 2 if colmajor else 4)
    cg = min(n, 8)
    nt = n // cg
    fwd = lambda rows: pl.BlockSpec((sg, None, cg, rows, GW), lambda s, t: (s, 0, t, 0, 0))
    bwd = lambda rows: pl.BlockSpec((sg, None, cg, rows, GW), lambda s, t: (s, 1, nt - 1 - t, 0, 0))
    st_spec = pl.BlockSpec((sg, 2, GW, GW), lambda s, t: (s, 0, 0, 0))
    o_shape = jax.ShapeDtypeStruct((nseq, n, CH, GW), f32)
    o_f, o_b, sfin = pl.pallas_call(
        functools.partial(_gdn_scan_kernel, sg=sg, cg=cg, nt=nt), grid=(nseq // sg, nt),
        in_specs=[fwd(CH), fwd(2 * CH), fwd(CH), fwd(CH), fwd(1), bwd(CH), bwd(2 * CH), bwd(CH), bwd(CH), bwd(1),
                  st_spec, full2(cst['mbd']), full2(cst['mbdf'])],
        out_specs=[pl.BlockSpec((sg, cg, CH, GW), lambda s, t: (s, t, 0, 0)),
                   pl.BlockSpec((sg, cg, CH, GW), lambda s, t: (s, nt - 1 - t, 0, 0)), st_spec],
        out_shape=[o_shape, o_shape, jax.ShapeDtypeStruct((nseq, 2, GW, GW), f32)],
        scratch_shapes=[pltpu.VMEM((sg, 2, GW, GW), f32)],
        compiler_params=_cparams(("arbitrary", "arbitrary")), name="gdn_scan",
    )(uu, wq, qkd, kd, egl, uu, wq, qkd, kd, egl, s0, cst['mbd'], cst['mbdf'])

    o_spec = pl.BlockSpec((None, g, CH, GW), lambda s, i: (s, i, 0, 0))
    y = pl.pallas_call(
        functools.partial(_gdn_fin_kernel, g=g, colmajor=colmajor), grid=(nseq, ngroups),
        in_specs=[o_spec, o_spec, main(C_DGATE), full2(p['gdn_norm_g']), full2(cst['mbd'])],
        out_specs=pl.BlockSpec(blk(GW), imap(0)),
        out_shape=jax.ShapeDtypeStruct((nseq, a, CH, GW), f32),
        compiler_params=_cparams(("arbitrary", "arbitrary")), name="gdn_fin",
    )(o_f, o_b, u4, p['gdn_norm_g'], cst['mbd'])
    return y.reshape(nseq * seq_len, GW), sfin


def _out_kernel(a_ref, gl_ref, gd_ref, sc_ref, x_ref, mod_ref, w_ref, lg_ref, lb_ref, o_ref):
    m = mod_ref[...]
    mix_in = jnp.concatenate([a_ref[...], gl_ref[...], gd_ref[...].astype(bf16), sc_ref[...]], axis=1)
    mix = jnp.dot(mix_in, w_ref[...], preferred_element_type=f32)
    o_ref[...] = _ln(ALPHA * x_ref[...] + m[:, 2 * D:3 * D] * mix, lg_ref[...], lb_ref[...])


def _out_call(a, gl, gd, sc, x, mod, p, tiles_per_mod, tm=512):
    n = x.shape[0]
    part = pl.BlockSpec((tm, GW), lambda i: (i, 0))
    full = lambda t: pl.BlockSpec(t.shape, lambda i: (0,) * t.ndim)
    return pl.pallas_call(
        _out_kernel, grid=(n // tm,),
        in_specs=[part, part, part, part, pl.BlockSpec((tm, D), lambda i: (i, 0)),
                  pl.BlockSpec((None, 1, 6 * D), lambda i: (i // tiles_per_mod, 0, 0)),
                  full(p['w_out']), full(p['ln1_g']), full(p['ln1_b'])],
        out_specs=pl.BlockSpec((tm, D), lambda i: (i, 0)),
        out_shape=jax.ShapeDtypeStruct((n, D), f32),
        compiler_params=_cparams(("arbitrary",)), name="out_proj_ln",
    )(a, gl, gd, sc, x, mod, p['w_out'], p['ln1_g'], p['ln1_b'])


def _ffn_kernel(*refs, moe, ngroups):
    if moe:
        x_ref, mod_ref, wg_ref, wu_ref, wd_ref, lg_ref, lb_ref, rt_ref, o_ref, h_scr, acc_scr, comb_scr = refs
    else:
        x_ref, mod_ref, wg_ref, wu_ref, wd_ref, lg_ref, lb_ref, o_ref, h_scr, acc_scr = refs
    g = pl.program_id(1)

    @pl.when(g == 0)
    def _():
        m = mod_ref[...]
        h = x_ref[...] * (1.0 + m[:, 4 * D:5 * D]) + m[:, 3 * D:4 * D]
        h_scr[...] = h.astype(bf16)
        acc_scr[...] = jnp.zeros(acc_scr.shape, f32)
        if moe:
            logits = jnp.dot(h, rt_ref[...], precision=lax.Precision.HIGHEST, preferred_element_type=f32)
            lane = lax.broadcasted_iota(jnp.int32, logits.shape, 1).astype(f32)
            neg = jnp.float32(-jnp.inf)
            logits = jnp.where(lane < N_EXPERTS, logits, neg)
            m1 = jnp.max(logits, axis=-1, keepdims=True)
            i1 = jnp.min(jnp.where(logits == m1, lane, float(LANE)), axis=-1, keepdims=True)
            rest = jnp.where(lane == i1, neg, logits)
            m2 = jnp.max(rest, axis=-1, keepdims=True)
            i2 = jnp.min(jnp.where(rest == m2, lane, float(LANE)), axis=-1, keepdims=True)
            e2 = jnp.exp(m2 - m1)
            p1 = 1.0 / (1.0 + e2)
            p2 = e2 / (1.0 + e2)
            comb_scr[...] = jnp.where(lane == i1, p1, 0.0) + jnp.where(lane == i2, p2, 0.0)

    hb = h_scr[...]
    t = _silu(jnp.dot(hb, wg_ref[...], preferred_element_type=f32)) * jnp.dot(hb, wu_ref[...], preferred_element_type=f32)
    out = jnp.dot(t.astype(bf16), wd_ref[...], preferred_element_type=f32)
    if moe:
        comb = comb_scr[...]
        lane = lax.broadcasted_iota(jnp.int32, comb.shape, 1)
        out = out * jnp.sum(jnp.where(lane == g, comb, 0.0), axis=-1, keepdims=True)
    acc_scr[...] += out

    @pl.when(g == ngroups - 1)
    def _():
        m = mod_ref[...]
        o_ref[...] = _ln(ALPHA * x_ref[...] + m[:, 5 * D:6 * D] * acc_scr[...], lg_ref[...], lb_ref[...])


def _ffn_call(x, mod, p, tiles_per_mod, moe, tm=512):
    n = x.shape[0]
    fw = D_FF_EXPERT
    full = lambda t: pl.BlockSpec(t.shape, lambda i, g: (0,) * t.ndim)
    if moe:
        ngroups = N_EXPERTS
        wspecs = [pl.BlockSpec((None, D, fw), lambda i, g: (g, 0, 0)),
                  pl.BlockSpec((None, D, fw), lambda i, g: (g, 0, 0)),
                  pl.BlockSpec((None, fw, D), lambda i, g: (g, 0, 0))]
        extra, extra_specs = (p['router'],), [full(p['router'])]
        scratch = [pltpu.VMEM((tm, LANE), f32)]
    else:
        ngroups = D_FF // fw
        wspecs = [pl.BlockSpec((D, fw), lambda i, g: (0, g)),
                  pl.BlockSpec((D, fw), lambda i, g: (0, g)),
                  pl.BlockSpec((fw, D), lambda i, g: (g, 0))]
        extra, extra_specs, scratch = (), [], []
    return pl.pallas_call(
        functools.partial(_ffn_kernel, moe=moe, ngroups=ngroups), grid=(n // tm, ngroups),
        in_specs=[pl.BlockSpec((tm, D), lambda i, g: (i, 0)),
                  pl.BlockSpec((None, 1, 6 * D), lambda i, g: (i // tiles_per_mod, 0, 0))] + wspecs
                 + [full(p['ln2_g']), full(p['ln2_b'])] + extra_specs,
        out_specs=pl.BlockSpec((tm, D), lambda i, g: (i, 0)),
        out_shape=jax.ShapeDtypeStruct((n, D), f32),
        scratch_shapes=[pltpu.VMEM((tm, D), bf16), pltpu.VMEM((tm, D), f32)] + scratch,
        compiler_params=_cparams(("arbitrary", "arbitrary")), name="moe_ffn" if moe else "dense_ffn",
    )(x, mod, p['w_gate'], p['w_up'], p['w_down'], p['ln2_g'], p['ln2_b'], *extra)


def _constants():
    hb = np.arange(GW) // DH
    mbd = (hb[:, None] == hb[None, :]).astype(np.float32)
    r = np.arange(CH)
    e = np.zeros((LANE, 3 * GW), np.float32)
    for h in range(NH):
        e[L_BETA + h, h * DH:(h + 1) * DH] = 1.0
        for d in range(2):
            e[L_DEC + NH * d + h, (1 + d) * GW + h * DH:(1 + d) * GW + (h + 1) * DH] = 1.0
    return dict(mbd=jnp.asarray(mbd, bf16), mbdf=jnp.asarray(mbd, f32),
                tril=jnp.asarray(r[:, None] >= r[None, :], bf16), triu=jnp.asarray(r[:, None] <= r[None, :], bf16),
                e_gdn=jnp.asarray(e, bf16))


_IN_SIZES = (GW, GW, GW, GW, GW, GW, 2 * GLA_LR, GW, GW, GW, GW, NH, 2 * NH, GW, GW, GW)


def _pack_w_in(w_in):
    offs = np.concatenate([[0], np.cumsum(_IN_SIZES)])
    big = [i for i, s in enumerate(_IN_SIZES) if s == GW]
    small = [i for i, s in enumerate(_IN_SIZES) if s != GW]
    idx = np.concatenate([np.arange(offs[i], offs[i + 1]) for i in big + small])
    w = jnp.take(w_in, jnp.asarray(idx, jnp.int32), axis=-1)
    pad = NU - idx.shape[0]
    return jnp.pad(w, ((0, 0), (0, 0), (0, pad))).astype(bf16)


def _to_bd(s, transpose):
    if transpose:
        s = jnp.swapaxes(s, -1, -2)
    b = s.shape[0]
    z = jnp.zeros_like(s)
    rows = [jnp.concatenate([s[:, :, h] if g == h else z[:, :, h] for g in range(NH)], axis=-1) for h in range(NH)]
    return jnp.concatenate(rows, axis=-2).reshape(b, 2, GW, GW)


def _from_bd(s, transpose):
    out = jnp.stack([s[:, :, h * DH:(h + 1) * DH, h * DH:(h + 1) * DH] for h in range(NH)], axis=2)
    return jnp.swapaxes(out, -1, -2) if transpose else out


def _stream_layer(x, mod, nseq, seq_len, s0_gla, s0_gdn, p, cst, colmajor, moe):
    tpm_512 = max(seq_len // 512, 1) if mod.shape[0] > 1 else x.shape[0] // 512
    u = _in_call(x, mod, p['w_in'], tpm_512)
    a, ysc = _conv_call(u, seq_len, p)
    ygla, sgla = _gla_call(u, nseq, seq_len, s0_gla, p, cst)
    ygdn, sgdn = _gdn_call(u, nseq, seq_len, s0_gdn, p, cst, colmajor)
    x1 = _out_call(a, ygla, ygdn, ysc, x, mod, p, tpm_512)
    x2 = _ffn_call(x1, mod, p['ffn'], tpm_512, moe)
    return x2, sgla, sgdn


def kernel(x_prompt, x_sample, c, state_gla, state_gdn, c_ctx, w_ada, b_ada, w_in, w_out, conv_w, conv_b, conv_ln_g, conv_ln_b, conv_pw, gla_w_lr, gla_b_lr, gla_norm_g, gdn_conv_w, gdn_a_log, gdn_dt_bias, gdn_norm_g, sc_conv_w, ln1_g, ln1_b, ln2_g, ln2_b, ffn_w_gate, ffn_w_up, ffn_w_down, moe_router, moe_w_gate, moe_w_up, moe_w_down):
    nb, seq, _ = x_prompt.shape
    db, dseq, _ = x_sample.shape
    depth = w_in.shape[0]
    cst = _constants()

    rows = 16
    cond = jnp.concatenate([c_ctx[None, :], c, jnp.zeros((rows - 1 - db, D), f32)], axis=0)
    mod = _mod_call(cond, w_ada, b_ada)

    w_in_p = _pack_w_in(w_in)
    w_out_b = w_out.astype(bf16)
    conv_pw_b = conv_pw.astype(bf16)
    wlr = jnp.zeros((depth, 2, LANE, GW), f32)
    wlr = wlr.at[:, 0, 0:GLA_LR].set(gla_w_lr[:, 0]).at[:, 1, GLA_LR:2 * GLA_LR].set(gla_w_lr[:, 1]).astype(bf16)
    rep = lambda t: jnp.repeat(t, DH, axis=-1)[:, :, None, :]
    row = lambda t: t[:, None, :]
    ffn_g, ffn_u, ffn_d = ffn_w_gate.astype(bf16), ffn_w_up.astype(bf16), ffn_w_down.astype(bf16)
    moe_g, moe_u, moe_d = moe_w_gate.astype(bf16), moe_w_up.astype(bf16), moe_w_down.astype(bf16)
    router = jnp.pad(moe_router, ((0, 0), (0, 0), (0, LANE - N_EXPERTS)))

    zero_bd = jnp.zeros((nb, 2, GW, GW), f32)
    y_p = x_prompt.reshape(nb * seq, D)
    y_s = x_sample.reshape(db * dseq, D)
    gla_states, gdn_states = [], []
    for l in range(depth):
        j = l // 2
        moe = l % 2 == 1
        if moe:
            ffn = dict(w_gate=moe_g[j], w_up=moe_u[j], w_down=moe_d[j], router=router[j],
                       ln2_g=row(ln2_g)[l], ln2_b=row(ln2_b)[l])
        else:
            ffn = dict(w_gate=ffn_g[j], w_up=ffn_u[j], w_down=ffn_d[j], ln2_g=row(ln2_g)[l], ln2_b=row(ln2_b)[l])
        p = dict(w_in=w_in_p[l], w_out=w_out_b[l], conv_w=conv_w[l], conv_b=row(conv_b)[l],
                 conv_ln_g=row(conv_ln_g)[l], conv_ln_b=row(conv_ln_b)[l], conv_pw=conv_pw_b[l],
                 sc_conv_w=sc_conv_w[l], gla_wlr=wlr[l], gla_blr=gla_b_lr[l][:, None, :],
                 gla_norm_g=row(gla_norm_g)[l], gdn_conv_w=gdn_conv_w[l], gdn_alog=rep(gdn_a_log)[l],
                 gdn_dtb=rep(gdn_dt_bias)[l], gdn_norm_g=row(gdn_norm_g)[l],
                 ln1_g=row(ln1_g)[l], ln1_b=row(ln1_b)[l], ffn=ffn)
        mod_ctx = mod[l, 0:1][:, None, :]
        mod_lat = mod[l, 1:1 + db][:, None, :]
        y_p, s_gla, s_gdn = _stream_layer(y_p, mod_ctx, nb, seq, zero_bd, zero_bd, p, cst, False, moe)
        gla_states.append(_from_bd(s_gla, True))
        gdn_states.append(_from_bd(s_gdn, False))
        y_s, _, _ = _stream_layer(y_s, mod_lat, db, dseq, _to_bd(state_gla[:, l], True),
                                  _to_bd(state_gdn[:, l], False), p, cst, True, moe)
    return (y_p.reshape(nb, seq, D), y_s.reshape(db, dseq, D),
            jnp.stack(gla_states, axis=1), jnp.stack(gdn_states, axis=1))
```

```python
import functools
import math

import numpy as np
import jax
import jax.numpy as jnp
from jax import lax
from jax.experimental import pallas as pl
from jax.experimental.pallas import tpu as pltpu

f32 = jnp.float32
bf16 = jnp.bfloat16

D = 1024
GW = 256
NH = 4
DH = 64
CH = 64
GRID_W = 64
CONV_K = 31
GLA_LR = 16
GLA_TAU = 16.0
DEPTH = 4
N_EXPERTS = 8
D_FF = 2816
D_FF_EXPERT = 1408
ALPHA = (2 * DEPTH) ** 0.25
EPS = 1e-5
NU = 13 * GW + 128
SMALL_BLK = 13 * GW // 128
LANE = 128
C_AVAL, C_AGATE, C_GQ, C_GK, C_GV, C_GG, C_DQ, C_DK, C_DV, C_DGATE, C_SB, C_SC, C_SX = range(13)
L_BETA = 2 * GLA_LR
L_DEC = L_BETA + NH

VMEM_LIMIT = 56 * 1024 * 1024


def _cparams(sem):
    return pltpu.CompilerParams(dimension_semantics=sem, vmem_limit_bytes=VMEM_LIMIT)


def _dot(a, b):
    return jnp.dot(a.astype(bf16), b.astype(bf16), preferred_element_type=f32)


def _dot_nt(a, b):
    return lax.dot_general(a.astype(bf16), b.astype(bf16), (((1,), (1,)), ((), ())),
                           preferred_element_type=f32)


def _dot_tn(a, b):
    return lax.dot_general(a.astype(bf16), b.astype(bf16), (((0,), (0,)), ((), ())),
                           preferred_element_type=f32)


def _split3(x):
    hi = x.astype(bf16)
    r = x - hi.astype(f32)
    mid = r.astype(bf16)
    lo = (r - mid.astype(f32)).astype(bf16)
    return hi, mid, lo


def _dot01_left(m01, x):
    hi, mid, lo = _split3(x)
    d = lambda t: jnp.dot(m01, t, preferred_element_type=f32)
    return d(hi) + d(mid) + d(lo)


def _dot01_right(x, m01):
    hi, mid, lo = _split3(x)
    d = lambda t: jnp.dot(t, m01, preferred_element_type=f32)
    return d(hi) + d(mid) + d(lo)


def _bd(x, mask):
    xb = x.astype(bf16)
    return jnp.concatenate([xb, xb, xb, xb], axis=0) * mask


def _dot3(a, b, mask):
    ah = a.astype(bf16)
    al = (a - ah.astype(f32)).astype(bf16)
    bh = b.astype(bf16)
    bdh = _bd(bh, mask)
    bdl = _bd(b - bh.astype(f32), mask)
    d = lambda x, y: jnp.dot(x, y, preferred_element_type=f32)
    return d(ah, bdh) + d(al, bdh) + d(ah, bdl)


def _silu(x):
    return x * jax.nn.sigmoid(x)


def _ln(x, g, b):
    xc = x - jnp.mean(x, axis=-1, keepdims=True)
    var = jnp.mean(xc * xc, axis=-1, keepdims=True)
    return xc * lax.rsqrt(var + EPS) * g + b


def _head_iotas():
    rowi = lax.broadcasted_iota(jnp.int32, (CH, GW), 0)
    coli = lax.broadcasted_iota(jnp.int32, (CH, GW), 1) & (DH - 1)
    return rowi, coli


def _mod_kernel(c_ref, w_ref, b_ref, o_ref):
    o_ref[...] = jnp.dot(_silu(c_ref[...]), w_ref[...], precision=lax.Precision.HIGHEST,
                         preferred_element_type=f32) + b_ref[...]


def _mod_call(cond, w_ada, b_ada):
    nl = w_ada.shape[0]
    rows = cond.shape[0]
    return pl.pallas_call(
        _mod_kernel, grid=(nl, 6),
        in_specs=[pl.BlockSpec((rows, D), lambda l, n: (0, 0)),
                  pl.BlockSpec((None, D, D), lambda l, n: (l, 0, n)),
                  pl.BlockSpec((None, 1, D), lambda l, n: (l, 0, n))],
        out_specs=pl.BlockSpec((None, rows, D), lambda l, n: (l, 0, n)),
        out_shape=jax.ShapeDtypeStruct((nl, rows, 6 * D), f32),
        compiler_params=_cparams(("arbitrary", "arbitrary")), name="adaln_mod",
    )(cond, w_ada, b_ada.reshape(nl, 1, 6 * D))


def _in_kernel(x_ref, mod_ref, w_ref, u_ref):
    m = mod_ref[...]
    h = x_ref[...] * (1.0 + m[:, D:2 * D]) + m[:, 0:D]
    u_ref[...] = jnp.dot(h.astype(bf16), w_ref[...], preferred_element_type=f32)


def _in_call(x, mod, w, tiles_per_mod, tm=512):
    n = x.shape[0]
    return pl.pallas_call(
        _in_kernel, grid=(n // tm,),
        in_specs=[pl.BlockSpec((tm, D), lambda i: (i, 0)),
                  pl.BlockSpec((None, 1, 6 * D), lambda i: (i // tiles_per_mod, 0, 0)),
                  pl.BlockSpec((D, NU), lambda i: (0, 0))],
        out_specs=pl.BlockSpec((tm, NU), lambda i: (i, 0)),
        out_shape=jax.ShapeDtypeStruct((n, NU), f32),
        compiler_params=_cparams(("arbitrary",)), name="in_proj",
    )(x, mod, w)


CT = 256
HALO = 16
SHALO = 8


def _conv_kernel(av, ag, avp, agp, avn, agn, sb, sc, sx, scp, sxp, scn, sxn,
                 cw, cb, lng, lnb, pw, scw, a_out, ysc_out, pad_a, pad_s, *, tiles_per_seq):
    i = pl.program_id(0)
    j = i % tiles_per_seq
    first = j == 0
    last = j == tiles_per_seq - 1
    glu = lambda v, g: v * jax.nn.sigmoid(g)
    pad_a[0:HALO, :] = jnp.where(first, 0.0, glu(avp[...], agp[...]))
    pad_a[HALO:HALO + CT, :] = glu(av[...], ag[...])
    pad_a[HALO + CT:2 * HALO + CT, :] = jnp.where(last, 0.0, glu(avn[...], agn[...]))
    acc = jnp.zeros((CT, GW), f32)
    for k in range(CONV_K):
        o = HALO - CONV_K // 2 + k
        acc = acc + pad_a[o:o + CT, :] * cw[k:k + 1, :]
    a = _ln(acc + cb[...], lng[...], lnb[...])
    a_out[...] = _dot(_silu(a), pw[...]).astype(bf16)

    pad_s[0:SHALO, :] = jnp.where(first, 0.0, scp[...] * sxp[...])
    pad_s[SHALO:SHALO + CT, :] = sc[...] * sx[...]
    pad_s[SHALO + CT:2 * SHALO + CT, :] = jnp.where(last, 0.0, scn[...] * sxn[...])
    acc = jnp.zeros((CT, GW), f32)
    for k in range(3):
        o = SHALO - 1 + k
        acc = acc + pad_s[o:o + CT, :] * scw[k:k + 1, :]
    ysc_out[...] = (sb[...] * acc).astype(bf16)


def _conv_call(u, seq_len, p):
    n = u.shape[0]
    tps = seq_len // CT
    nt = n // CT
    main = lambda c: pl.BlockSpec((CT, GW), lambda i: (i, c))

    def halo(c, rows, nxt):
        per = CT // rows
        if nxt:
            return pl.BlockSpec((rows, GW), lambda i: (jnp.minimum((i + 1) * per, nt * per - 1), c))
        return pl.BlockSpec((rows, GW), lambda i: (jnp.maximum(i * per - 1, 0), c))

    full = lambda a: pl.BlockSpec(a.shape, lambda i: (0,) * a.ndim)
    consts = (p['conv_w'], p['conv_b'], p['conv_ln_g'], p['conv_ln_b'], p['conv_pw'], p['sc_conv_w'])
    in_specs = [main(C_AVAL), main(C_AGATE), halo(C_AVAL, HALO, 0), halo(C_AGATE, HALO, 0),
                halo(C_AVAL, HALO, 1), halo(C_AGATE, HALO, 1),
                main(C_SB), main(C_SC), main(C_SX), halo(C_SC, SHALO, 0), halo(C_SX, SHALO, 0),
                halo(C_SC, SHALO, 1), halo(C_SX, SHALO, 1)] + [full(a) for a in consts]
    return pl.pallas_call(
        functools.partial(_conv_kernel, tiles_per_seq=tps), grid=(nt,),
        in_specs=in_specs,
        out_specs=[pl.BlockSpec((CT, GW), lambda i: (i, 0))] * 2,
        out_shape=[jax.ShapeDtypeStruct((n, GW), bf16)] * 2,
        scratch_shapes=[pltpu.VMEM((CT + 2 * HALO, GW), f32), pltpu.VMEM((CT + 2 * SHALO, GW), f32)],
        compiler_params=_cparams(("arbitrary",)), name="conv_mixers",
    )(*([u] * 13), *consts)


GLA_LOCK = 4


def _gla_kernel(q_ref, k_ref, v_ref, g_ref, sm_ref, s0_ref, wlr_ref, blr_ref, ng_ref,
                mbd_ref, mbdf_ref, tril_ref, triu_ref, y_ref, sfin_ref, st_scr, of_scr, *, tt, ntiles):
    p = pl.program_id(1)
    nch = tt // CH
    rowi, coli = _head_iotas()

    @pl.when(p == 0)
    def _():
        st_scr[...] = s0_ref[0]

    @pl.when(p == ntiles)
    def _():
        sfin_ref[0] = st_scr[...]
        st_scr[...] = s0_ref[1]

    def run(dirn):
        tri = tril_ref[...] if dirn == 0 else triu_ref[...]
        keep = (rowi >= coli) if dirn == 0 else (rowi <= coli)
        wlr = wlr_ref[dirn]
        blr = blr_ref[dirn]
        tile = p if dirn == 0 else 2 * ntiles - 1 - p

        def body(it, carry):
            cis = [it * GLA_LOCK + c for c in range(GLA_LOCK)]
            if dirn == 1:
                cis = [nch - 1 - ci for ci in cis]
            rows = [pl.ds(pl.multiple_of(ci * CH, CH), CH) for ci in cis]
            grows = [pl.ds(pl.multiple_of(tile * tt + ci * CH, CH), CH) for ci in cis]
            mbd = mbd_ref[...]
            zs = [jnp.dot(sm_ref[r, :].astype(bf16), wlr, preferred_element_type=f32) + blr for r in rows]
            cums = [_dot01_left(tri, jax.nn.log_sigmoid(z) * (1.0 / GLA_TAU)) for z in zs]
            tots = [cum[CH - 1:CH, :] if dirn == 0 else cum[0:1, :] for cum in cums]
            vs = [v_ref[r, :] for r in rows]
            qes = [q_ref[r, :] * (DH ** -0.5) * jnp.exp(cum) for r, cum in zip(rows, cums)]
            kes = [k_ref[r, :] * jnp.exp(-cum) for r, cum in zip(rows, cums)]
            kds = [k_ref[r, :] * jnp.exp(tot - cum) for r, cum, tot in zip(rows, cums, tots)]
            atts = [jnp.where(keep, _dot_nt(qe, _bd(ke, mbd)), 0.0) for qe, ke in zip(qes, kes)]
            ols = [_dot(att, _bd(v, mbd)) for att, v in zip(atts, vs)]
            upds = [_dot_tn(v, kd) for v, kd in zip(vs, kds)]
            st = st_scr[...]
            for c in range(GLA_LOCK):
                o = ols[c] + _dot_nt(qes[c], st)
                st = st * jnp.exp(tots[c]) + mbdf_ref[...] * upds[c]
                if dirn == 0:
                    of_scr[grows[c], :] = o
                else:
                    o = of_scr[grows[c], :] + o
                    ms = _dot(o * o, mbd) * (1.0 / DH)
                    y = o * lax.rsqrt(ms + EPS) * ng_ref[...] * _silu(g_ref[rows[c], :])
                    y_ref[rows[c], :] = y.astype(bf16)
            st_scr[...] = st
            return carry

        lax.fori_loop(0, nch // GLA_LOCK, body, 0)

    @pl.when(p < ntiles)
    def _():
        run(0)

    @pl.when(p >= ntiles)
    def _():
        run(1)

    @pl.when(p == 2 * ntiles - 1)
    def _():
        sfin_ref[1] = st_scr[...]


def _gla_call(u, nseq, seq_len, s0, p, cst):
    tt = min(seq_len, 512)
    ntiles = seq_len // tt
    tmap = lambda q: jnp.where(q < ntiles, q, 2 * ntiles - 1 - q)
    main = lambda c: pl.BlockSpec((tt, GW), lambda s, q: (s * ntiles + tmap(q), c))
    full = lambda a: pl.BlockSpec(a.shape, lambda s, q: (0,) * a.ndim)
    consts = (p['gla_wlr'], p['gla_blr'], p['gla_norm_g'], cst['mbd'], cst['mbdf'], cst['tril'], cst['triu'])
    return pl.pallas_call(
        functools.partial(_gla_kernel, tt=tt, ntiles=ntiles), grid=(nseq, 2 * ntiles),
        in_specs=[main(C_GQ), main(C_GK), main(C_GV), main(C_GG),
                  pl.BlockSpec((tt, LANE), lambda s, q: (s * ntiles + tmap(q), SMALL_BLK)),
                  pl.BlockSpec((None, 2, GW, GW), lambda s, q: (s, 0, 0, 0))] + [full(a) for a in consts],
        out_specs=[pl.BlockSpec((tt, GW), lambda s, q: (s * ntiles + jnp.where(q < ntiles, ntiles - 1, 2 * ntiles - 1 - q), 0)),
                   pl.BlockSpec((None, 2, GW, GW), lambda s, q: (s, 0, 0, 0))],
        out_shape=[jax.ShapeDtypeStruct((nseq * seq_len, GW), bf16),
                   jax.ShapeDtypeStruct((nseq, 2, GW, GW), f32)],
        scratch_shapes=[pltpu.VMEM((GW, GW), f32), pltpu.VMEM((seq_len, GW), f32)],
        compiler_params=_cparams(("arbitrary", "arbitrary")), name="gla",
    )(u, u, u, u, u, s0, *consts)


GDN_G = 8
HR = 8
NLOCK = 2


def _gdn_prep_kernel(*refs, g, ngroups, colmajor):
    if ngroups > 1:
        (q_ref, k_ref, v_ref, sm_ref, qp, kp, vp, qn, kn, vnx, cw_ref, alog_ref, dtb_ref, e_ref, mbd_ref,
         tril_ref, triu_ref, u_out, wq_out, qkd_out, kd_out, egl_out) = refs
    else:
        (q_ref, k_ref, v_ref, sm_ref, cw_ref, alog_ref, dtb_ref, e_ref, mbd_ref,
         tril_ref, triu_ref, u_out, wq_out, qkd_out, kd_out, egl_out) = refs
        qp = kp = vp = qn = kn = vnx = None
    gi = pl.program_id(1)
    rowi, coli = _head_iotas()
    row_first = rowi == 0
    row_last = rowi == CH - 1
    icat = jnp.where(rowi == coli, 1.0, 0.0)

    if colmajor:
        get = lambda ref, j: ref[:, j, :]
        last_row = lambda ref, j: ref[CH - 1, pl.ds(j, 1), :]
        first_row = lambda ref, j: ref[0, pl.ds(j, 1), :]
    else:
        get = lambda ref, j: ref[j]
        last_row = lambda ref, j: ref[j, CH - 1:CH, :]
        first_row = lambda ref, j: ref[j, 0:1, :]

    def shared(jj):
        jm = jnp.maximum(jj - 1, 0)
        jp = jnp.minimum(jj + 1, g - 1)

        def conv(ref, pref, nref, c0):
            x = get(ref, jj)
            if pref is None:
                pr_out = 0.0
                nx_out = 0.0
            else:
                pr_out = jnp.where(gi > 0, pref[HR - 1, HR - 1:HR, :], 0.0)
                nx_out = jnp.where(gi < ngroups - 1, nref[0, 0:1, :], 0.0)
            pr = jnp.where(jj > 0, last_row(ref, jm), pr_out)
            nx = jnp.where(jj < g - 1, first_row(ref, jp), nx_out)
            xd = jnp.where(row_first, pr, pltpu.roll(x, 1, 0))
            xu = jnp.where(row_last, nx, pltpu.roll(x, CH - 1, 0))
            return _silu(xd * cw_ref[0:1, c0:c0 + GW] + x * cw_ref[1:2, c0:c0 + GW] + xu * cw_ref[2:3, c0:c0 + GW])

        mbd = mbd_ref[...]
        q = conv(q_ref, qp, qn, 0)
        k = conv(k_ref, kp, kn, GW)
        v = conv(v_ref, vp, vnx, 2 * GW)
        q = q * lax.rsqrt(_dot(q * q, mbd) + 1e-6) * (DH ** -0.5)
        k = k * lax.rsqrt(_dot(k * k, mbd) + 1e-6)
        ex = _dot01_right(get(sm_ref, jj), e_ref[...])
        beta = jax.nn.sigmoid(ex[:, :GW])
        kb = k * beta
        gm = _dot_nt(jnp.concatenate([kb, q], axis=0), _bd(k, mbd))
        return q, k, kb, v * beta, ex, gm

    def gates(sh, dirn):
        q, k, kb, vb, ex, gm = sh
        la = -jnp.exp(alog_ref[dirn]) * jax.nn.softplus(ex[:, (1 + dirn) * GW:(2 + dirn) * GW] + dtb_ref[dirn])
        if dirn == 0:
            tri, keep, strict, keep_t = tril_ref[...], rowi >= coli, rowi > coli, rowi <= coli
        else:
            tri, keep, strict, keep_t = triu_ref[...], rowi <= coli, rowi < coli, rowi >= coli
        gx = _dot01_left(tri, la)
        grow = jnp.sum(jnp.where(keep_t, la, 0.0), axis=0, keepdims=True)
        decay = jnp.where(keep, jnp.exp(jnp.where(keep, gx - grow, 0.0)), 0.0)
        glast = gx[CH - 1:CH, :] if dirn == 0 else gx[0:1, :]
        nm = jnp.where(strict, -(gm[:CH] * decay), 0.0)
        return gx, glast, nm, gm[CH:] * decay

    def body(it, carry):
        jjs = [it * NLOCK + c for c in range(NLOCK)]
        shs = [shared(jj) for jj in jjs]
        chains = [(c, dirn) for c in range(NLOCK) for dirn in range(2)]
        gs = [gates(shs[c], dirn) for c, dirn in chains]
        mbd = mbd_ref[...]
        nms = [gt[2] for gt in gs]
        pws = [_dot3(nm, nm, mbd) for nm in nms]
        tinvs = [icat + nm for nm in nms]
        for _ in range(4):
            rs = [_dot3(jnp.concatenate([ti, pw], axis=0), pw, mbd) for ti, pw in zip(tinvs, pws)]
            tinvs = [ti + r[:CH] for ti, r in zip(tinvs, rs)]
            pws = [r[CH:] for r in rs]
        tinvs = [ti + _dot3(ti, pw, mbd) for ti, pw in zip(tinvs, pws)]
        egs = [jnp.exp(gt[0]) for gt in gs]
        us = [_dot(ti, _bd(shs[c][3], mbd)) for ti, (c, dirn) in zip(tinvs, chains)]
        ws = [_dot(ti, _bd(shs[c][2] * eg, mbd)) for ti, eg, (c, dirn) in zip(tinvs, egs, chains)]
        for i, (c, dirn) in enumerate(chains):
            q, k = shs[c][0], shs[c][1]
            gx, glast, _, qkd = gs[i]
            jj = jjs[c]
            u_out[dirn, jj] = us[i]
            wq_out[dirn, jj] = jnp.concatenate([ws[i], q * egs[i]], axis=0).astype(bf16)
            qkd_out[dirn, jj] = qkd.astype(bf16)
            kd_out[dirn, jj] = (k * jnp.exp(glast - gx)).astype(bf16)
            egl_out[dirn, jj] = jnp.exp(glast)
        return carry

    lax.fori_loop(0, g // NLOCK, body, 0)


def _gdn_scan_kernel(uf, wqf, qkf, kdf, egf, ub, wqb, qkb, kdb, egb, s0_ref, mbd_ref, mbdf_ref,
                     of_out, ob_out, sfin_ref, s_scr, *, sg, cg, nt):
    t = pl.program_id(1)

    @pl.when(t == 0)
    def _():
        s_scr[...] = s0_ref[...]

    def body(c, carry):
        chains = [(sq, dirn) for sq in range(sg) for dirn in range(2)]
        src = lambda dirn: (uf, wqf, qkf, kdf, egf, of_out, c) if dirn == 0 else (ub, wqb, qkb, kdb, egb, ob_out, cg - 1 - c)
        mbd = mbd_ref[...]
        ss = [s_scr[sq, dirn] for sq, dirn in chains]
        wss = [jnp.dot(src(dirn)[1][sq, src(dirn)[6]], st.astype(bf16), preferred_element_type=f32)
               for st, (sq, dirn) in zip(ss, chains)]
        vns = [src(dirn)[0][sq, src(dirn)[6]] - ws[:CH] for ws, (sq, dirn) in zip(wss, chains)]
        os_ = [ws[CH:] + jnp.dot(src(dirn)[2][sq, src(dirn)[6]], _bd(vn, mbd), preferred_element_type=f32)
               for ws, vn, (sq, dirn) in zip(wss, vns, chains)]
        upd = [_dot_tn(src(dirn)[3][sq, src(dirn)[6]], vn) for vn, (sq, dirn) in zip(vns, chains)]
        for st, o, up, (sq, dirn) in zip(ss, os_, upd, chains):
            _, _, _, _, eg, o_out, cc = src(dirn)
            o_out[sq, cc] = o
            s_scr[sq, dirn] = st * eg[sq, cc] + mbdf_ref[...] * up
        return carry

    lax.fori_loop(0, cg, body, 0)

    @pl.when(t == nt - 1)
    def _():
        sfin_ref[...] = s_scr[...]


def _gdn_fin_kernel(of_ref, ob_ref, g_ref, ng_ref, mbd_ref, y_ref, *, g, colmajor):
    def body(jj, carry):
        o = of_ref[jj] + ob_ref[jj]
        ms = _dot(o * o, mbd_ref[...]) * (1.0 / DH)
        gate = g_ref[:, jj, :] if colmajor else g_ref[jj]
        y = o * lax.rsqrt(ms + EPS) * ng_ref[...] * _silu(gate)
        if colmajor:
            y_ref[:, jj, :] = y
        else:
            y_ref[jj] = y
        return carry

    lax.fori_loop(0, g, body, 0)


def _gdn_call(u, nseq, seq_len, s0, p, cst, colmajor):
    n = seq_len // CH
    if colmajor:
        assert seq_len == GRID_W * CH
        g, a = GDN_G, GRID_W
        blk = lambda w: (None, a, g, w)
        imap = lambda c: (lambda s, i: (s, 0, i, c))
    else:
        g, a = n, n
        blk = lambda w: (None, g, CH, w)
        imap = lambda c: (lambda s, i: (s, i, 0, c))
    ngroups = n // g
    u4 = u.reshape(nseq, a, CH, NU)
    main = lambda c: pl.BlockSpec(blk(GW), imap(c))
    full2 = lambda x: pl.BlockSpec(x.shape, lambda s, i: (0,) * x.ndim)
    halos, halo_specs = (), []
    if ngroups > 1:
        per = g // HR
        prev = lambda c: pl.BlockSpec((None, HR, HR, GW), lambda s, i: (s, a // HR - 1, jnp.maximum(i * per - 1, 0), c))
        nxt = lambda c: pl.BlockSpec((None, HR, HR, GW), lambda s, i: (s, 0, jnp.minimum((i + 1) * per, a // HR - 1), c))
        halo_specs = [prev(C_DQ), prev(C_DK), prev(C_DV), nxt(C_DQ), nxt(C_DK), nxt(C_DV)]
        halos = (u4,) * 6
    consts = (p['gdn_conv_w'], p['gdn_alog'], p['gdn_dtb'], cst['e_gdn'], cst['mbd'], cst['tril'], cst['triu'])
    per_chunk = lambda rows, dt: jax.ShapeDtypeStruct((nseq, 2, n, rows, GW), dt)
    per_chunk_spec = lambda rows: pl.BlockSpec((None, 2, g, rows, GW), lambda s, i: (s, 0, i, 0, 0))
    uu, wq, qkd, kd, egl = pl.pallas_call(
        functools.partial(_gdn_prep_kernel, g=g, ngroups=ngroups, colmajor=colmajor), grid=(nseq, ngroups),
        in_specs=[main(C_DQ), main(C_DK), main(C_DV), pl.BlockSpec(blk(LANE), imap(SMALL_BLK))] + halo_specs
                 + [full2(x) for x in consts],
        out_specs=[per_chunk_spec(CH), per_chunk_spec(2 * CH), per_chunk_spec(CH), per_chunk_spec(CH), per_chunk_spec(1)],
        out_shape=[per_chunk(CH, f32), per_chunk(2 * CH, bf16), per_chunk(CH, bf16), per_chunk(CH, bf16), per_chunk(1, f32)],
        compiler_params=_cparams(("arbitrary", "arbitrary")), name="gdn_prep",
    )(u4, u4, u4, u4, *halos, *consts)

    sg = math.gcd(nseq, 2 if colmajor else 4)
    cg = min(n, 8)
    nt = n // cg
    fwd = lambda rows: pl.BlockSpec((sg, None, cg, rows, GW), lambda s, t: (s, 0, t, 0, 0))
    bwd = lambda rows: pl.BlockSpec((sg, None, cg, rows, GW), lambda s, t: (s, 1, nt - 1 - t, 0, 0))
    st_spec = pl.BlockSpec((sg, 2, GW, GW), lambda s, t: (s, 0, 0, 0))
    o_shape = jax.ShapeDtypeStruct((nseq, n, CH, GW), f32)
    o_f, o_b, sfin = pl.pallas_call(
        functools.partial(_gdn_scan_kernel, sg=sg, cg=cg, nt=nt), grid=(nseq // sg, nt),
        in_specs=[fwd(CH), fwd(2 * CH), fwd(CH), fwd(CH), fwd(1), bwd(CH), bwd(2 * CH), bwd(CH), bwd(CH), bwd(1),
                  st_spec, full2(cst['mbd']), full2(cst['mbdf'])],
        out_specs=[pl.BlockSpec((sg, cg, CH, GW), lambda s, t: (s, t, 0, 0)),
                   pl.BlockSpec((sg, cg, CH, GW), lambda s, t: (s, nt - 1 - t, 0, 0)), st_spec],
        out_shape=[o_shape, o_shape, jax.ShapeDtypeStruct((nseq, 2, GW, GW), f32)],
        scratch_shapes=[pltpu.VMEM((sg, 2, GW, GW), f32)],
        compiler_params=_cparams(("arbitrary", "arbitrary")), name="gdn_scan",
    )(uu, wq, qkd, kd, egl, uu, wq, qkd, kd, egl, s0, cst['mbd'], cst['mbdf'])

    o_spec = pl.BlockSpec((None, g, CH, GW), lambda s, i: (s, i, 0, 0))
    y = pl.pallas_call(
        functools.partial(_gdn_fin_kernel, g=g, colmajor=colmajor), grid=(nseq, ngroups),
        in_specs=[o_spec, o_spec, main(C_DGATE), full2(p['gdn_norm_g']), full2(cst['mbd'])],
        out_specs=pl.BlockSpec(blk(GW), imap(0)),
        out_shape=jax.ShapeDtypeStruct((nseq, a, CH, GW), f32),
        compiler_params=_cparams(("arbitrary", "arbitrary")), name="gdn_fin",
    )(o_f, o_b, u4, p['gdn_norm_g'], cst['mbd'])
    return y.reshape(nseq * seq_len, GW), sfin


def _out_kernel(a_ref, gl_ref, gd_ref, sc_ref, x_ref, mod_ref, w_ref, lg_ref, lb_ref, o_ref):
    m = mod_ref[...]
    mix_in = jnp.concatenate([a_ref[...], gl_ref[...], gd_ref[...].astype(bf16), sc_ref[...]], axis=1)
    mix = jnp.dot(mix_in, w_ref[...], preferred_element_type=f32)
    o_ref[...] = _ln(ALPHA * x_ref[...] + m[:, 2 * D:3 * D] * mix, lg_ref[...], lb_ref[...])


def _out_call(a, gl, gd, sc, x, mod, p, tiles_per_mod, tm=512):
    n = x.shape[0]
    part = pl.BlockSpec((tm, GW), lambda i: (i, 0))
    full = lambda t: pl.BlockSpec(t.shape, lambda i: (0,) * t.ndim)
    return pl.pallas_call(
        _out_kernel, grid=(n // tm,),
        in_specs=[part, part, part, part, pl.BlockSpec((tm, D), lambda i: (i, 0)),
                  pl.BlockSpec((None, 1, 6 * D), lambda i: (i // tiles_per_mod, 0, 0)),
                  full(p['w_out']), full(p['ln1_g']), full(p['ln1_b'])],
        out_specs=pl.BlockSpec((tm, D), lambda i: (i, 0)),
        out_shape=jax.ShapeDtypeStruct((n, D), f32),
        compiler_params=_cparams(("arbitrary",)), name="out_proj_ln",
    )(a, gl, gd, sc, x, mod, p['w_out'], p['ln1_g'], p['ln1_b'])


def _ffn_kernel(x_ref, mod_ref, wg_ref, wu_ref, wd_ref, lg_ref, lb_ref, o_ref, h_scr, acc_scr, *, ngroups):
    g = pl.program_id(1)

    @pl.when(g == 0)
    def _():
        m = mod_ref[...]
        h = x_ref[...] * (1.0 + m[:, 4 * D:5 * D]) + m[:, 3 * D:4 * D]
        h_scr[...] = h.astype(bf16)
        acc_scr[...] = jnp.zeros(acc_scr.shape, f32)

    hb = h_scr[...]
    t = _silu(jnp.dot(hb, wg_ref[...], preferred_element_type=f32)) * jnp.dot(hb, wu_ref[...], preferred_element_type=f32)
    acc_scr[...] += jnp.dot(t.astype(bf16), wd_ref[...], preferred_element_type=f32)

    @pl.when(g == ngroups - 1)
    def _():
        m = mod_ref[...]
        o_ref[...] = _ln(ALPHA * x_ref[...] + m[:, 5 * D:6 * D] * acc_scr[...], lg_ref[...], lb_ref[...])


def _ffn_call(x, mod, p, seq_len, tm=512):
    n = x.shape[0]
    fw = D_FF_EXPERT
    ngroups = D_FF // fw
    tiles_per_mod = seq_len // tm if mod.shape[0] > 1 else n // tm
    full = lambda t: pl.BlockSpec(t.shape, lambda i, g: (0,) * t.ndim)
    return pl.pallas_call(
        functools.partial(_ffn_kernel, ngroups=ngroups), grid=(n // tm, ngroups),
        in_specs=[pl.BlockSpec((tm, D), lambda i, g: (i, 0)),
                  pl.BlockSpec((None, 1, 6 * D), lambda i, g: (i // tiles_per_mod, 0, 0)),
                  pl.BlockSpec((D, fw), lambda i, g: (0, g)),
                  pl.BlockSpec((D, fw), lambda i, g: (0, g)),
                  pl.BlockSpec((fw, D), lambda i, g: (g, 0)),
                  full(p['ln2_g']), full(p['ln2_b'])],
        out_specs=pl.BlockSpec((tm, D), lambda i, g: (i, 0)),
        out_shape=jax.ShapeDtypeStruct((n, D), f32),
        scratch_shapes=[pltpu.VMEM((tm, D), bf16), pltpu.VMEM((tm, D), f32)],
        compiler_params=_cparams(("arbitrary", "arbitrary")), name="dense_ffn",
    )(x, mod, p['w_gate'], p['w_up'], p['w_down'], p['ln2_g'], p['ln2_b'])


MOE_TM = 1024
MOE_SB = 128


def _moe_kernel(x_ref, mod_ref, wg_ref, wu_ref, wd_ref, lg_ref, lb_ref, rt_ref, tri_ref, o_ref,
                h_scr, acc_scr, comb_scr, pos_scr):
    e = pl.program_id(1)
    tm = x_ref.shape[0]

    @pl.when(e == 0)
    def _():
        m = mod_ref[...]
        h = x_ref[...] * (1.0 + m[:, 4 * D:5 * D]) + m[:, 3 * D:4 * D]
        h_scr[...] = h.astype(bf16)
        acc_scr[...] = jnp.zeros(acc_scr.shape, f32)
        logits = jnp.dot(h, rt_ref[...], precision=lax.Precision.HIGHEST, preferred_element_type=f32)
        lane = lax.broadcasted_iota(jnp.int32, logits.shape, 1).astype(f32)
        neg = jnp.float32(-jnp.inf)
        logits = jnp.where(lane < N_EXPERTS, logits, neg)
        m1 = jnp.max(logits, axis=-1, keepdims=True)
        i1 = jnp.min(jnp.where(logits == m1, lane, float(LANE)), axis=-1, keepdims=True)
        rest = jnp.where(lane == i1, neg, logits)
        m2 = jnp.max(rest, axis=-1, keepdims=True)
        i2 = jnp.min(jnp.where(rest == m2, lane, float(LANE)), axis=-1, keepdims=True)
        e2 = jnp.exp(m2 - m1)
        p1 = 1.0 / (1.0 + e2)
        p2 = e2 / (1.0 + e2)
        sel1 = lane == i1
        sel2 = lane == i2
        comb_scr[...] = jnp.where(sel1, p1, 0.0) + jnp.where(sel2, p2, 0.0)
        routed = jnp.where(sel1 | sel2, 1.0, 0.0).astype(bf16)
        incl = lax.dot_general(routed, tri_ref[...], (((0,), (0,)), ((), ())), preferred_element_type=f32)
        tok = lax.broadcasted_iota(jnp.int32, incl.shape, 1)
        excl = jnp.where(tok == 0, 0.0, pltpu.roll(incl, 1, 1))
        pos_scr[...] = jnp.where(incl > excl, excl, -1.0)

    pos_row = pos_scr[pl.ds(e, 1), :]
    nrows = (jnp.max(pos_row) + 1.0).astype(jnp.int32)

    def run_pass(base, sb):
        slot = lax.broadcasted_iota(jnp.int32, (sb, tm), 0).astype(f32) + base.astype(f32)
        lane_sb = lax.broadcasted_iota(jnp.int32, (sb, LANE), 1)
        sel = jnp.where(pos_row == slot, 1.0, 0.0).astype(bf16)
        xs = jnp.dot(sel, h_scr[...], preferred_element_type=f32).astype(bf16)
        t = _silu(jnp.dot(xs, wg_ref[...], preferred_element_type=f32)) * jnp.dot(xs, wu_ref[...], preferred_element_type=f32)
        y = jnp.dot(t.astype(bf16), wd_ref[...], preferred_element_type=f32)
        cw = jnp.sum(jnp.where(lane_sb == e, _dot01_left(sel, comb_scr[...]), 0.0), axis=-1, keepdims=True)
        acc_scr[...] += lax.dot_general(sel, (y * cw).astype(bf16), (((0,), (0,)), ((), ())), preferred_element_type=f32)

    big = 2 * MOE_SB
    nbig = (nrows + MOE_SB - 1) // big

    def big_pass(i, carry):
        run_pass(i * big, big)
        return carry

    lax.fori_loop(0, nbig, big_pass, 0)

    @pl.when(nrows > nbig * big)
    def _():
        run_pass(nbig * big, MOE_SB)

    @pl.when(e == N_EXPERTS - 1)
    def _():
        m = mod_ref[...]
        o_ref[...] = _ln(ALPHA * x_ref[...] + m[:, 5 * D:6 * D] * acc_scr[...], lg_ref[...], lb_ref[...])


def _moe_call(x, mod, p, seq_len, cst, tm=MOE_TM):
    n = x.shape[0]
    tm = min(tm, n)
    fw = D_FF_EXPERT
    tiles_per_mod = seq_len // tm if mod.shape[0] > 1 else n // tm
    full = lambda t: pl.BlockSpec(t.shape, lambda i, g: (0,) * t.ndim)
    tri = cst['tri_tok'][:tm, :tm]
    return pl.pallas_call(
        _moe_kernel, grid=(n // tm, N_EXPERTS),
        in_specs=[pl.BlockSpec((tm, D), lambda i, g: (i, 0)),
                  pl.BlockSpec((None, 1, 6 * D), lambda i, g: (i // tiles_per_mod, 0, 0)),
                  pl.BlockSpec((None, D, fw), lambda i, g: (g, 0, 0)),
                  pl.BlockSpec((None, D, fw), lambda i, g: (g, 0, 0)),
                  pl.BlockSpec((None, fw, D), lambda i, g: (g, 0, 0)),
                  full(p['ln2_g']), full(p['ln2_b']), full(p['router']), full(tri)],
        out_specs=pl.BlockSpec((tm, D), lambda i, g: (i, 0)),
        out_shape=jax.ShapeDtypeStruct((n, D), f32),
        scratch_shapes=[pltpu.VMEM((tm, D), bf16), pltpu.VMEM((tm, D), f32),
                        pltpu.VMEM((tm, LANE), f32), pltpu.VMEM((LANE, tm), f32)],
        compiler_params=_cparams(("arbitrary", "arbitrary")), name="moe_ffn",
    )(x, mod, p['w_gate'], p['w_up'], p['w_down'], p['ln2_g'], p['ln2_b'], p['router'], tri)


def _constants():
    hb = np.arange(GW) // DH
    mbd = (hb[:, None] == hb[None, :]).astype(np.float32)
    r = np.arange(CH)
    t = np.arange(MOE_TM)
    e = np.zeros((LANE, 3 * GW), np.float32)
    for h in range(NH):
        e[L_BETA + h, h * DH:(h + 1) * DH] = 1.0
        for d in range(2):
            e[L_DEC + NH * d + h, (1 + d) * GW + h * DH:(1 + d) * GW + (h + 1) * DH] = 1.0
    return dict(mbd=jnp.asarray(mbd, bf16), mbdf=jnp.asarray(mbd, f32),
                tril=jnp.asarray(r[:, None] >= r[None, :], bf16), triu=jnp.asarray(r[:, None] <= r[None, :], bf16),
                e_gdn=jnp.asarray(e, bf16), tri_tok=jnp.asarray(t[:, None] <= t[None, :], bf16))


_IN_SIZES = (GW, GW, GW, GW, GW, GW, 2 * GLA_LR, GW, GW, GW, GW, NH, 2 * NH, GW, GW, GW)


def _pack_w_in(w_in):
    offs = np.concatenate([[0], np.cumsum(_IN_SIZES)])
    big = [i for i, s in enumerate(_IN_SIZES) if s == GW]
    small = [i for i, s in enumerate(_IN_SIZES) if s != GW]
    idx = np.concatenate([np.arange(offs[i], offs[i + 1]) for i in big + small])
    w = jnp.take(w_in, jnp.asarray(idx, jnp.int32), axis=-1)
    pad = NU - idx.shape[0]
    return jnp.pad(w, ((0, 0), (0, 0), (0, pad))).astype(bf16)


def _to_bd(s, transpose):
    if transpose:
        s = jnp.swapaxes(s, -1, -2)
    b = s.shape[0]
    z = jnp.zeros_like(s)
    rows = [jnp.concatenate([s[:, :, h] if g == h else z[:, :, h] for g in range(NH)], axis=-1) for h in range(NH)]
    return jnp.concatenate(rows, axis=-2).reshape(b, 2, GW, GW)


def _from_bd(s, transpose):
    out = jnp.stack([s[:, :, h * DH:(h + 1) * DH, h * DH:(h + 1) * DH] for h in range(NH)], axis=2)
    return jnp.swapaxes(out, -1, -2) if transpose else out


def _stream_layer(x, mod, nseq, seq_len, s0_gla, s0_gdn, p, cst, colmajor, moe):
    tpm_512 = max(seq_len // 512, 1) if mod.shape[0] > 1 else x.shape[0] // 512
    u = _in_call(x, mod, p['w_in'], tpm_512)
    a, ysc = _conv_call(u, seq_len, p)
    ygla, sgla = _gla_call(u, nseq, seq_len, s0_gla, p, cst)
    ygdn, sgdn = _gdn_call(u, nseq, seq_len, s0_gdn, p, cst, colmajor)
    x1 = _out_call(a, ygla, ygdn, ysc, x, mod, p, tpm_512)
    x2 = _moe_call(x1, mod, p['ffn'], seq_len, cst) if moe else _ffn_call(x1, mod, p['ffn'], seq_len)
    return x2, sgla, sgdn


def kernel(x_prompt, x_sample, c, state_gla, state_gdn, c_ctx, w_ada, b_ada, w_in, w_out, conv_w, conv_b, conv_ln_g, conv_ln_b, conv_pw, gla_w_lr, gla_b_lr, gla_norm_g, gdn_conv_w, gdn_a_log, gdn_dt_bias, gdn_norm_g, sc_conv_w, ln1_g, ln1_b, ln2_g, ln2_b, ffn_w_gate, ffn_w_up, ffn_w_down, moe_router, moe_w_gate, moe_w_up, moe_w_down):
    nb, seq, _ = x_prompt.shape
    db, dseq, _ = x_sample.shape
    depth = w_in.shape[0]
    cst = _constants()

    rows = 16
    cond = jnp.concatenate([c_ctx[None, :], c, jnp.zeros((rows - 1 - db, D), f32)], axis=0)
    mod = _mod_call(cond, w_ada, b_ada)

    w_in_p = _pack_w_in(w_in)
    w_out_b = w_out.astype(bf16)
    conv_pw_b = conv_pw.astype(bf16)
    wlr = jnp.zeros((depth, 2, LANE, GW), f32)
    wlr = wlr.at[:, 0, 0:GLA_LR].set(gla_w_lr[:, 0]).at[:, 1, GLA_LR:2 * GLA_LR].set(gla_w_lr[:, 1]).astype(bf16)
    rep = lambda t: jnp.repeat(t, DH, axis=-1)[:, :, None, :]
    row = lambda t: t[:, None, :]
    ffn_g, ffn_u, ffn_d = ffn_w_gate.astype(bf16), ffn_w_up.astype(bf16), ffn_w_down.astype(bf16)
    moe_g, moe_u, moe_d = moe_w_gate.astype(bf16), moe_w_up.astype(bf16), moe_w_down.astype(bf16)
    router = jnp.pad(moe_router, ((0, 0), (0, 0), (0, LANE - N_EXPERTS)))

    zero_bd = jnp.zeros((nb, 2, GW, GW), f32)
    y_p = x_prompt.reshape(nb * seq, D)
    y_s = x_sample.reshape(db * dseq, D)
    gla_states, gdn_states = [], []
    for l in range(depth):
        j = l // 2
        moe = l % 2 == 1
        if moe:
            ffn = dict(w_gate=moe_g[j], w_up=moe_u[j], w_down=moe_d[j], router=router[j],
                       ln2_g=row(ln2_g)[l], ln2_b=row(ln2_b)[l])
        else:
            ffn = dict(w_gate=ffn_g[j], w_up=ffn_u[j], w_down=ffn_d[j], ln2_g=row(ln2_g)[l], ln2_b=row(ln2_b)[l])
        p = dict(w_in=w_in_p[l], w_out=w_out_b[l], conv_w=conv_w[l], conv_b=row(conv_b)[l],
                 conv_ln_g=row(conv_ln_g)[l], conv_ln_b=row(conv_ln_b)[l], conv_pw=conv_pw_b[l],
                 sc_conv_w=sc_conv_w[l], gla_wlr=wlr[l], gla_blr=gla_b_lr[l][:, None, :],
                 gla_norm_g=row(gla_norm_g)[l], gdn_conv_w=gdn_conv_w[l], gdn_alog=rep(gdn_a_log)[l],
                 gdn_dtb=rep(gdn_dt_bias)[l], gdn_norm_g=row(gdn_norm_g)[l],
                 ln1_g=row(ln1_g)[l], ln1_b=row(ln1_b)[l], ffn=ffn)
        mod_ctx = mod[l, 0:1][:, None, :]
        mod_lat = mod[l, 1:1 + db][:, None, :]
        y_p, s_gla, s_gdn = _stream_layer(y_p, mod_ctx, nb, seq, zero_bd, zero_bd, p, cst, False, moe)
        gla_states.append(_from_bd(s_gla, True))
        gdn_states.append(_from_bd(s_gdn, False))
        y_s, _, _ = _stream_layer(y_s, mod_lat, db, dseq, _to_bd(state_gla[:, l], True),
                                  _to_bd(state_gdn[:, l], False), p, cst, True, moe)
    return (y_p.reshape(nb, seq, D), y_s.reshape(db, dseq, D),
            jnp.stack(gla_states, axis=1), jnp.stack(gdn_states, axis=1))
```

```python
import functools
import math

import numpy as np
import jax
import jax.numpy as jnp
from jax import lax
from jax.experimental import pallas as pl
from jax.experimental.pallas import tpu as pltpu

f32 = jnp.float32
bf16 = jnp.bfloat16

D = 1024
GW = 256
NH = 4
DH = 64
CH = 64
GRID_W = 64
CONV_K = 31
GLA_LR = 16
GLA_TAU = 16.0
DEPTH = 4
N_EXPERTS = 8
D_FF = 2816
D_FF_EXPERT = 1408
ALPHA = (2 * DEPTH) ** 0.25
EPS = 1e-5
NU = 13 * GW + 128
SMALL_BLK = 13 * GW // 128
LANE = 128
C_AVAL, C_AGATE, C_GQ, C_GK, C_GV, C_GG, C_DQ, C_DK, C_DV, C_DGATE, C_SB, C_SC, C_SX = range(13)
L_BETA = 2 * GLA_LR
L_DEC = L_BETA + NH

VMEM_LIMIT = 56 * 1024 * 1024


def _cparams(sem):
    return pltpu.CompilerParams(dimension_semantics=sem, vmem_limit_bytes=VMEM_LIMIT)


def _dot(a, b):
    return jnp.dot(a.astype(bf16), b.astype(bf16), preferred_element_type=f32)


def _dot_nt(a, b):
    return lax.dot_general(a.astype(bf16), b.astype(bf16), (((1,), (1,)), ((), ())),
                           preferred_element_type=f32)


def _dot_tn(a, b):
    return lax.dot_general(a.astype(bf16), b.astype(bf16), (((0,), (0,)), ((), ())),
                           preferred_element_type=f32)


def _split3(x):
    hi = x.astype(bf16)
    r = x - hi.astype(f32)
    mid = r.astype(bf16)
    lo = (r - mid.astype(f32)).astype(bf16)
    return hi, mid, lo


def _dot01_left(m01, x):
    hi, mid, lo = _split3(x)
    d = lambda t: jnp.dot(m01, t, preferred_element_type=f32)
    return d(hi) + d(mid) + d(lo)


def _dot01_right(x, m01):
    hi, mid, lo = _split3(x)
    d = lambda t: jnp.dot(t, m01, preferred_element_type=f32)
    return d(hi) + d(mid) + d(lo)


def _bd(x, mask):
    xb = x.astype(bf16)
    return jnp.concatenate([xb, xb, xb, xb], axis=0) * mask


def _split2(x):
    hi = x.astype(bf16)
    return hi, x - hi.astype(f32)


def _dot3(a_parts, b_parts, mask):
    ah, ar = a_parts
    bh, br = b_parts
    al = ar.astype(bf16)
    bdh = _bd(bh, mask)
    bdl = _bd(br, mask)
    d = lambda x, y: jnp.dot(x, y, preferred_element_type=f32)
    return d(ah, bdh) + d(al, bdh) + d(ah, bdl)


def _silu(x):
    return x * jax.nn.sigmoid(x)


def _ln(x, g, b):
    xc = x - jnp.mean(x, axis=-1, keepdims=True)
    var = jnp.mean(xc * xc, axis=-1, keepdims=True)
    return xc * lax.rsqrt(var + EPS) * g + b


def _head_iotas():
    rowi = lax.broadcasted_iota(jnp.int32, (CH, GW), 0)
    coli = lax.broadcasted_iota(jnp.int32, (CH, GW), 1) & (DH - 1)
    return rowi, coli


def _mod_kernel(c_ref, w_ref, b_ref, o_ref):
    o_ref[...] = jnp.dot(_silu(c_ref[...]), w_ref[...], precision=lax.Precision.HIGHEST,
                         preferred_element_type=f32) + b_ref[...]


def _mod_call(cond, w_ada, b_ada):
    nl = w_ada.shape[0]
    rows = cond.shape[0]
    return pl.pallas_call(
        _mod_kernel, grid=(nl, 6),
        in_specs=[pl.BlockSpec((rows, D), lambda l, n: (0, 0)),
                  pl.BlockSpec((None, D, D), lambda l, n: (l, 0, n)),
                  pl.BlockSpec((None, 1, D), lambda l, n: (l, 0, n))],
        out_specs=pl.BlockSpec((None, rows, D), lambda l, n: (l, 0, n)),
        out_shape=jax.ShapeDtypeStruct((nl, rows, 6 * D), f32),
        compiler_params=_cparams(("arbitrary", "arbitrary")), name="adaln_mod",
    )(cond, w_ada, b_ada.reshape(nl, 1, 6 * D))


def _in_kernel(x_ref, mod_ref, w_ref, u_ref):
    m = mod_ref[...]
    h = x_ref[...] * (1.0 + m[:, D:2 * D]) + m[:, 0:D]
    u_ref[...] = jnp.dot(h.astype(bf16), w_ref[...], preferred_element_type=f32)


def _in_call(x, mod, w, tiles_per_mod, tm=512):
    n = x.shape[0]
    return pl.pallas_call(
        _in_kernel, grid=(n // tm,),
        in_specs=[pl.BlockSpec((tm, D), lambda i: (i, 0)),
                  pl.BlockSpec((None, 1, 6 * D), lambda i: (i // tiles_per_mod, 0, 0)),
                  pl.BlockSpec((D, NU), lambda i: (0, 0))],
        out_specs=pl.BlockSpec((tm, NU), lambda i: (i, 0)),
        out_shape=jax.ShapeDtypeStruct((n, NU), f32),
        compiler_params=_cparams(("arbitrary",)), name="in_proj",
    )(x, mod, w)


CT = 256
HALO = 16
SHALO = 8


def _conv_kernel(av, ag, avp, agp, avn, agn, sb, sc, sx, scp, sxp, scn, sxn,
                 cw, cb, lng, lnb, pw, scw, a_out, ysc_out, pad_a, pad_s, *, tiles_per_seq):
    i = pl.program_id(0)
    j = i % tiles_per_seq
    first = j == 0
    last = j == tiles_per_seq - 1
    glu = lambda v, g: v * jax.nn.sigmoid(g)
    pad_a[0:HALO, :] = jnp.where(first, 0.0, glu(avp[...], agp[...]))
    pad_a[HALO:HALO + CT, :] = glu(av[...], ag[...])
    pad_a[HALO + CT:2 * HALO + CT, :] = jnp.where(last, 0.0, glu(avn[...], agn[...]))
    acc = jnp.zeros((CT, GW), f32)
    base = HALO - CONV_K // 2
    span = 8 * ((CONV_K + 6) // 8)
    for ph in range(8):
        win = pad_a[ph:ph + CT + span - 8, :]
        for al in range(0, span, 8):
            k = al + ph - base
            if 0 <= k < CONV_K:
                acc = acc + win[al:al + CT, :] * cw[k:k + 1, :]
    a = _ln(acc + cb[...], lng[...], lnb[...])
    a_out[...] = _dot(_silu(a), pw[...]).astype(bf16)

    pad_s[0:SHALO, :] = jnp.where(first, 0.0, scp[...] * sxp[...])
    pad_s[SHALO:SHALO + CT, :] = sc[...] * sx[...]
    pad_s[SHALO + CT:2 * SHALO + CT, :] = jnp.where(last, 0.0, scn[...] * sxn[...])
    acc = jnp.zeros((CT, GW), f32)
    for k in range(3):
        o = SHALO - 1 + k
        acc = acc + pad_s[o:o + CT, :] * scw[k:k + 1, :]
    ysc_out[...] = (sb[...] * acc).astype(bf16)


def _conv_call(u, seq_len, p):
    n = u.shape[0]
    tps = seq_len // CT
    nt = n // CT
    main = lambda c: pl.BlockSpec((CT, GW), lambda i: (i, c))

    def halo(c, rows, nxt):
        per = CT // rows
        if nxt:
            return pl.BlockSpec((rows, GW), lambda i: (jnp.minimum((i + 1) * per, nt * per - 1), c))
        return pl.BlockSpec((rows, GW), lambda i: (jnp.maximum(i * per - 1, 0), c))

    full = lambda a: pl.BlockSpec(a.shape, lambda i: (0,) * a.ndim)
    consts = (p['conv_w'], p['conv_b'], p['conv_ln_g'], p['conv_ln_b'], p['conv_pw'], p['sc_conv_w'])
    in_specs = [main(C_AVAL), main(C_AGATE), halo(C_AVAL, HALO, 0), halo(C_AGATE, HALO, 0),
                halo(C_AVAL, HALO, 1), halo(C_AGATE, HALO, 1),
                main(C_SB), main(C_SC), main(C_SX), halo(C_SC, SHALO, 0), halo(C_SX, SHALO, 0),
                halo(C_SC, SHALO, 1), halo(C_SX, SHALO, 1)] + [full(a) for a in consts]
    return pl.pallas_call(
        functools.partial(_conv_kernel, tiles_per_seq=tps), grid=(nt,),
        in_specs=in_specs,
        out_specs=[pl.BlockSpec((CT, GW), lambda i: (i, 0))] * 2,
        out_shape=[jax.ShapeDtypeStruct((n, GW), bf16)] * 2,
        scratch_shapes=[pltpu.VMEM((CT + 2 * HALO, GW), f32), pltpu.VMEM((CT + 2 * SHALO, GW), f32)],
        compiler_params=_cparams(("arbitrary",)), name="conv_mixers",
    )(*([u] * 13), *consts)


GLA_LOCK = 4


def _gla_kernel(q_ref, k_ref, v_ref, g_ref, sm_ref, s0_ref, wlr_ref, blr_ref, ng_ref,
                mbd_ref, mbdf_ref, tril_ref, triu_ref, y_ref, sfin_ref, st_scr, of_scr, *, tt, ntiles):
    p = pl.program_id(1)
    nch = tt // CH
    rowi, coli = _head_iotas()

    @pl.when(p == 0)
    def _():
        st_scr[...] = s0_ref[0]

    @pl.when(p == ntiles)
    def _():
        sfin_ref[0] = st_scr[...]
        st_scr[...] = s0_ref[1]

    def run(dirn):
        tri = tril_ref[...] if dirn == 0 else triu_ref[...]
        keep = (rowi >= coli) if dirn == 0 else (rowi <= coli)
        wlr = wlr_ref[dirn]
        blr = blr_ref[dirn]
        tile = p if dirn == 0 else 2 * ntiles - 1 - p

        def body(it, carry):
            cis = [it * GLA_LOCK + c for c in range(GLA_LOCK)]
            if dirn == 1:
                cis = [nch - 1 - ci for ci in cis]
            rows = [pl.ds(pl.multiple_of(ci * CH, CH), CH) for ci in cis]
            grows = [pl.ds(pl.multiple_of(tile * tt + ci * CH, CH), CH) for ci in cis]
            mbd = mbd_ref[...]
            zs = [jnp.dot(sm_ref[r, :].astype(bf16), wlr, preferred_element_type=f32) + blr for r in rows]
            cums = [_dot01_left(tri, jax.nn.log_sigmoid(z) * (1.0 / GLA_TAU)) for z in zs]
            tots = [cum[CH - 1:CH, :] if dirn == 0 else cum[0:1, :] for cum in cums]
            vs = [v_ref[r, :] for r in rows]
            qes = [q_ref[r, :] * (DH ** -0.5) * jnp.exp(cum) for r, cum in zip(rows, cums)]
            kes = [k_ref[r, :] * jnp.exp(-cum) for r, cum in zip(rows, cums)]
            kds = [k_ref[r, :] * jnp.exp(tot - cum) for r, cum, tot in zip(rows, cums, tots)]
            atts = [jnp.where(keep, _dot_nt(qe, _bd(ke, mbd)), 0.0) for qe, ke in zip(qes, kes)]
            ols = [_dot(att, _bd(v, mbd)) for att, v in zip(atts, vs)]
            upds = [_dot_tn(v, kd) for v, kd in zip(vs, kds)]
            st = st_scr[...]
            os_ = []
            for c in range(GLA_LOCK):
                os_.append(ols[c] + _dot_nt(qes[c], st))
                st = st * jnp.exp(tots[c]) + mbdf_ref[...] * upds[c]
            st_scr[...] = st
            if dirn == 0:
                for c in range(GLA_LOCK):
                    of_scr[grows[c], :] = os_[c]
            else:
                os_ = [of_scr[grows[c], :] + os_[c] for c in range(GLA_LOCK)]
                mss = [_dot(o * o, mbd) * (1.0 / DH) for o in os_]
                for c in range(GLA_LOCK):
                    y = os_[c] * lax.rsqrt(mss[c] + EPS) * ng_ref[...] * _silu(g_ref[rows[c], :])
                    y_ref[rows[c], :] = y.astype(bf16)
            return carry

        lax.fori_loop(0, nch // GLA_LOCK, body, 0)

    @pl.when(p < ntiles)
    def _():
        run(0)

    @pl.when(p >= ntiles)
    def _():
        run(1)

    @pl.when(p == 2 * ntiles - 1)
    def _():
        sfin_ref[1] = st_scr[...]


def _gla_call(u, nseq, seq_len, s0, p, cst):
    tt = min(seq_len, 512)
    ntiles = seq_len // tt
    tmap = lambda q: jnp.where(q < ntiles, q, 2 * ntiles - 1 - q)
    main = lambda c: pl.BlockSpec((tt, GW), lambda s, q: (s * ntiles + tmap(q), c))
    full = lambda a: pl.BlockSpec(a.shape, lambda s, q: (0,) * a.ndim)
    consts = (p['gla_wlr'], p['gla_blr'], p['gla_norm_g'], cst['mbd'], cst['mbdf'], cst['tril'], cst['triu'])
    return pl.pallas_call(
        functools.partial(_gla_kernel, tt=tt, ntiles=ntiles), grid=(nseq, 2 * ntiles),
        in_specs=[main(C_GQ), main(C_GK), main(C_GV), main(C_GG),
                  pl.BlockSpec((tt, LANE), lambda s, q: (s * ntiles + tmap(q), SMALL_BLK)),
                  pl.BlockSpec((None, 2, GW, GW), lambda s, q: (s, 0, 0, 0))] + [full(a) for a in consts],
        out_specs=[pl.BlockSpec((tt, GW), lambda s, q: (s * ntiles + jnp.where(q < ntiles, ntiles - 1, 2 * ntiles - 1 - q), 0)),
                   pl.BlockSpec((None, 2, GW, GW), lambda s, q: (s, 0, 0, 0))],
        out_shape=[jax.ShapeDtypeStruct((nseq * seq_len, GW), bf16),
                   jax.ShapeDtypeStruct((nseq, 2, GW, GW), f32)],
        scratch_shapes=[pltpu.VMEM((GW, GW), f32), pltpu.VMEM((seq_len, GW), f32)],
        compiler_params=_cparams(("arbitrary", "arbitrary")), name="gla",
    )(u, u, u, u, u, s0, *consts)


GDN_G = 8
HR = 8
NLOCK = 4
INV_HI_LEVELS = 4


def _gdn_prep_kernel(*refs, g, ngroups, colmajor):
    if ngroups > 1:
        (q_ref, k_ref, v_ref, sm_ref, qp, kp, vp, qn, kn, vnx, cw_ref, alog_ref, dtb_ref, e_ref, mbd_ref,
         tril_ref, triu_ref, u_out, wq_out, qkd_out, kd_out, egl_out) = refs
    else:
        (q_ref, k_ref, v_ref, sm_ref, cw_ref, alog_ref, dtb_ref, e_ref, mbd_ref,
         tril_ref, triu_ref, u_out, wq_out, qkd_out, kd_out, egl_out) = refs
        qp = kp = vp = qn = kn = vnx = None
    gi = pl.program_id(1)
    rowi, coli = _head_iotas()
    row_first = rowi == 0
    row_last = rowi == CH - 1
    icat = jnp.where(rowi == coli, 1.0, 0.0)

    if colmajor:
        get = lambda ref, j: ref[:, j, :]
        last_row = lambda ref, j: ref[CH - 1, pl.ds(j, 1), :]
        first_row = lambda ref, j: ref[0, pl.ds(j, 1), :]
    else:
        get = lambda ref, j: ref[j]
        last_row = lambda ref, j: ref[j, CH - 1:CH, :]
        first_row = lambda ref, j: ref[j, 0:1, :]

    def shared(jj):
        jm = jnp.maximum(jj - 1, 0)
        jp = jnp.minimum(jj + 1, g - 1)

        def conv(ref, pref, nref, c0):
            x = get(ref, jj)
            if pref is None:
                pr_out = 0.0
                nx_out = 0.0
            else:
                pr_out = jnp.where(gi > 0, pref[HR - 1, HR - 1:HR, :], 0.0)
                nx_out = jnp.where(gi < ngroups - 1, nref[0, 0:1, :], 0.0)
            pr = jnp.where(jj > 0, last_row(ref, jm), pr_out)
            nx = jnp.where(jj < g - 1, first_row(ref, jp), nx_out)
            xd = jnp.where(row_first, pr, pltpu.roll(x, 1, 0))
            xu = jnp.where(row_last, nx, pltpu.roll(x, CH - 1, 0))
            return _silu(xd * cw_ref[0:1, c0:c0 + GW] + x * cw_ref[1:2, c0:c0 + GW] + xu * cw_ref[2:3, c0:c0 + GW])

        mbd = mbd_ref[...]
        q = conv(q_ref, qp, qn, 0)
        k = conv(k_ref, kp, kn, GW)
        v = conv(v_ref, vp, vnx, 2 * GW)
        q = q * lax.rsqrt(_dot(q * q, mbd) + 1e-6) * (DH ** -0.5)
        k = k * lax.rsqrt(_dot(k * k, mbd) + 1e-6)
        ex = _dot01_right(get(sm_ref, jj), e_ref[...])
        beta = jax.nn.sigmoid(ex[:, :GW])
        kb = k * beta
        gm = _dot_nt(jnp.concatenate([kb, q], axis=0), _bd(k, mbd))
        return q, k, kb, v * beta, ex, gm

    def gates(sh, dirn):
        q, k, kb, vb, ex, gm = sh
        la = -jnp.exp(alog_ref[dirn]) * jax.nn.softplus(ex[:, (1 + dirn) * GW:(2 + dirn) * GW] + dtb_ref[dirn])
        if dirn == 0:
            tri, keep, strict, keep_t = tril_ref[...], rowi >= coli, rowi > coli, rowi <= coli
        else:
            tri, keep, strict, keep_t = triu_ref[...], rowi <= coli, rowi < coli, rowi >= coli
        gx = _dot01_left(tri, la)
        grow = jnp.sum(jnp.where(keep_t, la, 0.0), axis=0, keepdims=True)
        decay = jnp.where(keep, jnp.exp(jnp.where(keep, gx - grow, 0.0)), 0.0)
        glast = gx[CH - 1:CH, :] if dirn == 0 else gx[0:1, :]
        nm = jnp.where(strict, -(gm[:CH] * decay), 0.0)
        return gx, glast, nm, gm[CH:] * decay

    def body(it, carry):
        jjs = [it * NLOCK + c for c in range(NLOCK)]
        shs = [shared(jj) for jj in jjs]
        chains = [(c, dirn) for c in range(NLOCK) for dirn in range(2)]
        gs = [gates(shs[c], dirn) for c, dirn in chains]
        mbd = mbd_ref[...]
        nms = [gt[2] for gt in gs]
        cat = lambda x, y: jnp.concatenate([x, y], axis=0)
        nsp = [_split2(nm) for nm in nms]
        pws = [_dot3(sp, sp, mbd) for sp in nsp]
        tinvs = [icat + nm for nm in nms]
        for _ in range(INV_HI_LEVELS - 1):
            psp = [_split2(pw) for pw in pws]
            tsp = [_split2(ti) for ti in tinvs]
            rs = [_dot3((cat(th, ph), cat(tr, pr)), (ph, pr), mbd) for (th, tr), (ph, pr) in zip(tsp, psp)]
            tinvs = [ti + r[:CH] for ti, r in zip(tinvs, rs)]
            pws = [r[CH:] for r in rs]
        rs = [_dot(cat(ti, pw), _bd(pw, mbd)) for ti, pw in zip(tinvs, pws)]
        tinvs = [ti + r[:CH] for ti, r in zip(tinvs, rs)]
        tinvs = [ti + _dot(ti, _bd(r[CH:], mbd)) for ti, r in zip(tinvs, rs)]
        egs = [jnp.exp(gt[0]) for gt in gs]
        us = [_dot(ti, _bd(shs[c][3], mbd)) for ti, (c, dirn) in zip(tinvs, chains)]
        ws = [_dot(ti, _bd(shs[c][2] * eg, mbd)) for ti, eg, (c, dirn) in zip(tinvs, egs, chains)]
        for i, (c, dirn) in enumerate(chains):
            q, k = shs[c][0], shs[c][1]
            gx, glast, _, qkd = gs[i]
            jj = jjs[c]
            u_out[dirn, jj] = us[i]
            wq_out[dirn, jj] = jnp.concatenate([ws[i], q * egs[i]], axis=0).astype(bf16)
            qkd_out[dirn, jj] = qkd.astype(bf16)
            kd_out[dirn, jj] = (k * jnp.exp(glast - gx)).astype(bf16)
            egl_out[dirn, jj] = jnp.exp(glast)
        return carry

    lax.fori_loop(0, g // NLOCK, body, 0)


def _gdn_scan_kernel(uf, wqf, qkf, kdf, egf, ub, wqb, qkb, kdb, egb, s0_ref, mbd_ref, mbdf_ref,
                     of_out, ob_out, sfin_ref, s_scr, *, sg, cg, nt):
    t = pl.program_id(1)

    @pl.when(t == 0)
    def _():
        s_scr[...] = s0_ref[...]

    def body(c, carry):
        chains = [(sq, dirn) for sq in range(sg) for dirn in range(2)]
        src = lambda dirn: (uf, wqf, qkf, kdf, egf, of_out, c) if dirn == 0 else (ub, wqb, qkb, kdb, egb, ob_out, cg - 1 - c)
        mbd = mbd_ref[...]
        ss = [s_scr[sq, dirn] for sq, dirn in chains]
        wss = [jnp.dot(src(dirn)[1][sq, src(dirn)[6]], st.astype(bf16), preferred_element_type=f32)
               for st, (sq, dirn) in zip(ss, chains)]
        vns = [src(dirn)[0][sq, src(dirn)[6]] - ws[:CH] for ws, (sq, dirn) in zip(wss, chains)]
        os_ = [ws[CH:] + jnp.dot(src(dirn)[2][sq, src(dirn)[6]], _bd(vn, mbd), preferred_element_type=f32)
               for ws, vn, (sq, dirn) in zip(wss, vns, chains)]
        upd = [_dot_tn(src(dirn)[3][sq, src(dirn)[6]], vn) for vn, (sq, dirn) in zip(vns, chains)]
        for st, o, up, (sq, dirn) in zip(ss, os_, upd, chains):
            _, _, _, _, eg, o_out, cc = src(dirn)
            o_out[sq, cc] = o
            s_scr[sq, dirn] = st * eg[sq, cc] + mbdf_ref[...] * up
        return carry

    lax.fori_loop(0, cg, body, 0)

    @pl.when(t == nt - 1)
    def _():
        sfin_ref[...] = s_scr[...]


def _gdn_fin_kernel(of_ref, ob_ref, g_ref, ng_ref, mbd_ref, y_ref, *, g, colmajor):
    os_ = [of_ref[jj] + ob_ref[jj] for jj in range(g)]
    mss = [_dot(o * o, mbd_ref[...]) * (1.0 / DH) for o in os_]
    for jj in range(g):
        gate = g_ref[:, jj, :] if colmajor else g_ref[jj]
        y = os_[jj] * lax.rsqrt(mss[jj] + EPS) * ng_ref[...] * _silu(gate)
        if colmajor:
            y_ref[:, jj, :] = y
        else:
            y_ref[jj] = y


def _gdn_call(u, nseq, seq_len, s0, p, cst, colmajor):
    n = seq_len // CH
    if colmajor:
        assert seq_len == GRID_W * CH
        g, a = GDN_G, GRID_W
        blk = lambda w: (None, a, g, w)
        imap = lambda c: (lambda s, i: (s, 0, i, c))
    else:
        g, a = n, n
        blk = lambda w: (None, g, CH, w)
        imap = lambda c: (lambda s, i: (s, i, 0, c))
    ngroups = n // g
    u4 = u.reshape(nseq, a, CH, NU)
    main = lambda c: pl.BlockSpec(blk(GW), imap(c))
    full2 = lambda x: pl.BlockSpec(x.shape, lambda s, i: (0,) * x.ndim)
    halos, halo_specs = (), []
    if ngroups > 1:
        per = g // HR
        prev = lambda c: pl.BlockSpec((None, HR, HR, GW), lambda s, i: (s, a // HR - 1, jnp.maximum(i * per - 1, 0), c))
        nxt = lambda c: pl.BlockSpec((None, HR, HR, GW), lambda s, i: (s, 0, jnp.minimum((i + 1) * per, a // HR - 1), c))
        halo_specs = [prev(C_DQ), prev(C_DK), prev(C_DV), nxt(C_DQ), nxt(C_DK), nxt(C_DV)]
        halos = (u4,) * 6
    consts = (p['gdn_conv_w'], p['gdn_alog'], p['gdn_dtb'], cst['e_gdn'], cst['mbd'], cst['tril'], cst['triu'])
    per_chunk = lambda rows, dt: jax.ShapeDtypeStruct((nseq, 2, n, rows, GW), dt)
    per_chunk_spec = lambda rows: pl.BlockSpec((None, 2, g, rows, GW), lambda s, i: (s, 0, i, 0, 0))
    uu, wq, qkd, kd, egl = pl.pallas_call(
        functools.partial(_gdn_prep_kernel, g=g, ngroups=ngroups, colmajor=colmajor), grid=(nseq, ngroups),
        in_specs=[main(C_DQ), main(C_DK), main(C_DV), pl.BlockSpec(blk(LANE), imap(SMALL_BLK))] + halo_specs
                 + [full2(x) for x in consts],
        out_specs=[per_chunk_spec(CH), per_chunk_spec(2 * CH), per_chunk_spec(CH), per_chunk_spec(CH), per_chunk_spec(1)],
        out_shape=[per_chunk(CH, f32), per_chunk(2 * CH, bf16), per_chunk(CH, bf16), per_chunk(CH, bf16), per_chunk(1, f32)],
        compiler_params=_cparams(("arbitrary", "arbitrary")), name="gdn_prep",
    )(u4, u4, u4, u4, *halos, *consts)

    sg = math.gcd(nseq, 4)
    cg = min(n, 8)
    nt = n // cg
    fwd = lambda rows: pl.BlockSpec((sg, None, cg, rows, GW), lambda s, t: (s, 0, t, 0, 0))
    bwd = lambda rows: pl.BlockSpec((sg, None, cg, rows, GW), lambda s, t: (s, 1, nt - 1 - t, 0, 0))
    st_spec = pl.BlockSpec((sg, 2, GW, GW), lambda s, t: (s, 0, 0, 0))
    o_shape = jax.ShapeDtypeStruct((nseq, n, CH, GW), f32)
    o_f, o_b, sfin = pl.pallas_call(
        functools.partial(_gdn_scan_kernel, sg=sg, cg=cg, nt=nt), grid=(nseq // sg, nt),
        in_specs=[fwd(CH), fwd(2 * CH), fwd(CH), fwd(CH), fwd(1), bwd(CH), bwd(2 * CH), bwd(CH), bwd(CH), bwd(1),
                  st_spec, full2(cst['mbd']), full2(cst['mbdf'])],
        out_specs=[pl.BlockSpec((sg, cg, CH, GW), lambda s, t: (s, t, 0, 0)),
                   pl.BlockSpec((sg, cg, CH, GW), lambda s, t: (s, nt - 1 - t, 0, 0)), st_spec],
        out_shape=[o_shape, o_shape, jax.ShapeDtypeStruct((nseq, 2, GW, GW), f32)],
        scratch_shapes=[pltpu.VMEM((sg, 2, GW, GW), f32)],
        compiler_params=_cparams(("arbitrary", "arbitrary")), name="gdn_scan",
    )(uu, wq, qkd, kd, egl, uu, wq, qkd, kd, egl, s0, cst['mbd'], cst['mbdf'])

    o_spec = pl.BlockSpec((None, g, CH, GW), lambda s, i: (s, i, 0, 0))
    y = pl.pallas_call(
        functools.partial(_gdn_fin_kernel, g=g, colmajor=colmajor), grid=(nseq, ngroups),
        in_specs=[o_spec, o_spec, main(C_DGATE), full2(p['gdn_norm_g']), full2(cst['mbd'])],
        out_specs=pl.BlockSpec(blk(GW), imap(0)),
        out_shape=jax.ShapeDtypeStruct((nseq, a, CH, GW), f32),
        compiler_params=_cparams(("arbitrary", "arbitrary")), name="gdn_fin",
    )(o_f, o_b, u4, p['gdn_norm_g'], cst['mbd'])
    return y.reshape(nseq * seq_len, GW), sfin


def _out_kernel(a_ref, gl_ref, gd_ref, sc_ref, x_ref, mod_ref, w_ref, lg_ref, lb_ref, o_ref):
    m = mod_ref[...]
    mix_in = jnp.concatenate([a_ref[...], gl_ref[...], gd_ref[...].astype(bf16), sc_ref[...]], axis=1)
    mix = jnp.dot(mix_in, w_ref[...], preferred_element_type=f32)
    o_ref[...] = _ln(ALPHA * x_ref[...] + m[:, 2 * D:3 * D] * mix, lg_ref[...], lb_ref[...])


def _out_call(a, gl, gd, sc, x, mod, p, tiles_per_mod, tm=512):
    n = x.shape[0]
    part = pl.BlockSpec((tm, GW), lambda i: (i, 0))
    full = lambda t: pl.BlockSpec(t.shape, lambda i: (0,) * t.ndim)
    return pl.pallas_call(
        _out_kernel, grid=(n // tm,),
        in_specs=[part, part, part, part, pl.BlockSpec((tm, D), lambda i: (i, 0)),
                  pl.BlockSpec((None, 1, 6 * D), lambda i: (i // tiles_per_mod, 0, 0)),
                  full(p['w_out']), full(p['ln1_g']), full(p['ln1_b'])],
        out_specs=pl.BlockSpec((tm, D), lambda i: (i, 0)),
        out_shape=jax.ShapeDtypeStruct((n, D), f32),
        compiler_params=_cparams(("arbitrary",)), name="out_proj_ln",
    )(a, gl, gd, sc, x, mod, p['w_out'], p['ln1_g'], p['ln1_b'])


def _ffn_kernel(x_ref, mod_ref, wg_ref, wu_ref, wd_ref, lg_ref, lb_ref, o_ref, h_scr, acc_scr, *, ngroups):
    g = pl.program_id(1)

    @pl.when(g == 0)
    def _():
        m = mod_ref[...]
        h = x_ref[...] * (1.0 + m[:, 4 * D:5 * D]) + m[:, 3 * D:4 * D]
        h_scr[...] = h.astype(bf16)
        acc_scr[...] = jnp.zeros(acc_scr.shape, f32)

    hb = h_scr[...]
    t = _silu(jnp.dot(hb, wg_ref[...], preferred_element_type=f32)) * jnp.dot(hb, wu_ref[...], preferred_element_type=f32)
    acc_scr[...] += jnp.dot(t.astype(bf16), wd_ref[...], preferred_element_type=f32)

    @pl.when(g == ngroups - 1)
    def _():
        m = mod_ref[...]
        o_ref[...] = _ln(ALPHA * x_ref[...] + m[:, 5 * D:6 * D] * acc_scr[...], lg_ref[...], lb_ref[...])


def _ffn_call(x, mod, p, seq_len, tm=512, fw=D_FF_EXPERT):
    n = x.shape[0]
    ngroups = D_FF // fw
    tiles_per_mod = seq_len // tm if mod.shape[0] > 1 else n // tm
    full = lambda t: pl.BlockSpec(t.shape, lambda i, g: (0,) * t.ndim)
    return pl.pallas_call(
        functools.partial(_ffn_kernel, ngroups=ngroups), grid=(n // tm, ngroups),
        in_specs=[pl.BlockSpec((tm, D), lambda i, g: (i, 0)),
                  pl.BlockSpec((None, 1, 6 * D), lambda i, g: (i // tiles_per_mod, 0, 0)),
                  pl.BlockSpec((D, fw), lambda i, g: (0, g)),
                  pl.BlockSpec((D, fw), lambda i, g: (0, g)),
                  pl.BlockSpec((fw, D), lambda i, g: (g, 0)),
                  full(p['ln2_g']), full(p['ln2_b'])],
        out_specs=pl.BlockSpec((tm, D), lambda i, g: (i, 0)),
        out_shape=jax.ShapeDtypeStruct((n, D), f32),
        scratch_shapes=[pltpu.VMEM((tm, D), bf16), pltpu.VMEM((tm, D), f32)],
        compiler_params=_cparams(("arbitrary", "arbitrary")), name="dense_ffn",
    )(x, mod, p['w_gate'], p['w_up'], p['w_down'], p['ln2_g'], p['ln2_b'])


MOE_TM = 1024
MOE_SB = 128


def _moe_kernel(x_ref, mod_ref, wg_ref, wu_ref, wd_ref, lg_ref, lb_ref, rt_ref, tri_ref, o_ref,
                h_scr, acc_scr, comb_scr, pos_scr):
    e = pl.program_id(1)
    tm = x_ref.shape[0]

    @pl.when(e == 0)
    def _():
        m = mod_ref[...]
        h = x_ref[...] * (1.0 + m[:, 4 * D:5 * D]) + m[:, 3 * D:4 * D]
        h_scr[...] = h.astype(bf16)
        acc_scr[...] = jnp.zeros(acc_scr.shape, f32)
        logits = jnp.dot(h, rt_ref[...], precision=lax.Precision.HIGHEST, preferred_element_type=f32)
        lane = lax.broadcasted_iota(jnp.int32, logits.shape, 1).astype(f32)
        neg = jnp.float32(-jnp.inf)
        logits = jnp.where(lane < N_EXPERTS, logits, neg)
        m1 = jnp.max(logits, axis=-1, keepdims=True)
        i1 = jnp.min(jnp.where(logits == m1, lane, float(LANE)), axis=-1, keepdims=True)
        rest = jnp.where(lane == i1, neg, logits)
        m2 = jnp.max(rest, axis=-1, keepdims=True)
        i2 = jnp.min(jnp.where(rest == m2, lane, float(LANE)), axis=-1, keepdims=True)
        e2 = jnp.exp(m2 - m1)
        p1 = 1.0 / (1.0 + e2)
        p2 = e2 / (1.0 + e2)
        sel1 = lane == i1
        sel2 = lane == i2
        comb_scr[...] = jnp.where(sel1, p1, 0.0) + jnp.where(sel2, p2, 0.0)
        routed = jnp.where(sel1 | sel2, 1.0, 0.0).astype(bf16)
        incl = lax.dot_general(routed, tri_ref[...], (((0,), (0,)), ((), ())), preferred_element_type=f32)
        tok = lax.broadcasted_iota(jnp.int32, incl.shape, 1)
        excl = jnp.where(tok == 0, 0.0, pltpu.roll(incl, 1, 1))
        pos_scr[...] = jnp.where(incl > excl, excl, -1.0)

    pos_row = pos_scr[pl.ds(e, 1), :]
    nrows = (jnp.max(pos_row) + 1.0).astype(jnp.int32)

    def run_pass(base, sb):
        slot = lax.broadcasted_iota(jnp.int32, (sb, tm), 0).astype(f32) + base.astype(f32)
        lane_sb = lax.broadcasted_iota(jnp.int32, (sb, LANE), 1)
        sel = jnp.where(pos_row == slot, 1.0, 0.0).astype(bf16)
        xs = jnp.dot(sel, h_scr[...], preferred_element_type=f32).astype(bf16)
        t = _silu(jnp.dot(xs, wg_ref[...], preferred_element_type=f32)) * jnp.dot(xs, wu_ref[...], preferred_element_type=f32)
        y = jnp.dot(t.astype(bf16), wd_ref[...], preferred_element_type=f32)
        cw = jnp.sum(jnp.where(lane_sb == e, _dot01_left(sel, comb_scr[...]), 0.0), axis=-1, keepdims=True)
        acc_scr[...] += lax.dot_general(sel, (y * cw).astype(bf16), (((0,), (0,)), ((), ())), preferred_element_type=f32)

    big = 2 * MOE_SB
    nbig = (nrows + MOE_SB - 1) // big

    def big_pass(i, carry):
        run_pass(i * big, big)
        return carry

    lax.fori_loop(0, nbig, big_pass, 0)

    @pl.when(nrows > nbig * big)
    def _():
        run_pass(nbig * big, MOE_SB)

    @pl.when(e == N_EXPERTS - 1)
    def _():
        m = mod_ref[...]
        o_ref[...] = _ln(ALPHA * x_ref[...] + m[:, 5 * D:6 * D] * acc_scr[...], lg_ref[...], lb_ref[...])


def _moe_call(x, mod, p, seq_len, cst, tm=MOE_TM):
    n = x.shape[0]
    tm = min(tm, n)
    fw = D_FF_EXPERT
    tiles_per_mod = seq_len // tm if mod.shape[0] > 1 else n // tm
    full = lambda t: pl.BlockSpec(t.shape, lambda i, g: (0,) * t.ndim)
    tri = cst['tri_tok'][:tm, :tm]
    return pl.pallas_call(
        _moe_kernel, grid=(n // tm, N_EXPERTS),
        in_specs=[pl.BlockSpec((tm, D), lambda i, g: (i, 0)),
                  pl.BlockSpec((None, 1, 6 * D), lambda i, g: (i // tiles_per_mod, 0, 0)),
                  pl.BlockSpec((None, D, fw), lambda i, g: (g, 0, 0)),
                  pl.BlockSpec((None, D, fw), lambda i, g: (g, 0, 0)),
                  pl.BlockSpec((None, fw, D), lambda i, g: (g, 0, 0)),
                  full(p['ln2_g']), full(p['ln2_b']), full(p['router']), full(tri)],
        out_specs=pl.BlockSpec((tm, D), lambda i, g: (i, 0)),
        out_shape=jax.ShapeDtypeStruct((n, D), f32),
        scratch_shapes=[pltpu.VMEM((tm, D), bf16), pltpu.VMEM((tm, D), f32),
                        pltpu.VMEM((tm, LANE), f32), pltpu.VMEM((LANE, tm), f32)],
        compiler_params=_cparams(("arbitrary", "arbitrary")), name="moe_ffn",
    )(x, mod, p['w_gate'], p['w_up'], p['w_down'], p['ln2_g'], p['ln2_b'], p['router'], tri)


def _constants():
    hb = np.arange(GW) // DH
    mbd = (hb[:, None] == hb[None, :]).astype(np.float32)
    r = np.arange(CH)
    t = np.arange(MOE_TM)
    e = np.zeros((LANE, 3 * GW), np.float32)
    for h in range(NH):
        e[L_BETA + h, h * DH:(h + 1) * DH] = 1.0
        for d in range(2):
            e[L_DEC + NH * d + h, (1 + d) * GW + h * DH:(1 + d) * GW + (h + 1) * DH] = 1.0
    return dict(mbd=jnp.asarray(mbd, bf16), mbdf=jnp.asarray(mbd, f32),
                tril=jnp.asarray(r[:, None] >= r[None, :], bf16), triu=jnp.asarray(r[:, None] <= r[None, :], bf16),
                e_gdn=jnp.asarray(e, bf16), tri_tok=jnp.asarray(t[:, None] <= t[None, :], bf16))


_IN_SIZES = (GW, GW, GW, GW, GW, GW, 2 * GLA_LR, GW, GW, GW, GW, NH, 2 * NH, GW, GW, GW)


def _pack_w_in(w_in):
    offs = np.concatenate([[0], np.cumsum(_IN_SIZES)])
    big = [i for i, s in enumerate(_IN_SIZES) if s == GW]
    small = [i for i, s in enumerate(_IN_SIZES) if s != GW]
    idx = np.concatenate([np.arange(offs[i], offs[i + 1]) for i in big + small])
    w = jnp.take(w_in, jnp.asarray(idx, jnp.int32), axis=-1)
    pad = NU - idx.shape[0]
    return jnp.pad(w, ((0, 0), (0, 0), (0, pad))).astype(bf16)


def _to_bd(s, transpose):
    if transpose:
        s = jnp.swapaxes(s, -1, -2)
    b = s.shape[0]
    z = jnp.zeros_like(s)
    rows = [jnp.concatenate([s[:, :, h] if g == h else z[:, :, h] for g in range(NH)], axis=-1) for h in range(NH)]
    return jnp.concatenate(rows, axis=-2).reshape(b, 2, GW, GW)


def _from_bd(s, transpose):
    out = jnp.stack([s[:, :, h * DH:(h + 1) * DH, h * DH:(h + 1) * DH] for h in range(NH)], axis=2)
    return jnp.swapaxes(out, -1, -2) if transpose else out


def _stream_layer(x, mod, nseq, seq_len, s0_gla, s0_gdn, p, cst, colmajor, moe):
    tpm_512 = max(seq_len // 512, 1) if mod.shape[0] > 1 else x.shape[0] // 512
    u = _in_call(x, mod, p['w_in'], tpm_512)
    a, ysc = _conv_call(u, seq_len, p)
    ygla, sgla = _gla_call(u, nseq, seq_len, s0_gla, p, cst)
    ygdn, sgdn = _gdn_call(u, nseq, seq_len, s0_gdn, p, cst, colmajor)
    x1 = _out_call(a, ygla, ygdn, ysc, x, mod, p, tpm_512)
    x2 = _moe_call(x1, mod, p['ffn'], seq_len, cst) if moe else _ffn_call(x1, mod, p['ffn'], seq_len)
    return x2, sgla, sgdn


def kernel(x_prompt, x_sample, c, state_gla, state_gdn, c_ctx, w_ada, b_ada, w_in, w_out, conv_w, conv_b, conv_ln_g, conv_ln_b, conv_pw, gla_w_lr, gla_b_lr, gla_norm_g, gdn_conv_w, gdn_a_log, gdn_dt_bias, gdn_norm_g, sc_conv_w, ln1_g, ln1_b, ln2_g, ln2_b, ffn_w_gate, ffn_w_up, ffn_w_down, moe_router, moe_w_gate, moe_w_up, moe_w_down):
    nb, seq, _ = x_prompt.shape
    db, dseq, _ = x_sample.shape
    depth = w_in.shape[0]
    cst = _constants()

    rows = 16
    cond = jnp.concatenate([c_ctx[None, :], c, jnp.zeros((rows - 1 - db, D), f32)], axis=0)
    mod = _mod_call(cond, w_ada, b_ada)

    w_in_p = _pack_w_in(w_in)
    w_out_b = w_out.astype(bf16)
    conv_pw_b = conv_pw.astype(bf16)
    wlr = jnp.zeros((depth, 2, LANE, GW), f32)
    wlr = wlr.at[:, 0, 0:GLA_LR].set(gla_w_lr[:, 0]).at[:, 1, GLA_LR:2 * GLA_LR].set(gla_w_lr[:, 1]).astype(bf16)
    rep = lambda t: jnp.repeat(t, DH, axis=-1)[:, :, None, :]
    row = lambda t: t[:, None, :]
    ffn_g, ffn_u, ffn_d = ffn_w_gate.astype(bf16), ffn_w_up.astype(bf16), ffn_w_down.astype(bf16)
    moe_g, moe_u, moe_d = moe_w_gate.astype(bf16), moe_w_up.astype(bf16), moe_w_down.astype(bf16)
    router = jnp.pad(moe_router, ((0, 0), (0, 0), (0, LANE - N_EXPERTS)))

    zero_bd = jnp.zeros((nb, 2, GW, GW), f32)
    y_p = x_prompt.reshape(nb * seq, D)
    y_s = x_sample.reshape(db * dseq, D)
    gla_states, gdn_states = [], []
    for l in range(depth):
        j = l // 2
        moe = l % 2 == 1
        if moe:
            ffn = dict(w_gate=moe_g[j], w_up=moe_u[j], w_down=moe_d[j], router=router[j],
                       ln2_g=row(ln2_g)[l], ln2_b=row(ln2_b)[l])
        else:
            ffn = dict(w_gate=ffn_g[j], w_up=ffn_u[j], w_down=ffn_d[j], ln2_g=row(ln2_g)[l], ln2_b=row(ln2_b)[l])
        p = dict(w_in=w_in_p[l], w_out=w_out_b[l], conv_w=conv_w[l], conv_b=row(conv_b)[l],
                 conv_ln_g=row(conv_ln_g)[l], conv_ln_b=row(conv_ln_b)[l], conv_pw=conv_pw_b[l],
                 sc_conv_w=sc_conv_w[l], gla_wlr=wlr[l], gla_blr=gla_b_lr[l][:, None, :],
                 gla_norm_g=row(gla_norm_g)[l], gdn_conv_w=gdn_conv_w[l], gdn_alog=rep(gdn_a_log)[l],
                 gdn_dtb=rep(gdn_dt_bias)[l], gdn_norm_g=row(gdn_norm_g)[l],
                 ln1_g=row(ln1_g)[l], ln1_b=row(ln1_b)[l], ffn=ffn)
        mod_ctx = mod[l, 0:1][:, None, :]
        mod_lat = mod[l, 1:1 + db][:, None, :]
        y_p, s_gla, s_gdn = _stream_layer(y_p, mod_ctx, nb, seq, zero_bd, zero_bd, p, cst, False, moe)
        gla_states.append(_from_bd(s_gla, True))
        gdn_states.append(_from_bd(s_gdn, False))
        y_s, _, _ = _stream_layer(y_s, mod_lat, db, dseq, _to_bd(state_gla[:, l], True),
                                  _to_bd(state_gdn[:, l], False), p, cst, True, moe)
    return (y_p.reshape(nb, seq, D), y_s.reshape(db, dseq, D),
            jnp.stack(gla_states, axis=1), jnp.stack(gdn_states, axis=1))
```

```python
import functools
import math

import numpy as np
import jax
import jax.numpy as jnp
from jax import lax
from jax.experimental import pallas as pl
from jax.experimental.pallas import tpu as pltpu

f32 = jnp.float32
bf16 = jnp.bfloat16

D = 1024
GW = 256
NH = 4
DH = 64
CH = 64
GRID_W = 64
CONV_K = 31
GLA_LR = 16
GLA_TAU = 16.0
DEPTH = 4
N_EXPERTS = 8
D_FF = 2816
D_FF_EXPERT = 1408
ALPHA = (2 * DEPTH) ** 0.25
EPS = 1e-5
NU = 13 * GW + 128
SMALL_BLK = 13 * GW // 128
LANE = 128
C_AVAL, C_AGATE, C_GQ, C_GK, C_GV, C_GG, C_DQ, C_DK, C_DV, C_DGATE, C_SB, C_SC, C_SX = range(13)
L_BETA = 2 * GLA_LR
L_DEC = L_BETA + NH

VMEM_LIMIT = 56 * 1024 * 1024


def _cparams(sem):
    return pltpu.CompilerParams(dimension_semantics=sem, vmem_limit_bytes=VMEM_LIMIT)


def _dot(a, b):
    return jnp.dot(a.astype(bf16), b.astype(bf16), preferred_element_type=f32)


def _dot_nt(a, b):
    return lax.dot_general(a.astype(bf16), b.astype(bf16), (((1,), (1,)), ((), ())),
                           preferred_element_type=f32)


def _dot_tn(a, b):
    return lax.dot_general(a.astype(bf16), b.astype(bf16), (((0,), (0,)), ((), ())),
                           preferred_element_type=f32)


def _split3(x):
    hi = x.astype(bf16)
    r = x - hi.astype(f32)
    mid = r.astype(bf16)
    lo = (r - mid.astype(f32)).astype(bf16)
    return hi, mid, lo


def _dot01_left(m01, x):
    hi, mid, lo = _split3(x)
    d = lambda t: jnp.dot(m01, t, preferred_element_type=f32)
    return d(hi) + d(mid) + d(lo)


def _dot01_right(x, m01):
    hi, mid, lo = _split3(x)
    d = lambda t: jnp.dot(t, m01, preferred_element_type=f32)
    return d(hi) + d(mid) + d(lo)


def _bd(x, mask):
    xb = x.astype(bf16)
    return jnp.concatenate([xb, xb, xb, xb], axis=0) * mask


def _split2(x):
    hi = x.astype(bf16)
    return hi, x - hi.astype(f32)


def _dot3(a_parts, b_parts, mask):
    ah, ar = a_parts
    bh, br = b_parts
    al = ar.astype(bf16)
    bdh = _bd(bh, mask)
    bdl = _bd(br, mask)
    d = lambda x, y: jnp.dot(x, y, preferred_element_type=f32)
    return d(ah, bdh) + d(al, bdh) + d(ah, bdl)


def _silu(x):
    return x * jax.nn.sigmoid(x)


def _ln(x, g, b):
    xc = x - jnp.mean(x, axis=-1, keepdims=True)
    var = jnp.mean(xc * xc, axis=-1, keepdims=True)
    return xc * lax.rsqrt(var + EPS) * g + b


def _head_iotas():
    rowi = lax.broadcasted_iota(jnp.int32, (CH, GW), 0)
    coli = lax.broadcasted_iota(jnp.int32, (CH, GW), 1) & (DH - 1)
    return rowi, coli


def _mod_kernel(c_ref, w_ref, b_ref, o_ref):
    o_ref[...] = jnp.dot(_silu(c_ref[...]), w_ref[...], precision=lax.Precision.HIGHEST,
                         preferred_element_type=f32) + b_ref[...]


def _mod_call(cond, w_ada, b_ada):
    nl = w_ada.shape[0]
    rows = cond.shape[0]
    return pl.pallas_call(
        _mod_kernel, grid=(nl, 6),
        in_specs=[pl.BlockSpec((rows, D), lambda l, n: (0, 0)),
                  pl.BlockSpec((None, D, D), lambda l, n: (l, 0, n)),
                  pl.BlockSpec((None, 1, D), lambda l, n: (l, 0, n))],
        out_specs=pl.BlockSpec((None, rows, D), lambda l, n: (l, 0, n)),
        out_shape=jax.ShapeDtypeStruct((nl, rows, 6 * D), f32),
        compiler_params=_cparams(("arbitrary", "arbitrary")), name="adaln_mod",
    )(cond, w_ada, b_ada.reshape(nl, 1, 6 * D))


def _in_kernel(x_ref, mod_ref, w_ref, u_ref):
    m = mod_ref[...]
    h = x_ref[...] * (1.0 + m[:, D:2 * D]) + m[:, 0:D]
    u_ref[...] = jnp.dot(h.astype(bf16), w_ref[...], preferred_element_type=f32)


def _in_call(x, mod, w, tiles_per_mod, tm=512):
    n = x.shape[0]
    return pl.pallas_call(
        _in_kernel, grid=(n // tm,),
        in_specs=[pl.BlockSpec((tm, D), lambda i: (i, 0)),
                  pl.BlockSpec((None, 1, 6 * D), lambda i: (i // tiles_per_mod, 0, 0)),
                  pl.BlockSpec((D, NU), lambda i: (0, 0))],
        out_specs=pl.BlockSpec((tm, NU), lambda i: (i, 0)),
        out_shape=jax.ShapeDtypeStruct((n, NU), f32),
        compiler_params=_cparams(("arbitrary",)), name="in_proj",
    )(x, mod, w)


CT = 256
HALO = 16
SHALO = 8


def _conv_kernel(av, ag, avp, agp, avn, agn, sb, sc, sx, scp, sxp, scn, sxn,
                 cw, cb, lng, lnb, pw, scw, a_out, ysc_out, pad_a, pad_s, win_scr, *, tiles_per_seq):
    i = pl.program_id(0)
    j = i % tiles_per_seq
    first = j == 0
    last = j == tiles_per_seq - 1
    glu = lambda v, g: v * jax.nn.sigmoid(g)
    pad_a[0:HALO, :] = jnp.where(first, 0.0, glu(avp[...], agp[...]))
    pad_a[HALO:HALO + CT, :] = glu(av[...], ag[...])
    pad_a[HALO + CT:2 * HALO + CT, :] = jnp.where(last, 0.0, glu(avn[...], agn[...]))
    acc = jnp.zeros((CT, GW), f32)
    base = HALO - CONV_K // 2
    span = 8 * ((CONV_K + 6) // 8)
    for ph in range(8):
        win_scr[...] = pad_a[ph:ph + CT + span - 8, :]
        for al in range(0, span, 8):
            k = al + ph - base
            if 0 <= k < CONV_K:
                acc = acc + win_scr[al:al + CT, :] * cw[k:k + 1, :]
    a = _ln(acc + cb[...], lng[...], lnb[...])
    a_out[...] = _dot(_silu(a), pw[...]).astype(bf16)

    pad_s[0:SHALO, :] = jnp.where(first, 0.0, scp[...] * sxp[...])
    pad_s[SHALO:SHALO + CT, :] = sc[...] * sx[...]
    pad_s[SHALO + CT:2 * SHALO + CT, :] = jnp.where(last, 0.0, scn[...] * sxn[...])
    acc = jnp.zeros((CT, GW), f32)
    for k in range(3):
        o = SHALO - 1 + k
        acc = acc + pad_s[o:o + CT, :] * scw[k:k + 1, :]
    ysc_out[...] = (sb[...] * acc).astype(bf16)


def _conv_call(u, seq_len, p):
    n = u.shape[0]
    tps = seq_len // CT
    nt = n // CT
    main = lambda c: pl.BlockSpec((CT, GW), lambda i: (i, c))

    def halo(c, rows, nxt):
        per = CT // rows
        if nxt:
            return pl.BlockSpec((rows, GW), lambda i: (jnp.minimum((i + 1) * per, nt * per - 1), c))
        return pl.BlockSpec((rows, GW), lambda i: (jnp.maximum(i * per - 1, 0), c))

    full = lambda a: pl.BlockSpec(a.shape, lambda i: (0,) * a.ndim)
    consts = (p['conv_w'], p['conv_b'], p['conv_ln_g'], p['conv_ln_b'], p['conv_pw'], p['sc_conv_w'])
    in_specs = [main(C_AVAL), main(C_AGATE), halo(C_AVAL, HALO, 0), halo(C_AGATE, HALO, 0),
                halo(C_AVAL, HALO, 1), halo(C_AGATE, HALO, 1),
                main(C_SB), main(C_SC), main(C_SX), halo(C_SC, SHALO, 0), halo(C_SX, SHALO, 0),
                halo(C_SC, SHALO, 1), halo(C_SX, SHALO, 1)] + [full(a) for a in consts]
    return pl.pallas_call(
        functools.partial(_conv_kernel, tiles_per_seq=tps), grid=(nt,),
        in_specs=in_specs,
        out_specs=[pl.BlockSpec((CT, GW), lambda i: (i, 0))] * 2,
        out_shape=[jax.ShapeDtypeStruct((n, GW), bf16)] * 2,
        scratch_shapes=[pltpu.VMEM((CT + 2 * HALO, GW), f32), pltpu.VMEM((CT + 2 * SHALO, GW), f32),
                        pltpu.VMEM((CT + 8 * ((CONV_K + 6) // 8) - 8, GW), f32)],
        compiler_params=_cparams(("arbitrary",)), name="conv_mixers",
    )(*([u] * 13), *consts)


GLA_LOCK = 8


def _gla_kernel(q_ref, k_ref, v_ref, g_ref, sm_ref, s0_ref, wlr_ref, blr_ref, ng_ref,
                mbd_ref, mbdf_ref, tril_ref, triu_ref, y_ref, sfin_ref, st_scr, of_scr, *, tt, ntiles):
    p = pl.program_id(1)
    nch = tt // CH
    nlock = min(GLA_LOCK, nch)
    rowi, coli = _head_iotas()

    @pl.when(p == 0)
    def _():
        st_scr[...] = s0_ref[0]

    @pl.when(p == ntiles)
    def _():
        sfin_ref[0] = st_scr[...]
        st_scr[...] = s0_ref[1]

    def run(dirn):
        tri = tril_ref[...] if dirn == 0 else triu_ref[...]
        keep = (rowi >= coli) if dirn == 0 else (rowi <= coli)
        wlr = wlr_ref[dirn]
        blr = blr_ref[dirn]
        tile = p if dirn == 0 else 2 * ntiles - 1 - p

        def body(it, carry):
            cis = [it * nlock + c for c in range(nlock)]
            if dirn == 1:
                cis = [nch - 1 - ci for ci in cis]
            rows = [pl.ds(pl.multiple_of(ci * CH, CH), CH) for ci in cis]
            grows = [pl.ds(pl.multiple_of(tile * tt + ci * CH, CH), CH) for ci in cis]
            mbd = mbd_ref[...]
            zs = [jnp.dot(sm_ref[r, :].astype(bf16), wlr, preferred_element_type=f32) + blr for r in rows]
            cums = [_dot01_left(tri, jax.nn.log_sigmoid(z) * (1.0 / GLA_TAU)) for z in zs]
            tots = [cum[CH - 1:CH, :] if dirn == 0 else cum[0:1, :] for cum in cums]
            vs = [v_ref[r, :] for r in rows]
            qes = [q_ref[r, :] * (DH ** -0.5) * jnp.exp(cum) for r, cum in zip(rows, cums)]
            kes = [k_ref[r, :] * jnp.exp(-cum) for r, cum in zip(rows, cums)]
            kds = [k_ref[r, :] * jnp.exp(tot - cum) for r, cum, tot in zip(rows, cums, tots)]
            atts = [jnp.where(keep, _dot_nt(qe, _bd(ke, mbd)), 0.0) for qe, ke in zip(qes, kes)]
            ols = [_dot(att, _bd(v, mbd)) for att, v in zip(atts, vs)]
            upds = [_dot_tn(v, kd) for v, kd in zip(vs, kds)]
            st = st_scr[...]
            os_ = []
            for c in range(nlock):
                os_.append(ols[c] + _dot_nt(qes[c], st))
                st = st * jnp.exp(tots[c]) + mbdf_ref[...] * upds[c]
            st_scr[...] = st
            if dirn == 0:
                for c in range(nlock):
                    of_scr[grows[c], :] = os_[c]
            else:
                os_ = [of_scr[grows[c], :] + os_[c] for c in range(nlock)]
                mss = [_dot(o * o, mbd) * (1.0 / DH) for o in os_]
                for c in range(nlock):
                    y = os_[c] * lax.rsqrt(mss[c] + EPS) * ng_ref[...] * _silu(g_ref[rows[c], :])
                    y_ref[rows[c], :] = y.astype(bf16)
            return carry

        lax.fori_loop(0, nch // nlock, body, 0)

    @pl.when(p < ntiles)
    def _():
        run(0)

    @pl.when(p >= ntiles)
    def _():
        run(1)

    @pl.when(p == 2 * ntiles - 1)
    def _():
        sfin_ref[1] = st_scr[...]


def _gla_call(u, nseq, seq_len, s0, p, cst):
    tt = min(seq_len, 512)
    ntiles = seq_len // tt
    tmap = lambda q: jnp.where(q < ntiles, q, 2 * ntiles - 1 - q)
    main = lambda c: pl.BlockSpec((tt, GW), lambda s, q: (s * ntiles + tmap(q), c))
    full = lambda a: pl.BlockSpec(a.shape, lambda s, q: (0,) * a.ndim)
    consts = (p['gla_wlr'], p['gla_blr'], p['gla_norm_g'], cst['mbd'], cst['mbdf'], cst['tril'], cst['triu'])
    return pl.pallas_call(
        functools.partial(_gla_kernel, tt=tt, ntiles=ntiles), grid=(nseq, 2 * ntiles),
        in_specs=[main(C_GQ), main(C_GK), main(C_GV), main(C_GG),
                  pl.BlockSpec((tt, LANE), lambda s, q: (s * ntiles + tmap(q), SMALL_BLK)),
                  pl.BlockSpec((None, 2, GW, GW), lambda s, q: (s, 0, 0, 0))] + [full(a) for a in consts],
        out_specs=[pl.BlockSpec((tt, GW), lambda s, q: (s * ntiles + jnp.where(q < ntiles, ntiles - 1, 2 * ntiles - 1 - q), 0)),
                   pl.BlockSpec((None, 2, GW, GW), lambda s, q: (s, 0, 0, 0))],
        out_shape=[jax.ShapeDtypeStruct((nseq * seq_len, GW), bf16),
                   jax.ShapeDtypeStruct((nseq, 2, GW, GW), f32)],
        scratch_shapes=[pltpu.VMEM((GW, GW), f32), pltpu.VMEM((seq_len, GW), f32)],
        compiler_params=_cparams(("arbitrary", "arbitrary")), name="gla",
    )(u, u, u, u, u, s0, *consts)


GDN_G = 8
HR = 8
NLOCK = 4
INV_HI_LEVELS = 4


def _gdn_prep_kernel(*refs, g, ngroups, colmajor):
    if ngroups > 1:
        (q_ref, k_ref, v_ref, sm_ref, qp, kp, vp, qn, kn, vnx, cw_ref, alog_ref, dtb_ref, e_ref, mbd_ref,
         tril_ref, triu_ref, u_out, wq_out, qkd_out, kd_out, egl_out) = refs
    else:
        (q_ref, k_ref, v_ref, sm_ref, cw_ref, alog_ref, dtb_ref, e_ref, mbd_ref,
         tril_ref, triu_ref, u_out, wq_out, qkd_out, kd_out, egl_out) = refs
        qp = kp = vp = qn = kn = vnx = None
    gi = pl.program_id(1)
    rowi, coli = _head_iotas()
    row_first = rowi == 0
    row_last = rowi == CH - 1
    icat = jnp.where(rowi == coli, 1.0, 0.0)

    if colmajor:
        get = lambda ref, j: ref[:, j, :]
        last_row = lambda ref, j: ref[CH - 1, pl.ds(j, 1), :]
        first_row = lambda ref, j: ref[0, pl.ds(j, 1), :]
    else:
        get = lambda ref, j: ref[j]
        last_row = lambda ref, j: ref[j, CH - 1:CH, :]
        first_row = lambda ref, j: ref[j, 0:1, :]

    def shared(jj):
        jm = jnp.maximum(jj - 1, 0)
        jp = jnp.minimum(jj + 1, g - 1)

        def conv(ref, pref, nref, c0):
            x = get(ref, jj)
            if pref is None:
                pr_out = 0.0
                nx_out = 0.0
            else:
                pr_out = jnp.where(gi > 0, pref[HR - 1, HR - 1:HR, :], 0.0)
                nx_out = jnp.where(gi < ngroups - 1, nref[0, 0:1, :], 0.0)
            pr = jnp.where(jj > 0, last_row(ref, jm), pr_out)
            nx = jnp.where(jj < g - 1, first_row(ref, jp), nx_out)
            xd = jnp.where(row_first, pr, pltpu.roll(x, 1, 0))
            xu = jnp.where(row_last, nx, pltpu.roll(x, CH - 1, 0))
            return _silu(xd * cw_ref[0:1, c0:c0 + GW] + x * cw_ref[1:2, c0:c0 + GW] + xu * cw_ref[2:3, c0:c0 + GW])

        mbd = mbd_ref[...]
        q = conv(q_ref, qp, qn, 0)
        k = conv(k_ref, kp, kn, GW)
        v = conv(v_ref, vp, vnx, 2 * GW)
        q = q * lax.rsqrt(_dot(q * q, mbd) + 1e-6) * (DH ** -0.5)
        k = k * lax.rsqrt(_dot(k * k, mbd) + 1e-6)
        ex = _dot01_right(get(sm_ref, jj), e_ref[...])
        beta = jax.nn.sigmoid(ex[:, :GW])
        kb = k * beta
        gm = _dot_nt(jnp.concatenate([kb, q], axis=0), _bd(k, mbd))
        return q, k, kb, v * beta, ex, gm

    def gates(sh, dirn):
        q, k, kb, vb, ex, gm = sh
        la = -jnp.exp(alog_ref[dirn]) * jax.nn.softplus(ex[:, (1 + dirn) * GW:(2 + dirn) * GW] + dtb_ref[dirn])
        if dirn == 0:
            tri, keep, strict, keep_t = tril_ref[...], rowi >= coli, rowi > coli, rowi <= coli
        else:
            tri, keep, strict, keep_t = triu_ref[...], rowi <= coli, rowi < coli, rowi >= coli
        gx = _dot01_left(tri, la)
        grow = jnp.sum(jnp.where(keep_t, la, 0.0), axis=0, keepdims=True)
        decay = jnp.where(keep, jnp.exp(jnp.where(keep, gx - grow, 0.0)), 0.0)
        glast = gx[CH - 1:CH, :] if dirn == 0 else gx[0:1, :]
        nm = jnp.where(strict, -(gm[:CH] * decay), 0.0)
        return gx, glast, nm, gm[CH:] * decay

    def body(it, carry):
        jjs = [it * NLOCK + c for c in range(NLOCK)]
        shs = [shared(jj) for jj in jjs]
        chains = [(c, dirn) for c in range(NLOCK) for dirn in range(2)]
        gs = [gates(shs[c], dirn) for c, dirn in chains]
        mbd = mbd_ref[...]
        nms = [gt[2] for gt in gs]
        cat = lambda x, y: jnp.concatenate([x, y], axis=0)
        nsp = [_split2(nm) for nm in nms]
        pws = [_dot3(sp, sp, mbd) for sp in nsp]
        tinvs = [icat + nm for nm in nms]
        for _ in range(INV_HI_LEVELS - 1):
            psp = [_split2(pw) for pw in pws]
            tsp = [_split2(ti) for ti in tinvs]
            rs = [_dot3((cat(th, ph), cat(tr, pr)), (ph, pr), mbd) for (th, tr), (ph, pr) in zip(tsp, psp)]
            tinvs = [ti + r[:CH] for ti, r in zip(tinvs, rs)]
            pws = [r[CH:] for r in rs]
        rs = [_dot(cat(ti, pw), _bd(pw, mbd)) for ti, pw in zip(tinvs, pws)]
        tinvs = [ti + r[:CH] for ti, r in zip(tinvs, rs)]
        tinvs = [ti + _dot(ti, _bd(r[CH:], mbd)) for ti, r in zip(tinvs, rs)]
        egs = [jnp.exp(gt[0]) for gt in gs]
        us = [_dot(ti, _bd(shs[c][3], mbd)) for ti, (c, dirn) in zip(tinvs, chains)]
        ws = [_dot(ti, _bd(shs[c][2] * eg, mbd)) for ti, eg, (c, dirn) in zip(tinvs, egs, chains)]
        for i, (c, dirn) in enumerate(chains):
            q, k = shs[c][0], shs[c][1]
            gx, glast, _, qkd = gs[i]
            jj = jjs[c]
            u_out[dirn, jj] = us[i]
            wq_out[dirn, jj] = jnp.concatenate([ws[i], q * egs[i]], axis=0).astype(bf16)
            qkd_out[dirn, jj] = qkd.astype(bf16)
            kd_out[dirn, jj] = (k * jnp.exp(glast - gx)).astype(bf16)
            egl_out[dirn, jj] = jnp.exp(glast)
        return carry

    lax.fori_loop(0, g // NLOCK, body, 0)


def _gdn_scan_kernel(uf, wqf, qkf, kdf, egf, ub, wqb, qkb, kdb, egb, s0_ref, mbd_ref, mbdf_ref,
                     of_out, ob_out, sfin_ref, s_scr, *, sg, cg, nt):
    t = pl.program_id(1)

    @pl.when(t == 0)
    def _():
        s_scr[...] = s0_ref[...]

    def body(c, carry):
        chains = [(sq, dirn) for sq in range(sg) for dirn in range(2)]
        src = lambda dirn: (uf, wqf, qkf, kdf, egf, of_out, c) if dirn == 0 else (ub, wqb, qkb, kdb, egb, ob_out, cg - 1 - c)
        mbd = mbd_ref[...]
        ss = [s_scr[sq, dirn] for sq, dirn in chains]
        wss = [jnp.dot(src(dirn)[1][sq, src(dirn)[6]], st.astype(bf16), preferred_element_type=f32)
               for st, (sq, dirn) in zip(ss, chains)]
        vns = [src(dirn)[0][sq, src(dirn)[6]] - ws[:CH] for ws, (sq, dirn) in zip(wss, chains)]
        os_ = [ws[CH:] + jnp.dot(src(dirn)[2][sq, src(dirn)[6]], _bd(vn, mbd), preferred_element_type=f32)
               for ws, vn, (sq, dirn) in zip(wss, vns, chains)]
        upd = [_dot_tn(src(dirn)[3][sq, src(dirn)[6]], vn) for vn, (sq, dirn) in zip(vns, chains)]
        for st, o, up, (sq, dirn) in zip(ss, os_, upd, chains):
            _, _, _, _, eg, o_out, cc = src(dirn)
            o_out[sq, cc] = o
            s_scr[sq, dirn] = st * eg[sq, cc] + mbdf_ref[...] * up
        return carry

    lax.fori_loop(0, cg, body, 0)

    @pl.when(t == nt - 1)
    def _():
        sfin_ref[...] = s_scr[...]


def _gdn_fin_kernel(of_ref, ob_ref, g_ref, ng_ref, mbd_ref, y_ref, *, g, colmajor):
    os_ = [of_ref[jj] + ob_ref[jj] for jj in range(g)]
    mss = [_dot(o * o, mbd_ref[...]) * (1.0 / DH) for o in os_]
    for jj in range(g):
        gate = g_ref[:, jj, :] if colmajor else g_ref[jj]
        y = os_[jj] * lax.rsqrt(mss[jj] + EPS) * ng_ref[...] * _silu(gate)
        if colmajor:
            y_ref[:, jj, :] = y
        else:
            y_ref[jj] = y


def _gdn_call(u, nseq, seq_len, s0, p, cst, colmajor):
    n = seq_len // CH
    if colmajor:
        assert seq_len == GRID_W * CH
        g, a = GDN_G, GRID_W
        blk = lambda w: (None, a, g, w)
        imap = lambda c: (lambda s, i: (s, 0, i, c))
    else:
        g, a = n, n
        blk = lambda w: (None, g, CH, w)
        imap = lambda c: (lambda s, i: (s, i, 0, c))
    ngroups = n // g
    u4 = u.reshape(nseq, a, CH, NU)
    main = lambda c: pl.BlockSpec(blk(GW), imap(c))
    full2 = lambda x: pl.BlockSpec(x.shape, lambda s, i: (0,) * x.ndim)
    halos, halo_specs = (), []
    if ngroups > 1:
        per = g // HR
        prev = lambda c: pl.BlockSpec((None, HR, HR, GW), lambda s, i: (s, a // HR - 1, jnp.maximum(i * per - 1, 0), c))
        nxt = lambda c: pl.BlockSpec((None, HR, HR, GW), lambda s, i: (s, 0, jnp.minimum((i + 1) * per, a // HR - 1), c))
        halo_specs = [prev(C_DQ), prev(C_DK), prev(C_DV), nxt(C_DQ), nxt(C_DK), nxt(C_DV)]
        halos = (u4,) * 6
    consts = (p['gdn_conv_w'], p['gdn_alog'], p['gdn_dtb'], cst['e_gdn'], cst['mbd'], cst['tril'], cst['triu'])
    per_chunk = lambda rows, dt: jax.ShapeDtypeStruct((nseq, 2, n, rows, GW), dt)
    per_chunk_spec = lambda rows: pl.BlockSpec((None, 2, g, rows, GW), lambda s, i: (s, 0, i, 0, 0))
    uu, wq, qkd, kd, egl = pl.pallas_call(
        functools.partial(_gdn_prep_kernel, g=g, ngroups=ngroups, colmajor=colmajor), grid=(nseq, ngroups),
        in_specs=[main(C_DQ), main(C_DK), main(C_DV), pl.BlockSpec(blk(LANE), imap(SMALL_BLK))] + halo_specs
                 + [full2(x) for x in consts],
        out_specs=[per_chunk_spec(CH), per_chunk_spec(2 * CH), per_chunk_spec(CH), per_chunk_spec(CH), per_chunk_spec(1)],
        out_shape=[per_chunk(CH, f32), per_chunk(2 * CH, bf16), per_chunk(CH, bf16), per_chunk(CH, bf16), per_chunk(1, f32)],
        compiler_params=_cparams(("arbitrary", "arbitrary")), name="gdn_prep",
    )(u4, u4, u4, u4, *halos, *consts)

    sg = math.gcd(nseq, 4)
    cg = min(n, 8)
    nt = n // cg
    fwd = lambda rows: pl.BlockSpec((sg, None, cg, rows, GW), lambda s, t: (s, 0, t, 0, 0))
    bwd = lambda rows: pl.BlockSpec((sg, None, cg, rows, GW), lambda s, t: (s, 1, nt - 1 - t, 0, 0))
    st_spec = pl.BlockSpec((sg, 2, GW, GW), lambda s, t: (s, 0, 0, 0))
    o_shape = jax.ShapeDtypeStruct((nseq, n, CH, GW), f32)
    o_f, o_b, sfin = pl.pallas_call(
        functools.partial(_gdn_scan_kernel, sg=sg, cg=cg, nt=nt), grid=(nseq // sg, nt),
        in_specs=[fwd(CH), fwd(2 * CH), fwd(CH), fwd(CH), fwd(1), bwd(CH), bwd(2 * CH), bwd(CH), bwd(CH), bwd(1),
                  st_spec, full2(cst['mbd']), full2(cst['mbdf'])],
        out_specs=[pl.BlockSpec((sg, cg, CH, GW), lambda s, t: (s, t, 0, 0)),
                   pl.BlockSpec((sg, cg, CH, GW), lambda s, t: (s, nt - 1 - t, 0, 0)), st_spec],
        out_shape=[o_shape, o_shape, jax.ShapeDtypeStruct((nseq, 2, GW, GW), f32)],
        scratch_shapes=[pltpu.VMEM((sg, 2, GW, GW), f32)],
        compiler_params=_cparams(("arbitrary", "arbitrary")), name="gdn_scan",
    )(uu, wq, qkd, kd, egl, uu, wq, qkd, kd, egl, s0, cst['mbd'], cst['mbdf'])

    o_spec = pl.BlockSpec((None, g, CH, GW), lambda s, i: (s, i, 0, 0))
    y = pl.pallas_call(
        functools.partial(_gdn_fin_kernel, g=g, colmajor=colmajor), grid=(nseq, ngroups),
        in_specs=[o_spec, o_spec, main(C_DGATE), full2(p['gdn_norm_g']), full2(cst['mbd'])],
        out_specs=pl.BlockSpec(blk(GW), imap(0)),
        out_shape=jax.ShapeDtypeStruct((nseq, a, CH, GW), f32),
        compiler_params=_cparams(("arbitrary", "arbitrary")), name="gdn_fin",
    )(o_f, o_b, u4, p['gdn_norm_g'], cst['mbd'])
    return y.reshape(nseq * seq_len, GW), sfin


def _out_kernel(a_ref, gl_ref, gd_ref, sc_ref, x_ref, mod_ref, w_ref, lg_ref, lb_ref, o_ref):
    m = mod_ref[...]
    mix_in = jnp.concatenate([a_ref[...], gl_ref[...], gd_ref[...].astype(bf16), sc_ref[...]], axis=1)
    mix = jnp.dot(mix_in, w_ref[...], preferred_element_type=f32)
    o_ref[...] = _ln(ALPHA * x_ref[...] + m[:, 2 * D:3 * D] * mix, lg_ref[...], lb_ref[...])


def _out_call(a, gl, gd, sc, x, mod, p, tiles_per_mod, tm=512):
    n = x.shape[0]
    part = pl.BlockSpec((tm, GW), lambda i: (i, 0))
    full = lambda t: pl.BlockSpec(t.shape, lambda i: (0,) * t.ndim)
    return pl.pallas_call(
        _out_kernel, grid=(n // tm,),
        in_specs=[part, part, part, part, pl.BlockSpec((tm, D), lambda i: (i, 0)),
                  pl.BlockSpec((None, 1, 6 * D), lambda i: (i // tiles_per_mod, 0, 0)),
                  full(p['w_out']), full(p['ln1_g']), full(p['ln1_b'])],
        out_specs=pl.BlockSpec((tm, D), lambda i: (i, 0)),
        out_shape=jax.ShapeDtypeStruct((n, D), f32),
        compiler_params=_cparams(("arbitrary",)), name="out_proj_ln",
    )(a, gl, gd, sc, x, mod, p['w_out'], p['ln1_g'], p['ln1_b'])


def _ffn_kernel(x_ref, mod_ref, wg_ref, wu_ref, wd_ref, lg_ref, lb_ref, o_ref, h_scr, acc_scr, *, ngroups):
    g = pl.program_id(1)

    @pl.when(g == 0)
    def _():
        m = mod_ref[...]
        h = x_ref[...] * (1.0 + m[:, 4 * D:5 * D]) + m[:, 3 * D:4 * D]
        h_scr[...] = h.astype(bf16)
        acc_scr[...] = jnp.zeros(acc_scr.shape, f32)

    hb = h_scr[...]
    t = _silu(jnp.dot(hb, wg_ref[...], preferred_element_type=f32)) * jnp.dot(hb, wu_ref[...], preferred_element_type=f32)
    acc_scr[...] += jnp.dot(t.astype(bf16), wd_ref[...], preferred_element_type=f32)

    @pl.when(g == ngroups - 1)
    def _():
        m = mod_ref[...]
        o_ref[...] = _ln(ALPHA * x_ref[...] + m[:, 5 * D:6 * D] * acc_scr[...], lg_ref[...], lb_ref[...])


def _ffn_call(x, mod, p, seq_len, tm=512, fw=D_FF_EXPERT):
    n = x.shape[0]
    ngroups = D_FF // fw
    tiles_per_mod = seq_len // tm if mod.shape[0] > 1 else n // tm
    full = lambda t: pl.BlockSpec(t.shape, lambda i, g: (0,) * t.ndim)
    return pl.pallas_call(
        functools.partial(_ffn_kernel, ngroups=ngroups), grid=(n // tm, ngroups),
        in_specs=[pl.BlockSpec((tm, D), lambda i, g: (i, 0)),
                  pl.BlockSpec((None, 1, 6 * D), lambda i, g: (i // tiles_per_mod, 0, 0)),
                  pl.BlockSpec((D, fw), lambda i, g: (0, g)),
                  pl.BlockSpec((D, fw), lambda i, g: (0, g)),
                  pl.BlockSpec((fw, D), lambda i, g: (g, 0)),
                  full(p['ln2_g']), full(p['ln2_b'])],
        out_specs=pl.BlockSpec((tm, D), lambda i, g: (i, 0)),
        out_shape=jax.ShapeDtypeStruct((n, D), f32),
        scratch_shapes=[pltpu.VMEM((tm, D), bf16), pltpu.VMEM((tm, D), f32)],
        compiler_params=_cparams(("arbitrary", "arbitrary")), name="dense_ffn",
    )(x, mod, p['w_gate'], p['w_up'], p['w_down'], p['ln2_g'], p['ln2_b'])


MOE_TM = 1024
MOE_SB = 128


def _moe_kernel(x_ref, mod_ref, wg_ref, wu_ref, wd_ref, lg_ref, lb_ref, rt_ref, tri_ref, o_ref,
                h_scr, acc_scr, comb_scr, pos_scr):
    e = pl.program_id(1)
    tm = x_ref.shape[0]

    @pl.when(e == 0)
    def _():
        m = mod_ref[...]
        h = x_ref[...] * (1.0 + m[:, 4 * D:5 * D]) + m[:, 3 * D:4 * D]
        h_scr[...] = h.astype(bf16)
        acc_scr[...] = jnp.zeros(acc_scr.shape, f32)
        hh, hr = _split2(h)
        rh, rr = _split2(rt_ref[...])
        d = lambda x, y: jnp.dot(x, y, preferred_element_type=f32)
        logits = d(hh, rh) + d(hr.astype(bf16), rh) + d(hh, rr.astype(bf16))
        lane = lax.broadcasted_iota(jnp.int32, logits.shape, 1).astype(f32)
        neg = jnp.float32(-jnp.inf)
        logits = jnp.where(lane < N_EXPERTS, logits, neg)
        m1 = jnp.max(logits, axis=-1, keepdims=True)
        i1 = jnp.min(jnp.where(logits == m1, lane, float(LANE)), axis=-1, keepdims=True)
        rest = jnp.where(lane == i1, neg, logits)
        m2 = jnp.max(rest, axis=-1, keepdims=True)
        i2 = jnp.min(jnp.where(rest == m2, lane, float(LANE)), axis=-1, keepdims=True)
        e2 = jnp.exp(m2 - m1)
        p1 = 1.0 / (1.0 + e2)
        p2 = e2 / (1.0 + e2)
        sel1 = lane == i1
        sel2 = lane == i2
        comb = jnp.where(sel1, p1, 0.0) + jnp.where(sel2, p2, 0.0)
        c_hi, c_mid, c_lo = _split3(comb)
        comb_scr[...] = jnp.where(lane < N_EXPERTS, c_hi.astype(f32),
                                  jnp.where(lane < 2 * N_EXPERTS, pltpu.roll(c_mid.astype(f32), N_EXPERTS, 1),
                                            pltpu.roll(c_lo.astype(f32), 2 * N_EXPERTS, 1))).astype(bf16)
        routed = jnp.where(sel1 | sel2, 1.0, 0.0).astype(bf16)
        incl = lax.dot_general(routed, tri_ref[...], (((0,), (0,)), ((), ())), preferred_element_type=f32)
        tok = lax.broadcasted_iota(jnp.int32, incl.shape, 1)
        excl = jnp.where(tok == 0, 0.0, pltpu.roll(incl, 1, 1))
        pos_scr[...] = jnp.where(incl > excl, excl, -1.0)

    pos_row = pos_scr[pl.ds(e, 1), :]
    nrows = (jnp.max(pos_row) + 1.0).astype(jnp.int32)

    def run_pass(base, sb):
        slot = lax.broadcasted_iota(jnp.int32, (sb, tm), 0).astype(f32) + base.astype(f32)
        lane_sb = lax.broadcasted_iota(jnp.int32, (sb, LANE), 1)
        sel = jnp.where(pos_row == slot, 1.0, 0.0).astype(bf16)
        xs = jnp.dot(sel, h_scr[...], preferred_element_type=f32).astype(bf16)
        t = _silu(jnp.dot(xs, wg_ref[...], preferred_element_type=f32)) * jnp.dot(xs, wu_ref[...], preferred_element_type=f32)
        y = jnp.dot(t.astype(bf16), wd_ref[...], preferred_element_type=f32)
        cw3 = jnp.dot(sel, comb_scr[...], preferred_element_type=f32)
        cw = jnp.sum(jnp.where((lane_sb & (N_EXPERTS - 1)) == e, cw3, 0.0), axis=-1, keepdims=True)
        acc_scr[...] += lax.dot_general(sel, (y * cw).astype(bf16), (((0,), (0,)), ((), ())), preferred_element_type=f32)

    big = 2 * MOE_SB
    nbig = (nrows + MOE_SB - 1) // big

    def big_pass(i, carry):
        run_pass(i * big, big)
        return carry

    lax.fori_loop(0, nbig, big_pass, 0)

    @pl.when(nrows > nbig * big)
    def _():
        run_pass(nbig * big, MOE_SB)

    @pl.when(e == N_EXPERTS - 1)
    def _():
        m = mod_ref[...]
        o_ref[...] = _ln(ALPHA * x_ref[...] + m[:, 5 * D:6 * D] * acc_scr[...], lg_ref[...], lb_ref[...])


def _moe_call(x, mod, p, seq_len, cst, tm=MOE_TM):
    n = x.shape[0]
    tm = min(tm, n)
    fw = D_FF_EXPERT
    tiles_per_mod = seq_len // tm if mod.shape[0] > 1 else n // tm
    full = lambda t: pl.BlockSpec(t.shape, lambda i, g: (0,) * t.ndim)
    tri = cst['tri_tok'][:tm, :tm]
    return pl.pallas_call(
        _moe_kernel, grid=(n // tm, N_EXPERTS),
        in_specs=[pl.BlockSpec((tm, D), lambda i, g: (i, 0)),
                  pl.BlockSpec((None, 1, 6 * D), lambda i, g: (i // tiles_per_mod, 0, 0)),
                  pl.BlockSpec((None, D, fw), lambda i, g: (g, 0, 0)),
                  pl.BlockSpec((None, D, fw), lambda i, g: (g, 0, 0)),
                  pl.BlockSpec((None, fw, D), lambda i, g: (g, 0, 0)),
                  full(p['ln2_g']), full(p['ln2_b']), full(p['router']), full(tri)],
        out_specs=pl.BlockSpec((tm, D), lambda i, g: (i, 0)),
        out_shape=jax.ShapeDtypeStruct((n, D), f32),
        scratch_shapes=[pltpu.VMEM((tm, D), bf16), pltpu.VMEM((tm, D), f32),
                        pltpu.VMEM((tm, LANE), bf16), pltpu.VMEM((LANE, tm), f32)],
        compiler_params=_cparams(("arbitrary", "arbitrary")), name="moe_ffn",
    )(x, mod, p['w_gate'], p['w_up'], p['w_down'], p['ln2_g'], p['ln2_b'], p['router'], tri)


def _constants():
    hb = np.arange(GW) // DH
    mbd = (hb[:, None] == hb[None, :]).astype(np.float32)
    r = np.arange(CH)
    t = np.arange(MOE_TM)
    e = np.zeros((LANE, 3 * GW), np.float32)
    for h in range(NH):
        e[L_BETA + h, h * DH:(h + 1) * DH] = 1.0
        for d in range(2):
            e[L_DEC + NH * d + h, (1 + d) * GW + h * DH:(1 + d) * GW + (h + 1) * DH] = 1.0
    return dict(mbd=jnp.asarray(mbd, bf16), mbdf=jnp.asarray(mbd, f32),
                tril=jnp.asarray(r[:, None] >= r[None, :], bf16), triu=jnp.asarray(r[:, None] <= r[None, :], bf16),
                e_gdn=jnp.asarray(e, bf16), tri_tok=jnp.asarray(t[:, None] <= t[None, :], bf16))


_IN_SIZES = (GW, GW, GW, GW, GW, GW, 2 * GLA_LR, GW, GW, GW, GW, NH, 2 * NH, GW, GW, GW)


def _pack_w_in(w_in):
    offs = np.concatenate([[0], np.cumsum(_IN_SIZES)])
    big = [i for i, s in enumerate(_IN_SIZES) if s == GW]
    small = [i for i, s in enumerate(_IN_SIZES) if s != GW]
    parts = [w_in[..., int(offs[i]):int(offs[i + 1])] for i in big + small]
    pad = NU - int(offs[-1])
    parts.append(jnp.zeros(w_in.shape[:-1] + (pad,), w_in.dtype))
    return jnp.concatenate(parts, axis=-1).astype(bf16)


def _to_bd(s, transpose):
    if transpose:
        s = jnp.swapaxes(s, -1, -2)
    b = s.shape[0]
    z = jnp.zeros_like(s)
    rows = [jnp.concatenate([s[:, :, h] if g == h else z[:, :, h] for g in range(NH)], axis=-1) for h in range(NH)]
    return jnp.concatenate(rows, axis=-2).reshape(b, 2, GW, GW)


def _from_bd(s, transpose):
    out = jnp.stack([s[:, :, h * DH:(h + 1) * DH, h * DH:(h + 1) * DH] for h in range(NH)], axis=2)
    return jnp.swapaxes(out, -1, -2) if transpose else out


def _stream_layer(x, mod, nseq, seq_len, s0_gla, s0_gdn, p, cst, colmajor, moe):
    tpm_512 = max(seq_len // 512, 1) if mod.shape[0] > 1 else x.shape[0] // 512
    u = _in_call(x, mod, p['w_in'], tpm_512)
    a, ysc = _conv_call(u, seq_len, p)
    ygla, sgla = _gla_call(u, nseq, seq_len, s0_gla, p, cst)
    ygdn, sgdn = _gdn_call(u, nseq, seq_len, s0_gdn, p, cst, colmajor)
    x1 = _out_call(a, ygla, ygdn, ysc, x, mod, p, tpm_512)
    x2 = _moe_call(x1, mod, p['ffn'], seq_len, cst) if moe else _ffn_call(x1, mod, p['ffn'], seq_len)
    return x2, sgla, sgdn


def kernel(x_prompt, x_sample, c, state_gla, state_gdn, c_ctx, w_ada, b_ada, w_in, w_out, conv_w, conv_b, conv_ln_g, conv_ln_b, conv_pw, gla_w_lr, gla_b_lr, gla_norm_g, gdn_conv_w, gdn_a_log, gdn_dt_bias, gdn_norm_g, sc_conv_w, ln1_g, ln1_b, ln2_g, ln2_b, ffn_w_gate, ffn_w_up, ffn_w_down, moe_router, moe_w_gate, moe_w_up, moe_w_down):
    nb, seq, _ = x_prompt.shape
    db, dseq, _ = x_sample.shape
    depth = w_in.shape[0]
    cst = _constants()

    rows = 16
    cond = jnp.concatenate([c_ctx[None, :], c, jnp.zeros((rows - 1 - db, D), f32)], axis=0)
    mod = _mod_call(cond, w_ada, b_ada)

    w_in_p = _pack_w_in(w_in)
    w_out_b = w_out.astype(bf16)
    conv_pw_b = conv_pw.astype(bf16)
    wlr = jnp.zeros((depth, 2, LANE, GW), f32)
    wlr = wlr.at[:, 0, 0:GLA_LR].set(gla_w_lr[:, 0]).at[:, 1, GLA_LR:2 * GLA_LR].set(gla_w_lr[:, 1]).astype(bf16)
    rep = lambda t: jnp.repeat(t, DH, axis=-1)[:, :, None, :]
    row = lambda t: t[:, None, :]
    ffn_g, ffn_u, ffn_d = ffn_w_gate.astype(bf16), ffn_w_up.astype(bf16), ffn_w_down.astype(bf16)
    moe_g, moe_u, moe_d = moe_w_gate.astype(bf16), moe_w_up.astype(bf16), moe_w_down.astype(bf16)
    router = jnp.pad(moe_router, ((0, 0), (0, 0), (0, LANE - N_EXPERTS)))

    zero_bd = jnp.zeros((nb, 2, GW, GW), f32)
    y_p = x_prompt.reshape(nb * seq, D)
    y_s = x_sample.reshape(db * dseq, D)
    gla_states, gdn_states = [], []
    for l in range(depth):
        j = l // 2
        moe = l % 2 == 1
        if moe:
            ffn = dict(w_gate=moe_g[j], w_up=moe_u[j], w_down=moe_d[j], router=router[j],
                       ln2_g=row(ln2_g)[l], ln2_b=row(ln2_b)[l])
        else:
            ffn = dict(w_gate=ffn_g[j], w_up=ffn_u[j], w_down=ffn_d[j], ln2_g=row(ln2_g)[l], ln2_b=row(ln2_b)[l])
        p = dict(w_in=w_in_p[l], w_out=w_out_b[l], conv_w=conv_w[l], conv_b=row(conv_b)[l],
                 conv_ln_g=row(conv_ln_g)[l], conv_ln_b=row(conv_ln_b)[l], conv_pw=conv_pw_b[l],
                 sc_conv_w=sc_conv_w[l], gla_wlr=wlr[l], gla_blr=gla_b_lr[l][:, None, :],
                 gla_norm_g=row(gla_norm_g)[l], gdn_conv_w=gdn_conv_w[l], gdn_alog=rep(gdn_a_log)[l],
                 gdn_dtb=rep(gdn_dt_bias)[l], gdn_norm_g=row(gdn_norm_g)[l],
                 ln1_g=row(ln1_g)[l], ln1_b=row(ln1_b)[l], ffn=ffn)
        mod_ctx = mod[l, 0:1][:, None, :]
        mod_lat = mod[l, 1:1 + db][:, None, :]
        y_p, s_gla, s_gdn = _stream_layer(y_p, mod_ctx, nb, seq, zero_bd, zero_bd, p, cst, False, moe)
        gla_states.append(_from_bd(s_gla, True))
        gdn_states.append(_from_bd(s_gdn, False))
        y_s, _, _ = _stream_layer(y_s, mod_lat, db, dseq, _to_bd(state_gla[:, l], True),
                                  _to_bd(state_gdn[:, l], False), p, cst, True, moe)
    return (y_p.reshape(nb, seq, D), y_s.reshape(db, dseq, D),
            jnp.stack(gla_states, axis=1), jnp.stack(gdn_states, axis=1))
```

```python
import functools
import math

import numpy as np
import jax
import jax.numpy as jnp
from jax import lax
from jax.experimental import pallas as pl
from jax.experimental.pallas import tpu as pltpu

f32 = jnp.float32
bf16 = jnp.bfloat16

D = 1024
GW = 256
NH = 4
DH = 64
CH = 64
GRID_W = 64
CONV_K = 31
GLA_LR = 16
GLA_TAU = 16.0
DEPTH = 4
N_EXPERTS = 8
D_FF = 2816
D_FF_EXPERT = 1408
ALPHA = (2 * DEPTH) ** 0.25
EPS = 1e-5
NU = 13 * GW + 128
SMALL_BLK = 13 * GW // 128
LANE = 128
C_AVAL, C_AGATE, C_GQ, C_GK, C_GV, C_GG, C_DQ, C_DK, C_DV, C_DGATE, C_SB, C_SC, C_SX = range(13)
L_BETA = 2 * GLA_LR
L_DEC = L_BETA + NH

VMEM_LIMIT = 56 * 1024 * 1024


def _cparams(sem):
    return pltpu.CompilerParams(dimension_semantics=sem, vmem_limit_bytes=VMEM_LIMIT)


def _dot(a, b):
    return jnp.dot(a.astype(bf16), b.astype(bf16), preferred_element_type=f32)


def _dot_nt(a, b):
    return lax.dot_general(a.astype(bf16), b.astype(bf16), (((1,), (1,)), ((), ())),
                           preferred_element_type=f32)


def _dot_tn(a, b):
    return lax.dot_general(a.astype(bf16), b.astype(bf16), (((0,), (0,)), ((), ())),
                           preferred_element_type=f32)


def _split2(x):
    hi = x.astype(bf16)
    return hi, x - hi.astype(f32)


def _split3(x):
    hi = x.astype(bf16)
    r = x - hi.astype(f32)
    mid = r.astype(bf16)
    lo = (r - mid.astype(f32)).astype(bf16)
    return hi, mid, lo


def _dot01_left(m01, x):
    hi, r = _split2(x)
    d = lambda t: jnp.dot(m01, t, preferred_element_type=f32)
    return d(hi) + d(r.astype(bf16))


def _dot01_right(x, m01):
    hi, r = _split2(x)
    d = lambda t: jnp.dot(t, m01, preferred_element_type=f32)
    return d(hi) + d(r.astype(bf16))


def _bd(x, mask):
    xb = x.astype(bf16)
    return jnp.concatenate([xb, xb, xb, xb], axis=0) * mask


def _dot3(a_parts, b_parts, mask):
    ah, ar = a_parts
    bh, br = b_parts
    al = ar.astype(bf16)
    bdh = _bd(bh, mask)
    bdl = _bd(br, mask)
    d = lambda x, y: jnp.dot(x, y, preferred_element_type=f32)
    return d(ah, bdh) + d(al, bdh) + d(ah, bdl)


def _silu(x):
    return x * jax.nn.sigmoid(x)


def _ln(x, g, b):
    xc = x - jnp.mean(x, axis=-1, keepdims=True)
    var = jnp.mean(xc * xc, axis=-1, keepdims=True)
    return xc * lax.rsqrt(var + EPS) * g + b


def _head_iotas():
    rowi = lax.broadcasted_iota(jnp.int32, (CH, GW), 0)
    coli = lax.broadcasted_iota(jnp.int32, (CH, GW), 1) & (DH - 1)
    return rowi, coli


def _mod_kernel(c_ref, w_ref, b_ref, o_ref):
    o_ref[...] = jnp.dot(_silu(c_ref[...]), w_ref[...], precision=lax.Precision.HIGHEST,
                         preferred_element_type=f32) + b_ref[...]


def _mod_call(cond, w_ada, b_ada):
    nl = w_ada.shape[0]
    rows = cond.shape[0]
    return pl.pallas_call(
        _mod_kernel, grid=(nl, 6),
        in_specs=[pl.BlockSpec((rows, D), lambda l, n: (0, 0)),
                  pl.BlockSpec((None, D, D), lambda l, n: (l, 0, n)),
                  pl.BlockSpec((None, 1, D), lambda l, n: (l, 0, n))],
        out_specs=pl.BlockSpec((None, rows, D), lambda l, n: (l, 0, n)),
        out_shape=jax.ShapeDtypeStruct((nl, rows, 6 * D), f32),
        compiler_params=_cparams(("arbitrary", "arbitrary")), name="adaln_mod",
    )(cond, w_ada, b_ada.reshape(nl, 1, 6 * D))


def _in_kernel(x_ref, mod_ref, w_ref, u_ref):
    m = mod_ref[...]
    h = x_ref[...] * (1.0 + m[:, D:2 * D]) + m[:, 0:D]
    u_ref[...] = jnp.dot(h.astype(bf16), w_ref[...], preferred_element_type=f32)


def _in_call(x, mod, w, li, tiles_per_mod, tm=512):
    n = x.shape[0]
    return pl.pallas_call(
        _in_kernel, grid=(n // tm,),
        in_specs=[pl.BlockSpec((tm, D), lambda i: (i, 0)),
                  pl.BlockSpec((None, 1, 6 * D), lambda i: (i // tiles_per_mod, 0, 0)),
                  pl.BlockSpec((None, D, NU), lambda i: (li, 0, 0))],
        out_specs=pl.BlockSpec((tm, NU), lambda i: (i, 0)),
        out_shape=jax.ShapeDtypeStruct((n, NU), f32),
        compiler_params=_cparams(("arbitrary",)), name="in_proj",
    )(x, mod, w)


CT = 256
HALO = 16
SHALO = 8


def _conv_kernel(av, ag, avp, agp, avn, agn, sb, sc, sx, scp, sxp, scn, sxn,
                 cw, cb, lng, lnb, pw, scw, a_out, ysc_out, pad_a, pad_s, win_scr, *, tiles_per_seq):
    i = pl.program_id(0)
    j = i % tiles_per_seq
    first = j == 0
    last = j == tiles_per_seq - 1
    glu = lambda v, g: v * jax.nn.sigmoid(g)
    pad_a[0:HALO, :] = jnp.where(first, 0.0, glu(avp[...], agp[...]))
    pad_a[HALO:HALO + CT, :] = glu(av[...], ag[...])
    pad_a[HALO + CT:2 * HALO + CT, :] = jnp.where(last, 0.0, glu(avn[...], agn[...]))
    acc = jnp.zeros((CT, GW), f32)
    base = HALO - CONV_K // 2
    span = 8 * ((CONV_K + 6) // 8)
    for ph in range(8):
        win_scr[...] = pad_a[ph:ph + CT + span - 8, :]
        for al in range(0, span, 8):
            k = al + ph - base
            if 0 <= k < CONV_K:
                acc = acc + win_scr[al:al + CT, :] * cw[k:k + 1, :]
    a = _ln(acc + cb[...], lng[...], lnb[...])
    a_out[...] = _dot(_silu(a), pw[...]).astype(bf16)

    pad_s[0:SHALO, :] = jnp.where(first, 0.0, scp[...] * sxp[...])
    pad_s[SHALO:SHALO + CT, :] = sc[...] * sx[...]
    pad_s[SHALO + CT:2 * SHALO + CT, :] = jnp.where(last, 0.0, scn[...] * sxn[...])
    acc = jnp.zeros((CT, GW), f32)
    for k in range(3):
        o = SHALO - 1 + k
        acc = acc + pad_s[o:o + CT, :] * scw[k:k + 1, :]
    ysc_out[...] = (sb[...] * acc).astype(bf16)


def _conv_call(u, seq_len, p):
    n = u.shape[0]
    tps = seq_len // CT
    nt = n // CT
    main = lambda c: pl.BlockSpec((CT, GW), lambda i: (i, c))

    def halo(c, rows, nxt):
        per = CT // rows
        if nxt:
            return pl.BlockSpec((rows, GW), lambda i: (jnp.minimum((i + 1) * per, nt * per - 1), c))
        return pl.BlockSpec((rows, GW), lambda i: (jnp.maximum(i * per - 1, 0), c))

    full = lambda a: pl.BlockSpec(a.shape, lambda i: (0,) * a.ndim)
    consts = (p['conv_w'], p['conv_b'], p['conv_ln_g'], p['conv_ln_b'], p['conv_pw'], p['sc_conv_w'])
    in_specs = [main(C_AVAL), main(C_AGATE), halo(C_AVAL, HALO, 0), halo(C_AGATE, HALO, 0),
                halo(C_AVAL, HALO, 1), halo(C_AGATE, HALO, 1),
                main(C_SB), main(C_SC), main(C_SX), halo(C_SC, SHALO, 0), halo(C_SX, SHALO, 0),
                halo(C_SC, SHALO, 1), halo(C_SX, SHALO, 1)] + [full(a) for a in consts]
    return pl.pallas_call(
        functools.partial(_conv_kernel, tiles_per_seq=tps), grid=(nt,),
        in_specs=in_specs,
        out_specs=[pl.BlockSpec((CT, GW), lambda i: (i, 0))] * 2,
        out_shape=[jax.ShapeDtypeStruct((n, GW), bf16)] * 2,
        scratch_shapes=[pltpu.VMEM((CT + 2 * HALO, GW), f32), pltpu.VMEM((CT + 2 * SHALO, GW), f32),
                        pltpu.VMEM((CT + 8 * ((CONV_K + 6) // 8) - 8, GW), f32)],
        compiler_params=_cparams(("arbitrary",)), name="conv_mixers",
    )(*([u] * 13), *consts)


GLA_LOCK = 8


def _gla_kernel(q_ref, k_ref, v_ref, g_ref, sm_ref, s0_ref, wlr_ref, blr_ref, ng_ref,
                mbd_ref, mbdf_ref, tril_ref, triu_ref, y_ref, sfin_ref, st_scr, of_scr, *, tt, ntiles):
    p = pl.program_id(1)
    nch = tt // CH
    nlock = min(GLA_LOCK, nch)
    rowi, coli = _head_iotas()

    @pl.when(p == 0)
    def _():
        st_scr[...] = s0_ref[0]

    @pl.when(p == ntiles)
    def _():
        sfin_ref[0] = st_scr[...]
        st_scr[...] = s0_ref[1]

    def run(dirn):
        tri = tril_ref[...] if dirn == 0 else triu_ref[...]
        keep = (rowi >= coli) if dirn == 0 else (rowi <= coli)
        wlr = wlr_ref[dirn]
        blr = blr_ref[dirn]
        tile = p if dirn == 0 else 2 * ntiles - 1 - p

        def body(it, carry):
            cis = [it * nlock + c for c in range(nlock)]
            if dirn == 1:
                cis = [nch - 1 - ci for ci in cis]
            rows = [pl.ds(pl.multiple_of(ci * CH, CH), CH) for ci in cis]
            grows = [pl.ds(pl.multiple_of(tile * tt + ci * CH, CH), CH) for ci in cis]
            mbd = mbd_ref[...]
            zs = [jnp.dot(sm_ref[r, :].astype(bf16), wlr, preferred_element_type=f32) + blr for r in rows]
            cums = [_dot01_left(tri, jax.nn.log_sigmoid(z) * (1.0 / GLA_TAU)) for z in zs]
            tots = [cum[CH - 1:CH, :] if dirn == 0 else cum[0:1, :] for cum in cums]
            vs = [v_ref[r, :] for r in rows]
            qes = [q_ref[r, :] * (DH ** -0.5) * jnp.exp(cum) for r, cum in zip(rows, cums)]
            kes = [k_ref[r, :] * jnp.exp(-cum) for r, cum in zip(rows, cums)]
            kds = [k_ref[r, :] * jnp.exp(tot - cum) for r, cum, tot in zip(rows, cums, tots)]
            atts = [jnp.where(keep, _dot_nt(qe, _bd(ke, mbd)), 0.0) for qe, ke in zip(qes, kes)]
            ols = [_dot(att, _bd(v, mbd)) for att, v in zip(atts, vs)]
            upds = [_dot_tn(v, kd) for v, kd in zip(vs, kds)]
            st = st_scr[...]
            os_ = []
            for c in range(nlock):
                os_.append(ols[c] + _dot_nt(qes[c], st))
                st = st * jnp.exp(tots[c]) + mbdf_ref[...] * upds[c]
            st_scr[...] = st
            if dirn == 0:
                for c in range(nlock):
                    of_scr[grows[c], :] = os_[c]
            else:
                os_ = [of_scr[grows[c], :] + os_[c] for c in range(nlock)]
                mss = [_dot(o * o, mbd) * (1.0 / DH) for o in os_]
                for c in range(nlock):
                    y = os_[c] * lax.rsqrt(mss[c] + EPS) * ng_ref[...] * _silu(g_ref[rows[c], :])
                    y_ref[rows[c], :] = y.astype(bf16)
            return carry

        lax.fori_loop(0, nch // nlock, body, 0)

    @pl.when(p < ntiles)
    def _():
        run(0)

    @pl.when(p >= ntiles)
    def _():
        run(1)

    @pl.when(p == 2 * ntiles - 1)
    def _():
        sfin_ref[1] = st_scr[...]


def _gla_call(u, nseq, seq_len, s0, p, cst):
    tt = min(seq_len, 512)
    ntiles = seq_len // tt
    tmap = lambda q: jnp.where(q < ntiles, q, 2 * ntiles - 1 - q)
    main = lambda c: pl.BlockSpec((tt, GW), lambda s, q: (s * ntiles + tmap(q), c))
    full = lambda a: pl.BlockSpec(a.shape, lambda s, q: (0,) * a.ndim)
    consts = (p['gla_wlr'], p['gla_blr'], p['gla_norm_g'], cst['mbd'], cst['mbdf'], cst['tril'], cst['triu'])
    return pl.pallas_call(
        functools.partial(_gla_kernel, tt=tt, ntiles=ntiles), grid=(nseq, 2 * ntiles),
        in_specs=[main(C_GQ), main(C_GK), main(C_GV), main(C_GG),
                  pl.BlockSpec((tt, LANE), lambda s, q: (s * ntiles + tmap(q), SMALL_BLK)),
                  pl.BlockSpec((None, 2, GW, GW), lambda s, q: (s, 0, 0, 0))] + [full(a) for a in consts],
        out_specs=[pl.BlockSpec((tt, GW), lambda s, q: (s * ntiles + jnp.where(q < ntiles, ntiles - 1, 2 * ntiles - 1 - q), 0)),
                   pl.BlockSpec((None, 2, GW, GW), lambda s, q: (s, 0, 0, 0))],
        out_shape=[jax.ShapeDtypeStruct((nseq * seq_len, GW), bf16),
                   jax.ShapeDtypeStruct((nseq, 2, GW, GW), f32)],
        scratch_shapes=[pltpu.VMEM((GW, GW), f32), pltpu.VMEM((seq_len, GW), f32)],
        compiler_params=_cparams(("arbitrary", "arbitrary")), name="gla",
    )(u, u, u, u, u, s0, *consts)


GDN_G = 8
HR = 8
NLOCK = 4
INV_HI_LEVELS = 4


def _gdn_prep_kernel(*refs, g, ngroups, colmajor):
    if ngroups > 1:
        (q_ref, k_ref, v_ref, sm_ref, qp, kp, vp, qn, kn, vnx, cw_ref, alog_ref, dtb_ref, e_ref, mbd_ref,
         tril_ref, triu_ref, u_out, wq_out, qkd_out, kd_out, egl_out) = refs
    else:
        (q_ref, k_ref, v_ref, sm_ref, cw_ref, alog_ref, dtb_ref, e_ref, mbd_ref,
         tril_ref, triu_ref, u_out, wq_out, qkd_out, kd_out, egl_out) = refs
        qp = kp = vp = qn = kn = vnx = None
    gi = pl.program_id(1)
    rowi, coli = _head_iotas()
    row_first = rowi == 0
    row_last = rowi == CH - 1
    icat = jnp.where(rowi == coli, 1.0, 0.0)

    if colmajor:
        get = lambda ref, j: ref[:, j, :]
        last_row = lambda ref, j: ref[CH - 1, pl.ds(j, 1), :]
        first_row = lambda ref, j: ref[0, pl.ds(j, 1), :]
    else:
        get = lambda ref, j: ref[j]
        last_row = lambda ref, j: ref[j, CH - 1:CH, :]
        first_row = lambda ref, j: ref[j, 0:1, :]

    def shared(jj):
        jm = jnp.maximum(jj - 1, 0)
        jp = jnp.minimum(jj + 1, g - 1)

        def conv(ref, pref, nref, c0):
            x = get(ref, jj)
            if pref is None:
                pr_out = 0.0
                nx_out = 0.0
            else:
                pr_out = jnp.where(gi > 0, pref[HR - 1, HR - 1:HR, :], 0.0)
                nx_out = jnp.where(gi < ngroups - 1, nref[0, 0:1, :], 0.0)
            pr = jnp.where(jj > 0, last_row(ref, jm), pr_out)
            nx = jnp.where(jj < g - 1, first_row(ref, jp), nx_out)
            xd = jnp.where(row_first, pr, pltpu.roll(x, 1, 0))
            xu = jnp.where(row_last, nx, pltpu.roll(x, CH - 1, 0))
            return _silu(xd * cw_ref[0:1, c0:c0 + GW] + x * cw_ref[1:2, c0:c0 + GW] + xu * cw_ref[2:3, c0:c0 + GW])

        mbd = mbd_ref[...]
        q = conv(q_ref, qp, qn, 0)
        k = conv(k_ref, kp, kn, GW)
        v = conv(v_ref, vp, vnx, 2 * GW)
        q = q * lax.rsqrt(_dot(q * q, mbd) + 1e-6) * (DH ** -0.5)
        k = k * lax.rsqrt(_dot(k * k, mbd) + 1e-6)
        ex = _dot01_right(get(sm_ref, jj), e_ref[...])
        beta = jax.nn.sigmoid(ex[:, :GW])
        kb = k * beta
        gm = _dot_nt(jnp.concatenate([kb, q], axis=0), _bd(k, mbd))
        return q, k, kb, v * beta, ex, gm

    def gates(sh, dirn):
        q, k, kb, vb, ex, gm = sh
        la = -jnp.exp(alog_ref[dirn]) * jax.nn.softplus(ex[:, (1 + dirn) * GW:(2 + dirn) * GW] + dtb_ref[dirn])
        if dirn == 0:
            tri, keep, strict, keep_t = tril_ref[...], rowi >= coli, rowi > coli, rowi <= coli
        else:
            tri, keep, strict, keep_t = triu_ref[...], rowi <= coli, rowi < coli, rowi >= coli
        gx = _dot01_left(tri, la)
        grow = jnp.sum(jnp.where(keep_t, la, 0.0), axis=0, keepdims=True)
        decay = jnp.where(keep, jnp.exp(jnp.where(keep, gx - grow, 0.0)), 0.0)
        glast = gx[CH - 1:CH, :] if dirn == 0 else gx[0:1, :]
        nm = jnp.where(strict, -(gm[:CH] * decay), 0.0)
        return gx, glast, nm, gm[CH:] * decay

    def body(it, carry):
        jjs = [it * NLOCK + c for c in range(NLOCK)]
        shs = [shared(jj) for jj in jjs]
        chains = [(c, dirn) for c in range(NLOCK) for dirn in range(2)]
        gs = [gates(shs[c], dirn) for c, dirn in chains]
        mbd = mbd_ref[...]
        nms = [gt[2] for gt in gs]
        cat = lambda x, y: jnp.concatenate([x, y], axis=0)
        nsp = [_split2(nm) for nm in nms]
        pws = [_dot3(sp, sp, mbd) for sp in nsp]
        tinvs = [icat + nm for nm in nms]
        for _ in range(INV_HI_LEVELS - 1):
            psp = [_split2(pw) for pw in pws]
            tsp = [_split2(ti) for ti in tinvs]
            rs = [_dot3((cat(th, ph), cat(tr, pr)), (ph, pr), mbd) for (th, tr), (ph, pr) in zip(tsp, psp)]
            tinvs = [ti + r[:CH] for ti, r in zip(tinvs, rs)]
            pws = [r[CH:] for r in rs]
        rs = [_dot(cat(ti, pw), _bd(pw, mbd)) for ti, pw in zip(tinvs, pws)]
        tinvs = [ti + r[:CH] for ti, r in zip(tinvs, rs)]
        tinvs = [ti + _dot(ti, _bd(r[CH:], mbd)) for ti, r in zip(tinvs, rs)]
        egs = [jnp.exp(gt[0]) for gt in gs]
        us = [_dot(ti, _bd(shs[c][3], mbd)) for ti, (c, dirn) in zip(tinvs, chains)]
        ws = [_dot(ti, _bd(shs[c][2] * eg, mbd)) for ti, eg, (c, dirn) in zip(tinvs, egs, chains)]
        for i, (c, dirn) in enumerate(chains):
            q, k = shs[c][0], shs[c][1]
            gx, glast, _, qkd = gs[i]
            jj = jjs[c]
            u_out[dirn, jj] = us[i]
            wq_out[dirn, jj] = jnp.concatenate([ws[i], q * egs[i]], axis=0).astype(bf16)
            qkd_out[dirn, jj] = qkd.astype(bf16)
            kd_out[dirn, jj] = (k * jnp.exp(glast - gx)).astype(bf16)
            egl_out[dirn, jj] = jnp.exp(glast)
        return carry

    lax.fori_loop(0, g // NLOCK, body, 0)


def _gdn_scan_kernel(uf, wqf, qkf, kdf, egf, ub, wqb, qkb, kdb, egb, s0_ref, mbd_ref, mbdf_ref,
                     of_out, ob_out, sfin_ref, s_scr, *, sg, cg, nt):
    t = pl.program_id(1)

    @pl.when(t == 0)
    def _():
        s_scr[...] = s0_ref[...]

    def body(c, carry):
        chains = [(sq, dirn) for sq in range(sg) for dirn in range(2)]
        src = lambda dirn: (uf, wqf, qkf, kdf, egf, of_out, c) if dirn == 0 else (ub, wqb, qkb, kdb, egb, ob_out, cg - 1 - c)
        mbd = mbd_ref[...]
        ss = [s_scr[sq, dirn] for sq, dirn in chains]
        wss = [jnp.dot(src(dirn)[1][sq, src(dirn)[6]], st.astype(bf16), preferred_element_type=f32)
               for st, (sq, dirn) in zip(ss, chains)]
        vns = [src(dirn)[0][sq, src(dirn)[6]] - ws[:CH] for ws, (sq, dirn) in zip(wss, chains)]
        os_ = [ws[CH:] + jnp.dot(src(dirn)[2][sq, src(dirn)[6]], _bd(vn, mbd), preferred_element_type=f32)
               for ws, vn, (sq, dirn) in zip(wss, vns, chains)]
        upd = [_dot_tn(src(dirn)[3][sq, src(dirn)[6]], vn) for vn, (sq, dirn) in zip(vns, chains)]
        for st, o, up, (sq, dirn) in zip(ss, os_, upd, chains):
            _, _, _, _, eg, o_out, cc = src(dirn)
            o_out[sq, cc] = o
            s_scr[sq, dirn] = st * eg[sq, cc] + mbdf_ref[...] * up
        return carry

    lax.fori_loop(0, cg, body, 0)

    @pl.when(t == nt - 1)
    def _():
        sfin_ref[...] = s_scr[...]


def _gdn_call(u, nseq, seq_len, s0, p, cst, colmajor):
    n = seq_len // CH
    if colmajor:
        assert seq_len == GRID_W * CH
        g, a = GDN_G, GRID_W
        blk = lambda w: (None, a, g, w)
        imap = lambda c: (lambda s, i: (s, 0, i, c))
    else:
        g, a = n, n
        blk = lambda w: (None, g, CH, w)
        imap = lambda c: (lambda s, i: (s, i, 0, c))
    ngroups = n // g
    u4 = u.reshape(nseq, a, CH, NU)
    main = lambda c: pl.BlockSpec(blk(GW), imap(c))
    full2 = lambda x: pl.BlockSpec(x.shape, lambda s, i: (0,) * x.ndim)
    halos, halo_specs = (), []
    if ngroups > 1:
        per = g // HR
        prev = lambda c: pl.BlockSpec((None, HR, HR, GW), lambda s, i: (s, a // HR - 1, jnp.maximum(i * per - 1, 0), c))
        nxt = lambda c: pl.BlockSpec((None, HR, HR, GW), lambda s, i: (s, 0, jnp.minimum((i + 1) * per, a // HR - 1), c))
        halo_specs = [prev(C_DQ), prev(C_DK), prev(C_DV), nxt(C_DQ), nxt(C_DK), nxt(C_DV)]
        halos = (u4,) * 6
    consts = (p['gdn_conv_w'], p['gdn_alog'], p['gdn_dtb'], cst['e_gdn'], cst['mbd'], cst['tril'], cst['triu'])
    per_chunk = lambda rows, dt: jax.ShapeDtypeStruct((nseq, 2, n, rows, GW), dt)
    per_chunk_spec = lambda rows: pl.BlockSpec((None, 2, g, rows, GW), lambda s, i: (s, 0, i, 0, 0))
    uu, wq, qkd, kd, egl = pl.pallas_call(
        functools.partial(_gdn_prep_kernel, g=g, ngroups=ngroups, colmajor=colmajor), grid=(nseq, ngroups),
        in_specs=[main(C_DQ), main(C_DK), main(C_DV), pl.BlockSpec(blk(LANE), imap(SMALL_BLK))] + halo_specs
                 + [full2(x) for x in consts],
        out_specs=[per_chunk_spec(CH), per_chunk_spec(2 * CH), per_chunk_spec(CH), per_chunk_spec(CH), per_chunk_spec(1)],
        out_shape=[per_chunk(CH, f32), per_chunk(2 * CH, bf16), per_chunk(CH, bf16), per_chunk(CH, bf16), per_chunk(1, f32)],
        compiler_params=_cparams(("arbitrary", "arbitrary")), name="gdn_prep",
    )(u4, u4, u4, u4, *halos, *consts)

    sg = math.gcd(nseq, 4)
    cg = min(n, 8)
    nt = n // cg
    fwd = lambda rows: pl.BlockSpec((sg, None, cg, rows, GW), lambda s, t: (s, 0, t, 0, 0))
    bwd = lambda rows: pl.BlockSpec((sg, None, cg, rows, GW), lambda s, t: (s, 1, nt - 1 - t, 0, 0))
    st_spec = pl.BlockSpec((sg, 2, GW, GW), lambda s, t: (s, 0, 0, 0))
    o_shape = jax.ShapeDtypeStruct((nseq, n, CH, GW), f32)
    o_f, o_b, sfin = pl.pallas_call(
        functools.partial(_gdn_scan_kernel, sg=sg, cg=cg, nt=nt), grid=(nseq // sg, nt),
        in_specs=[fwd(CH), fwd(2 * CH), fwd(CH), fwd(CH), fwd(1), bwd(CH), bwd(2 * CH), bwd(CH), bwd(CH), bwd(1),
                  st_spec, full2(cst['mbd']), full2(cst['mbdf'])],
        out_specs=[pl.BlockSpec((sg, cg, CH, GW), lambda s, t: (s, t, 0, 0)),
                   pl.BlockSpec((sg, cg, CH, GW), lambda s, t: (s, nt - 1 - t, 0, 0)), st_spec],
        out_shape=[o_shape, o_shape, jax.ShapeDtypeStruct((nseq, 2, GW, GW), f32)],
        scratch_shapes=[pltpu.VMEM((sg, 2, GW, GW), f32)],
        compiler_params=_cparams(("arbitrary", "arbitrary")), name="gdn_scan",
    )(uu, wq, qkd, kd, egl, uu, wq, qkd, kd, egl, s0, cst['mbd'], cst['mbdf'])

    return o_f, o_b, sfin


def _out_kernel(a_ref, gl_ref, of_ref, ob_ref, gate_ref, sc_ref, x_ref, mod_ref, w_ref, lg_ref, lb_ref,
                ng_ref, mbd_ref, o_ref, *, colmajor):
    if colmajor:
        o = jnp.concatenate([of_ref[:, rr, :] + ob_ref[:, rr, :] for rr in range(of_ref.shape[1])], axis=0)
    else:
        o = of_ref[...] + ob_ref[...]
    ms = _dot(o * o, mbd_ref[...]) * (1.0 / DH)
    gd = o * lax.rsqrt(ms + EPS) * ng_ref[...] * _silu(gate_ref[...])
    m = mod_ref[...]
    mix_in = jnp.concatenate([a_ref[...], gl_ref[...], gd.astype(bf16), sc_ref[...]], axis=1)
    mix = jnp.dot(mix_in, w_ref[...], preferred_element_type=f32)
    o_ref[...] = _ln(ALPHA * x_ref[...] + m[:, 2 * D:3 * D] * mix, lg_ref[...], lb_ref[...])


def _out_call(a, gl, o_f, o_b, u, sc, x, mod, p, cst, tiles_per_mod, seq_len, colmajor, tm=512):
    n = x.shape[0]
    part = pl.BlockSpec((tm, GW), lambda i: (i, 0))
    full = lambda t: pl.BlockSpec(t.shape, lambda i: (0,) * t.ndim)
    if colmajor:
        rows = tm // GRID_W
        per_seq = seq_len // tm
        o_spec = pl.BlockSpec((None, GRID_W, rows, GW), lambda i: (i // per_seq, 0, i % per_seq, 0))
    else:
        o_f, o_b = o_f.reshape(n, GW), o_b.reshape(n, GW)
        o_spec = part
    return pl.pallas_call(
        functools.partial(_out_kernel, colmajor=colmajor), grid=(n // tm,),
        in_specs=[part, part, o_spec, o_spec, pl.BlockSpec((tm, GW), lambda i: (i, C_DGATE)), part,
                  pl.BlockSpec((tm, D), lambda i: (i, 0)),
                  pl.BlockSpec((None, 1, 6 * D), lambda i: (i // tiles_per_mod, 0, 0)),
                  pl.BlockSpec((None, D, D), lambda i: (p['layer'], 0, 0)),
                  full(p['ln1_g']), full(p['ln1_b']), full(p['gdn_norm_g']), full(cst['mbd'])],
        out_specs=pl.BlockSpec((tm, D), lambda i: (i, 0)),
        out_shape=jax.ShapeDtypeStruct((n, D), f32),
        compiler_params=_cparams(("arbitrary",)), name="out_proj_ln",
    )(a, gl, o_f, o_b, u, sc, x, mod, p['w_out'], p['ln1_g'], p['ln1_b'], p['gdn_norm_g'], cst['mbd'])


def _ffn_kernel(x_ref, mod_ref, wg_ref, wu_ref, wd_ref, lg_ref, lb_ref, o_ref):
    m = mod_ref[...]
    x = x_ref[...]
    hb = (x * (1.0 + m[:, 4 * D:5 * D]) + m[:, 3 * D:4 * D]).astype(bf16)
    t = _silu(jnp.dot(hb, wg_ref[...], preferred_element_type=f32)) * jnp.dot(hb, wu_ref[...], preferred_element_type=f32)
    f = jnp.dot(t.astype(bf16), wd_ref[...], preferred_element_type=f32)
    o_ref[...] = _ln(ALPHA * x + m[:, 5 * D:6 * D] * f, lg_ref[...], lb_ref[...])


def _ffn_call(x, mod, p, seq_len, tm=512):
    n = x.shape[0]
    tiles_per_mod = seq_len // tm if mod.shape[0] > 1 else n // tm
    once = lambda t: pl.BlockSpec(t.shape, lambda i: (0,) * t.ndim, pipeline_mode=pl.Buffered(1))
    wspec = lambda t: pl.BlockSpec((None,) + t.shape[1:], lambda i: (p['index'], 0, 0), pipeline_mode=pl.Buffered(1))
    return pl.pallas_call(
        _ffn_kernel, grid=(n // tm,),
        in_specs=[pl.BlockSpec((tm, D), lambda i: (i, 0)),
                  pl.BlockSpec((None, 1, 6 * D), lambda i: (i // tiles_per_mod, 0, 0)),
                  wspec(p['w_gate']), wspec(p['w_up']), wspec(p['w_down']), once(p['ln2_g']), once(p['ln2_b'])],
        out_specs=pl.BlockSpec((tm, D), lambda i: (i, 0)),
        out_shape=jax.ShapeDtypeStruct((n, D), f32),
        compiler_params=_cparams(("arbitrary",)), name="dense_ffn",
    )(x, mod, p['w_gate'], p['w_up'], p['w_down'], p['ln2_g'], p['ln2_b'])


MOE_TM = 1024
MOE_SB = 128


def _moe_kernel(x_ref, mod_ref, wg_ref, wu_ref, wd_ref, lg_ref, lb_ref, rt_ref, tri_ref, o_ref,
                h_scr, acc_scr, comb_scr, pos_scr):
    e = pl.program_id(1)
    tm = x_ref.shape[0]

    @pl.when(e == 0)
    def _():
        m = mod_ref[...]
        h = x_ref[...] * (1.0 + m[:, 4 * D:5 * D]) + m[:, 3 * D:4 * D]
        h_scr[...] = h.astype(bf16)
        acc_scr[...] = jnp.zeros(acc_scr.shape, f32)
        hh, hr = _split2(h)
        rh, rr = _split2(rt_ref[...])
        d = lambda x, y: jnp.dot(x, y, preferred_element_type=f32)
        logits = d(hh, rh) + d(hr.astype(bf16), rh) + d(hh, rr.astype(bf16))
        lane = lax.broadcasted_iota(jnp.int32, logits.shape, 1).astype(f32)
        neg = jnp.float32(-jnp.inf)
        logits = jnp.where(lane < N_EXPERTS, logits, neg)
        m1 = jnp.max(logits, axis=-1, keepdims=True)
        i1 = jnp.min(jnp.where(logits == m1, lane, float(LANE)), axis=-1, keepdims=True)
        rest = jnp.where(lane == i1, neg, logits)
        m2 = jnp.max(rest, axis=-1, keepdims=True)
        i2 = jnp.min(jnp.where(rest == m2, lane, float(LANE)), axis=-1, keepdims=True)
        e2 = jnp.exp(m2 - m1)
        p1 = 1.0 / (1.0 + e2)
        p2 = e2 / (1.0 + e2)
        sel1 = lane == i1
        sel2 = lane == i2
        comb = jnp.where(sel1, p1, 0.0) + jnp.where(sel2, p2, 0.0)
        c_hi, c_mid, c_lo = _split3(comb)
        comb_scr[...] = jnp.where(lane < N_EXPERTS, c_hi.astype(f32),
                                  jnp.where(lane < 2 * N_EXPERTS, pltpu.roll(c_mid.astype(f32), N_EXPERTS, 1),
                                            pltpu.roll(c_lo.astype(f32), 2 * N_EXPERTS, 1))).astype(bf16)
        routed = jnp.where(sel1 | sel2, 1.0, 0.0).astype(bf16)
        incl = lax.dot_general(routed, tri_ref[...], (((0,), (0,)), ((), ())), preferred_element_type=f32)
        tok = lax.broadcasted_iota(jnp.int32, incl.shape, 1)
        excl = jnp.where(tok == 0, 0.0, pltpu.roll(incl, 1, 1))
        pos_scr[...] = jnp.where(incl > excl, excl, -1.0)

    pos_row = pos_scr[pl.ds(e, 1), :]
    nrows = (jnp.max(pos_row) + 1.0).astype(jnp.int32)

    def run_pass(base, sb):
        slot = lax.broadcasted_iota(jnp.int32, (sb, tm), 0).astype(f32) + base.astype(f32)
        lane_sb = lax.broadcasted_iota(jnp.int32, (sb, LANE), 1)
        sel = jnp.where(pos_row == slot, 1.0, 0.0).astype(bf16)
        xs = jnp.dot(sel, h_scr[...], preferred_element_type=f32).astype(bf16)
        t = _silu(jnp.dot(xs, wg_ref[...], preferred_element_type=f32)) * jnp.dot(xs, wu_ref[...], preferred_element_type=f32)
        y = jnp.dot(t.astype(bf16), wd_ref[...], preferred_element_type=f32)
        cw3 = jnp.dot(sel, comb_scr[...], preferred_element_type=f32)
        cw = jnp.sum(jnp.where((lane_sb & (N_EXPERTS - 1)) == e, cw3, 0.0), axis=-1, keepdims=True)
        acc_scr[...] += lax.dot_general(sel, (y * cw).astype(bf16), (((0,), (0,)), ((), ())), preferred_element_type=f32)

    big = 2 * MOE_SB
    nbig = (nrows + MOE_SB - 1) // big

    def big_pass(i, carry):
        run_pass(i * big, big)
        return carry

    lax.fori_loop(0, nbig, big_pass, 0)

    @pl.when(nrows > nbig * big)
    def _():
        run_pass(nbig * big, MOE_SB)

    @pl.when(e == N_EXPERTS - 1)
    def _():
        m = mod_ref[...]
        o_ref[...] = _ln(ALPHA * x_ref[...] + m[:, 5 * D:6 * D] * acc_scr[...], lg_ref[...], lb_ref[...])


def _moe_call(x, mod, p, seq_len, cst, tm=MOE_TM):
    n = x.shape[0]
    tm = min(tm, n)
    fw = D_FF_EXPERT
    tiles_per_mod = seq_len // tm if mod.shape[0] > 1 else n // tm
    full = lambda t: pl.BlockSpec(t.shape, lambda i, g: (0,) * t.ndim)
    tri = cst['tri_tok'][:tm, :tm]
    return pl.pallas_call(
        _moe_kernel, grid=(n // tm, N_EXPERTS),
        in_specs=[pl.BlockSpec((tm, D), lambda i, g: (i, 0)),
                  pl.BlockSpec((None, 1, 6 * D), lambda i, g: (i // tiles_per_mod, 0, 0)),
                  pl.BlockSpec((None, None, D, fw), lambda i, g: (p['index'], g, 0, 0)),
                  pl.BlockSpec((None, None, D, fw), lambda i, g: (p['index'], g, 0, 0)),
                  pl.BlockSpec((None, None, fw, D), lambda i, g: (p['index'], g, 0, 0)),
                  full(p['ln2_g']), full(p['ln2_b']), full(p['router']), full(tri)],
        out_specs=pl.BlockSpec((tm, D), lambda i, g: (i, 0)),
        out_shape=jax.ShapeDtypeStruct((n, D), f32),
        scratch_shapes=[pltpu.VMEM((tm, D), bf16), pltpu.VMEM((tm, D), f32),
                        pltpu.VMEM((tm, LANE), bf16), pltpu.VMEM((LANE, tm), f32)],
        compiler_params=_cparams(("arbitrary", "arbitrary")), name="moe_ffn",
    )(x, mod, p['w_gate'], p['w_up'], p['w_down'], p['ln2_g'], p['ln2_b'], p['router'], tri)


def _constants():
    hb = np.arange(GW) // DH
    mbd = (hb[:, None] == hb[None, :]).astype(np.float32)
    r = np.arange(CH)
    t = np.arange(MOE_TM)
    e = np.zeros((LANE, 3 * GW), np.float32)
    for h in range(NH):
        e[L_BETA + h, h * DH:(h + 1) * DH] = 1.0
        for d in range(2):
            e[L_DEC + NH * d + h, (1 + d) * GW + h * DH:(1 + d) * GW + (h + 1) * DH] = 1.0
    return dict(mbd=jnp.asarray(mbd, bf16), mbdf=jnp.asarray(mbd, f32),
                tril=jnp.asarray(r[:, None] >= r[None, :], bf16), triu=jnp.asarray(r[:, None] <= r[None, :], bf16),
                e_gdn=jnp.asarray(e, bf16), tri_tok=jnp.asarray(t[:, None] <= t[None, :], bf16))


_IN_SIZES = (GW, GW, GW, GW, GW, GW, 2 * GLA_LR, GW, GW, GW, GW, NH, 2 * NH, GW, GW, GW)


def _pack_w_in(w_in):
    offs = np.concatenate([[0], np.cumsum(_IN_SIZES)])
    big = [i for i, s in enumerate(_IN_SIZES) if s == GW]
    small = [i for i, s in enumerate(_IN_SIZES) if s != GW]
    parts = [w_in[..., int(offs[i]):int(offs[i + 1])] for i in big + small]
    pad = NU - int(offs[-1])
    parts.append(jnp.zeros(w_in.shape[:-1] + (pad,), w_in.dtype))
    return jnp.concatenate(parts, axis=-1).astype(bf16)


def _to_bd(s, transpose):
    if transpose:
        s = jnp.swapaxes(s, -1, -2)
    b = s.shape[0]
    z = jnp.zeros_like(s)
    rows = [jnp.concatenate([s[:, :, h] if g == h else z[:, :, h] for g in range(NH)], axis=-1) for h in range(NH)]
    return jnp.concatenate(rows, axis=-2).reshape(b, 2, GW, GW)


def _from_bd(s, transpose):
    out = jnp.stack([s[:, :, h * DH:(h + 1) * DH, h * DH:(h + 1) * DH] for h in range(NH)], axis=2)
    return jnp.swapaxes(out, -1, -2) if transpose else out


def _stream_layer(x, mod, nseq, seq_len, s0_gla, s0_gdn, p, cst, colmajor, moe):
    tpm_512 = max(seq_len // 512, 1) if mod.shape[0] > 1 else x.shape[0] // 512
    u = _in_call(x, mod, p['w_in'], p['layer'], tpm_512)
    a, ysc = _conv_call(u, seq_len, p)
    ygla, sgla = _gla_call(u, nseq, seq_len, s0_gla, p, cst)
    o_f, o_b, sgdn = _gdn_call(u, nseq, seq_len, s0_gdn, p, cst, colmajor)
    x1 = _out_call(a, ygla, o_f, o_b, u, ysc, x, mod, p, cst, tpm_512, seq_len, colmajor)
    x2 = _moe_call(x1, mod, p['ffn'], seq_len, cst) if moe else _ffn_call(x1, mod, p['ffn'], seq_len)
    return x2, sgla, sgdn


def kernel(x_prompt, x_sample, c, state_gla, state_gdn, c_ctx, w_ada, b_ada, w_in, w_out, conv_w, conv_b, conv_ln_g, conv_ln_b, conv_pw, gla_w_lr, gla_b_lr, gla_norm_g, gdn_conv_w, gdn_a_log, gdn_dt_bias, gdn_norm_g, sc_conv_w, ln1_g, ln1_b, ln2_g, ln2_b, ffn_w_gate, ffn_w_up, ffn_w_down, moe_router, moe_w_gate, moe_w_up, moe_w_down):
    nb, seq, _ = x_prompt.shape
    db, dseq, _ = x_sample.shape
    depth = w_in.shape[0]
    cst = _constants()

    rows = 16
    cond = jnp.concatenate([c_ctx[None, :], c, jnp.zeros((rows - 1 - db, D), f32)], axis=0)
    mod = _mod_call(cond, w_ada, b_ada)

    w_in_p = _pack_w_in(w_in)
    w_out_b = w_out.astype(bf16)
    conv_pw_b = conv_pw.astype(bf16)
    wlr = jnp.zeros((depth, 2, LANE, GW), f32)
    wlr = wlr.at[:, 0, 0:GLA_LR].set(gla_w_lr[:, 0]).at[:, 1, GLA_LR:2 * GLA_LR].set(gla_w_lr[:, 1]).astype(bf16)
    rep = lambda t: jnp.repeat(t, DH, axis=-1)[:, :, None, :]
    row = lambda t: t[:, None, :]
    ffn_g, ffn_u, ffn_d = ffn_w_gate.astype(bf16), ffn_w_up.astype(bf16), ffn_w_down.astype(bf16)
    moe_g, moe_u, moe_d = moe_w_gate.astype(bf16), moe_w_up.astype(bf16), moe_w_down.astype(bf16)
    router = jnp.pad(moe_router, ((0, 0), (0, 0), (0, LANE - N_EXPERTS)))

    zero_bd = jnp.zeros((nb, 2, GW, GW), f32)
    y_p = x_prompt.reshape(nb * seq, D)
    y_s = x_sample.reshape(db * dseq, D)
    gla_states, gdn_states = [], []
    for l in range(depth):
        j = l // 2
        moe = l % 2 == 1
        if moe:
            ffn = dict(w_gate=moe_g, w_up=moe_u, w_down=moe_d, index=j, router=router[j],
                       ln2_g=row(ln2_g)[l], ln2_b=row(ln2_b)[l])
        else:
            ffn = dict(w_gate=ffn_g, w_up=ffn_u, w_down=ffn_d, index=j, ln2_g=row(ln2_g)[l], ln2_b=row(ln2_b)[l])
        p = dict(layer=l, w_in=w_in_p, w_out=w_out_b, conv_w=conv_w[l], conv_b=row(conv_b)[l],
                 conv_ln_g=row(conv_ln_g)[l], conv_ln_b=row(conv_ln_b)[l], conv_pw=conv_pw_b[l],
                 sc_conv_w=sc_conv_w[l], gla_wlr=wlr[l], gla_blr=gla_b_lr[l][:, None, :],
                 gla_norm_g=row(gla_norm_g)[l], gdn_conv_w=gdn_conv_w[l], gdn_alog=rep(gdn_a_log)[l],
                 gdn_dtb=rep(gdn_dt_bias)[l], gdn_norm_g=row(gdn_norm_g)[l],
                 ln1_g=row(ln1_g)[l], ln1_b=row(ln1_b)[l], ffn=ffn)
        mod_ctx = mod[l, 0:1][:, None, :]
        mod_lat = mod[l, 1:1 + db][:, None, :]
        y_p, s_gla, s_gdn = _stream_layer(y_p, mod_ctx, nb, seq, zero_bd, zero_bd, p, cst, False, moe)
        gla_states.append(_from_bd(s_gla, True))
        gdn_states.append(_from_bd(s_gdn, False))
        y_s, _, _ = _stream_layer(y_s, mod_lat, db, dseq, _to_bd(state_gla[:, l], True),
                                  _to_bd(state_gdn[:, l], False), p, cst, True, moe)
    return (y_p.reshape(nb, seq, D), y_s.reshape(db, dseq, D),
            jnp.stack(gla_states, axis=1), jnp.stack(gdn_states, axis=1))
```

```python
import functools
import math

import numpy as np
import jax
import jax.numpy as jnp
from jax import lax
from jax.experimental import pallas as pl
from jax.experimental.pallas import tpu as pltpu

f32 = jnp.float32
bf16 = jnp.bfloat16

D = 1024
GW = 256
NH = 4
DH = 64
CH = 64
GRID_W = 64
CONV_K = 31
GLA_LR = 16
GLA_TAU = 16.0
DEPTH = 4
N_EXPERTS = 8
D_FF = 2816
D_FF_EXPERT = 1408
ALPHA = (2 * DEPTH) ** 0.25
EPS = 1e-5
NU = 13 * GW + 128
SMALL_BLK = 13 * GW // 128
LANE = 128
C_AVAL, C_AGATE, C_GQ, C_GK, C_GV, C_GG, C_DQ, C_DK, C_DV, C_DGATE, C_SB, C_SC, C_SX = range(13)
L_BETA = 2 * GLA_LR
L_DEC = L_BETA + NH

VMEM_LIMIT = 56 * 1024 * 1024


def _cparams(sem):
    return pltpu.CompilerParams(dimension_semantics=sem, vmem_limit_bytes=VMEM_LIMIT)


def _dot(a, b):
    return jnp.dot(a.astype(bf16), b.astype(bf16), preferred_element_type=f32)


def _dot_nt(a, b):
    return lax.dot_general(a.astype(bf16), b.astype(bf16), (((1,), (1,)), ((), ())),
                           preferred_element_type=f32)


def _dot_tn(a, b):
    return lax.dot_general(a.astype(bf16), b.astype(bf16), (((0,), (0,)), ((), ())),
                           preferred_element_type=f32)


def _split2(x):
    hi = x.astype(bf16)
    return hi, x - hi.astype(f32)


def _split3(x):
    hi = x.astype(bf16)
    r = x - hi.astype(f32)
    mid = r.astype(bf16)
    lo = (r - mid.astype(f32)).astype(bf16)
    return hi, mid, lo


def _dot01_left(m01, x):
    hi, r = _split2(x)
    d = lambda t: jnp.dot(m01, t, preferred_element_type=f32)
    return d(hi) + d(r.astype(bf16))


def _dot01_right(x, m01):
    hi, r = _split2(x)
    d = lambda t: jnp.dot(t, m01, preferred_element_type=f32)
    return d(hi) + d(r.astype(bf16))


def _bd(x, mask):
    xb = x.astype(bf16)
    return jnp.concatenate([xb, xb, xb, xb], axis=0) * mask


def _dot3(a_parts, b_parts, mask):
    ah, ar = a_parts
    bh, br = b_parts
    al = ar.astype(bf16)
    bdh = _bd(bh, mask)
    bdl = _bd(br, mask)
    d = lambda x, y: jnp.dot(x, y, preferred_element_type=f32)
    return d(ah, bdh) + d(al, bdh) + d(ah, bdl)


def _silu(x):
    return x * jax.nn.sigmoid(x)


def _ln(x, g, b):
    xc = x - jnp.mean(x, axis=-1, keepdims=True)
    var = jnp.mean(xc * xc, axis=-1, keepdims=True)
    return xc * lax.rsqrt(var + EPS) * g + b


def _head_iotas():
    rowi = lax.broadcasted_iota(jnp.int32, (CH, GW), 0)
    coli = lax.broadcasted_iota(jnp.int32, (CH, GW), 1) & (DH - 1)
    return rowi, coli


def _mod_kernel(c_ref, w_ref, b_ref, o_ref):
    o_ref[...] = jnp.dot(_silu(c_ref[...]), w_ref[...], precision=lax.Precision.HIGHEST,
                         preferred_element_type=f32) + b_ref[...]


def _mod_call(cond, w_ada, b_ada):
    nl = w_ada.shape[0]
    rows = cond.shape[0]
    return pl.pallas_call(
        _mod_kernel, grid=(nl, 6),
        in_specs=[pl.BlockSpec((rows, D), lambda l, n: (0, 0)),
                  pl.BlockSpec((None, D, D), lambda l, n: (l, 0, n)),
                  pl.BlockSpec((None, 1, D), lambda l, n: (l, 0, n))],
        out_specs=pl.BlockSpec((None, rows, D), lambda l, n: (l, 0, n)),
        out_shape=jax.ShapeDtypeStruct((nl, rows, 6 * D), f32),
        compiler_params=_cparams(("arbitrary", "arbitrary")), name="adaln_mod",
    )(cond, w_ada, b_ada.reshape(nl, 1, 6 * D))


TM_IN = 512
CT = 256
HALO = 16
SHALO = 8


def _in_conv_kernel(x_ref, mod_ref, w_ref, cw, cb, lng, lnb, pw, scw, u_ref, a_out, ysc_out,
                    a_st, s_st, sb_st, tail_a, tail_s, pad_a, pad_s, win_scr, *, subs_per_seq):
    i = pl.program_id(0)
    slot = i % 2
    pslot = 1 - slot

    @pl.when(i == 0)
    def _():
        a_st[...] = jnp.zeros(a_st.shape, f32)
        s_st[...] = jnp.zeros(s_st.shape, f32)
        sb_st[...] = jnp.zeros(sb_st.shape, f32)

    tail_a[...] = a_st[slot, TM_IN - HALO:TM_IN, :]
    tail_s[...] = s_st[slot, TM_IN - SHALO:TM_IN, :]

    m = mod_ref[...]
    h = x_ref[...] * (1.0 + m[:, D:2 * D]) + m[:, 0:D]
    u = jnp.dot(h.astype(bf16), w_ref[...], preferred_element_type=f32)
    u_ref[...] = u
    slab = lambda c: u[:, c * GW:(c + 1) * GW]
    a_st[slot] = slab(C_AVAL) * jax.nn.sigmoid(slab(C_AGATE))
    s_st[slot] = slab(C_SC) * slab(C_SX)
    sb_st[slot] = slab(C_SB)

    base = HALO - CONV_K // 2
    span = 8 * ((CONV_K + 6) // 8)
    for hf in range(TM_IN // CT):
        g = (i - 1) * (TM_IN // CT) + hf
        first = g % subs_per_seq == 0
        last = g % subs_per_seq == subs_per_seq - 1
        lo, hi = hf * CT, (hf + 1) * CT
        if hf == 0:
            prev_a, prev_s = tail_a[...], tail_s[...]
        else:
            prev_a, prev_s = a_st[pslot, lo - HALO:lo, :], s_st[pslot, lo - SHALO:lo, :]
        if hi == TM_IN:
            next_a, next_s = a_st[slot, 0:HALO, :], s_st[slot, 0:SHALO, :]
        else:
            next_a, next_s = a_st[pslot, hi:hi + HALO, :], s_st[pslot, hi:hi + SHALO, :]
        pad_a[0:HALO, :] = jnp.where(first, 0.0, prev_a)
        pad_a[HALO:HALO + CT, :] = a_st[pslot, lo:hi, :]
        pad_a[HALO + CT:2 * HALO + CT, :] = jnp.where(last, 0.0, next_a)
        acc = jnp.zeros((CT, GW), f32)
        for ph in range(8):
            win_scr[...] = pad_a[ph:ph + CT + span - 8, :]
            for al in range(0, span, 8):
                k = al + ph - base
                if 0 <= k < CONV_K:
                    acc = acc + win_scr[al:al + CT, :] * cw[k:k + 1, :]
        a = _ln(acc + cb[...], lng[...], lnb[...])
        a_out[lo:hi, :] = _dot(_silu(a), pw[...]).astype(bf16)

        pad_s[0:SHALO, :] = jnp.where(first, 0.0, prev_s)
        pad_s[SHALO:SHALO + CT, :] = s_st[pslot, lo:hi, :]
        pad_s[SHALO + CT:2 * SHALO + CT, :] = jnp.where(last, 0.0, next_s)
        acc = jnp.zeros((CT, GW), f32)
        for k in range(3):
            o = SHALO - 1 + k
            acc = acc + pad_s[o:o + CT, :] * scw[k:k + 1, :]
        ysc_out[lo:hi, :] = (sb_st[pslot, lo:hi, :] * acc).astype(bf16)


def _in_conv_call(x, mod, w, li, tiles_per_mod, seq_len, p):
    n = x.shape[0]
    tm = TM_IN
    nt = n // tm
    cur = lambda i: jnp.minimum(i, nt - 1)
    prv = lambda i: jnp.maximum(i - 1, 0)
    full = lambda a: pl.BlockSpec(a.shape, lambda i: (0,) * a.ndim)
    consts = (p['conv_w'], p['conv_b'], p['conv_ln_g'], p['conv_ln_b'], p['conv_pw'], p['sc_conv_w'])
    return pl.pallas_call(
        functools.partial(_in_conv_kernel, subs_per_seq=seq_len // CT), grid=(nt + 1,),
        in_specs=[pl.BlockSpec((tm, D), lambda i: (cur(i), 0)),
                  pl.BlockSpec((None, 1, 6 * D), lambda i: (cur(i) // tiles_per_mod, 0, 0)),
                  pl.BlockSpec((None, D, NU), lambda i: (li, 0, 0))] + [full(a) for a in consts],
        out_specs=[pl.BlockSpec((tm, NU), lambda i: (cur(i), 0)),
                   pl.BlockSpec((tm, GW), lambda i: (prv(i), 0)),
                   pl.BlockSpec((tm, GW), lambda i: (prv(i), 0))],
        out_shape=[jax.ShapeDtypeStruct((n, NU), f32), jax.ShapeDtypeStruct((n, GW), bf16),
                   jax.ShapeDtypeStruct((n, GW), bf16)],
        scratch_shapes=[pltpu.VMEM((2, tm, GW), f32), pltpu.VMEM((2, tm, GW), f32), pltpu.VMEM((2, tm, GW), f32),
                        pltpu.VMEM((HALO, GW), f32), pltpu.VMEM((SHALO, GW), f32),
                        pltpu.VMEM((CT + 2 * HALO, GW), f32), pltpu.VMEM((CT + 2 * SHALO, GW), f32),
                        pltpu.VMEM((CT + 8 * ((CONV_K + 6) // 8) - 8, GW), f32)],
        compiler_params=_cparams(("arbitrary",)), name="in_proj_conv",
    )(x, mod, w, *consts)


GLA_LOCK = 8


def _gla_kernel(q_ref, k_ref, v_ref, g_ref, sm_ref, s0_ref, wlr_ref, blr_ref, ng_ref,
                mbd_ref, mbdf_ref, tril_ref, triu_ref, y_ref, sfin_ref, st_scr, of_scr, *, tt, ntiles):
    p = pl.program_id(1)
    nch = tt // CH
    nlock = min(GLA_LOCK, nch)
    rowi, coli = _head_iotas()

    @pl.when(p == 0)
    def _():
        st_scr[...] = s0_ref[0]

    @pl.when(p == ntiles)
    def _():
        sfin_ref[0] = st_scr[...]
        st_scr[...] = s0_ref[1]

    def run(dirn):
        tri = tril_ref[...] if dirn == 0 else triu_ref[...]
        keep = (rowi >= coli) if dirn == 0 else (rowi <= coli)
        wlr = wlr_ref[dirn]
        blr = blr_ref[dirn]
        tile = p if dirn == 0 else 2 * ntiles - 1 - p

        def body(it, carry):
            cis = [it * nlock + c for c in range(nlock)]
            if dirn == 1:
                cis = [nch - 1 - ci for ci in cis]
            rows = [pl.ds(pl.multiple_of(ci * CH, CH), CH) for ci in cis]
            grows = [pl.ds(pl.multiple_of(tile * tt + ci * CH, CH), CH) for ci in cis]
            mbd = mbd_ref[...]
            zs = [jnp.dot(sm_ref[r, :].astype(bf16), wlr, preferred_element_type=f32) + blr for r in rows]
            cums = [_dot01_left(tri, jax.nn.log_sigmoid(z) * (1.0 / GLA_TAU)) for z in zs]
            tots = [cum[CH - 1:CH, :] if dirn == 0 else cum[0:1, :] for cum in cums]
            vs = [v_ref[r, :] for r in rows]
            qes = [q_ref[r, :] * (DH ** -0.5) * jnp.exp(cum) for r, cum in zip(rows, cums)]
            kes = [k_ref[r, :] * jnp.exp(-cum) for r, cum in zip(rows, cums)]
            kds = [k_ref[r, :] * jnp.exp(tot - cum) for r, cum, tot in zip(rows, cums, tots)]
            atts = [jnp.where(keep, _dot_nt(qe, _bd(ke, mbd)), 0.0) for qe, ke in zip(qes, kes)]
            ols = [_dot(att, _bd(v, mbd)) for att, v in zip(atts, vs)]
            upds = [_dot_tn(v, kd) for v, kd in zip(vs, kds)]
            st = st_scr[...]
            os_ = []
            for c in range(nlock):
                os_.append(ols[c] + _dot_nt(qes[c], st))
                st = st * jnp.exp(tots[c]) + mbdf_ref[...] * upds[c]
            st_scr[...] = st
            if dirn == 0:
                for c in range(nlock):
                    of_scr[grows[c], :] = os_[c]
            else:
                os_ = [of_scr[grows[c], :] + os_[c] for c in range(nlock)]
                mss = [_dot(o * o, mbd) * (1.0 / DH) for o in os_]
                for c in range(nlock):
                    y = os_[c] * lax.rsqrt(mss[c] + EPS) * ng_ref[...] * _silu(g_ref[rows[c], :])
                    y_ref[rows[c], :] = y.astype(bf16)
            return carry

        lax.fori_loop(0, nch // nlock, body, 0)

    @pl.when(p < ntiles)
    def _():
        run(0)

    @pl.when(p >= ntiles)
    def _():
        run(1)

    @pl.when(p == 2 * ntiles - 1)
    def _():
        sfin_ref[1] = st_scr[...]


def _gla_call(u, nseq, seq_len, s0, p, cst):
    tt = min(seq_len, 512)
    ntiles = seq_len // tt
    tmap = lambda q: jnp.where(q < ntiles, q, 2 * ntiles - 1 - q)
    main = lambda c: pl.BlockSpec((tt, GW), lambda s, q: (s * ntiles + tmap(q), c))
    full = lambda a: pl.BlockSpec(a.shape, lambda s, q: (0,) * a.ndim)
    consts = (p['gla_wlr'], p['gla_blr'], p['gla_norm_g'], cst['mbd'], cst['mbdf'], cst['tril'], cst['triu'])
    return pl.pallas_call(
        functools.partial(_gla_kernel, tt=tt, ntiles=ntiles), grid=(nseq, 2 * ntiles),
        in_specs=[main(C_GQ), main(C_GK), main(C_GV), main(C_GG),
                  pl.BlockSpec((tt, LANE), lambda s, q: (s * ntiles + tmap(q), SMALL_BLK)),
                  pl.BlockSpec((None, 2, GW, GW), lambda s, q: (s, 0, 0, 0))] + [full(a) for a in consts],
        out_specs=[pl.BlockSpec((tt, GW), lambda s, q: (s * ntiles + jnp.where(q < ntiles, ntiles - 1, 2 * ntiles - 1 - q), 0)),
                   pl.BlockSpec((None, 2, GW, GW), lambda s, q: (s, 0, 0, 0))],
        out_shape=[jax.ShapeDtypeStruct((nseq * seq_len, GW), bf16),
                   jax.ShapeDtypeStruct((nseq, 2, GW, GW), f32)],
        scratch_shapes=[pltpu.VMEM((GW, GW), f32), pltpu.VMEM((seq_len, GW), f32)],
        compiler_params=_cparams(("arbitrary", "arbitrary")), name="gla",
    )(u, u, u, u, u, s0, *consts)


GDN_G = 8
HR = 8
NLOCK = 4
INV_HI_LEVELS = 4


def _gdn_prep_kernel(*refs, g, ngroups, colmajor):
    if ngroups > 1:
        (q_ref, k_ref, v_ref, sm_ref, qp, kp, vp, qn, kn, vnx, cw_ref, alog_ref, dtb_ref, e_ref, mbd_ref,
         tril_ref, triu_ref, u_out, wq_out, qkd_out, kd_out, egl_out) = refs
    else:
        (q_ref, k_ref, v_ref, sm_ref, cw_ref, alog_ref, dtb_ref, e_ref, mbd_ref,
         tril_ref, triu_ref, u_out, wq_out, qkd_out, kd_out, egl_out) = refs
        qp = kp = vp = qn = kn = vnx = None
    gi = pl.program_id(1)
    rowi, coli = _head_iotas()
    row_first = rowi == 0
    row_last = rowi == CH - 1
    icat = jnp.where(rowi == coli, 1.0, 0.0)

    if colmajor:
        get = lambda ref, j: ref[:, j, :]
        last_row = lambda ref, j: ref[CH - 1, pl.ds(j, 1), :]
        first_row = lambda ref, j: ref[0, pl.ds(j, 1), :]
    else:
        get = lambda ref, j: ref[j]
        last_row = lambda ref, j: ref[j, CH - 1:CH, :]
        first_row = lambda ref, j: ref[j, 0:1, :]

    def shared(jj):
        jm = jnp.maximum(jj - 1, 0)
        jp = jnp.minimum(jj + 1, g - 1)

        def conv(ref, pref, nref, c0):
            x = get(ref, jj)
            if pref is None:
                pr_out = 0.0
                nx_out = 0.0
            else:
                pr_out = jnp.where(gi > 0, pref[HR - 1, HR - 1:HR, :], 0.0)
                nx_out = jnp.where(gi < ngroups - 1, nref[0, 0:1, :], 0.0)
            pr = jnp.where(jj > 0, last_row(ref, jm), pr_out)
            nx = jnp.where(jj < g - 1, first_row(ref, jp), nx_out)
            xd = jnp.where(row_first, pr, pltpu.roll(x, 1, 0))
            xu = jnp.where(row_last, nx, pltpu.roll(x, CH - 1, 0))
            return _silu(xd * cw_ref[0:1, c0:c0 + GW] + x * cw_ref[1:2, c0:c0 + GW] + xu * cw_ref[2:3, c0:c0 + GW])

        mbd = mbd_ref[...]
        q = conv(q_ref, qp, qn, 0)
        k = conv(k_ref, kp, kn, GW)
        v = conv(v_ref, vp, vnx, 2 * GW)
        q = q * lax.rsqrt(_dot(q * q, mbd) + 1e-6) * (DH ** -0.5)
        k = k * lax.rsqrt(_dot(k * k, mbd) + 1e-6)
        ex = _dot01_right(get(sm_ref, jj), e_ref[...])
        beta = jax.nn.sigmoid(ex[:, :GW])
        kb = k * beta
        gm = _dot_nt(jnp.concatenate([kb, q], axis=0), _bd(k, mbd))
        return q, k, kb, v * beta, ex, gm

    def gates(sh, dirn):
        q, k, kb, vb, ex, gm = sh
        la = -jnp.exp(alog_ref[dirn]) * jax.nn.softplus(ex[:, (1 + dirn) * GW:(2 + dirn) * GW] + dtb_ref[dirn])
        if dirn == 0:
            tri, keep, strict, keep_t = tril_ref[...], rowi >= coli, rowi > coli, rowi <= coli
        else:
            tri, keep, strict, keep_t = triu_ref[...], rowi <= coli, rowi < coli, rowi >= coli
        gx = _dot01_left(tri, la)
        grow = jnp.sum(jnp.where(keep_t, la, 0.0), axis=0, keepdims=True)
        decay = jnp.where(keep, jnp.exp(jnp.where(keep, gx - grow, 0.0)), 0.0)
        glast = gx[CH - 1:CH, :] if dirn == 0 else gx[0:1, :]
        nm = jnp.where(strict, -(gm[:CH] * decay), 0.0)
        return gx, glast, nm, gm[CH:] * decay

    def body(it, carry):
        jjs = [it * NLOCK + c for c in range(NLOCK)]
        shs = [shared(jj) for jj in jjs]
        chains = [(c, dirn) for c in range(NLOCK) for dirn in range(2)]
        gs = [gates(shs[c], dirn) for c, dirn in chains]
        mbd = mbd_ref[...]
        nms = [gt[2] for gt in gs]
        cat = lambda x, y: jnp.concatenate([x, y], axis=0)
        nsp = [_split2(nm) for nm in nms]
        pws = [_dot3(sp, sp, mbd) for sp in nsp]
        tinvs = [icat + nm for nm in nms]
        for _ in range(INV_HI_LEVELS - 1):
            psp = [_split2(pw) for pw in pws]
            tsp = [_split2(ti) for ti in tinvs]
            rs = [_dot3((cat(th, ph), cat(tr, pr)), (ph, pr), mbd) for (th, tr), (ph, pr) in zip(tsp, psp)]
            tinvs = [ti + r[:CH] for ti, r in zip(tinvs, rs)]
            pws = [r[CH:] for r in rs]
        rs = [_dot(cat(ti, pw), _bd(pw, mbd)) for ti, pw in zip(tinvs, pws)]
        tinvs = [ti + r[:CH] for ti, r in zip(tinvs, rs)]
        tinvs = [ti + _dot(ti, _bd(r[CH:], mbd)) for ti, r in zip(tinvs, rs)]
        egs = [jnp.exp(gt[0]) for gt in gs]
        us = [_dot(ti, _bd(shs[c][3], mbd)) for ti, (c, dirn) in zip(tinvs, chains)]
        ws = [_dot(ti, _bd(shs[c][2] * eg, mbd)) for ti, eg, (c, dirn) in zip(tinvs, egs, chains)]
        for i, (c, dirn) in enumerate(chains):
            q, k = shs[c][0], shs[c][1]
            gx, glast, _, qkd = gs[i]
            jj = jjs[c]
            u_out[dirn, jj] = us[i]
            wq_out[dirn, jj] = jnp.concatenate([ws[i], q * egs[i]], axis=0).astype(bf16)
            qkd_out[dirn, jj] = qkd.astype(bf16)
            kd_out[dirn, jj] = (k * jnp.exp(glast - gx)).astype(bf16)
            egl_out[dirn, jj] = jnp.exp(glast)
        return carry

    lax.fori_loop(0, g // NLOCK, body, 0)


def _gdn_scan_kernel(uf, wqf, qkf, kdf, egf, ub, wqb, qkb, kdb, egb, s0_ref, mbd_ref, mbdf_ref,
                     of_out, ob_out, sfin_ref, s_scr, *, sg, cg, nt):
    t = pl.program_id(1)

    @pl.when(t == 0)
    def _():
        s_scr[...] = s0_ref[...]

    def body(c, carry):
        chains = [(sq, dirn) for sq in range(sg) for dirn in range(2)]
        src = lambda dirn: (uf, wqf, qkf, kdf, egf, of_out, c) if dirn == 0 else (ub, wqb, qkb, kdb, egb, ob_out, cg - 1 - c)
        mbd = mbd_ref[...]
        ss = [s_scr[sq, dirn] for sq, dirn in chains]
        wss = [jnp.dot(src(dirn)[1][sq, src(dirn)[6]], st.astype(bf16), preferred_element_type=f32)
               for st, (sq, dirn) in zip(ss, chains)]
        vns = [src(dirn)[0][sq, src(dirn)[6]] - ws[:CH] for ws, (sq, dirn) in zip(wss, chains)]
        os_ = [ws[CH:] + jnp.dot(src(dirn)[2][sq, src(dirn)[6]], _bd(vn, mbd), preferred_element_type=f32)
               for ws, vn, (sq, dirn) in zip(wss, vns, chains)]
        upd = [_dot_tn(src(dirn)[3][sq, src(dirn)[6]], vn) for vn, (sq, dirn) in zip(vns, chains)]
        for st, o, up, (sq, dirn) in zip(ss, os_, upd, chains):
            _, _, _, _, eg, o_out, cc = src(dirn)
            o_out[sq, cc] = o
            s_scr[sq, dirn] = st * eg[sq, cc] + mbdf_ref[...] * up
        return carry

    lax.fori_loop(0, cg, body, 0)

    @pl.when(t == nt - 1)
    def _():
        sfin_ref[...] = s_scr[...]


def _gdn_call(u, nseq, seq_len, s0, p, cst, colmajor):
    n = seq_len // CH
    if colmajor:
        assert seq_len == GRID_W * CH
        g, a = GDN_G, GRID_W
        blk = lambda w: (None, a, g, w)
        imap = lambda c: (lambda s, i: (s, 0, i, c))
    else:
        g, a = n, n
        blk = lambda w: (None, g, CH, w)
        imap = lambda c: (lambda s, i: (s, i, 0, c))
    ngroups = n // g
    u4 = u.reshape(nseq, a, CH, NU)
    main = lambda c: pl.BlockSpec(blk(GW), imap(c))
    full2 = lambda x: pl.BlockSpec(x.shape, lambda s, i: (0,) * x.ndim)
    halos, halo_specs = (), []
    if ngroups > 1:
        per = g // HR
        prev = lambda c: pl.BlockSpec((None, HR, HR, GW), lambda s, i: (s, a // HR - 1, jnp.maximum(i * per - 1, 0), c))
        nxt = lambda c: pl.BlockSpec((None, HR, HR, GW), lambda s, i: (s, 0, jnp.minimum((i + 1) * per, a // HR - 1), c))
        halo_specs = [prev(C_DQ), prev(C_DK), prev(C_DV), nxt(C_DQ), nxt(C_DK), nxt(C_DV)]
        halos = (u4,) * 6
    consts = (p['gdn_conv_w'], p['gdn_alog'], p['gdn_dtb'], cst['e_gdn'], cst['mbd'], cst['tril'], cst['triu'])
    per_chunk = lambda rows, dt: jax.ShapeDtypeStruct((nseq, 2, n, rows, GW), dt)
    per_chunk_spec = lambda rows: pl.BlockSpec((None, 2, g, rows, GW), lambda s, i: (s, 0, i, 0, 0))
    uu, wq, qkd, kd, egl = pl.pallas_call(
        functools.partial(_gdn_prep_kernel, g=g, ngroups=ngroups, colmajor=colmajor), grid=(nseq, ngroups),
        in_specs=[main(C_DQ), main(C_DK), main(C_DV), pl.BlockSpec(blk(LANE), imap(SMALL_BLK))] + halo_specs
                 + [full2(x) for x in consts],
        out_specs=[per_chunk_spec(CH), per_chunk_spec(2 * CH), per_chunk_spec(CH), per_chunk_spec(CH), per_chunk_spec(1)],
        out_shape=[per_chunk(CH, f32), per_chunk(2 * CH, bf16), per_chunk(CH, bf16), per_chunk(CH, bf16), per_chunk(1, f32)],
        compiler_params=_cparams(("arbitrary", "arbitrary")), name="gdn_prep",
    )(u4, u4, u4, u4, *halos, *consts)

    sg = math.gcd(nseq, 8)
    cg = min(n, 4)
    nt = n // cg
    fwd = lambda rows: pl.BlockSpec((sg, None, cg, rows, GW), lambda s, t: (s, 0, t, 0, 0))
    bwd = lambda rows: pl.BlockSpec((sg, None, cg, rows, GW), lambda s, t: (s, 1, nt - 1 - t, 0, 0))
    st_spec = pl.BlockSpec((sg, 2, GW, GW), lambda s, t: (s, 0, 0, 0))
    o_shape = jax.ShapeDtypeStruct((nseq, n, CH, GW), f32)
    o_f, o_b, sfin = pl.pallas_call(
        functools.partial(_gdn_scan_kernel, sg=sg, cg=cg, nt=nt), grid=(nseq // sg, nt),
        in_specs=[fwd(CH), fwd(2 * CH), fwd(CH), fwd(CH), fwd(1), bwd(CH), bwd(2 * CH), bwd(CH), bwd(CH), bwd(1),
                  st_spec, full2(cst['mbd']), full2(cst['mbdf'])],
        out_specs=[pl.BlockSpec((sg, cg, CH, GW), lambda s, t: (s, t, 0, 0)),
                   pl.BlockSpec((sg, cg, CH, GW), lambda s, t: (s, nt - 1 - t, 0, 0)), st_spec],
        out_shape=[o_shape, o_shape, jax.ShapeDtypeStruct((nseq, 2, GW, GW), f32)],
        scratch_shapes=[pltpu.VMEM((sg, 2, GW, GW), f32)],
        compiler_params=_cparams(("arbitrary", "arbitrary")), name="gdn_scan",
    )(uu, wq, qkd, kd, egl, uu, wq, qkd, kd, egl, s0, cst['mbd'], cst['mbdf'])

    return o_f, o_b, sfin


def _out_kernel(a_ref, gl_ref, of_ref, ob_ref, gate_ref, sc_ref, x_ref, mod_ref, w_ref, lg_ref, lb_ref,
                ng_ref, mbd_ref, o_ref, *, colmajor):
    if colmajor:
        o = jnp.concatenate([of_ref[:, rr, :] + ob_ref[:, rr, :] for rr in range(of_ref.shape[1])], axis=0)
    else:
        o = of_ref[...] + ob_ref[...]
    ms = _dot(o * o, mbd_ref[...]) * (1.0 / DH)
    gd = o * lax.rsqrt(ms + EPS) * ng_ref[...] * _silu(gate_ref[...])
    m = mod_ref[...]
    mix_in = jnp.concatenate([a_ref[...], gl_ref[...], gd.astype(bf16), sc_ref[...]], axis=1)
    mix = jnp.dot(mix_in, w_ref[...], preferred_element_type=f32)
    o_ref[...] = _ln(ALPHA * x_ref[...] + m[:, 2 * D:3 * D] * mix, lg_ref[...], lb_ref[...])


def _out_call(a, gl, o_f, o_b, u, sc, x, mod, p, cst, tiles_per_mod, seq_len, colmajor, tm=512):
    n = x.shape[0]
    part = pl.BlockSpec((tm, GW), lambda i: (i, 0))
    full = lambda t: pl.BlockSpec(t.shape, lambda i: (0,) * t.ndim)
    if colmajor:
        rows = tm // GRID_W
        per_seq = seq_len // tm
        o_spec = pl.BlockSpec((None, GRID_W, rows, GW), lambda i: (i // per_seq, 0, i % per_seq, 0))
    else:
        o_f, o_b = o_f.reshape(n, GW), o_b.reshape(n, GW)
        o_spec = part
    return pl.pallas_call(
        functools.partial(_out_kernel, colmajor=colmajor), grid=(n // tm,),
        in_specs=[part, part, o_spec, o_spec, pl.BlockSpec((tm, GW), lambda i: (i, C_DGATE)), part,
                  pl.BlockSpec((tm, D), lambda i: (i, 0)),
                  pl.BlockSpec((None, 1, 6 * D), lambda i: (i // tiles_per_mod, 0, 0)),
                  pl.BlockSpec((None, D, D), lambda i: (p['layer'], 0, 0)),
                  full(p['ln1_g']), full(p['ln1_b']), full(p['gdn_norm_g']), full(cst['mbd'])],
        out_specs=pl.BlockSpec((tm, D), lambda i: (i, 0)),
        out_shape=jax.ShapeDtypeStruct((n, D), f32),
        compiler_params=_cparams(("arbitrary",)), name="out_proj_ln",
    )(a, gl, o_f, o_b, u, sc, x, mod, p['w_out'], p['ln1_g'], p['ln1_b'], p['gdn_norm_g'], cst['mbd'])


def _ffn_kernel(x_ref, mod_ref, wg_ref, wu_ref, wd_ref, lg_ref, lb_ref, o_ref):
    m = mod_ref[...]
    x = x_ref[...]
    hb = (x * (1.0 + m[:, 4 * D:5 * D]) + m[:, 3 * D:4 * D]).astype(bf16)
    t = _silu(jnp.dot(hb, wg_ref[...], preferred_element_type=f32)) * jnp.dot(hb, wu_ref[...], preferred_element_type=f32)
    f = jnp.dot(t.astype(bf16), wd_ref[...], preferred_element_type=f32)
    o_ref[...] = _ln(ALPHA * x + m[:, 5 * D:6 * D] * f, lg_ref[...], lb_ref[...])


def _ffn_call(x, mod, p, seq_len, tm=512):
    n = x.shape[0]
    tiles_per_mod = seq_len // tm if mod.shape[0] > 1 else n // tm
    once = lambda t: pl.BlockSpec(t.shape, lambda i: (0,) * t.ndim, pipeline_mode=pl.Buffered(1))
    wspec = lambda t: pl.BlockSpec((None,) + t.shape[1:], lambda i: (p['index'], 0, 0), pipeline_mode=pl.Buffered(1))
    return pl.pallas_call(
        _ffn_kernel, grid=(n // tm,),
        in_specs=[pl.BlockSpec((tm, D), lambda i: (i, 0)),
                  pl.BlockSpec((None, 1, 6 * D), lambda i: (i // tiles_per_mod, 0, 0)),
                  wspec(p['w_gate']), wspec(p['w_up']), wspec(p['w_down']), once(p['ln2_g']), once(p['ln2_b'])],
        out_specs=pl.BlockSpec((tm, D), lambda i: (i, 0)),
        out_shape=jax.ShapeDtypeStruct((n, D), f32),
        compiler_params=_cparams(("arbitrary",)), name="dense_ffn",
    )(x, mod, p['w_gate'], p['w_up'], p['w_down'], p['ln2_g'], p['ln2_b'])


MOE_TM = 1024
MOE_SB = 128


def _moe_kernel(x_ref, mod_ref, wg_ref, wu_ref, wd_ref, lg_ref, lb_ref, rt_ref, tri_ref, o_ref,
                h_scr, acc_scr, comb_scr, pos_scr):
    e = pl.program_id(1)
    tm = x_ref.shape[0]

    @pl.when(e == 0)
    def _():
        m = mod_ref[...]
        h = x_ref[...] * (1.0 + m[:, 4 * D:5 * D]) + m[:, 3 * D:4 * D]
        h_scr[...] = h.astype(bf16)
        acc_scr[...] = jnp.zeros(acc_scr.shape, f32)
        hh, hr = _split2(h)
        rh, rr = _split2(rt_ref[...])
        d = lambda x, y: jnp.dot(x, y, preferred_element_type=f32)
        logits = d(hh, rh) + d(hr.astype(bf16), rh) + d(hh, rr.astype(bf16))
        lane = lax.broadcasted_iota(jnp.int32, logits.shape, 1).astype(f32)
        neg = jnp.float32(-jnp.inf)
        logits = jnp.where(lane < N_EXPERTS, logits, neg)
        m1 = jnp.max(logits, axis=-1, keepdims=True)
        i1 = jnp.min(jnp.where(logits == m1, lane, float(LANE)), axis=-1, keepdims=True)
        rest = jnp.where(lane == i1, neg, logits)
        m2 = jnp.max(rest, axis=-1, keepdims=True)
        i2 = jnp.min(jnp.where(rest == m2, lane, float(LANE)), axis=-1, keepdims=True)
        e2 = jnp.exp(m2 - m1)
        p1 = 1.0 / (1.0 + e2)
        p2 = e2 / (1.0 + e2)
        sel1 = lane == i1
        sel2 = lane == i2
        comb = jnp.where(sel1, p1, 0.0) + jnp.where(sel2, p2, 0.0)
        c_hi, c_mid, c_lo = _split3(comb)
        comb_scr[...] = jnp.where(lane < N_EXPERTS, c_hi.astype(f32),
                                  jnp.where(lane < 2 * N_EXPERTS, pltpu.roll(c_mid.astype(f32), N_EXPERTS, 1),
                                            pltpu.roll(c_lo.astype(f32), 2 * N_EXPERTS, 1))).astype(bf16)
        routed = jnp.where(sel1 | sel2, 1.0, 0.0).astype(bf16)
        incl = lax.dot_general(routed, tri_ref[...], (((0,), (0,)), ((), ())), preferred_element_type=f32)
        tok = lax.broadcasted_iota(jnp.int32, incl.shape, 1)
        excl = jnp.where(tok == 0, 0.0, pltpu.roll(incl, 1, 1))
        pos_scr[...] = jnp.where(incl > excl, excl, -1.0)

    pos_row = pos_scr[pl.ds(e, 1), :]
    nrows = (jnp.max(pos_row) + 1.0).astype(jnp.int32)

    def run_pass(base, sb):
        slot = lax.broadcasted_iota(jnp.int32, (sb, tm), 0).astype(f32) + base.astype(f32)
        lane_sb = lax.broadcasted_iota(jnp.int32, (sb, LANE), 1)
        sel = jnp.where(pos_row == slot, 1.0, 0.0).astype(bf16)
        xs = jnp.dot(sel, h_scr[...], preferred_element_type=f32).astype(bf16)
        t = _silu(jnp.dot(xs, wg_ref[...], preferred_element_type=f32)) * jnp.dot(xs, wu_ref[...], preferred_element_type=f32)
        y = jnp.dot(t.astype(bf16), wd_ref[...], preferred_element_type=f32)
        cw3 = jnp.dot(sel, comb_scr[...], preferred_element_type=f32)
        cw = jnp.sum(jnp.where((lane_sb & (N_EXPERTS - 1)) == e, cw3, 0.0), axis=-1, keepdims=True)
        acc_scr[...] += lax.dot_general(sel, (y * cw).astype(bf16), (((0,), (0,)), ((), ())), preferred_element_type=f32)

    big = 2 * MOE_SB
    nbig = (nrows + MOE_SB - 1) // big

    def big_pass(i, carry):
        run_pass(i * big, big)
        return carry

    lax.fori_loop(0, nbig, big_pass, 0)

    @pl.when(nrows > nbig * big)
    def _():
        run_pass(nbig * big, MOE_SB)

    @pl.when(e == N_EXPERTS - 1)
    def _():
        m = mod_ref[...]
        o_ref[...] = _ln(ALPHA * x_ref[...] + m[:, 5 * D:6 * D] * acc_scr[...], lg_ref[...], lb_ref[...])


def _moe_call(x, mod, p, seq_len, cst, tm=MOE_TM):
    n = x.shape[0]
    tm = min(tm, n)
    fw = D_FF_EXPERT
    tiles_per_mod = seq_len // tm if mod.shape[0] > 1 else n // tm
    full = lambda t: pl.BlockSpec(t.shape, lambda i, g: (0,) * t.ndim)
    tri = cst['tri_tok'][:tm, :tm]
    return pl.pallas_call(
        _moe_kernel, grid=(n // tm, N_EXPERTS),
        in_specs=[pl.BlockSpec((tm, D), lambda i, g: (i, 0)),
                  pl.BlockSpec((None, 1, 6 * D), lambda i, g: (i // tiles_per_mod, 0, 0)),
                  pl.BlockSpec((None, None, D, fw), lambda i, g: (p['index'], g, 0, 0)),
                  pl.BlockSpec((None, None, D, fw), lambda i, g: (p['index'], g, 0, 0)),
                  pl.BlockSpec((None, None, fw, D), lambda i, g: (p['index'], g, 0, 0)),
                  full(p['ln2_g']), full(p['ln2_b']), full(p['router']), full(tri)],
        out_specs=pl.BlockSpec((tm, D), lambda i, g: (i, 0)),
        out_shape=jax.ShapeDtypeStruct((n, D), f32),
        scratch_shapes=[pltpu.VMEM((tm, D), bf16), pltpu.VMEM((tm, D), f32),
                        pltpu.VMEM((tm, LANE), bf16), pltpu.VMEM((LANE, tm), f32)],
        compiler_params=_cparams(("arbitrary", "arbitrary")), name="moe_ffn",
    )(x, mod, p['w_gate'], p['w_up'], p['w_down'], p['ln2_g'], p['ln2_b'], p['router'], tri)


def _constants():
    hb = np.arange(GW) // DH
    mbd = (hb[:, None] == hb[None, :]).astype(np.float32)
    r = np.arange(CH)
    t = np.arange(MOE_TM)
    e = np.zeros((LANE, 3 * GW), np.float32)
    for h in range(NH):
        e[L_BETA + h, h * DH:(h + 1) * DH] = 1.0
        for d in range(2):
            e[L_DEC + NH * d + h, (1 + d) * GW + h * DH:(1 + d) * GW + (h + 1) * DH] = 1.0
    return dict(mbd=jnp.asarray(mbd, bf16), mbdf=jnp.asarray(mbd, f32),
                tril=jnp.asarray(r[:, None] >= r[None, :], bf16), triu=jnp.asarray(r[:, None] <= r[None, :], bf16),
                e_gdn=jnp.asarray(e, bf16), tri_tok=jnp.asarray(t[:, None] <= t[None, :], bf16))


_IN_SIZES = (GW, GW, GW, GW, GW, GW, 2 * GLA_LR, GW, GW, GW, GW, NH, 2 * NH, GW, GW, GW)


def _pack_w_in(w_in):
    offs = np.concatenate([[0], np.cumsum(_IN_SIZES)])
    big = [i for i, s in enumerate(_IN_SIZES) if s == GW]
    small = [i for i, s in enumerate(_IN_SIZES) if s != GW]
    parts = [w_in[..., int(offs[i]):int(offs[i + 1])] for i in big + small]
    pad = NU - int(offs[-1])
    parts.append(jnp.zeros(w_in.shape[:-1] + (pad,), w_in.dtype))
    return jnp.concatenate(parts, axis=-1).astype(bf16)


def _to_bd(s, transpose):
    if transpose:
        s = jnp.swapaxes(s, -1, -2)
    b = s.shape[0]
    z = jnp.zeros_like(s)
    rows = [jnp.concatenate([s[:, :, h] if g == h else z[:, :, h] for g in range(NH)], axis=-1) for h in range(NH)]
    return jnp.concatenate(rows, axis=-2).reshape(b, 2, GW, GW)


def _from_bd(s, transpose):
    out = jnp.stack([s[:, :, h * DH:(h + 1) * DH, h * DH:(h + 1) * DH] for h in range(NH)], axis=2)
    return jnp.swapaxes(out, -1, -2) if transpose else out


def _stream_layer(x, mod, nseq, seq_len, s0_gla, s0_gdn, p, cst, colmajor, moe):
    tpm_512 = max(seq_len // 512, 1) if mod.shape[0] > 1 else x.shape[0] // 512
    u, a, ysc = _in_conv_call(x, mod, p['w_in'], p['layer'], tpm_512, seq_len, p)
    ygla, sgla = _gla_call(u, nseq, seq_len, s0_gla, p, cst)
    o_f, o_b, sgdn = _gdn_call(u, nseq, seq_len, s0_gdn, p, cst, colmajor)
    x1 = _out_call(a, ygla, o_f, o_b, u, ysc, x, mod, p, cst, tpm_512, seq_len, colmajor)
    x2 = _moe_call(x1, mod, p['ffn'], seq_len, cst) if moe else _ffn_call(x1, mod, p['ffn'], seq_len)
    return x2, sgla, sgdn


def kernel(x_prompt, x_sample, c, state_gla, state_gdn, c_ctx, w_ada, b_ada, w_in, w_out, conv_w, conv_b, conv_ln_g, conv_ln_b, conv_pw, gla_w_lr, gla_b_lr, gla_norm_g, gdn_conv_w, gdn_a_log, gdn_dt_bias, gdn_norm_g, sc_conv_w, ln1_g, ln1_b, ln2_g, ln2_b, ffn_w_gate, ffn_w_up, ffn_w_down, moe_router, moe_w_gate, moe_w_up, moe_w_down):
    nb, seq, _ = x_prompt.shape
    db, dseq, _ = x_sample.shape
    depth = w_in.shape[0]
    cst = _constants()

    rows = 16
    cond = jnp.concatenate([c_ctx[None, :], c, jnp.zeros((rows - 1 - db, D), f32)], axis=0)
    mod = _mod_call(cond, w_ada, b_ada)

    w_in_p = _pack_w_in(w_in)
    w_out_b = w_out.astype(bf16)
    conv_pw_b = conv_pw.astype(bf16)
    wlr = jnp.zeros((depth, 2, LANE, GW), f32)
    wlr = wlr.at[:, 0, 0:GLA_LR].set(gla_w_lr[:, 0]).at[:, 1, GLA_LR:2 * GLA_LR].set(gla_w_lr[:, 1]).astype(bf16)
    rep = lambda t: jnp.repeat(t, DH, axis=-1)[:, :, None, :]
    row = lambda t: t[:, None, :]
    ffn_g, ffn_u, ffn_d = ffn_w_gate.astype(bf16), ffn_w_up.astype(bf16), ffn_w_down.astype(bf16)
    moe_g, moe_u, moe_d = moe_w_gate.astype(bf16), moe_w_up.astype(bf16), moe_w_down.astype(bf16)
    router = jnp.pad(moe_router, ((0, 0), (0, 0), (0, LANE - N_EXPERTS)))

    zero_bd = jnp.zeros((nb, 2, GW, GW), f32)
    y_p = x_prompt.reshape(nb * seq, D)
    y_s = x_sample.reshape(db * dseq, D)
    gla_states, gdn_states = [], []
    for l in range(depth):
        j = l // 2
        moe = l % 2 == 1
        if moe:
            ffn = dict(w_gate=moe_g, w_up=moe_u, w_down=moe_d, index=j, router=router[j],
                       ln2_g=row(ln2_g)[l], ln2_b=row(ln2_b)[l])
        else:
            ffn = dict(w_gate=ffn_g, w_up=ffn_u, w_down=ffn_d, index=j, ln2_g=row(ln2_g)[l], ln2_b=row(ln2_b)[l])
        p = dict(layer=l, w_in=w_in_p, w_out=w_out_b, conv_w=conv_w[l], conv_b=row(conv_b)[l],
                 conv_ln_g=row(conv_ln_g)[l], conv_ln_b=row(conv_ln_b)[l], conv_pw=conv_pw_b[l],
                 sc_conv_w=sc_conv_w[l], gla_wlr=wlr[l], gla_blr=gla_b_lr[l][:, None, :],
                 gla_norm_g=row(gla_norm_g)[l], gdn_conv_w=gdn_conv_w[l], gdn_alog=rep(gdn_a_log)[l],
                 gdn_dtb=rep(gdn_dt_bias)[l], gdn_norm_g=row(gdn_norm_g)[l],
                 ln1_g=row(ln1_g)[l], ln1_b=row(ln1_b)[l], ffn=ffn)
        mod_ctx = mod[l, 0:1][:, None, :]
        mod_lat = mod[l, 1:1 + db][:, None, :]
        y_p, s_gla, s_gdn = _stream_layer(y_p, mod_ctx, nb, seq, zero_bd, zero_bd, p, cst, False, moe)
        gla_states.append(_from_bd(s_gla, True))
        gdn_states.append(_from_bd(s_gdn, False))
        y_s, _, _ = _stream_layer(y_s, mod_lat, db, dseq, _to_bd(state_gla[:, l], True),
                                  _to_bd(state_gdn[:, l], False), p, cst, True, moe)
    return (y_p.reshape(nb, seq, D), y_s.reshape(db, dseq, D),
            jnp.stack(gla_states, axis=1), jnp.stack(gdn_states, axis=1))
```

```python
import functools
import math

import numpy as np
import jax
import jax.numpy as jnp
from jax import lax
from jax.experimental import pallas as pl
from jax.experimental.pallas import tpu as pltpu

f32 = jnp.float32
bf16 = jnp.bfloat16

D = 1024
GW = 256
NH = 4
DH = 64
CH = 64
GRID_W = 64
CONV_K = 31
GLA_LR = 16
GLA_TAU = 16.0
DEPTH = 4
N_EXPERTS = 8
D_FF = 2816
D_FF_EXPERT = 1408
ALPHA = (2 * DEPTH) ** 0.25
EPS = 1e-5
NU = 13 * GW + 128
SMALL_BLK = 13 * GW // 128
LANE = 128
C_AVAL, C_AGATE, C_GQ, C_GK, C_GV, C_GG, C_DQ, C_DK, C_DV, C_DGATE, C_SB, C_SC, C_SX = range(13)
L_BETA = 2 * GLA_LR
L_DEC = L_BETA + NH

VMEM_LIMIT = 56 * 1024 * 1024


def _cparams(sem):
    return pltpu.CompilerParams(dimension_semantics=sem, vmem_limit_bytes=VMEM_LIMIT)


def _dot(a, b):
    return jnp.dot(a.astype(bf16), b.astype(bf16), preferred_element_type=f32)


def _dot_nt(a, b):
    return lax.dot_general(a.astype(bf16), b.astype(bf16), (((1,), (1,)), ((), ())),
                           preferred_element_type=f32)


def _dot_tn(a, b):
    return lax.dot_general(a.astype(bf16), b.astype(bf16), (((0,), (0,)), ((), ())),
                           preferred_element_type=f32)


def _split2(x):
    hi = x.astype(bf16)
    return hi, x - hi.astype(f32)


def _split3(x):
    hi = x.astype(bf16)
    r = x - hi.astype(f32)
    mid = r.astype(bf16)
    lo = (r - mid.astype(f32)).astype(bf16)
    return hi, mid, lo


def _dot01_left(m01, x):
    hi, r = _split2(x)
    d = lambda t: jnp.dot(m01, t, preferred_element_type=f32)
    return d(hi) + d(r.astype(bf16))


def _dot01_right(x, m01):
    hi, r = _split2(x)
    d = lambda t: jnp.dot(t, m01, preferred_element_type=f32)
    return d(hi) + d(r.astype(bf16))


def _bd(x, mask):
    xb = x.astype(bf16)
    return jnp.concatenate([xb, xb, xb, xb], axis=0) * mask


def _dot3(a_parts, b_parts, mask):
    ah, ar = a_parts
    bh, br = b_parts
    al = ar.astype(bf16)
    bdh = _bd(bh, mask)
    bdl = _bd(br, mask)
    d = lambda x, y: jnp.dot(x, y, preferred_element_type=f32)
    return d(ah, bdh) + d(al, bdh) + d(ah, bdl)


def _silu(x):
    return x * jax.nn.sigmoid(x)


def _ln(x, g, b):
    xc = x - jnp.mean(x, axis=-1, keepdims=True)
    var = jnp.mean(xc * xc, axis=-1, keepdims=True)
    return xc * lax.rsqrt(var + EPS) * g + b


def _head_iotas():
    rowi = lax.broadcasted_iota(jnp.int32, (CH, GW), 0)
    coli = lax.broadcasted_iota(jnp.int32, (CH, GW), 1) & (DH - 1)
    return rowi, coli


def _mod_kernel(c_ref, w_ref, b_ref, o_ref):
    o_ref[...] = jnp.dot(_silu(c_ref[...]), w_ref[...], precision=lax.Precision.HIGHEST,
                         preferred_element_type=f32) + b_ref[...]


def _mod_call(cond, w_ada, b_ada):
    nl = w_ada.shape[0]
    rows = cond.shape[0]
    return pl.pallas_call(
        _mod_kernel, grid=(nl, 6),
        in_specs=[pl.BlockSpec((rows, D), lambda l, n: (0, 0)),
                  pl.BlockSpec((None, D, D), lambda l, n: (l, 0, n)),
                  pl.BlockSpec((None, 1, D), lambda l, n: (l, 0, n))],
        out_specs=pl.BlockSpec((None, rows, D), lambda l, n: (l, 0, n)),
        out_shape=jax.ShapeDtypeStruct((nl, rows, 6 * D), f32),
        compiler_params=_cparams(("arbitrary", "arbitrary")), name="adaln_mod",
    )(cond, w_ada, b_ada.reshape(nl, 1, 6 * D))


TM_IN = 512
CT = 256
HALO = 16
SHALO = 8


def _in_conv_kernel(x_ref, mod_ref, w_ref, cw, cb, lng, lnb, pw, scw, u_ref, a_out, ysc_out,
                    a_st, s_st, sb_st, tail_a, tail_s, pad_a, pad_s, win_scr, *, subs_per_seq):
    i = pl.program_id(0)
    slot = i % 2
    pslot = 1 - slot

    @pl.when(i == 0)
    def _():
        a_st[...] = jnp.zeros(a_st.shape, f32)
        s_st[...] = jnp.zeros(s_st.shape, f32)
        sb_st[...] = jnp.zeros(sb_st.shape, f32)

    tail_a[...] = a_st[slot, TM_IN - HALO:TM_IN, :]
    tail_s[...] = s_st[slot, TM_IN - SHALO:TM_IN, :]

    m = mod_ref[...]
    h = x_ref[...] * (1.0 + m[:, D:2 * D]) + m[:, 0:D]
    u = jnp.dot(h.astype(bf16), w_ref[...], preferred_element_type=f32)
    u_ref[...] = u
    slab = lambda c: u[:, c * GW:(c + 1) * GW]
    a_st[slot] = slab(C_AVAL) * jax.nn.sigmoid(slab(C_AGATE))
    s_st[slot] = slab(C_SC) * slab(C_SX)
    sb_st[slot] = slab(C_SB)

    base = HALO - CONV_K // 2
    span = 8 * ((CONV_K + 6) // 8)
    for hf in range(TM_IN // CT):
        g = (i - 1) * (TM_IN // CT) + hf
        first = g % subs_per_seq == 0
        last = g % subs_per_seq == subs_per_seq - 1
        lo, hi = hf * CT, (hf + 1) * CT
        if hf == 0:
            prev_a, prev_s = tail_a[...], tail_s[...]
        else:
            prev_a, prev_s = a_st[pslot, lo - HALO:lo, :], s_st[pslot, lo - SHALO:lo, :]
        if hi == TM_IN:
            next_a, next_s = a_st[slot, 0:HALO, :], s_st[slot, 0:SHALO, :]
        else:
            next_a, next_s = a_st[pslot, hi:hi + HALO, :], s_st[pslot, hi:hi + SHALO, :]
        pad_a[0:HALO, :] = jnp.where(first, 0.0, prev_a)
        pad_a[HALO:HALO + CT, :] = a_st[pslot, lo:hi, :]
        pad_a[HALO + CT:2 * HALO + CT, :] = jnp.where(last, 0.0, next_a)
        acc = jnp.zeros((CT, GW), f32)
        for ph in range(8):
            win_scr[...] = pad_a[ph:ph + CT + span - 8, :]
            for al in range(0, span, 8):
                k = al + ph - base
                if 0 <= k < CONV_K:
                    acc = acc + win_scr[al:al + CT, :] * cw[k:k + 1, :]
        a = _ln(acc + cb[...], lng[...], lnb[...])
        a_out[lo:hi, :] = _dot(_silu(a), pw[...]).astype(bf16)

        pad_s[0:SHALO, :] = jnp.where(first, 0.0, prev_s)
        pad_s[SHALO:SHALO + CT, :] = s_st[pslot, lo:hi, :]
        pad_s[SHALO + CT:2 * SHALO + CT, :] = jnp.where(last, 0.0, next_s)
        acc = jnp.zeros((CT, GW), f32)
        for k in range(3):
            o = SHALO - 1 + k
            acc = acc + pad_s[o:o + CT, :] * scw[k:k + 1, :]
        ysc_out[lo:hi, :] = (sb_st[pslot, lo:hi, :] * acc).astype(bf16)


def _in_conv_call(x, mod, w, li, tiles_per_mod, seq_len, p):
    n = x.shape[0]
    tm = TM_IN
    nt = n // tm
    cur = lambda i: jnp.minimum(i, nt - 1)
    prv = lambda i: jnp.maximum(i - 1, 0)
    full = lambda a: pl.BlockSpec(a.shape, lambda i: (0,) * a.ndim)
    consts = (p['conv_w'], p['conv_b'], p['conv_ln_g'], p['conv_ln_b'], p['conv_pw'], p['sc_conv_w'])
    return pl.pallas_call(
        functools.partial(_in_conv_kernel, subs_per_seq=seq_len // CT), grid=(nt + 1,),
        in_specs=[pl.BlockSpec((tm, D), lambda i: (cur(i), 0)),
                  pl.BlockSpec((None, 1, 6 * D), lambda i: (cur(i) // tiles_per_mod, 0, 0)),
                  pl.BlockSpec((None, D, NU), lambda i: (li, 0, 0))] + [full(a) for a in consts],
        out_specs=[pl.BlockSpec((tm, NU), lambda i: (cur(i), 0)),
                   pl.BlockSpec((tm, GW), lambda i: (prv(i), 0)),
                   pl.BlockSpec((tm, GW), lambda i: (prv(i), 0))],
        out_shape=[jax.ShapeDtypeStruct((n, NU), f32), jax.ShapeDtypeStruct((n, GW), bf16),
                   jax.ShapeDtypeStruct((n, GW), bf16)],
        scratch_shapes=[pltpu.VMEM((2, tm, GW), f32), pltpu.VMEM((2, tm, GW), f32), pltpu.VMEM((2, tm, GW), f32),
                        pltpu.VMEM((HALO, GW), f32), pltpu.VMEM((SHALO, GW), f32),
                        pltpu.VMEM((CT + 2 * HALO, GW), f32), pltpu.VMEM((CT + 2 * SHALO, GW), f32),
                        pltpu.VMEM((CT + 8 * ((CONV_K + 6) // 8) - 8, GW), f32)],
        compiler_params=_cparams(("arbitrary",)), name="in_proj_conv",
    )(x, mod, w, *consts)


GLA_LOCK = 8


def _gla_kernel(q_ref, k_ref, v_ref, g_ref, sm_ref, s0_ref, wlr_ref, blr_ref, ng_ref,
                mbd_ref, mbdf_ref, tril_ref, triu_ref, y_ref, sfin_ref, st_scr, of_scr, *, tt, ntiles):
    p = pl.program_id(1)
    nch = tt // CH
    nlock = min(GLA_LOCK, nch)
    rowi, coli = _head_iotas()

    @pl.when(p == 0)
    def _():
        st_scr[...] = s0_ref[0]

    @pl.when(p == ntiles)
    def _():
        sfin_ref[0] = st_scr[...]
        st_scr[...] = s0_ref[1]

    def run(dirn):
        tri = tril_ref[...] if dirn == 0 else triu_ref[...]
        keep = (rowi >= coli) if dirn == 0 else (rowi <= coli)
        wlr = wlr_ref[dirn]
        blr = blr_ref[dirn]
        tile = p if dirn == 0 else 2 * ntiles - 1 - p

        def body(it, carry):
            cis = [it * nlock + c for c in range(nlock)]
            if dirn == 1:
                cis = [nch - 1 - ci for ci in cis]
            rows = [pl.ds(pl.multiple_of(ci * CH, CH), CH) for ci in cis]
            grows = [pl.ds(pl.multiple_of(tile * tt + ci * CH, CH), CH) for ci in cis]
            mbd = mbd_ref[...]
            zs = [jnp.dot(sm_ref[r, :].astype(bf16), wlr, preferred_element_type=f32) + blr for r in rows]
            cums = [_dot01_left(tri, jax.nn.log_sigmoid(z) * (1.0 / GLA_TAU)) for z in zs]
            tots = [cum[CH - 1:CH, :] if dirn == 0 else cum[0:1, :] for cum in cums]
            vs = [v_ref[r, :] for r in rows]
            qes = [q_ref[r, :] * (DH ** -0.5) * jnp.exp(cum) for r, cum in zip(rows, cums)]
            kes = [k_ref[r, :] * jnp.exp(-cum) for r, cum in zip(rows, cums)]
            kds = [k_ref[r, :] * jnp.exp(tot - cum) for r, cum, tot in zip(rows, cums, tots)]
            atts = [jnp.where(keep, _dot_nt(qe, _bd(ke, mbd)), 0.0) for qe, ke in zip(qes, kes)]
            ols = [_dot(att, _bd(v, mbd)) for att, v in zip(atts, vs)]
            upds = [_dot_tn(v, kd) for v, kd in zip(vs, kds)]
            st = st_scr[...]
            os_ = []
            for c in range(nlock):
                os_.append(ols[c] + _dot_nt(qes[c], st))
                st = st * jnp.exp(tots[c]) + mbdf_ref[...] * upds[c]
            st_scr[...] = st
            if dirn == 0:
                for c in range(nlock):
                    of_scr[grows[c], :] = os_[c]
            else:
                os_ = [of_scr[grows[c], :] + os_[c] for c in range(nlock)]
                mss = [_dot(o * o, mbd) * (1.0 / DH) for o in os_]
                for c in range(nlock):
                    y = os_[c] * lax.rsqrt(mss[c] + EPS) * ng_ref[...] * _silu(g_ref[rows[c], :])
                    y_ref[rows[c], :] = y.astype(bf16)
            return carry

        lax.fori_loop(0, nch // nlock, body, 0)

    @pl.when(p < ntiles)
    def _():
        run(0)

    @pl.when(p >= ntiles)
    def _():
        run(1)

    @pl.when(p == 2 * ntiles - 1)
    def _():
        sfin_ref[1] = st_scr[...]


def _gla_call(u, nseq, seq_len, s0, p, cst):
    tt = min(seq_len, 512)
    ntiles = seq_len // tt
    tmap = lambda q: jnp.where(q < ntiles, q, 2 * ntiles - 1 - q)
    main = lambda c: pl.BlockSpec((tt, GW), lambda s, q: (s * ntiles + tmap(q), c))
    full = lambda a: pl.BlockSpec(a.shape, lambda s, q: (0,) * a.ndim)
    consts = (p['gla_wlr'], p['gla_blr'], p['gla_norm_g'], cst['mbd'], cst['mbdf'], cst['tril'], cst['triu'])
    return pl.pallas_call(
        functools.partial(_gla_kernel, tt=tt, ntiles=ntiles), grid=(nseq, 2 * ntiles),
        in_specs=[main(C_GQ), main(C_GK), main(C_GV), main(C_GG),
                  pl.BlockSpec((tt, LANE), lambda s, q: (s * ntiles + tmap(q), SMALL_BLK)),
                  pl.BlockSpec((None, 2, GW, GW), lambda s, q: (s, 0, 0, 0))] + [full(a) for a in consts],
        out_specs=[pl.BlockSpec((tt, GW), lambda s, q: (s * ntiles + jnp.where(q < ntiles, ntiles - 1, 2 * ntiles - 1 - q), 0)),
                   pl.BlockSpec((None, 2, GW, GW), lambda s, q: (s, 0, 0, 0))],
        out_shape=[jax.ShapeDtypeStruct((nseq * seq_len, GW), bf16),
                   jax.ShapeDtypeStruct((nseq, 2, GW, GW), f32)],
        scratch_shapes=[pltpu.VMEM((GW, GW), f32), pltpu.VMEM((seq_len, GW), f32)],
        compiler_params=_cparams(("arbitrary", "arbitrary")), name="gla",
    )(u, u, u, u, u, s0, *consts)


GDN_G = 8
HR = 8
NLOCK = 4
INV_LEVELS = 6
INV_HI_LEVELS = 4


def _gdn_prep_kernel(*refs, g, ngroups, colmajor):
    if ngroups > 1:
        (q_ref, k_ref, v_ref, sm_ref, qp, kp, vp, qn, kn, vnx, cw_ref, alog_ref, dtb_ref, e_ref, mbd_ref,
         tril_ref, triu_ref, u_out, wq_out, qkd_out, kd_out, egl_out) = refs
    else:
        (q_ref, k_ref, v_ref, sm_ref, cw_ref, alog_ref, dtb_ref, e_ref, mbd_ref,
         tril_ref, triu_ref, u_out, wq_out, qkd_out, kd_out, egl_out) = refs
        qp = kp = vp = qn = kn = vnx = None
    gi = pl.program_id(1)
    rowi, coli = _head_iotas()
    row_first = rowi == 0
    row_last = rowi == CH - 1
    icat = jnp.where(rowi == coli, 1.0, 0.0)

    if colmajor:
        get = lambda ref, j: ref[:, j, :]
        last_row = lambda ref, j: ref[CH - 1, pl.ds(j, 1), :]
        first_row = lambda ref, j: ref[0, pl.ds(j, 1), :]
    else:
        get = lambda ref, j: ref[j]
        last_row = lambda ref, j: ref[j, CH - 1:CH, :]
        first_row = lambda ref, j: ref[j, 0:1, :]

    def shared(jj):
        jm = jnp.maximum(jj - 1, 0)
        jp = jnp.minimum(jj + 1, g - 1)

        def conv(ref, pref, nref, c0):
            x = get(ref, jj)
            if pref is None:
                pr_out = 0.0
                nx_out = 0.0
            else:
                pr_out = jnp.where(gi > 0, pref[HR - 1, HR - 1:HR, :], 0.0)
                nx_out = jnp.where(gi < ngroups - 1, nref[0, 0:1, :], 0.0)
            pr = jnp.where(jj > 0, last_row(ref, jm), pr_out)
            nx = jnp.where(jj < g - 1, first_row(ref, jp), nx_out)
            xd = jnp.where(row_first, pr, pltpu.roll(x, 1, 0))
            xu = jnp.where(row_last, nx, pltpu.roll(x, CH - 1, 0))
            return _silu(xd * cw_ref[0:1, c0:c0 + GW] + x * cw_ref[1:2, c0:c0 + GW] + xu * cw_ref[2:3, c0:c0 + GW])

        mbd = mbd_ref[...]
        q = conv(q_ref, qp, qn, 0)
        k = conv(k_ref, kp, kn, GW)
        v = conv(v_ref, vp, vnx, 2 * GW)
        q = q * lax.rsqrt(_dot(q * q, mbd) + 1e-6) * (DH ** -0.5)
        k = k * lax.rsqrt(_dot(k * k, mbd) + 1e-6)
        ex = _dot01_right(get(sm_ref, jj), e_ref[...])
        beta = jax.nn.sigmoid(ex[:, :GW])
        kb = k * beta
        gm = _dot_nt(jnp.concatenate([kb, q], axis=0), _bd(k, mbd))
        return q, k, kb, v * beta, ex, gm

    def gates(sh, dirn):
        q, k, kb, vb, ex, gm = sh
        la = -jnp.exp(alog_ref[dirn]) * jax.nn.softplus(ex[:, (1 + dirn) * GW:(2 + dirn) * GW] + dtb_ref[dirn])
        if dirn == 0:
            tri, keep, strict, keep_t = tril_ref[...], rowi >= coli, rowi > coli, rowi <= coli
        else:
            tri, keep, strict, keep_t = triu_ref[...], rowi <= coli, rowi < coli, rowi >= coli
        gx = _dot01_left(tri, la)
        grow = jnp.sum(jnp.where(keep_t, la, 0.0), axis=0, keepdims=True)
        decay = jnp.where(keep, jnp.exp(jnp.where(keep, gx - grow, 0.0)), 0.0)
        glast = gx[CH - 1:CH, :] if dirn == 0 else gx[0:1, :]
        nm = jnp.where(strict, -(gm[:CH] * decay), 0.0)
        return gx, glast, nm, gm[CH:] * decay

    def body(it, carry):
        jjs = [it * NLOCK + c for c in range(NLOCK)]
        shs = [shared(jj) for jj in jjs]
        chains = [(c, dirn) for c in range(NLOCK) for dirn in range(2)]
        gs = [gates(shs[c], dirn) for c, dirn in chains]
        mbd = mbd_ref[...]
        nms = [gt[2] for gt in gs]
        cat = lambda x, y: jnp.concatenate([x, y], axis=0)
        nsp = [_split2(nm) for nm in nms]
        pws = [_dot3(sp, sp, mbd) for sp in nsp]
        tinvs = [icat + nm for nm in nms]
        for lvl in range(1, INV_LEVELS):
            last = lvl == INV_LEVELS - 1
            lhs = (lambda t, q: t) if last else cat
            if lvl < INV_HI_LEVELS:
                psp = [_split2(pw) for pw in pws]
                tsp = [_split2(ti) for ti in tinvs]
                rs = [_dot3((lhs(th, ph), lhs(tr, pr)), (ph, pr), mbd) for (th, tr), (ph, pr) in zip(tsp, psp)]
            else:
                rs = [_dot(lhs(ti, pw), _bd(pw, mbd)) for ti, pw in zip(tinvs, pws)]
            tinvs = [ti + r[:CH] for ti, r in zip(tinvs, rs)]
            pws = [r[CH:] for r in rs]
        egs = [jnp.exp(gt[0]) for gt in gs]
        us = [_dot(ti, _bd(shs[c][3], mbd)) for ti, (c, dirn) in zip(tinvs, chains)]
        ws = [_dot(ti, _bd(shs[c][2] * eg, mbd)) for ti, eg, (c, dirn) in zip(tinvs, egs, chains)]
        for i, (c, dirn) in enumerate(chains):
            q, k = shs[c][0], shs[c][1]
            gx, glast, _, qkd = gs[i]
            jj = jjs[c]
            u_out[dirn, jj] = us[i]
            wq_out[dirn, jj] = jnp.concatenate([ws[i], q * egs[i]], axis=0).astype(bf16)
            qkd_out[dirn, jj] = qkd.astype(bf16)
            kd_out[dirn, jj] = (k * jnp.exp(glast - gx)).astype(bf16)
            egl_out[dirn, jj] = jnp.exp(glast)
        return carry

    lax.fori_loop(0, g // NLOCK, body, 0)


def _gdn_scan_kernel(uf, wqf, qkf, kdf, egf, ub, wqb, qkb, kdb, egb, s0_ref, mbd_ref, mbdf_ref,
                     of_out, ob_out, sfin_ref, s_scr, *, sg, cg, nt):
    t = pl.program_id(1)

    @pl.when(t == 0)
    def _():
        s_scr[...] = s0_ref[...]

    def body(c, carry):
        chains = [(sq, dirn) for sq in range(sg) for dirn in range(2)]
        src = lambda dirn: (uf, wqf, qkf, kdf, egf, of_out, c) if dirn == 0 else (ub, wqb, qkb, kdb, egb, ob_out, cg - 1 - c)
        mbd = mbd_ref[...]
        ss = [s_scr[sq, dirn] for sq, dirn in chains]
        wss = [jnp.dot(src(dirn)[1][sq, src(dirn)[6]], st.astype(bf16), preferred_element_type=f32)
               for st, (sq, dirn) in zip(ss, chains)]
        vns = [src(dirn)[0][sq, src(dirn)[6]] - ws[:CH] for ws, (sq, dirn) in zip(wss, chains)]
        os_ = [ws[CH:] + jnp.dot(src(dirn)[2][sq, src(dirn)[6]], _bd(vn, mbd), preferred_element_type=f32)
               for ws, vn, (sq, dirn) in zip(wss, vns, chains)]
        upd = [_dot_tn(src(dirn)[3][sq, src(dirn)[6]], vn) for vn, (sq, dirn) in zip(vns, chains)]
        for st, o, up, (sq, dirn) in zip(ss, os_, upd, chains):
            _, _, _, _, eg, o_out, cc = src(dirn)
            o_out[sq, cc] = o
            s_scr[sq, dirn] = st * eg[sq, cc] + mbdf_ref[...] * up
        return carry

    lax.fori_loop(0, cg, body, 0)

    @pl.when(t == nt - 1)
    def _():
        sfin_ref[...] = s_scr[...]


def _gdn_call(u, nseq, seq_len, s0, p, cst, colmajor):
    n = seq_len // CH
    if colmajor:
        assert seq_len == GRID_W * CH
        g, a = GDN_G, GRID_W
        blk = lambda w: (None, a, g, w)
        imap = lambda c: (lambda s, i: (s, 0, i, c))
    else:
        g, a = n, n
        blk = lambda w: (None, g, CH, w)
        imap = lambda c: (lambda s, i: (s, i, 0, c))
    ngroups = n // g
    u4 = u.reshape(nseq, a, CH, NU)
    main = lambda c: pl.BlockSpec(blk(GW), imap(c))
    full2 = lambda x: pl.BlockSpec(x.shape, lambda s, i: (0,) * x.ndim)
    halos, halo_specs = (), []
    if ngroups > 1:
        per = g // HR
        prev = lambda c: pl.BlockSpec((None, HR, HR, GW), lambda s, i: (s, a // HR - 1, jnp.maximum(i * per - 1, 0), c))
        nxt = lambda c: pl.BlockSpec((None, HR, HR, GW), lambda s, i: (s, 0, jnp.minimum((i + 1) * per, a // HR - 1), c))
        halo_specs = [prev(C_DQ), prev(C_DK), prev(C_DV), nxt(C_DQ), nxt(C_DK), nxt(C_DV)]
        halos = (u4,) * 6
    consts = (p['gdn_conv_w'], p['gdn_alog'], p['gdn_dtb'], cst['e_gdn'], cst['mbd'], cst['tril'], cst['triu'])
    per_chunk = lambda rows, dt: jax.ShapeDtypeStruct((nseq, 2, n, rows, GW), dt)
    per_chunk_spec = lambda rows: pl.BlockSpec((None, 2, g, rows, GW), lambda s, i: (s, 0, i, 0, 0))
    uu, wq, qkd, kd, egl = pl.pallas_call(
        functools.partial(_gdn_prep_kernel, g=g, ngroups=ngroups, colmajor=colmajor), grid=(nseq, ngroups),
        in_specs=[main(C_DQ), main(C_DK), main(C_DV), pl.BlockSpec(blk(LANE), imap(SMALL_BLK))] + halo_specs
                 + [full2(x) for x in consts],
        out_specs=[per_chunk_spec(CH), per_chunk_spec(2 * CH), per_chunk_spec(CH), per_chunk_spec(CH), per_chunk_spec(1)],
        out_shape=[per_chunk(CH, f32), per_chunk(2 * CH, bf16), per_chunk(CH, bf16), per_chunk(CH, bf16), per_chunk(1, f32)],
        compiler_params=_cparams(("arbitrary", "arbitrary")), name="gdn_prep",
    )(u4, u4, u4, u4, *halos, *consts)

    sg = math.gcd(nseq, 8)
    cg = min(n, 4)
    nt = n // cg
    fwd = lambda rows: pl.BlockSpec((sg, None, cg, rows, GW), lambda s, t: (s, 0, t, 0, 0))
    bwd = lambda rows: pl.BlockSpec((sg, None, cg, rows, GW), lambda s, t: (s, 1, nt - 1 - t, 0, 0))
    st_spec = pl.BlockSpec((sg, 2, GW, GW), lambda s, t: (s, 0, 0, 0))
    o_shape = jax.ShapeDtypeStruct((nseq, n, CH, GW), f32)
    o_f, o_b, sfin = pl.pallas_call(
        functools.partial(_gdn_scan_kernel, sg=sg, cg=cg, nt=nt), grid=(nseq // sg, nt),
        in_specs=[fwd(CH), fwd(2 * CH), fwd(CH), fwd(CH), fwd(1), bwd(CH), bwd(2 * CH), bwd(CH), bwd(CH), bwd(1),
                  st_spec, full2(cst['mbd']), full2(cst['mbdf'])],
        out_specs=[pl.BlockSpec((sg, cg, CH, GW), lambda s, t: (s, t, 0, 0)),
                   pl.BlockSpec((sg, cg, CH, GW), lambda s, t: (s, nt - 1 - t, 0, 0)), st_spec],
        out_shape=[o_shape, o_shape, jax.ShapeDtypeStruct((nseq, 2, GW, GW), f32)],
        scratch_shapes=[pltpu.VMEM((sg, 2, GW, GW), f32)],
        compiler_params=_cparams(("arbitrary", "arbitrary")), name="gdn_scan",
    )(uu, wq, qkd, kd, egl, uu, wq, qkd, kd, egl, s0, cst['mbd'], cst['mbdf'])

    return o_f, o_b, sfin


def _out_kernel(a_ref, gl_ref, of_ref, ob_ref, gate_ref, sc_ref, x_ref, mod_ref, w_ref, lg_ref, lb_ref,
                ng_ref, mbd_ref, o_ref, *, colmajor):
    if colmajor:
        o = jnp.concatenate([of_ref[:, rr, :] + ob_ref[:, rr, :] for rr in range(of_ref.shape[1])], axis=0)
    else:
        o = of_ref[...] + ob_ref[...]
    ms = _dot(o * o, mbd_ref[...]) * (1.0 / DH)
    gd = o * lax.rsqrt(ms + EPS) * ng_ref[...] * _silu(gate_ref[...])
    m = mod_ref[...]
    mix_in = jnp.concatenate([a_ref[...], gl_ref[...], gd.astype(bf16), sc_ref[...]], axis=1)
    mix = jnp.dot(mix_in, w_ref[...], preferred_element_type=f32)
    o_ref[...] = _ln(ALPHA * x_ref[...] + m[:, 2 * D:3 * D] * mix, lg_ref[...], lb_ref[...])


def _out_call(a, gl, o_f, o_b, u, sc, x, mod, p, cst, seq_len, colmajor, tm=512):
    n = x.shape[0]
    tm = min(tm, n)
    tiles_per_mod = seq_len // tm if mod.shape[0] > 1 else n // tm
    part = pl.BlockSpec((tm, GW), lambda i: (i, 0))
    full = lambda t: pl.BlockSpec(t.shape, lambda i: (0,) * t.ndim)
    if colmajor:
        rows = tm // GRID_W
        per_seq = seq_len // tm
        o_spec = pl.BlockSpec((None, GRID_W, rows, GW), lambda i: (i // per_seq, 0, i % per_seq, 0))
    else:
        o_f, o_b = o_f.reshape(n, GW), o_b.reshape(n, GW)
        o_spec = part
    return pl.pallas_call(
        functools.partial(_out_kernel, colmajor=colmajor), grid=(n // tm,),
        in_specs=[part, part, o_spec, o_spec, pl.BlockSpec((tm, GW), lambda i: (i, C_DGATE)), part,
                  pl.BlockSpec((tm, D), lambda i: (i, 0)),
                  pl.BlockSpec((None, 1, 6 * D), lambda i: (i // tiles_per_mod, 0, 0)),
                  pl.BlockSpec((None, D, D), lambda i: (p['layer'], 0, 0)),
                  full(p['ln1_g']), full(p['ln1_b']), full(p['gdn_norm_g']), full(cst['mbd'])],
        out_specs=pl.BlockSpec((tm, D), lambda i: (i, 0)),
        out_shape=jax.ShapeDtypeStruct((n, D), f32),
        compiler_params=_cparams(("arbitrary",)), name="out_proj_ln",
    )(a, gl, o_f, o_b, u, sc, x, mod, p['w_out'], p['ln1_g'], p['ln1_b'], p['gdn_norm_g'], cst['mbd'])


def _ffn_kernel(x_ref, mod_ref, wg_ref, wu_ref, wd_ref, lg_ref, lb_ref, o_ref):
    m = mod_ref[...]
    x = x_ref[...]
    hb = (x * (1.0 + m[:, 4 * D:5 * D]) + m[:, 3 * D:4 * D]).astype(bf16)
    t = _silu(jnp.dot(hb, wg_ref[...], preferred_element_type=f32)) * jnp.dot(hb, wu_ref[...], preferred_element_type=f32)
    f = jnp.dot(t.astype(bf16), wd_ref[...], preferred_element_type=f32)
    o_ref[...] = _ln(ALPHA * x + m[:, 5 * D:6 * D] * f, lg_ref[...], lb_ref[...])


def _ffn_call(x, mod, p, seq_len, tm=512):
    n = x.shape[0]
    tiles_per_mod = seq_len // tm if mod.shape[0] > 1 else n // tm
    once = lambda t: pl.BlockSpec(t.shape, lambda i: (0,) * t.ndim, pipeline_mode=pl.Buffered(1))
    wspec = lambda t: pl.BlockSpec((None,) + t.shape[1:], lambda i: (p['index'], 0, 0), pipeline_mode=pl.Buffered(1))
    return pl.pallas_call(
        _ffn_kernel, grid=(n // tm,),
        in_specs=[pl.BlockSpec((tm, D), lambda i: (i, 0)),
                  pl.BlockSpec((None, 1, 6 * D), lambda i: (i // tiles_per_mod, 0, 0)),
                  wspec(p['w_gate']), wspec(p['w_up']), wspec(p['w_down']), once(p['ln2_g']), once(p['ln2_b'])],
        out_specs=pl.BlockSpec((tm, D), lambda i: (i, 0)),
        out_shape=jax.ShapeDtypeStruct((n, D), f32),
        compiler_params=_cparams(("arbitrary",)), name="dense_ffn",
    )(x, mod, p['w_gate'], p['w_up'], p['w_down'], p['ln2_g'], p['ln2_b'])


MOE_TM = 1024
MOE_SB = 128
MOE_WIDE = 320


def _moe_kernel(x_ref, mod_ref, wg_ref, wu_ref, wd_ref, lg_ref, lb_ref, rt_ref, tri_ref, o_ref,
                h_scr, acc_scr, comb_scr, pos_scr):
    e = pl.program_id(1)
    tm = x_ref.shape[0]

    @pl.when(e == 0)
    def _():
        m = mod_ref[...]
        h = x_ref[...] * (1.0 + m[:, 4 * D:5 * D]) + m[:, 3 * D:4 * D]
        h_scr[...] = h.astype(bf16)
        acc_scr[...] = jnp.zeros(acc_scr.shape, f32)
        hh, hr = _split2(h)
        rh, rr = _split2(rt_ref[...])
        d = lambda x, y: jnp.dot(x, y, preferred_element_type=f32)
        logits = d(hh, rh) + d(hr.astype(bf16), rh) + d(hh, rr.astype(bf16))
        lane = lax.broadcasted_iota(jnp.int32, logits.shape, 1).astype(f32)
        neg = jnp.float32(-jnp.inf)
        logits = jnp.where(lane < N_EXPERTS, logits, neg)
        m1 = jnp.max(logits, axis=-1, keepdims=True)
        i1 = jnp.min(jnp.where(logits == m1, lane, float(LANE)), axis=-1, keepdims=True)
        rest = jnp.where(lane == i1, neg, logits)
        m2 = jnp.max(rest, axis=-1, keepdims=True)
        i2 = jnp.min(jnp.where(rest == m2, lane, float(LANE)), axis=-1, keepdims=True)
        e2 = jnp.exp(m2 - m1)
        p1 = 1.0 / (1.0 + e2)
        p2 = e2 / (1.0 + e2)
        sel1 = lane == i1
        sel2 = lane == i2
        comb = jnp.where(sel1, p1, 0.0) + jnp.where(sel2, p2, 0.0)
        c_hi, c_mid, c_lo = _split3(comb)
        comb_scr[...] = jnp.where(lane < N_EXPERTS, c_hi.astype(f32),
                                  jnp.where(lane < 2 * N_EXPERTS, pltpu.roll(c_mid.astype(f32), N_EXPERTS, 1),
                                            pltpu.roll(c_lo.astype(f32), 2 * N_EXPERTS, 1))).astype(bf16)
        routed = jnp.where(sel1 | sel2, 1.0, 0.0).astype(bf16)
        incl = lax.dot_general(routed, tri_ref[...], (((0,), (0,)), ((), ())), preferred_element_type=f32)
        tok = lax.broadcasted_iota(jnp.int32, incl.shape, 1)
        excl = jnp.where(tok == 0, 0.0, pltpu.roll(incl, 1, 1))
        pos_scr[...] = jnp.where(incl > excl, excl, -1.0)

    pos_row = pos_scr[pl.ds(e, 1), :]
    nrows = (jnp.max(pos_row) + 1.0).astype(jnp.int32)

    def run_pass(base, sb):
        slot = lax.broadcasted_iota(jnp.int32, (sb, tm), 0).astype(f32) + base.astype(f32)
        lane_sb = lax.broadcasted_iota(jnp.int32, (sb, LANE), 1)
        sel = jnp.where(pos_row == slot, 1.0, 0.0).astype(bf16)
        xs = jnp.dot(sel, h_scr[...], preferred_element_type=f32).astype(bf16)
        t = _silu(jnp.dot(xs, wg_ref[...], preferred_element_type=f32)) * jnp.dot(xs, wu_ref[...], preferred_element_type=f32)
        y = jnp.dot(t.astype(bf16), wd_ref[...], preferred_element_type=f32)
        cw3 = jnp.dot(sel, comb_scr[...], preferred_element_type=f32)
        cw = jnp.sum(jnp.where((lane_sb & (N_EXPERTS - 1)) == e, cw3, 0.0), axis=-1, keepdims=True)
        acc_scr[...] += lax.dot_general(sel, (y * cw).astype(bf16), (((0,), (0,)), ((), ())), preferred_element_type=f32)

    big = 2 * MOE_SB
    one_wide = (nrows > big) & (nrows <= MOE_WIDE)

    @pl.when(one_wide)
    def _():
        run_pass(jnp.int32(0), MOE_WIDE)

    @pl.when(jnp.logical_not(one_wide))
    def _():
        nbig = (nrows + MOE_SB - 1) // big

        def big_pass(i, carry):
            run_pass(i * big, big)
            return carry

        lax.fori_loop(0, nbig, big_pass, 0)

        @pl.when(nrows > nbig * big)
        def _():
            run_pass(nbig * big, MOE_SB)

    @pl.when(e == N_EXPERTS - 1)
    def _():
        m = mod_ref[...]
        o_ref[...] = _ln(ALPHA * x_ref[...] + m[:, 5 * D:6 * D] * acc_scr[...], lg_ref[...], lb_ref[...])


def _moe_call(x, mod, p, seq_len, cst, tm=MOE_TM):
    n = x.shape[0]
    tm = min(tm, n)
    fw = D_FF_EXPERT
    tiles_per_mod = seq_len // tm if mod.shape[0] > 1 else n // tm
    full = lambda t: pl.BlockSpec(t.shape, lambda i, g: (0,) * t.ndim)
    tri = cst['tri_tok'][:tm, :tm]
    return pl.pallas_call(
        _moe_kernel, grid=(n // tm, N_EXPERTS),
        in_specs=[pl.BlockSpec((tm, D), lambda i, g: (i, 0)),
                  pl.BlockSpec((None, 1, 6 * D), lambda i, g: (i // tiles_per_mod, 0, 0)),
                  pl.BlockSpec((None, None, D, fw), lambda i, g: (p['index'], g, 0, 0)),
                  pl.BlockSpec((None, None, D, fw), lambda i, g: (p['index'], g, 0, 0)),
                  pl.BlockSpec((None, None, fw, D), lambda i, g: (p['index'], g, 0, 0)),
                  full(p['ln2_g']), full(p['ln2_b']), full(p['router']), full(tri)],
        out_specs=pl.BlockSpec((tm, D), lambda i, g: (i, 0)),
        out_shape=jax.ShapeDtypeStruct((n, D), f32),
        scratch_shapes=[pltpu.VMEM((tm, D), bf16), pltpu.VMEM((tm, D), f32),
                        pltpu.VMEM((tm, LANE), bf16), pltpu.VMEM((LANE, tm), f32)],
        compiler_params=_cparams(("arbitrary", "arbitrary")), name="moe_ffn",
    )(x, mod, p['w_gate'], p['w_up'], p['w_down'], p['ln2_g'], p['ln2_b'], p['router'], tri)


def _constants():
    hb = np.arange(GW) // DH
    mbd = (hb[:, None] == hb[None, :]).astype(np.float32)
    r = np.arange(CH)
    t = np.arange(MOE_TM)
    e = np.zeros((LANE, 3 * GW), np.float32)
    for h in range(NH):
        e[L_BETA + h, h * DH:(h + 1) * DH] = 1.0
        for d in range(2):
            e[L_DEC + NH * d + h, (1 + d) * GW + h * DH:(1 + d) * GW + (h + 1) * DH] = 1.0
    return dict(mbd=jnp.asarray(mbd, bf16), mbdf=jnp.asarray(mbd, f32),
                tril=jnp.asarray(r[:, None] >= r[None, :], bf16), triu=jnp.asarray(r[:, None] <= r[None, :], bf16),
                e_gdn=jnp.asarray(e, bf16), tri_tok=jnp.asarray(t[:, None] <= t[None, :], bf16))


_IN_SIZES = (GW, GW, GW, GW, GW, GW, 2 * GLA_LR, GW, GW, GW, GW, NH, 2 * NH, GW, GW, GW)


def _pack_w_in(w_in):
    offs = np.concatenate([[0], np.cumsum(_IN_SIZES)])
    big = [i for i, s in enumerate(_IN_SIZES) if s == GW]
    small = [i for i, s in enumerate(_IN_SIZES) if s != GW]
    parts = [w_in[..., int(offs[i]):int(offs[i + 1])] for i in big + small]
    pad = NU - int(offs[-1])
    parts.append(jnp.zeros(w_in.shape[:-1] + (pad,), w_in.dtype))
    return jnp.concatenate(parts, axis=-1).astype(bf16)


def _to_bd(s, transpose):
    if transpose:
        s = jnp.swapaxes(s, -1, -2)
    b = s.shape[0]
    z = jnp.zeros_like(s)
    rows = [jnp.concatenate([s[:, :, h] if g == h else z[:, :, h] for g in range(NH)], axis=-1) for h in range(NH)]
    return jnp.concatenate(rows, axis=-2).reshape(b, 2, GW, GW)


def _from_bd(s, transpose):
    out = jnp.stack([s[:, :, h * DH:(h + 1) * DH, h * DH:(h + 1) * DH] for h in range(NH)], axis=2)
    return jnp.swapaxes(out, -1, -2) if transpose else out


def _stream_layer(x, mod, nseq, seq_len, s0_gla, s0_gdn, p, cst, colmajor, moe):
    tpm_512 = max(seq_len // 512, 1) if mod.shape[0] > 1 else x.shape[0] // 512
    u, a, ysc = _in_conv_call(x, mod, p['w_in'], p['layer'], tpm_512, seq_len, p)
    ygla, sgla = _gla_call(u, nseq, seq_len, s0_gla, p, cst)
    o_f, o_b, sgdn = _gdn_call(u, nseq, seq_len, s0_gdn, p, cst, colmajor)
    x1 = _out_call(a, ygla, o_f, o_b, u, ysc, x, mod, p, cst, seq_len, colmajor)
    x2 = _moe_call(x1, mod, p['ffn'], seq_len, cst) if moe else _ffn_call(x1, mod, p['ffn'], seq_len)
    return x2, sgla, sgdn


def kernel(x_prompt, x_sample, c, state_gla, state_gdn, c_ctx, w_ada, b_ada, w_in, w_out, conv_w, conv_b, conv_ln_g, conv_ln_b, conv_pw, gla_w_lr, gla_b_lr, gla_norm_g, gdn_conv_w, gdn_a_log, gdn_dt_bias, gdn_norm_g, sc_conv_w, ln1_g, ln1_b, ln2_g, ln2_b, ffn_w_gate, ffn_w_up, ffn_w_down, moe_router, moe_w_gate, moe_w_up, moe_w_down):
    nb, seq, _ = x_prompt.shape
    db, dseq, _ = x_sample.shape
    depth = w_in.shape[0]
    cst = _constants()

    rows = 16
    cond = jnp.concatenate([c_ctx[None, :], c, jnp.zeros((rows - 1 - db, D), f32)], axis=0)
    mod = _mod_call(cond, w_ada, b_ada)

    w_in_p = _pack_w_in(w_in)
    w_out_b = w_out.astype(bf16)
    conv_pw_b = conv_pw.astype(bf16)
    wlr = jnp.zeros((depth, 2, LANE, GW), f32)
    wlr = wlr.at[:, 0, 0:GLA_LR].set(gla_w_lr[:, 0]).at[:, 1, GLA_LR:2 * GLA_LR].set(gla_w_lr[:, 1]).astype(bf16)
    rep = lambda t: jnp.repeat(t, DH, axis=-1)[:, :, None, :]
    row = lambda t: t[:, None, :]
    ffn_g, ffn_u, ffn_d = ffn_w_gate.astype(bf16), ffn_w_up.astype(bf16), ffn_w_down.astype(bf16)
    moe_g, moe_u, moe_d = moe_w_gate.astype(bf16), moe_w_up.astype(bf16), moe_w_down.astype(bf16)
    router = jnp.pad(moe_router, ((0, 0), (0, 0), (0, LANE - N_EXPERTS)))

    zero_bd = jnp.zeros((nb, 2, GW, GW), f32)
    y_p = x_prompt.reshape(nb * seq, D)
    y_s = x_sample.reshape(db * dseq, D)
    gla_states, gdn_states = [], []
    for l in range(depth):
        j = l // 2
        moe = l % 2 == 1
        if moe:
            ffn = dict(w_gate=moe_g, w_up=moe_u, w_down=moe_d, index=j, router=router[j],
                       ln2_g=row(ln2_g)[l], ln2_b=row(ln2_b)[l])
        else:
            ffn = dict(w_gate=ffn_g, w_up=ffn_u, w_down=ffn_d, index=j, ln2_g=row(ln2_g)[l], ln2_b=row(ln2_b)[l])
        p = dict(layer=l, w_in=w_in_p, w_out=w_out_b, conv_w=conv_w[l], conv_b=row(conv_b)[l],
                 conv_ln_g=row(conv_ln_g)[l], conv_ln_b=row(conv_ln_b)[l], conv_pw=conv_pw_b[l],
                 sc_conv_w=sc_conv_w[l], gla_wlr=wlr[l], gla_blr=gla_b_lr[l][:, None, :],
                 gla_norm_g=row(gla_norm_g)[l], gdn_conv_w=gdn_conv_w[l], gdn_alog=rep(gdn_a_log)[l],
                 gdn_dtb=rep(gdn_dt_bias)[l], gdn_norm_g=row(gdn_norm_g)[l],
                 ln1_g=row(ln1_g)[l], ln1_b=row(ln1_b)[l], ffn=ffn)
        mod_ctx = mod[l, 0:1][:, None, :]
        mod_lat = mod[l, 1:1 + db][:, None, :]
        y_p, s_gla, s_gdn = _stream_layer(y_p, mod_ctx, nb, seq, zero_bd, zero_bd, p, cst, False, moe)
        gla_states.append(_from_bd(s_gla, True))
        gdn_states.append(_from_bd(s_gdn, False))
        y_s, _, _ = _stream_layer(y_s, mod_lat, db, dseq, _to_bd(state_gla[:, l], True),
                                  _to_bd(state_gdn[:, l], False), p, cst, True, moe)
    return (y_p.reshape(nb, seq, D), y_s.reshape(db, dseq, D),
            jnp.stack(gla_states, axis=1), jnp.stack(gdn_states, axis=1))
```

```python
import functools
import math

import numpy as np
import jax
import jax.numpy as jnp
from jax import lax
from jax.experimental import pallas as pl
from jax.experimental.pallas import tpu as pltpu

f32 = jnp.float32
bf16 = jnp.bfloat16

D = 1024
GW = 256
NH = 4
DH = 64
CH = 64
GRID_W = 64
CONV_K = 31
GLA_LR = 16
GLA_TAU = 16.0
DEPTH = 4
N_EXPERTS = 8
D_FF = 2816
D_FF_EXPERT = 1408
ALPHA = (2 * DEPTH) ** 0.25
EPS = 1e-5
NU = 13 * GW + 128
SMALL_BLK = 13 * GW // 128
LANE = 128
C_AVAL, C_AGATE, C_GQ, C_GK, C_GV, C_GG, C_DQ, C_DK, C_DV, C_DGATE, C_SB, C_SC, C_SX = range(13)
L_BETA = 2 * GLA_LR
L_DEC = L_BETA + NH

VMEM_LIMIT = 56 * 1024 * 1024


def _cparams(sem):
    return pltpu.CompilerParams(dimension_semantics=sem, vmem_limit_bytes=VMEM_LIMIT)


def _dot(a, b):
    return jnp.dot(a.astype(bf16), b.astype(bf16), preferred_element_type=f32)


def _dot_nt(a, b):
    return lax.dot_general(a.astype(bf16), b.astype(bf16), (((1,), (1,)), ((), ())),
                           preferred_element_type=f32)


def _dot_tn(a, b):
    return lax.dot_general(a.astype(bf16), b.astype(bf16), (((0,), (0,)), ((), ())),
                           preferred_element_type=f32)


def _split2(x):
    hi = x.astype(bf16)
    return hi, x - hi.astype(f32)


def _split3(x):
    hi = x.astype(bf16)
    r = x - hi.astype(f32)
    mid = r.astype(bf16)
    lo = (r - mid.astype(f32)).astype(bf16)
    return hi, mid, lo


def _dot01_left(m01, x):
    hi, r = _split2(x)
    d = lambda t: jnp.dot(m01, t, preferred_element_type=f32)
    return d(hi) + d(r.astype(bf16))


def _dot01_right(x, m01):
    hi, r = _split2(x)
    d = lambda t: jnp.dot(t, m01, preferred_element_type=f32)
    return d(hi) + d(r.astype(bf16))


def _bd(x, mask):
    xb = x.astype(bf16)
    return jnp.concatenate([xb, xb, xb, xb], axis=0) * mask


def _dot3(a_parts, b_parts, mask):
    ah, ar = a_parts
    bh, br = b_parts
    al = ar.astype(bf16)
    bdh = _bd(bh, mask)
    bdl = _bd(br, mask)
    d = lambda x, y: jnp.dot(x, y, preferred_element_type=f32)
    return d(ah, bdh) + d(al, bdh) + d(ah, bdl)


def _silu(x):
    return x * jax.nn.sigmoid(x)


def _ln(x, g, b):
    xc = x - jnp.mean(x, axis=-1, keepdims=True)
    var = jnp.mean(xc * xc, axis=-1, keepdims=True)
    return xc * lax.rsqrt(var + EPS) * g + b


def _head_iotas():
    rowi = lax.broadcasted_iota(jnp.int32, (CH, GW), 0)
    coli = lax.broadcasted_iota(jnp.int32, (CH, GW), 1) & (DH - 1)
    return rowi, coli


def _mod_kernel(c_ref, w_ref, b_ref, o_ref):
    o_ref[...] = jnp.dot(_silu(c_ref[...]), w_ref[...], precision=lax.Precision.HIGHEST,
                         preferred_element_type=f32) + b_ref[...]


def _mod_call(cond, w_ada, b_ada):
    nl = w_ada.shape[0]
    rows = cond.shape[0]
    return pl.pallas_call(
        _mod_kernel, grid=(nl, 6),
        in_specs=[pl.BlockSpec((rows, D), lambda l, n: (0, 0)),
                  pl.BlockSpec((None, D, D), lambda l, n: (l, 0, n)),
                  pl.BlockSpec((None, 1, D), lambda l, n: (l, 0, n))],
        out_specs=pl.BlockSpec((None, rows, D), lambda l, n: (l, 0, n)),
        out_shape=jax.ShapeDtypeStruct((nl, rows, 6 * D), f32),
        compiler_params=_cparams(("arbitrary", "arbitrary")), name="adaln_mod",
    )(cond, w_ada, b_ada.reshape(nl, 1, 6 * D))


TM_IN = 512
CT = 256
HALO = 16
SHALO = 8


def _in_conv_kernel(x_ref, mod_ref, w_ref, cw, cb, lng, lnb, pw, scw, u_ref, a_out, ysc_out,
                    a_st, s_st, sb_st, tail_a, tail_s, pad_a, pad_s, win_scr, *, subs_per_seq):
    i = pl.program_id(0)
    slot = i % 2
    pslot = 1 - slot

    @pl.when(i == 0)
    def _():
        a_st[...] = jnp.zeros(a_st.shape, f32)
        s_st[...] = jnp.zeros(s_st.shape, f32)
        sb_st[...] = jnp.zeros(sb_st.shape, f32)

    tail_a[...] = a_st[slot, TM_IN - HALO:TM_IN, :]
    tail_s[...] = s_st[slot, TM_IN - SHALO:TM_IN, :]

    m = mod_ref[...]
    h = x_ref[...] * (1.0 + m[:, D:2 * D]) + m[:, 0:D]
    u = jnp.dot(h.astype(bf16), w_ref[...], preferred_element_type=f32)
    u_ref[...] = u
    slab = lambda c: u[:, c * GW:(c + 1) * GW]
    a_st[slot] = slab(C_AVAL) * jax.nn.sigmoid(slab(C_AGATE))
    s_st[slot] = slab(C_SC) * slab(C_SX)
    sb_st[slot] = slab(C_SB)

    base = HALO - CONV_K // 2
    span = 8 * ((CONV_K + 6) // 8)
    for hf in range(TM_IN // CT):
        g = (i - 1) * (TM_IN // CT) + hf
        first = g % subs_per_seq == 0
        last = g % subs_per_seq == subs_per_seq - 1
        lo, hi = hf * CT, (hf + 1) * CT
        if hf == 0:
            prev_a, prev_s = tail_a[...], tail_s[...]
        else:
            prev_a, prev_s = a_st[pslot, lo - HALO:lo, :], s_st[pslot, lo - SHALO:lo, :]
        if hi == TM_IN:
            next_a, next_s = a_st[slot, 0:HALO, :], s_st[slot, 0:SHALO, :]
        else:
            next_a, next_s = a_st[pslot, hi:hi + HALO, :], s_st[pslot, hi:hi + SHALO, :]
        pad_a[0:HALO, :] = jnp.where(first, 0.0, prev_a)
        pad_a[HALO:HALO + CT, :] = a_st[pslot, lo:hi, :]
        pad_a[HALO + CT:2 * HALO + CT, :] = jnp.where(last, 0.0, next_a)
        acc = jnp.zeros((CT, GW), f32)
        for ph in range(8):
            win_scr[...] = pad_a[ph:ph + CT + span - 8, :]
            for al in range(0, span, 8):
                k = al + ph - base
                if 0 <= k < CONV_K:
                    acc = acc + win_scr[al:al + CT, :] * cw[k:k + 1, :]
        a = _ln(acc + cb[...], lng[...], lnb[...])
        a_out[lo:hi, :] = _dot(_silu(a), pw[...]).astype(bf16)

        pad_s[0:SHALO, :] = jnp.where(first, 0.0, prev_s)
        pad_s[SHALO:SHALO + CT, :] = s_st[pslot, lo:hi, :]
        pad_s[SHALO + CT:2 * SHALO + CT, :] = jnp.where(last, 0.0, next_s)
        acc = jnp.zeros((CT, GW), f32)
        for k in range(3):
            o = SHALO - 1 + k
            acc = acc + pad_s[o:o + CT, :] * scw[k:k + 1, :]
        ysc_out[lo:hi, :] = (sb_st[pslot, lo:hi, :] * acc).astype(bf16)


def _in_conv_call(x, mod, w, li, tiles_per_mod, seq_len, p):
    n = x.shape[0]
    tm = TM_IN
    nt = n // tm
    cur = lambda i: jnp.minimum(i, nt - 1)
    prv = lambda i: jnp.maximum(i - 1, 0)
    full = lambda a: pl.BlockSpec(a.shape, lambda i: (0,) * a.ndim)
    consts = (p['conv_w'], p['conv_b'], p['conv_ln_g'], p['conv_ln_b'], p['conv_pw'], p['sc_conv_w'])
    return pl.pallas_call(
        functools.partial(_in_conv_kernel, subs_per_seq=seq_len // CT), grid=(nt + 1,),
        in_specs=[pl.BlockSpec((tm, D), lambda i: (cur(i), 0)),
                  pl.BlockSpec((None, 1, 6 * D), lambda i: (cur(i) // tiles_per_mod, 0, 0)),
                  pl.BlockSpec((None, D, NU), lambda i: (li, 0, 0))] + [full(a) for a in consts],
        out_specs=[pl.BlockSpec((tm, NU), lambda i: (cur(i), 0)),
                   pl.BlockSpec((tm, GW), lambda i: (prv(i), 0)),
                   pl.BlockSpec((tm, GW), lambda i: (prv(i), 0))],
        out_shape=[jax.ShapeDtypeStruct((n, NU), f32), jax.ShapeDtypeStruct((n, GW), bf16),
                   jax.ShapeDtypeStruct((n, GW), bf16)],
        scratch_shapes=[pltpu.VMEM((2, tm, GW), f32), pltpu.VMEM((2, tm, GW), f32), pltpu.VMEM((2, tm, GW), f32),
                        pltpu.VMEM((HALO, GW), f32), pltpu.VMEM((SHALO, GW), f32),
                        pltpu.VMEM((CT + 2 * HALO, GW), f32), pltpu.VMEM((CT + 2 * SHALO, GW), f32),
                        pltpu.VMEM((CT + 8 * ((CONV_K + 6) // 8) - 8, GW), f32)],
        compiler_params=_cparams(("arbitrary",)), name="in_proj_conv",
    )(x, mod, w, *consts)


GLA_LOCK = 8


def _gla_kernel(q_ref, k_ref, v_ref, g_ref, sm_ref, s0_ref, wlr_ref, blr_ref, ng_ref,
                mbd_ref, mbdf_ref, tril_ref, triu_ref, y_ref, sfin_ref, st_scr, of_scr, *, tt, ntiles):
    p = pl.program_id(1)
    nch = tt // CH
    nlock = min(GLA_LOCK, nch)
    rowi, coli = _head_iotas()

    @pl.when(p == 0)
    def _():
        st_scr[...] = s0_ref[0]

    @pl.when(p == ntiles)
    def _():
        sfin_ref[0] = st_scr[...]
        st_scr[...] = s0_ref[1]

    def run(dirn):
        tri = tril_ref[...] if dirn == 0 else triu_ref[...]
        keep = (rowi >= coli) if dirn == 0 else (rowi <= coli)
        wlr = wlr_ref[dirn]
        blr = blr_ref[dirn]
        tile = p if dirn == 0 else 2 * ntiles - 1 - p

        def body(it, carry):
            cis = [it * nlock + c for c in range(nlock)]
            if dirn == 1:
                cis = [nch - 1 - ci for ci in cis]
            rows = [pl.ds(pl.multiple_of(ci * CH, CH), CH) for ci in cis]
            grows = [pl.ds(pl.multiple_of(tile * tt + ci * CH, CH), CH) for ci in cis]
            mbd = mbd_ref[...]
            zs = [jnp.dot(sm_ref[r, :].astype(bf16), wlr, preferred_element_type=f32) + blr for r in rows]
            cums = [_dot01_left(tri, jax.nn.log_sigmoid(z) * (1.0 / GLA_TAU)) for z in zs]
            tots = [cum[CH - 1:CH, :] if dirn == 0 else cum[0:1, :] for cum in cums]
            vs = [v_ref[r, :] for r in rows]
            qes = [q_ref[r, :] * (DH ** -0.5) * jnp.exp(cum) for r, cum in zip(rows, cums)]
            kes = [k_ref[r, :] * jnp.exp(-cum) for r, cum in zip(rows, cums)]
            kds = [k_ref[r, :] * jnp.exp(tot - cum) for r, cum, tot in zip(rows, cums, tots)]
            atts = [jnp.where(keep, _dot_nt(qe, _bd(ke, mbd)), 0.0) for qe, ke in zip(qes, kes)]
            ols = [_dot(att, _bd(v, mbd)) for att, v in zip(atts, vs)]
            upds = [_dot_tn(v, kd) for v, kd in zip(vs, kds)]
            st = st_scr[...]
            os_ = []
            for c in range(nlock):
                os_.append(ols[c] + _dot_nt(qes[c], st))
                st = st * jnp.exp(tots[c]) + mbdf_ref[...] * upds[c]
            st_scr[...] = st
            if dirn == 0:
                for c in range(nlock):
                    of_scr[grows[c], :] = os_[c]
            else:
                os_ = [of_scr[grows[c], :] + os_[c] for c in range(nlock)]
                mss = [_dot(o * o, mbd) * (1.0 / DH) for o in os_]
                for c in range(nlock):
                    y = os_[c] * lax.rsqrt(mss[c] + EPS) * ng_ref[...] * _silu(g_ref[rows[c], :])
                    y_ref[rows[c], :] = y.astype(bf16)
            return carry

        lax.fori_loop(0, nch // nlock, body, 0)

    @pl.when(p < ntiles)
    def _():
        run(0)

    @pl.when(p >= ntiles)
    def _():
        run(1)

    @pl.when(p == 2 * ntiles - 1)
    def _():
        sfin_ref[1] = st_scr[...]


def _gla_call(u, nseq, seq_len, s0, p, cst):
    tt = min(seq_len, 512)
    ntiles = seq_len // tt
    tmap = lambda q: jnp.where(q < ntiles, q, 2 * ntiles - 1 - q)
    main = lambda c: pl.BlockSpec((tt, GW), lambda s, q: (s * ntiles + tmap(q), c))
    full = lambda a: pl.BlockSpec(a.shape, lambda s, q: (0,) * a.ndim)
    consts = (p['gla_wlr'], p['gla_blr'], p['gla_norm_g'], cst['mbd'], cst['mbdf'], cst['tril'], cst['triu'])
    return pl.pallas_call(
        functools.partial(_gla_kernel, tt=tt, ntiles=ntiles), grid=(nseq, 2 * ntiles),
        in_specs=[main(C_GQ), main(C_GK), main(C_GV), main(C_GG),
                  pl.BlockSpec((tt, LANE), lambda s, q: (s * ntiles + tmap(q), SMALL_BLK)),
                  pl.BlockSpec((None, 2, GW, GW), lambda s, q: (s, 0, 0, 0))] + [full(a) for a in consts],
        out_specs=[pl.BlockSpec((tt, GW), lambda s, q: (s * ntiles + jnp.where(q < ntiles, ntiles - 1, 2 * ntiles - 1 - q), 0)),
                   pl.BlockSpec((None, 2, GW, GW), lambda s, q: (s, 0, 0, 0))],
        out_shape=[jax.ShapeDtypeStruct((nseq * seq_len, GW), bf16),
                   jax.ShapeDtypeStruct((nseq, 2, GW, GW), f32)],
        scratch_shapes=[pltpu.VMEM((GW, GW), f32), pltpu.VMEM((seq_len, GW), f32)],
        compiler_params=_cparams(("arbitrary", "arbitrary")), name="gla",
    )(u, u, u, u, u, s0, *consts)


GDN_G = 8
HR = 8
NLOCK = 4
INV_LEVELS = 6
INV_HI_LEVELS = 4


def _gdn_prep_kernel(*refs, g, ngroups, colmajor):
    if ngroups > 1:
        (q_ref, k_ref, v_ref, sm_ref, qp, kp, vp, qn, kn, vnx, cw_ref, alog_ref, dtb_ref, e_ref, mbd_ref,
         tril_ref, triu_ref, u_out, wq_out, qkd_out, kd_out, egl_out) = refs
    else:
        (q_ref, k_ref, v_ref, sm_ref, cw_ref, alog_ref, dtb_ref, e_ref, mbd_ref,
         tril_ref, triu_ref, u_out, wq_out, qkd_out, kd_out, egl_out) = refs
        qp = kp = vp = qn = kn = vnx = None
    gi = pl.program_id(1)
    rowi, coli = _head_iotas()
    row_first = rowi == 0
    row_last = rowi == CH - 1
    icat = jnp.where(rowi == coli, 1.0, 0.0)

    if colmajor:
        get = lambda ref, j: ref[:, j, :]
        last_row = lambda ref, j: ref[CH - 1, pl.ds(j, 1), :]
        first_row = lambda ref, j: ref[0, pl.ds(j, 1), :]
    else:
        get = lambda ref, j: ref[j]
        last_row = lambda ref, j: ref[j, CH - 1:CH, :]
        first_row = lambda ref, j: ref[j, 0:1, :]

    def shared(jj):
        jm = jnp.maximum(jj - 1, 0)
        jp = jnp.minimum(jj + 1, g - 1)

        def conv(ref, pref, nref, c0):
            x = get(ref, jj)
            if pref is None:
                pr_out = 0.0
                nx_out = 0.0
            else:
                pr_out = jnp.where(gi > 0, pref[HR - 1, HR - 1:HR, :], 0.0)
                nx_out = jnp.where(gi < ngroups - 1, nref[0, 0:1, :], 0.0)
            pr = jnp.where(jj > 0, last_row(ref, jm), pr_out)
            nx = jnp.where(jj < g - 1, first_row(ref, jp), nx_out)
            xd = jnp.where(row_first, pr, pltpu.roll(x, 1, 0))
            xu = jnp.where(row_last, nx, pltpu.roll(x, CH - 1, 0))
            return _silu(xd * cw_ref[0:1, c0:c0 + GW] + x * cw_ref[1:2, c0:c0 + GW] + xu * cw_ref[2:3, c0:c0 + GW])

        mbd = mbd_ref[...]
        q = conv(q_ref, qp, qn, 0)
        k = conv(k_ref, kp, kn, GW)
        v = conv(v_ref, vp, vnx, 2 * GW)
        q = q * lax.rsqrt(_dot(q * q, mbd) + 1e-6) * (DH ** -0.5)
        k = k * lax.rsqrt(_dot(k * k, mbd) + 1e-6)
        ex = _dot01_right(get(sm_ref, jj), e_ref[...])
        beta = jax.nn.sigmoid(ex[:, :GW])
        kb = k * beta
        gm = _dot_nt(jnp.concatenate([kb, q], axis=0), _bd(k, mbd))
        return q, k, kb, v * beta, ex, gm

    def gates(sh, dirn):
        q, k, kb, vb, ex, gm = sh
        la = -jnp.exp(alog_ref[dirn]) * jax.nn.softplus(ex[:, (1 + dirn) * GW:(2 + dirn) * GW] + dtb_ref[dirn])
        if dirn == 0:
            tri, keep, strict, keep_t = tril_ref[...], rowi >= coli, rowi > coli, rowi <= coli
        else:
            tri, keep, strict, keep_t = triu_ref[...], rowi <= coli, rowi < coli, rowi >= coli
        gx = _dot01_left(tri, la)
        grow = jnp.sum(jnp.where(keep_t, la, 0.0), axis=0, keepdims=True)
        decay = jnp.where(keep, jnp.exp(jnp.where(keep, gx - grow, 0.0)), 0.0)
        glast = gx[CH - 1:CH, :] if dirn == 0 else gx[0:1, :]
        nm = jnp.where(strict, -(gm[:CH] * decay), 0.0)
        return gx, glast, nm, gm[CH:] * decay

    def body(it, carry):
        jjs = [it * NLOCK + c for c in range(NLOCK)]
        shs = [shared(jj) for jj in jjs]
        chains = [(c, dirn) for c in range(NLOCK) for dirn in range(2)]
        gs = [gates(shs[c], dirn) for c, dirn in chains]
        mbd = mbd_ref[...]
        nms = [gt[2] for gt in gs]
        cat = lambda x, y: jnp.concatenate([x, y], axis=0)
        nsp = [_split2(nm) for nm in nms]
        pws = [_dot3(sp, sp, mbd) for sp in nsp]
        tinvs = [icat + nm for nm in nms]
        for lvl in range(1, INV_LEVELS):
            last = lvl == INV_LEVELS - 1
            lhs = (lambda t, q: t) if last else cat
            if lvl < INV_HI_LEVELS:
                psp = [_split2(pw) for pw in pws]
                tsp = [_split2(ti) for ti in tinvs]
                rs = [_dot3((lhs(th, ph), lhs(tr, pr)), (ph, pr), mbd) for (th, tr), (ph, pr) in zip(tsp, psp)]
            else:
                rs = [_dot(lhs(ti, pw), _bd(pw, mbd)) for ti, pw in zip(tinvs, pws)]
            tinvs = [ti + r[:CH] for ti, r in zip(tinvs, rs)]
            pws = [r[CH:] for r in rs]
        egs = [jnp.exp(gt[0]) for gt in gs]
        us = [_dot(ti, _bd(shs[c][3], mbd)) for ti, (c, dirn) in zip(tinvs, chains)]
        ws = [_dot(ti, _bd(shs[c][2] * eg, mbd)) for ti, eg, (c, dirn) in zip(tinvs, egs, chains)]
        for i, (c, dirn) in enumerate(chains):
            q, k = shs[c][0], shs[c][1]
            gx, glast, _, qkd = gs[i]
            jj = jjs[c]
            u_out[dirn, jj] = us[i]
            wq_out[dirn, jj] = jnp.concatenate([ws[i], q * egs[i]], axis=0).astype(bf16)
            qkd_out[dirn, jj] = qkd.astype(bf16)
            kd_out[dirn, jj] = (k * jnp.exp(glast - gx)).astype(bf16)
            egl_out[dirn, jj] = jnp.exp(glast)
        return carry

    lax.fori_loop(0, g // NLOCK, body, 0)


def _gdn_scan_kernel(uf, wqf, qkf, kdf, egf, ub, wqb, qkb, kdb, egb, s0_ref, mbd_ref, mbdf_ref,
                     of_out, ob_out, sfin_ref, s_scr, *, sg, cg, nt):
    t = pl.program_id(1)

    @pl.when(t == 0)
    def _():
        s_scr[...] = s0_ref[...]

    def body(c, carry):
        chains = [(sq, dirn) for sq in range(sg) for dirn in range(2)]
        src = lambda dirn: (uf, wqf, qkf, kdf, egf, of_out, c) if dirn == 0 else (ub, wqb, qkb, kdb, egb, ob_out, cg - 1 - c)
        mbd = mbd_ref[...]
        ss = [s_scr[sq, dirn] for sq, dirn in chains]
        wss = [jnp.dot(src(dirn)[1][sq, src(dirn)[6]], st.astype(bf16), preferred_element_type=f32)
               for st, (sq, dirn) in zip(ss, chains)]
        vns = [src(dirn)[0][sq, src(dirn)[6]] - ws[:CH] for ws, (sq, dirn) in zip(wss, chains)]
        os_ = [ws[CH:] + jnp.dot(src(dirn)[2][sq, src(dirn)[6]], _bd(vn, mbd), preferred_element_type=f32)
               for ws, vn, (sq, dirn) in zip(wss, vns, chains)]
        upd = [_dot_tn(src(dirn)[3][sq, src(dirn)[6]], vn) for vn, (sq, dirn) in zip(vns, chains)]
        for st, o, up, (sq, dirn) in zip(ss, os_, upd, chains):
            _, _, _, _, eg, o_out, cc = src(dirn)
            o_out[sq, cc] = o
            s_scr[sq, dirn] = st * eg[sq, cc] + mbdf_ref[...] * up
        return carry

    lax.fori_loop(0, cg, body, 0)

    @pl.when(t == nt - 1)
    def _():
        sfin_ref[...] = s_scr[...]


def _gdn_call(u, nseq, seq_len, s0, p, cst, colmajor):
    n = seq_len // CH
    if colmajor:
        assert seq_len == GRID_W * CH
        g, a = GDN_G, GRID_W
        blk = lambda w: (None, a, g, w)
        imap = lambda c: (lambda s, i: (s, 0, i, c))
    else:
        g, a = n, n
        blk = lambda w: (None, g, CH, w)
        imap = lambda c: (lambda s, i: (s, i, 0, c))
    ngroups = n // g
    u4 = u.reshape(nseq, a, CH, NU)
    main = lambda c: pl.BlockSpec(blk(GW), imap(c))
    full2 = lambda x: pl.BlockSpec(x.shape, lambda s, i: (0,) * x.ndim)
    halos, halo_specs = (), []
    if ngroups > 1:
        per = g // HR
        prev = lambda c: pl.BlockSpec((None, HR, HR, GW), lambda s, i: (s, a // HR - 1, jnp.maximum(i * per - 1, 0), c))
        nxt = lambda c: pl.BlockSpec((None, HR, HR, GW), lambda s, i: (s, 0, jnp.minimum((i + 1) * per, a // HR - 1), c))
        halo_specs = [prev(C_DQ), prev(C_DK), prev(C_DV), nxt(C_DQ), nxt(C_DK), nxt(C_DV)]
        halos = (u4,) * 6
    consts = (p['gdn_conv_w'], p['gdn_alog'], p['gdn_dtb'], cst['e_gdn'], cst['mbd'], cst['tril'], cst['triu'])
    per_chunk = lambda rows, dt: jax.ShapeDtypeStruct((nseq, 2, n, rows, GW), dt)
    per_chunk_spec = lambda rows: pl.BlockSpec((None, 2, g, rows, GW), lambda s, i: (s, 0, i, 0, 0))
    uu, wq, qkd, kd, egl = pl.pallas_call(
        functools.partial(_gdn_prep_kernel, g=g, ngroups=ngroups, colmajor=colmajor), grid=(nseq, ngroups),
        in_specs=[main(C_DQ), main(C_DK), main(C_DV), pl.BlockSpec(blk(LANE), imap(SMALL_BLK))] + halo_specs
                 + [full2(x) for x in consts],
        out_specs=[per_chunk_spec(CH), per_chunk_spec(2 * CH), per_chunk_spec(CH), per_chunk_spec(CH), per_chunk_spec(1)],
        out_shape=[per_chunk(CH, f32), per_chunk(2 * CH, bf16), per_chunk(CH, bf16), per_chunk(CH, bf16), per_chunk(1, f32)],
        compiler_params=_cparams(("arbitrary", "arbitrary")), name="gdn_prep",
    )(u4, u4, u4, u4, *halos, *consts)

    sg = math.gcd(nseq, 8)
    cg = min(n, 4)
    nt = n // cg
    fwd = lambda rows: pl.BlockSpec((sg, None, cg, rows, GW), lambda s, t: (s, 0, t, 0, 0))
    bwd = lambda rows: pl.BlockSpec((sg, None, cg, rows, GW), lambda s, t: (s, 1, nt - 1 - t, 0, 0))
    st_spec = pl.BlockSpec((sg, 2, GW, GW), lambda s, t: (s, 0, 0, 0))
    o_shape = jax.ShapeDtypeStruct((nseq, n, CH, GW), f32)
    o_f, o_b, sfin = pl.pallas_call(
        functools.partial(_gdn_scan_kernel, sg=sg, cg=cg, nt=nt), grid=(nseq // sg, nt),
        in_specs=[fwd(CH), fwd(2 * CH), fwd(CH), fwd(CH), fwd(1), bwd(CH), bwd(2 * CH), bwd(CH), bwd(CH), bwd(1),
                  st_spec, full2(cst['mbd']), full2(cst['mbdf'])],
        out_specs=[pl.BlockSpec((sg, cg, CH, GW), lambda s, t: (s, t, 0, 0)),
                   pl.BlockSpec((sg, cg, CH, GW), lambda s, t: (s, nt - 1 - t, 0, 0)), st_spec],
        out_shape=[o_shape, o_shape, jax.ShapeDtypeStruct((nseq, 2, GW, GW), f32)],
        scratch_shapes=[pltpu.VMEM((sg, 2, GW, GW), f32)],
        compiler_params=_cparams(("arbitrary", "arbitrary")), name="gdn_scan",
    )(uu, wq, qkd, kd, egl, uu, wq, qkd, kd, egl, s0, cst['mbd'], cst['mbdf'])

    return o_f, o_b, sfin


OUT_SPLIT = 2


def _out_kernel(a_ref, gl_ref, of_ref, ob_ref, gate_ref, sc_ref, x_ref, mod_ref, w_ref, lg_ref, lb_ref,
                ng_ref, mbd_ref, o_ref, *, colmajor):
    tm = x_ref.shape[0]
    hs = tm // OUT_SPLIT
    rows = [slice(h * hs, (h + 1) * hs) for h in range(OUT_SPLIT)]
    if colmajor:
        per = hs // GRID_W
        os_ = [jnp.concatenate([of_ref[:, rr, :] + ob_ref[:, rr, :] for rr in range(h * per, (h + 1) * per)], axis=0)
               for h in range(OUT_SPLIT)]
    else:
        os_ = [of_ref[r, :] + ob_ref[r, :] for r in rows]
    mss = [_dot(o * o, mbd_ref[...]) * (1.0 / DH) for o in os_]
    gds = [o * lax.rsqrt(ms + EPS) * ng_ref[...] * _silu(gate_ref[r, :]) for o, ms, r in zip(os_, mss, rows)]
    mixes = [jnp.dot(jnp.concatenate([a_ref[r, :], gl_ref[r, :], gd.astype(bf16), sc_ref[r, :]], axis=1), w_ref[...],
                     preferred_element_type=f32) for gd, r in zip(gds, rows)]
    m = mod_ref[...]
    for mix, r in zip(mixes, rows):
        o_ref[r, :] = _ln(ALPHA * x_ref[r, :] + m[:, 2 * D:3 * D] * mix, lg_ref[...], lb_ref[...])


def _out_call(a, gl, o_f, o_b, u, sc, x, mod, p, cst, seq_len, colmajor, tm=512):
    n = x.shape[0]
    tm = min(tm, n)
    tiles_per_mod = seq_len // tm if mod.shape[0] > 1 else n // tm
    part = pl.BlockSpec((tm, GW), lambda i: (i, 0))
    full = lambda t: pl.BlockSpec(t.shape, lambda i: (0,) * t.ndim)
    if colmajor:
        rows = tm // GRID_W
        per_seq = seq_len // tm
        o_spec = pl.BlockSpec((None, GRID_W, rows, GW), lambda i: (i // per_seq, 0, i % per_seq, 0))
    else:
        o_f, o_b = o_f.reshape(n, GW), o_b.reshape(n, GW)
        o_spec = part
    return pl.pallas_call(
        functools.partial(_out_kernel, colmajor=colmajor), grid=(n // tm,),
        in_specs=[part, part, o_spec, o_spec, pl.BlockSpec((tm, GW), lambda i: (i, C_DGATE)), part,
                  pl.BlockSpec((tm, D), lambda i: (i, 0)),
                  pl.BlockSpec((None, 1, 6 * D), lambda i: (i // tiles_per_mod, 0, 0)),
                  pl.BlockSpec((None, D, D), lambda i: (p['layer'], 0, 0)),
                  full(p['ln1_g']), full(p['ln1_b']), full(p['gdn_norm_g']), full(cst['mbd'])],
        out_specs=pl.BlockSpec((tm, D), lambda i: (i, 0)),
        out_shape=jax.ShapeDtypeStruct((n, D), f32),
        compiler_params=_cparams(("arbitrary",)), name="out_proj_ln",
    )(a, gl, o_f, o_b, u, sc, x, mod, p['w_out'], p['ln1_g'], p['ln1_b'], p['gdn_norm_g'], cst['mbd'])


def _ffn_kernel(x_ref, mod_ref, wg_ref, wu_ref, wd_ref, lg_ref, lb_ref, o_ref):
    m = mod_ref[...]
    x = x_ref[...]
    hb = (x * (1.0 + m[:, 4 * D:5 * D]) + m[:, 3 * D:4 * D]).astype(bf16)
    t = _silu(jnp.dot(hb, wg_ref[...], preferred_element_type=f32)) * jnp.dot(hb, wu_ref[...], preferred_element_type=f32)
    f = jnp.dot(t.astype(bf16), wd_ref[...], preferred_element_type=f32)
    o_ref[...] = _ln(ALPHA * x + m[:, 5 * D:6 * D] * f, lg_ref[...], lb_ref[...])


def _ffn_call(x, mod, p, seq_len, tm=512):
    n = x.shape[0]
    tiles_per_mod = seq_len // tm if mod.shape[0] > 1 else n // tm
    once = lambda t: pl.BlockSpec(t.shape, lambda i: (0,) * t.ndim, pipeline_mode=pl.Buffered(1))
    wspec = lambda t: pl.BlockSpec((None,) + t.shape[1:], lambda i: (p['index'], 0, 0), pipeline_mode=pl.Buffered(1))
    return pl.pallas_call(
        _ffn_kernel, grid=(n // tm,),
        in_specs=[pl.BlockSpec((tm, D), lambda i: (i, 0)),
                  pl.BlockSpec((None, 1, 6 * D), lambda i: (i // tiles_per_mod, 0, 0)),
                  wspec(p['w_gate']), wspec(p['w_up']), wspec(p['w_down']), once(p['ln2_g']), once(p['ln2_b'])],
        out_specs=pl.BlockSpec((tm, D), lambda i: (i, 0)),
        out_shape=jax.ShapeDtypeStruct((n, D), f32),
        compiler_params=_cparams(("arbitrary",)), name="dense_ffn",
    )(x, mod, p['w_gate'], p['w_up'], p['w_down'], p['ln2_g'], p['ln2_b'])


MOE_TM = 1024
MOE_SB = 128
MOE_WIDE = 288


def _moe_kernel(x_ref, mod_ref, wg_ref, wu_ref, wd_ref, lg_ref, lb_ref, rt_ref, tri_ref, o_ref,
                h_scr, acc_scr, comb_scr, pos_scr):
    e = pl.program_id(1)
    tm = x_ref.shape[0]

    @pl.when(e == 0)
    def _():
        m = mod_ref[...]
        h = x_ref[...] * (1.0 + m[:, 4 * D:5 * D]) + m[:, 3 * D:4 * D]
        h_scr[...] = h.astype(bf16)
        acc_scr[...] = jnp.zeros(acc_scr.shape, f32)
        hh, hr = _split2(h)
        rh, rr = _split2(rt_ref[...])
        d = lambda x, y: jnp.dot(x, y, preferred_element_type=f32)
        logits = d(hh, rh) + d(hr.astype(bf16), rh) + d(hh, rr.astype(bf16))
        lane = lax.broadcasted_iota(jnp.int32, logits.shape, 1).astype(f32)
        neg = jnp.float32(-jnp.inf)
        logits = jnp.where(lane < N_EXPERTS, logits, neg)
        m1 = jnp.max(logits, axis=-1, keepdims=True)
        i1 = jnp.min(jnp.where(logits == m1, lane, float(LANE)), axis=-1, keepdims=True)
        rest = jnp.where(lane == i1, neg, logits)
        m2 = jnp.max(rest, axis=-1, keepdims=True)
        i2 = jnp.min(jnp.where(rest == m2, lane, float(LANE)), axis=-1, keepdims=True)
        e2 = jnp.exp(m2 - m1)
        p1 = 1.0 / (1.0 + e2)
        p2 = e2 / (1.0 + e2)
        sel1 = lane == i1
        sel2 = lane == i2
        comb = jnp.where(sel1, p1, 0.0) + jnp.where(sel2, p2, 0.0)
        c_hi, c_mid, c_lo = _split3(comb)
        comb_scr[...] = jnp.where(lane < N_EXPERTS, c_hi.astype(f32),
                                  jnp.where(lane < 2 * N_EXPERTS, pltpu.roll(c_mid.astype(f32), N_EXPERTS, 1),
                                            pltpu.roll(c_lo.astype(f32), 2 * N_EXPERTS, 1))).astype(bf16)
        routed = jnp.where(sel1 | sel2, 1.0, 0.0).astype(bf16)
        incl = lax.dot_general(routed, tri_ref[...], (((0,), (0,)), ((), ())), preferred_element_type=f32)
        tok = lax.broadcasted_iota(jnp.int32, incl.shape, 1)
        excl = jnp.where(tok == 0, 0.0, pltpu.roll(incl, 1, 1))
        pos_scr[...] = jnp.where(incl > excl, excl, -1.0)

    pos_row = pos_scr[pl.ds(e, 1), :]
    nrows = (jnp.max(pos_row) + 1.0).astype(jnp.int32)

    def run_pass(base, sb):
        slot = lax.broadcasted_iota(jnp.int32, (sb, tm), 0).astype(f32) + base.astype(f32)
        lane_sb = lax.broadcasted_iota(jnp.int32, (sb, LANE), 1)
        sel = jnp.where(pos_row == slot, 1.0, 0.0).astype(bf16)
        xs = jnp.dot(sel, h_scr[...], preferred_element_type=f32).astype(bf16)
        t = _silu(jnp.dot(xs, wg_ref[...], preferred_element_type=f32)) * jnp.dot(xs, wu_ref[...], preferred_element_type=f32)
        y = jnp.dot(t.astype(bf16), wd_ref[...], preferred_element_type=f32)
        cw3 = jnp.dot(sel, comb_scr[...], preferred_element_type=f32)
        cw = jnp.sum(jnp.where((lane_sb & (N_EXPERTS - 1)) == e, cw3, 0.0), axis=-1, keepdims=True)
        acc_scr[...] += lax.dot_general(sel, (y * cw).astype(bf16), (((0,), (0,)), ((), ())), preferred_element_type=f32)

    big = 2 * MOE_SB
    one_wide = (nrows > big) & (nrows <= MOE_WIDE)

    @pl.when(one_wide)
    def _():
        run_pass(jnp.int32(0), MOE_WIDE)

    @pl.when(jnp.logical_not(one_wide))
    def _():
        nbig = (nrows + MOE_SB - 1) // big

        def big_pass(i, carry):
            run_pass(i * big, big)
            return carry

        lax.fori_loop(0, nbig, big_pass, 0)

        @pl.when(nrows > nbig * big)
        def _():
            run_pass(nbig * big, MOE_SB)

    @pl.when(e == N_EXPERTS - 1)
    def _():
        m = mod_ref[...]
        o_ref[...] = _ln(ALPHA * x_ref[...] + m[:, 5 * D:6 * D] * acc_scr[...], lg_ref[...], lb_ref[...])


def _moe_call(x, mod, p, seq_len, cst, tm=MOE_TM):
    n = x.shape[0]
    tm = min(tm, n)
    fw = D_FF_EXPERT
    tiles_per_mod = seq_len // tm if mod.shape[0] > 1 else n // tm
    full = lambda t: pl.BlockSpec(t.shape, lambda i, g: (0,) * t.ndim)
    tri = cst['tri_tok'][:tm, :tm]
    return pl.pallas_call(
        _moe_kernel, grid=(n // tm, N_EXPERTS),
        in_specs=[pl.BlockSpec((tm, D), lambda i, g: (i, 0)),
                  pl.BlockSpec((None, 1, 6 * D), lambda i, g: (i // tiles_per_mod, 0, 0)),
                  pl.BlockSpec((None, None, D, fw), lambda i, g: (p['index'], g, 0, 0)),
                  pl.BlockSpec((None, None, D, fw), lambda i, g: (p['index'], g, 0, 0)),
                  pl.BlockSpec((None, None, fw, D), lambda i, g: (p['index'], g, 0, 0)),
                  full(p['ln2_g']), full(p['ln2_b']), full(p['router']), full(tri)],
        out_specs=pl.BlockSpec((tm, D), lambda i, g: (i, 0)),
        out_shape=jax.ShapeDtypeStruct((n, D), f32),
        scratch_shapes=[pltpu.VMEM((tm, D), bf16), pltpu.VMEM((tm, D), f32),
                        pltpu.VMEM((tm, LANE), bf16), pltpu.VMEM((LANE, tm), f32)],
        compiler_params=_cparams(("arbitrary", "arbitrary")), name="moe_ffn",
    )(x, mod, p['w_gate'], p['w_up'], p['w_down'], p['ln2_g'], p['ln2_b'], p['router'], tri)


def _constants():
    hb = np.arange(GW) // DH
    mbd = (hb[:, None] == hb[None, :]).astype(np.float32)
    r = np.arange(CH)
    t = np.arange(MOE_TM)
    e = np.zeros((LANE, 3 * GW), np.float32)
    for h in range(NH):
        e[L_BETA + h, h * DH:(h + 1) * DH] = 1.0
        for d in range(2):
            e[L_DEC + NH * d + h, (1 + d) * GW + h * DH:(1 + d) * GW + (h + 1) * DH] = 1.0
    return dict(mbd=jnp.asarray(mbd, bf16), mbdf=jnp.asarray(mbd, f32),
                tril=jnp.asarray(r[:, None] >= r[None, :], bf16), triu=jnp.asarray(r[:, None] <= r[None, :], bf16),
                e_gdn=jnp.asarray(e, bf16), tri_tok=jnp.asarray(t[:, None] <= t[None, :], bf16))


_IN_SIZES = (GW, GW, GW, GW, GW, GW, 2 * GLA_LR, GW, GW, GW, GW, NH, 2 * NH, GW, GW, GW)


def _pack_w_in(w_in):
    offs = np.concatenate([[0], np.cumsum(_IN_SIZES)])
    big = [i for i, s in enumerate(_IN_SIZES) if s == GW]
    small = [i for i, s in enumerate(_IN_SIZES) if s != GW]
    parts = [w_in[..., int(offs[i]):int(offs[i + 1])] for i in big + small]
    pad = NU - int(offs[-1])
    parts.append(jnp.zeros(w_in.shape[:-1] + (pad,), w_in.dtype))
    return jnp.concatenate(parts, axis=-1).astype(bf16)


def _to_bd(s, transpose):
    if transpose:
        s = jnp.swapaxes(s, -1, -2)
    b = s.shape[0]
    z = jnp.zeros_like(s)
    rows = [jnp.concatenate([s[:, :, h] if g == h else z[:, :, h] for g in range(NH)], axis=-1) for h in range(NH)]
    return jnp.concatenate(rows, axis=-2).reshape(b, 2, GW, GW)


def _from_bd(s, transpose):
    out = jnp.stack([s[:, :, h * DH:(h + 1) * DH, h * DH:(h + 1) * DH] for h in range(NH)], axis=2)
    return jnp.swapaxes(out, -1, -2) if transpose else out


def _stream_layer(x, mod, nseq, seq_len, s0_gla, s0_gdn, p, cst, colmajor, moe):
    tpm_512 = max(seq_len // 512, 1) if mod.shape[0] > 1 else x.shape[0] // 512
    u, a, ysc = _in_conv_call(x, mod, p['w_in'], p['layer'], tpm_512, seq_len, p)
    ygla, sgla = _gla_call(u, nseq, seq_len, s0_gla, p, cst)
    o_f, o_b, sgdn = _gdn_call(u, nseq, seq_len, s0_gdn, p, cst, colmajor)
    x1 = _out_call(a, ygla, o_f, o_b, u, ysc, x, mod, p, cst, seq_len, colmajor)
    x2 = _moe_call(x1, mod, p['ffn'], seq_len, cst) if moe else _ffn_call(x1, mod, p['ffn'], seq_len)
    return x2, sgla, sgdn


def kernel(x_prompt, x_sample, c, state_gla, state_gdn, c_ctx, w_ada, b_ada, w_in, w_out, conv_w, conv_b, conv_ln_g, conv_ln_b, conv_pw, gla_w_lr, gla_b_lr, gla_norm_g, gdn_conv_w, gdn_a_log, gdn_dt_bias, gdn_norm_g, sc_conv_w, ln1_g, ln1_b, ln2_g, ln2_b, ffn_w_gate, ffn_w_up, ffn_w_down, moe_router, moe_w_gate, moe_w_up, moe_w_down):
    nb, seq, _ = x_prompt.shape
    db, dseq, _ = x_sample.shape
    depth = w_in.shape[0]
    cst = _constants()

    rows = 16
    cond = jnp.concatenate([c_ctx[None, :], c, jnp.zeros((rows - 1 - db, D), f32)], axis=0)
    mod = _mod_call(cond, w_ada, b_ada)

    w_in_p = _pack_w_in(w_in)
    w_out_b = w_out.astype(bf16)
    conv_pw_b = conv_pw.astype(bf16)
    wlr = jnp.zeros((depth, 2, LANE, GW), f32)
    wlr = wlr.at[:, 0, 0:GLA_LR].set(gla_w_lr[:, 0]).at[:, 1, GLA_LR:2 * GLA_LR].set(gla_w_lr[:, 1]).astype(bf16)
    rep = lambda t: jnp.repeat(t, DH, axis=-1)[:, :, None, :]
    row = lambda t: t[:, None, :]
    ffn_g, ffn_u, ffn_d = ffn_w_gate.astype(bf16), ffn_w_up.astype(bf16), ffn_w_down.astype(bf16)
    moe_g, moe_u, moe_d = moe_w_gate.astype(bf16), moe_w_up.astype(bf16), moe_w_down.astype(bf16)
    router = jnp.pad(moe_router, ((0, 0), (0, 0), (0, LANE - N_EXPERTS)))

    zero_bd = jnp.zeros((nb, 2, GW, GW), f32)
    y_p = x_prompt.reshape(nb * seq, D)
    y_s = x_sample.reshape(db * dseq, D)
    gla_states, gdn_states = [], []
    for l in range(depth):
        j = l // 2
        moe = l % 2 == 1
        if moe:
            ffn = dict(w_gate=moe_g, w_up=moe_u, w_down=moe_d, index=j, router=router[j],
                       ln2_g=row(ln2_g)[l], ln2_b=row(ln2_b)[l])
        else:
            ffn = dict(w_gate=ffn_g, w_up=ffn_u, w_down=ffn_d, index=j, ln2_g=row(ln2_g)[l], ln2_b=row(ln2_b)[l])
        p = dict(layer=l, w_in=w_in_p, w_out=w_out_b, conv_w=conv_w[l], conv_b=row(conv_b)[l],
                 conv_ln_g=row(conv_ln_g)[l], conv_ln_b=row(conv_ln_b)[l], conv_pw=conv_pw_b[l],
                 sc_conv_w=sc_conv_w[l], gla_wlr=wlr[l], gla_blr=gla_b_lr[l][:, None, :],
                 gla_norm_g=row(gla_norm_g)[l], gdn_conv_w=gdn_conv_w[l], gdn_alog=rep(gdn_a_log)[l],
                 gdn_dtb=rep(gdn_dt_bias)[l], gdn_norm_g=row(gdn_norm_g)[l],
                 ln1_g=row(ln1_g)[l], ln1_b=row(ln1_b)[l], ffn=ffn)
        mod_ctx = mod[l, 0:1][:, None, :]
        mod_lat = mod[l, 1:1 + db][:, None, :]
        y_p, s_gla, s_gdn = _stream_layer(y_p, mod_ctx, nb, seq, zero_bd, zero_bd, p, cst, False, moe)
        gla_states.append(_from_bd(s_gla, True))
        gdn_states.append(_from_bd(s_gdn, False))
        y_s, _, _ = _stream_layer(y_s, mod_lat, db, dseq, _to_bd(state_gla[:, l], True),
                                  _to_bd(state_gdn[:, l], False), p, cst, True, moe)
    return (y_p.reshape(nb, seq, D), y_s.reshape(db, dseq, D),
            jnp.stack(gla_states, axis=1), jnp.stack(gdn_states, axis=1))
```

```python
import functools
import math

import numpy as np
import jax
import jax.numpy as jnp
from jax import lax
from jax.experimental import pallas as pl
from jax.experimental.pallas import tpu as pltpu

f32 = jnp.float32
bf16 = jnp.bfloat16

D = 1024
GW = 256
NH = 4
DH = 64
CH = 64
GRID_W = 64
CONV_K = 31
GLA_LR = 16
GLA_TAU = 16.0
DEPTH = 4
N_EXPERTS = 8
D_FF = 2816
D_FF_EXPERT = 1408
ALPHA = (2 * DEPTH) ** 0.25
EPS = 1e-5
NU = 13 * GW + 128
SMALL_BLK = 13 * GW // 128
LANE = 128
C_AVAL, C_AGATE, C_GQ, C_GK, C_GV, C_GG, C_DQ, C_DK, C_DV, C_DGATE, C_SB, C_SC, C_SX = range(13)
L_BETA = 2 * GLA_LR
L_DEC = L_BETA + NH

VMEM_LIMIT = 56 * 1024 * 1024


def _cparams(sem):
    return pltpu.CompilerParams(dimension_semantics=sem, vmem_limit_bytes=VMEM_LIMIT)


def _dot(a, b):
    return jnp.dot(a.astype(bf16), b.astype(bf16), preferred_element_type=f32)


def _dot_nt(a, b):
    return lax.dot_general(a.astype(bf16), b.astype(bf16), (((1,), (1,)), ((), ())),
                           preferred_element_type=f32)


def _dot_tn(a, b):
    return lax.dot_general(a.astype(bf16), b.astype(bf16), (((0,), (0,)), ((), ())),
                           preferred_element_type=f32)


def _split2(x):
    hi = x.astype(bf16)
    return hi, x - hi.astype(f32)


def _split3(x):
    hi = x.astype(bf16)
    r = x - hi.astype(f32)
    mid = r.astype(bf16)
    lo = (r - mid.astype(f32)).astype(bf16)
    return hi, mid, lo


def _dot01_left(m01, x):
    hi, r = _split2(x)
    d = lambda t: jnp.dot(m01, t, preferred_element_type=f32)
    return d(hi) + d(r.astype(bf16))


def _dot01_right(x, m01):
    hi, r = _split2(x)
    d = lambda t: jnp.dot(t, m01, preferred_element_type=f32)
    return d(hi) + d(r.astype(bf16))


def _bd(x, mask):
    xb = x.astype(bf16)
    return jnp.concatenate([xb, xb, xb, xb], axis=0) * mask


def _dot3(a_parts, b_parts, mask):
    ah, ar = a_parts
    bh, br = b_parts
    al = ar.astype(bf16)
    bdh = _bd(bh, mask)
    bdl = _bd(br, mask)
    d = lambda x, y: jnp.dot(x, y, preferred_element_type=f32)
    return d(ah, bdh) + d(al, bdh) + d(ah, bdl)


def _silu(x):
    return x * jax.nn.sigmoid(x)


def _ln(x, g, b):
    xc = x - jnp.mean(x, axis=-1, keepdims=True)
    var = jnp.mean(xc * xc, axis=-1, keepdims=True)
    return xc * lax.rsqrt(var + EPS) * g + b


def _head_iotas():
    rowi = lax.broadcasted_iota(jnp.int32, (CH, GW), 0)
    coli = lax.broadcasted_iota(jnp.int32, (CH, GW), 1) & (DH - 1)
    return rowi, coli


def _mod_kernel(c_ref, w_ref, b_ref, o_ref):
    o_ref[...] = jnp.dot(_silu(c_ref[...]), w_ref[...], precision=lax.Precision.HIGHEST,
                         preferred_element_type=f32) + b_ref[...]


def _mod_call(cond, w_ada, b_ada):
    nl = w_ada.shape[0]
    rows = cond.shape[0]
    return pl.pallas_call(
        _mod_kernel, grid=(nl, 6),
        in_specs=[pl.BlockSpec((rows, D), lambda l, n: (0, 0)),
                  pl.BlockSpec((None, D, D), lambda l, n: (l, 0, n)),
                  pl.BlockSpec((None, 1, D), lambda l, n: (l, 0, n))],
        out_specs=pl.BlockSpec((None, rows, D), lambda l, n: (l, 0, n)),
        out_shape=jax.ShapeDtypeStruct((nl, rows, 6 * D), f32),
        compiler_params=_cparams(("arbitrary", "arbitrary")), name="adaln_mod",
    )(cond, w_ada, b_ada.reshape(nl, 1, 6 * D))


TM_IN = 512
CT = 256
HALO = 16
SHALO = 8


def _in_conv_kernel(x_ref, mod_ref, w_ref, cw, cb, lng, lnb, pw, scw, u_ref, a_out, ysc_out,
                    a_st, s_st, sb_st, tail_a, tail_s, pad_a, pad_s, win_scr, *, subs_per_seq):
    i = pl.program_id(0)
    slot = i % 2
    pslot = 1 - slot

    @pl.when(i == 0)
    def _():
        a_st[...] = jnp.zeros(a_st.shape, f32)
        s_st[...] = jnp.zeros(s_st.shape, f32)
        sb_st[...] = jnp.zeros(sb_st.shape, f32)

    tail_a[...] = a_st[slot, TM_IN - HALO:TM_IN, :]
    tail_s[...] = s_st[slot, TM_IN - SHALO:TM_IN, :]

    m = mod_ref[...]
    h = x_ref[...] * (1.0 + m[:, D:2 * D]) + m[:, 0:D]
    u = jnp.dot(h.astype(bf16), w_ref[...], preferred_element_type=f32)
    u_ref[...] = u
    slab = lambda c: u[:, c * GW:(c + 1) * GW]
    a_st[slot] = slab(C_AVAL) * jax.nn.sigmoid(slab(C_AGATE))
    s_st[slot] = slab(C_SC) * slab(C_SX)
    sb_st[slot] = slab(C_SB)

    base = HALO - CONV_K // 2
    span = 8 * ((CONV_K + 6) // 8)
    for hf in range(TM_IN // CT):
        g = (i - 1) * (TM_IN // CT) + hf
        first = g % subs_per_seq == 0
        last = g % subs_per_seq == subs_per_seq - 1
        lo, hi = hf * CT, (hf + 1) * CT
        if hf == 0:
            prev_a, prev_s = tail_a[...], tail_s[...]
        else:
            prev_a, prev_s = a_st[pslot, lo - HALO:lo, :], s_st[pslot, lo - SHALO:lo, :]
        if hi == TM_IN:
            next_a, next_s = a_st[slot, 0:HALO, :], s_st[slot, 0:SHALO, :]
        else:
            next_a, next_s = a_st[pslot, hi:hi + HALO, :], s_st[pslot, hi:hi + SHALO, :]
        pad_a[0:HALO, :] = jnp.where(first, 0.0, prev_a)
        pad_a[HALO:HALO + CT, :] = a_st[pslot, lo:hi, :]
        pad_a[HALO + CT:2 * HALO + CT, :] = jnp.where(last, 0.0, next_a)
        acc = jnp.zeros((CT, GW), f32)
        for ph in range(8):
            win_scr[...] = pad_a[ph:ph + CT + span - 8, :]
            for al in range(0, span, 8):
                k = al + ph - base
                if 0 <= k < CONV_K:
                    acc = acc + win_scr[al:al + CT, :] * cw[k:k + 1, :]
        a = _ln(acc + cb[...], lng[...], lnb[...])
        a_out[lo:hi, :] = _dot(_silu(a), pw[...]).astype(bf16)

        pad_s[0:SHALO, :] = jnp.where(first, 0.0, prev_s)
        pad_s[SHALO:SHALO + CT, :] = s_st[pslot, lo:hi, :]
        pad_s[SHALO + CT:2 * SHALO + CT, :] = jnp.where(last, 0.0, next_s)
        acc = jnp.zeros((CT, GW), f32)
        for k in range(3):
            o = SHALO - 1 + k
            acc = acc + pad_s[o:o + CT, :] * scw[k:k + 1, :]
        ysc_out[lo:hi, :] = (sb_st[pslot, lo:hi, :] * acc).astype(bf16)


def _in_conv_call(x, mod, w, li, tiles_per_mod, seq_len, p):
    n = x.shape[0]
    tm = TM_IN
    nt = n // tm
    cur = lambda i: jnp.minimum(i, nt - 1)
    prv = lambda i: jnp.maximum(i - 1, 0)
    full = lambda a: pl.BlockSpec(a.shape, lambda i: (0,) * a.ndim)
    consts = (p['conv_w'], p['conv_b'], p['conv_ln_g'], p['conv_ln_b'], p['conv_pw'], p['sc_conv_w'])
    return pl.pallas_call(
        functools.partial(_in_conv_kernel, subs_per_seq=seq_len // CT), grid=(nt + 1,),
        in_specs=[pl.BlockSpec((tm, D), lambda i: (cur(i), 0)),
                  pl.BlockSpec((None, 1, 6 * D), lambda i: (cur(i) // tiles_per_mod, 0, 0)),
                  pl.BlockSpec((None, D, NU), lambda i: (li, 0, 0))] + [full(a) for a in consts],
        out_specs=[pl.BlockSpec((tm, NU), lambda i: (cur(i), 0)),
                   pl.BlockSpec((tm, GW), lambda i: (prv(i), 0)),
                   pl.BlockSpec((tm, GW), lambda i: (prv(i), 0))],
        out_shape=[jax.ShapeDtypeStruct((n, NU), f32), jax.ShapeDtypeStruct((n, GW), bf16),
                   jax.ShapeDtypeStruct((n, GW), bf16)],
        scratch_shapes=[pltpu.VMEM((2, tm, GW), f32), pltpu.VMEM((2, tm, GW), f32), pltpu.VMEM((2, tm, GW), f32),
                        pltpu.VMEM((HALO, GW), f32), pltpu.VMEM((SHALO, GW), f32),
                        pltpu.VMEM((CT + 2 * HALO, GW), f32), pltpu.VMEM((CT + 2 * SHALO, GW), f32),
                        pltpu.VMEM((CT + 8 * ((CONV_K + 6) // 8) - 8, GW), f32)],
        compiler_params=_cparams(("arbitrary",)), name="in_proj_conv",
    )(x, mod, w, *consts)


GLA_LOCK = 16


def _gla_kernel(q_ref, k_ref, v_ref, g_ref, sm_ref, s0_ref, wlr_ref, blr_ref, ng_ref,
                mbd_ref, mbdf_ref, tril_ref, triu_ref, y_ref, sfin_ref, st_scr, of_scr, *, tt, ntiles):
    p = pl.program_id(1)
    nch = tt // CH
    nlock = min(GLA_LOCK, nch)
    rowi, coli = _head_iotas()

    @pl.when(p == 0)
    def _():
        st_scr[...] = s0_ref[0]

    @pl.when(p == ntiles)
    def _():
        sfin_ref[0] = st_scr[...]
        st_scr[...] = s0_ref[1]

    def run(dirn):
        tri = tril_ref[...] if dirn == 0 else triu_ref[...]
        keep = (rowi >= coli) if dirn == 0 else (rowi <= coli)
        wlr = wlr_ref[dirn]
        blr = blr_ref[dirn]
        tile = p if dirn == 0 else 2 * ntiles - 1 - p

        def body(it, carry):
            cis = [it * nlock + c for c in range(nlock)]
            if dirn == 1:
                cis = [nch - 1 - ci for ci in cis]
            rows = [pl.ds(pl.multiple_of(ci * CH, CH), CH) for ci in cis]
            grows = [pl.ds(pl.multiple_of(tile * tt + ci * CH, CH), CH) for ci in cis]
            mbd = mbd_ref[...]
            zs = [jnp.dot(sm_ref[r, :].astype(bf16), wlr, preferred_element_type=f32) + blr for r in rows]
            cums = [_dot01_left(tri, jax.nn.log_sigmoid(z) * (1.0 / GLA_TAU)) for z in zs]
            tots = [cum[CH - 1:CH, :] if dirn == 0 else cum[0:1, :] for cum in cums]
            vs = [v_ref[r, :] for r in rows]
            qes = [q_ref[r, :] * (DH ** -0.5) * jnp.exp(cum) for r, cum in zip(rows, cums)]
            kes = [k_ref[r, :] * jnp.exp(-cum) for r, cum in zip(rows, cums)]
            kds = [k_ref[r, :] * jnp.exp(tot - cum) for r, cum, tot in zip(rows, cums, tots)]
            atts = [jnp.where(keep, _dot_nt(qe, _bd(ke, mbd)), 0.0) for qe, ke in zip(qes, kes)]
            ols = [_dot(att, _bd(v, mbd)) for att, v in zip(atts, vs)]
            upds = [_dot_tn(v, kd) for v, kd in zip(vs, kds)]
            st = st_scr[...]
            os_ = []
            for c in range(nlock):
                os_.append(ols[c] + _dot_nt(qes[c], st))
                st = st * jnp.exp(tots[c]) + mbdf_ref[...] * upds[c]
            st_scr[...] = st
            if dirn == 0:
                for c in range(nlock):
                    of_scr[grows[c], :] = os_[c]
            else:
                os_ = [of_scr[grows[c], :] + os_[c] for c in range(nlock)]
                mss = [_dot(o * o, mbd) * (1.0 / DH) for o in os_]
                for c in range(nlock):
                    y = os_[c] * lax.rsqrt(mss[c] + EPS) * ng_ref[...] * _silu(g_ref[rows[c], :])
                    y_ref[rows[c], :] = y.astype(bf16)
            return carry

        lax.fori_loop(0, nch // nlock, body, 0)

    @pl.when(p < ntiles)
    def _():
        run(0)

    @pl.when(p >= ntiles)
    def _():
        run(1)

    @pl.when(p == 2 * ntiles - 1)
    def _():
        sfin_ref[1] = st_scr[...]


def _gla_call(u, nseq, seq_len, s0, p, cst):
    tt = min(seq_len, 1024)
    ntiles = seq_len // tt
    tmap = lambda q: jnp.where(q < ntiles, q, 2 * ntiles - 1 - q)
    main = lambda c: pl.BlockSpec((tt, GW), lambda s, q: (s * ntiles + tmap(q), c))
    full = lambda a: pl.BlockSpec(a.shape, lambda s, q: (0,) * a.ndim)
    consts = (p['gla_wlr'], p['gla_blr'], p['gla_norm_g'], cst['mbd'], cst['mbdf'], cst['tril'], cst['triu'])
    return pl.pallas_call(
        functools.partial(_gla_kernel, tt=tt, ntiles=ntiles), grid=(nseq, 2 * ntiles),
        in_specs=[main(C_GQ), main(C_GK), main(C_GV), main(C_GG),
                  pl.BlockSpec((tt, LANE), lambda s, q: (s * ntiles + tmap(q), SMALL_BLK)),
                  pl.BlockSpec((None, 2, GW, GW), lambda s, q: (s, 0, 0, 0))] + [full(a) for a in consts],
        out_specs=[pl.BlockSpec((tt, GW), lambda s, q: (s * ntiles + jnp.where(q < ntiles, ntiles - 1, 2 * ntiles - 1 - q), 0)),
                   pl.BlockSpec((None, 2, GW, GW), lambda s, q: (s, 0, 0, 0))],
        out_shape=[jax.ShapeDtypeStruct((nseq * seq_len, GW), bf16),
                   jax.ShapeDtypeStruct((nseq, 2, GW, GW), f32)],
        scratch_shapes=[pltpu.VMEM((GW, GW), f32), pltpu.VMEM((seq_len, GW), f32)],
        compiler_params=_cparams(("arbitrary", "arbitrary")), name="gla",
    )(u, u, u, u, u, s0, *consts)


GDN_G = 8
HR = 8
NLOCK = 4
INV_LEVELS = 6
INV_HI_LEVELS = 4


def _gdn_prep_kernel(*refs, g, ngroups, colmajor):
    if ngroups > 1:
        (q_ref, k_ref, v_ref, sm_ref, qp, kp, vp, qn, kn, vnx, cw_ref, alog_ref, dtb_ref, e_ref, mbd_ref,
         tril_ref, triu_ref, u_out, wq_out, qkd_out, kd_out, egl_out) = refs
    else:
        (q_ref, k_ref, v_ref, sm_ref, cw_ref, alog_ref, dtb_ref, e_ref, mbd_ref,
         tril_ref, triu_ref, u_out, wq_out, qkd_out, kd_out, egl_out) = refs
        qp = kp = vp = qn = kn = vnx = None
    gi = pl.program_id(1)
    rowi, coli = _head_iotas()
    row_first = rowi == 0
    row_last = rowi == CH - 1
    icat = jnp.where(rowi == coli, 1.0, 0.0)

    if colmajor:
        get = lambda ref, j: ref[:, j, :]
        last_row = lambda ref, j: ref[CH - 1, pl.ds(j, 1), :]
        first_row = lambda ref, j: ref[0, pl.ds(j, 1), :]
    else:
        get = lambda ref, j: ref[j]
        last_row = lambda ref, j: ref[j, CH - 1:CH, :]
        first_row = lambda ref, j: ref[j, 0:1, :]

    def shared(jj):
        jm = jnp.maximum(jj - 1, 0)
        jp = jnp.minimum(jj + 1, g - 1)

        def conv(ref, pref, nref, c0):
            x = get(ref, jj)
            if pref is None:
                pr_out = 0.0
                nx_out = 0.0
            else:
                pr_out = jnp.where(gi > 0, pref[HR - 1, HR - 1:HR, :], 0.0)
                nx_out = jnp.where(gi < ngroups - 1, nref[0, 0:1, :], 0.0)
            pr = jnp.where(jj > 0, last_row(ref, jm), pr_out)
            nx = jnp.where(jj < g - 1, first_row(ref, jp), nx_out)
            xd = jnp.where(row_first, pr, pltpu.roll(x, 1, 0))
            xu = jnp.where(row_last, nx, pltpu.roll(x, CH - 1, 0))
            return _silu(xd * cw_ref[0:1, c0:c0 + GW] + x * cw_ref[1:2, c0:c0 + GW] + xu * cw_ref[2:3, c0:c0 + GW])

        mbd = mbd_ref[...]
        q = conv(q_ref, qp, qn, 0)
        k = conv(k_ref, kp, kn, GW)
        v = conv(v_ref, vp, vnx, 2 * GW)
        q = q * lax.rsqrt(_dot(q * q, mbd) + 1e-6) * (DH ** -0.5)
        k = k * lax.rsqrt(_dot(k * k, mbd) + 1e-6)
        ex = _dot01_right(get(sm_ref, jj), e_ref[...])
        beta = jax.nn.sigmoid(ex[:, :GW])
        kb = k * beta
        gm = _dot_nt(jnp.concatenate([kb, q], axis=0), _bd(k, mbd))
        return q, k, kb, v * beta, ex, gm

    def gates(sh, dirn):
        q, k, kb, vb, ex, gm = sh
        la = -jnp.exp(alog_ref[dirn]) * jax.nn.softplus(ex[:, (1 + dirn) * GW:(2 + dirn) * GW] + dtb_ref[dirn])
        if dirn == 0:
            tri, keep, strict, keep_t = tril_ref[...], rowi >= coli, rowi > coli, rowi <= coli
        else:
            tri, keep, strict, keep_t = triu_ref[...], rowi <= coli, rowi < coli, rowi >= coli
        gx = _dot01_left(tri, la)
        grow = jnp.sum(jnp.where(keep_t, la, 0.0), axis=0, keepdims=True)
        decay = jnp.where(keep, jnp.exp(jnp.where(keep, gx - grow, 0.0)), 0.0)
        glast = gx[CH - 1:CH, :] if dirn == 0 else gx[0:1, :]
        nm = jnp.where(strict, -(gm[:CH] * decay), 0.0)
        return gx, glast, nm, gm[CH:] * decay

    def body(it, carry):
        jjs = [it * NLOCK + c for c in range(NLOCK)]
        shs = [shared(jj) for jj in jjs]
        chains = [(c, dirn) for c in range(NLOCK) for dirn in range(2)]
        gs = [gates(shs[c], dirn) for c, dirn in chains]
        mbd = mbd_ref[...]
        nms = [gt[2] for gt in gs]
        cat = lambda x, y: jnp.concatenate([x, y], axis=0)
        nsp = [_split2(nm) for nm in nms]
        pws = [_dot3(sp, sp, mbd) for sp in nsp]
        tinvs = [icat + nm for nm in nms]
        for lvl in range(1, INV_LEVELS):
            last = lvl == INV_LEVELS - 1
            lhs = (lambda t, q: t) if last else cat
            if lvl < INV_HI_LEVELS:
                psp = [_split2(pw) for pw in pws]
                tsp = [_split2(ti) for ti in tinvs]
                rs = [_dot3((lhs(th, ph), lhs(tr, pr)), (ph, pr), mbd) for (th, tr), (ph, pr) in zip(tsp, psp)]
            else:
                rs = [_dot(lhs(ti, pw), _bd(pw, mbd)) for ti, pw in zip(tinvs, pws)]
            tinvs = [ti + r[:CH] for ti, r in zip(tinvs, rs)]
            pws = [r[CH:] for r in rs]
        egs = [jnp.exp(gt[0]) for gt in gs]
        us = [_dot(ti, _bd(shs[c][3], mbd)) for ti, (c, dirn) in zip(tinvs, chains)]
        ws = [_dot(ti, _bd(shs[c][2] * eg, mbd)) for ti, eg, (c, dirn) in zip(tinvs, egs, chains)]
        for i, (c, dirn) in enumerate(chains):
            q, k = shs[c][0], shs[c][1]
            gx, glast, _, qkd = gs[i]
            jj = jjs[c]
            u_out[dirn, jj] = us[i]
            wq_out[dirn, jj] = jnp.concatenate([ws[i], q * egs[i]], axis=0).astype(bf16)
            qkd_out[dirn, jj] = qkd.astype(bf16)
            kd_out[dirn, jj] = (k * jnp.exp(glast - gx)).astype(bf16)
            egl_out[dirn, jj] = jnp.exp(glast)
        return carry

    lax.fori_loop(0, g // NLOCK, body, 0)


def _gdn_scan_kernel(uf, wqf, qkf, kdf, egf, ub, wqb, qkb, kdb, egb, s0_ref, mbd_ref, mbdf_ref,
                     of_out, ob_out, sfin_ref, s_scr, *, sg, cg, nt):
    t = pl.program_id(1)

    @pl.when(t == 0)
    def _():
        s_scr[...] = s0_ref[...]

    def body(c, carry):
        chains = [(sq, dirn) for sq in range(sg) for dirn in range(2)]
        src = lambda dirn: (uf, wqf, qkf, kdf, egf, of_out, c) if dirn == 0 else (ub, wqb, qkb, kdb, egb, ob_out, cg - 1 - c)
        mbd = mbd_ref[...]
        ss = [s_scr[sq, dirn] for sq, dirn in chains]
        wss = [jnp.dot(src(dirn)[1][sq, src(dirn)[6]], st.astype(bf16), preferred_element_type=f32)
               for st, (sq, dirn) in zip(ss, chains)]
        vns = [src(dirn)[0][sq, src(dirn)[6]] - ws[:CH] for ws, (sq, dirn) in zip(wss, chains)]
        os_ = [ws[CH:] + jnp.dot(src(dirn)[2][sq, src(dirn)[6]], _bd(vn, mbd), preferred_element_type=f32)
               for ws, vn, (sq, dirn) in zip(wss, vns, chains)]
        upd = [_dot_tn(src(dirn)[3][sq, src(dirn)[6]], vn) for vn, (sq, dirn) in zip(vns, chains)]
        for st, o, up, (sq, dirn) in zip(ss, os_, upd, chains):
            _, _, _, _, eg, o_out, cc = src(dirn)
            o_out[sq, cc] = o
            s_scr[sq, dirn] = st * eg[sq, cc] + mbdf_ref[...] * up
        return carry

    lax.fori_loop(0, cg, body, 0)

    @pl.when(t == nt - 1)
    def _():
        sfin_ref[...] = s_scr[...]


def _gdn_call(u, nseq, seq_len, s0, p, cst, colmajor):
    n = seq_len // CH
    if colmajor:
        assert seq_len == GRID_W * CH
        g, a = GDN_G, GRID_W
        blk = lambda w: (None, a, g, w)
        imap = lambda c: (lambda s, i: (s, 0, i, c))
    else:
        g, a = n, n
        blk = lambda w: (None, g, CH, w)
        imap = lambda c: (lambda s, i: (s, i, 0, c))
    ngroups = n // g
    u4 = u.reshape(nseq, a, CH, NU)
    main = lambda c: pl.BlockSpec(blk(GW), imap(c))
    full2 = lambda x: pl.BlockSpec(x.shape, lambda s, i: (0,) * x.ndim)
    halos, halo_specs = (), []
    if ngroups > 1:
        per = g // HR
        prev = lambda c: pl.BlockSpec((None, HR, HR, GW), lambda s, i: (s, a // HR - 1, jnp.maximum(i * per - 1, 0), c))
        nxt = lambda c: pl.BlockSpec((None, HR, HR, GW), lambda s, i: (s, 0, jnp.minimum((i + 1) * per, a // HR - 1), c))
        halo_specs = [prev(C_DQ), prev(C_DK), prev(C_DV), nxt(C_DQ), nxt(C_DK), nxt(C_DV)]
        halos = (u4,) * 6
    consts = (p['gdn_conv_w'], p['gdn_alog'], p['gdn_dtb'], cst['e_gdn'], cst['mbd'], cst['tril'], cst['triu'])
    per_chunk = lambda rows, dt: jax.ShapeDtypeStruct((nseq, 2, n, rows, GW), dt)
    per_chunk_spec = lambda rows: pl.BlockSpec((None, 2, g, rows, GW), lambda s, i: (s, 0, i, 0, 0))
    uu, wq, qkd, kd, egl = pl.pallas_call(
        functools.partial(_gdn_prep_kernel, g=g, ngroups=ngroups, colmajor=colmajor), grid=(nseq, ngroups),
        in_specs=[main(C_DQ), main(C_DK), main(C_DV), pl.BlockSpec(blk(LANE), imap(SMALL_BLK))] + halo_specs
                 + [full2(x) for x in consts],
        out_specs=[per_chunk_spec(CH), per_chunk_spec(2 * CH), per_chunk_spec(CH), per_chunk_spec(CH), per_chunk_spec(1)],
        out_shape=[per_chunk(CH, f32), per_chunk(2 * CH, bf16), per_chunk(CH, bf16), per_chunk(CH, bf16), per_chunk(1, f32)],
        compiler_params=_cparams(("arbitrary", "arbitrary")), name="gdn_prep",
    )(u4, u4, u4, u4, *halos, *consts)

    sg = math.gcd(nseq, 8)
    cg = min(n, 4)
    nt = n // cg
    fwd = lambda rows: pl.BlockSpec((sg, None, cg, rows, GW), lambda s, t: (s, 0, t, 0, 0))
    bwd = lambda rows: pl.BlockSpec((sg, None, cg, rows, GW), lambda s, t: (s, 1, nt - 1 - t, 0, 0))
    st_spec = pl.BlockSpec((sg, 2, GW, GW), lambda s, t: (s, 0, 0, 0))
    o_shape = jax.ShapeDtypeStruct((nseq, n, CH, GW), f32)
    o_f, o_b, sfin = pl.pallas_call(
        functools.partial(_gdn_scan_kernel, sg=sg, cg=cg, nt=nt), grid=(nseq // sg, nt),
        in_specs=[fwd(CH), fwd(2 * CH), fwd(CH), fwd(CH), fwd(1), bwd(CH), bwd(2 * CH), bwd(CH), bwd(CH), bwd(1),
                  st_spec, full2(cst['mbd']), full2(cst['mbdf'])],
        out_specs=[pl.BlockSpec((sg, cg, CH, GW), lambda s, t: (s, t, 0, 0)),
                   pl.BlockSpec((sg, cg, CH, GW), lambda s, t: (s, nt - 1 - t, 0, 0)), st_spec],
        out_shape=[o_shape, o_shape, jax.ShapeDtypeStruct((nseq, 2, GW, GW), f32)],
        scratch_shapes=[pltpu.VMEM((sg, 2, GW, GW), f32)],
        compiler_params=_cparams(("arbitrary", "arbitrary")), name="gdn_scan",
    )(uu, wq, qkd, kd, egl, uu, wq, qkd, kd, egl, s0, cst['mbd'], cst['mbdf'])

    return o_f, o_b, sfin


OUT_SPLIT = 2


def _out_kernel(a_ref, gl_ref, of_ref, ob_ref, gate_ref, sc_ref, x_ref, mod_ref, w_ref, lg_ref, lb_ref,
                ng_ref, mbd_ref, o_ref, *, colmajor):
    tm = x_ref.shape[0]
    hs = tm // OUT_SPLIT
    rows = [slice(h * hs, (h + 1) * hs) for h in range(OUT_SPLIT)]
    if colmajor:
        per = hs // GRID_W
        os_ = [jnp.concatenate([of_ref[:, rr, :] + ob_ref[:, rr, :] for rr in range(h * per, (h + 1) * per)], axis=0)
               for h in range(OUT_SPLIT)]
    else:
        os_ = [of_ref[r, :] + ob_ref[r, :] for r in rows]
    mss = [_dot(o * o, mbd_ref[...]) * (1.0 / DH) for o in os_]
    gds = [o * lax.rsqrt(ms + EPS) * ng_ref[...] * _silu(gate_ref[r, :]) for o, ms, r in zip(os_, mss, rows)]
    mixes = [jnp.dot(jnp.concatenate([a_ref[r, :], gl_ref[r, :], gd.astype(bf16), sc_ref[r, :]], axis=1), w_ref[...],
                     preferred_element_type=f32) for gd, r in zip(gds, rows)]
    m = mod_ref[...]
    for mix, r in zip(mixes, rows):
        o_ref[r, :] = _ln(ALPHA * x_ref[r, :] + m[:, 2 * D:3 * D] * mix, lg_ref[...], lb_ref[...])


def _out_call(a, gl, o_f, o_b, u, sc, x, mod, p, cst, seq_len, colmajor, tm=512):
    n = x.shape[0]
    tm = min(tm, n)
    tiles_per_mod = seq_len // tm if mod.shape[0] > 1 else n // tm
    part = pl.BlockSpec((tm, GW), lambda i: (i, 0))
    full = lambda t: pl.BlockSpec(t.shape, lambda i: (0,) * t.ndim)
    if colmajor:
        rows = tm // GRID_W
        per_seq = seq_len // tm
        o_spec = pl.BlockSpec((None, GRID_W, rows, GW), lambda i: (i // per_seq, 0, i % per_seq, 0))
    else:
        o_f, o_b = o_f.reshape(n, GW), o_b.reshape(n, GW)
        o_spec = part
    return pl.pallas_call(
        functools.partial(_out_kernel, colmajor=colmajor), grid=(n // tm,),
        in_specs=[part, part, o_spec, o_spec, pl.BlockSpec((tm, GW), lambda i: (i, C_DGATE)), part,
                  pl.BlockSpec((tm, D), lambda i: (i, 0)),
                  pl.BlockSpec((None, 1, 6 * D), lambda i: (i // tiles_per_mod, 0, 0)),
                  pl.BlockSpec((None, D, D), lambda i: (p['layer'], 0, 0)),
                  full(p['ln1_g']), full(p['ln1_b']), full(p['gdn_norm_g']), full(cst['mbd'])],
        out_specs=pl.BlockSpec((tm, D), lambda i: (i, 0)),
        out_shape=jax.ShapeDtypeStruct((n, D), f32),
        compiler_params=_cparams(("arbitrary",)), name="out_proj_ln",
    )(a, gl, o_f, o_b, u, sc, x, mod, p['w_out'], p['ln1_g'], p['ln1_b'], p['gdn_norm_g'], cst['mbd'])


def _ffn_kernel(x_ref, mod_ref, wg_ref, wu_ref, wd_ref, lg_ref, lb_ref, o_ref):
    m = mod_ref[...]
    x = x_ref[...]
    hb = (x * (1.0 + m[:, 4 * D:5 * D]) + m[:, 3 * D:4 * D]).astype(bf16)
    t = _silu(jnp.dot(hb, wg_ref[...], preferred_element_type=f32)) * jnp.dot(hb, wu_ref[...], preferred_element_type=f32)
    f = jnp.dot(t.astype(bf16), wd_ref[...], preferred_element_type=f32)
    o_ref[...] = _ln(ALPHA * x + m[:, 5 * D:6 * D] * f, lg_ref[...], lb_ref[...])


def _ffn_call(x, mod, p, seq_len, tm=512):
    n = x.shape[0]
    tiles_per_mod = seq_len // tm if mod.shape[0] > 1 else n // tm
    once = lambda t: pl.BlockSpec(t.shape, lambda i: (0,) * t.ndim, pipeline_mode=pl.Buffered(1))
    wspec = lambda t: pl.BlockSpec((None,) + t.shape[1:], lambda i: (p['index'], 0, 0), pipeline_mode=pl.Buffered(1))
    return pl.pallas_call(
        _ffn_kernel, grid=(n // tm,),
        in_specs=[pl.BlockSpec((tm, D), lambda i: (i, 0)),
                  pl.BlockSpec((None, 1, 6 * D), lambda i: (i // tiles_per_mod, 0, 0)),
                  wspec(p['w_gate']), wspec(p['w_up']), wspec(p['w_down']), once(p['ln2_g']), once(p['ln2_b'])],
        out_specs=pl.BlockSpec((tm, D), lambda i: (i, 0)),
        out_shape=jax.ShapeDtypeStruct((n, D), f32),
        compiler_params=_cparams(("arbitrary",)), name="dense_ffn",
    )(x, mod, p['w_gate'], p['w_up'], p['w_down'], p['ln2_g'], p['ln2_b'])


MOE_TM = 1024
MOE_SB = 128
MOE_WIDE = 320


def _moe_kernel(x_ref, mod_ref, wg_ref, wu_ref, wd_ref, lg_ref, lb_ref, rt_ref, tri_ref, o_ref,
                h_scr, acc_scr, comb_scr, pos_scr):
    e = pl.program_id(1)
    tm = x_ref.shape[0]

    @pl.when(e == 0)
    def _():
        m = mod_ref[...]
        h = x_ref[...] * (1.0 + m[:, 4 * D:5 * D]) + m[:, 3 * D:4 * D]
        h_scr[...] = h.astype(bf16)
        acc_scr[...] = jnp.zeros(acc_scr.shape, f32)
        hh, hr = _split2(h)
        rh, rr = _split2(rt_ref[...])
        d = lambda x, y: jnp.dot(x, y, preferred_element_type=f32)
        logits = d(hh, rh) + d(hr.astype(bf16), rh) + d(hh, rr.astype(bf16))
        lane = lax.broadcasted_iota(jnp.int32, logits.shape, 1).astype(f32)
        neg = jnp.float32(-jnp.inf)
        logits = jnp.where(lane < N_EXPERTS, logits, neg)
        m1 = jnp.max(logits, axis=-1, keepdims=True)
        i1 = jnp.min(jnp.where(logits == m1, lane, float(LANE)), axis=-1, keepdims=True)
        rest = jnp.where(lane == i1, neg, logits)
        m2 = jnp.max(rest, axis=-1, keepdims=True)
        i2 = jnp.min(jnp.where(rest == m2, lane, float(LANE)), axis=-1, keepdims=True)
        e2 = jnp.exp(m2 - m1)
        p1 = 1.0 / (1.0 + e2)
        p2 = e2 / (1.0 + e2)
        sel1 = lane == i1
        sel2 = lane == i2
        comb = jnp.where(sel1, p1, 0.0) + jnp.where(sel2, p2, 0.0)
        c_hi, c_mid, c_lo = _split3(comb)
        comb_scr[...] = jnp.where(lane < N_EXPERTS, c_hi.astype(f32),
                                  jnp.where(lane < 2 * N_EXPERTS, pltpu.roll(c_mid.astype(f32), N_EXPERTS, 1),
                                            pltpu.roll(c_lo.astype(f32), 2 * N_EXPERTS, 1))).astype(bf16)
        routed = jnp.where(sel1 | sel2, 1.0, 0.0).astype(bf16)
        incl = lax.dot_general(routed, tri_ref[...], (((0,), (0,)), ((), ())), preferred_element_type=f32)
        tok = lax.broadcasted_iota(jnp.int32, incl.shape, 1)
        excl = jnp.where(tok == 0, 0.0, pltpu.roll(incl, 1, 1))
        pos_scr[...] = jnp.where(incl > excl, excl, -1.0)

    pos_row = pos_scr[pl.ds(e, 1), :]
    nrows = (jnp.max(pos_row) + 1.0).astype(jnp.int32)

    def run_pass(base, sb):
        slot = lax.broadcasted_iota(jnp.int32, (sb, tm), 0).astype(f32) + base.astype(f32)
        lane_sb = lax.broadcasted_iota(jnp.int32, (sb, LANE), 1)
        sel = jnp.where(pos_row == slot, 1.0, 0.0).astype(bf16)
        xs = jnp.dot(sel, h_scr[...], preferred_element_type=f32).astype(bf16)
        t = _silu(jnp.dot(xs, wg_ref[...], preferred_element_type=f32)) * jnp.dot(xs, wu_ref[...], preferred_element_type=f32)
        y = jnp.dot(t.astype(bf16), wd_ref[...], preferred_element_type=f32)
        cw3 = jnp.dot(sel, comb_scr[...], preferred_element_type=f32)
        cw = jnp.sum(jnp.where((lane_sb & (N_EXPERTS - 1)) == e, cw3, 0.0), axis=-1, keepdims=True)
        acc_scr[...] += lax.dot_general(sel, (y * cw).astype(bf16), (((0,), (0,)), ((), ())), preferred_element_type=f32)

    big = 2 * MOE_SB
    one_wide = (nrows > big) & (nrows <= MOE_WIDE)

    @pl.when(one_wide)
    def _():
        run_pass(jnp.int32(0), MOE_WIDE)

    @pl.when(jnp.logical_not(one_wide))
    def _():
        nbig = (nrows + MOE_SB - 1) // big

        def big_pass(i, carry):
            run_pass(i * big, big)
            return carry

        lax.fori_loop(0, nbig, big_pass, 0)

        @pl.when(nrows > nbig * big)
        def _():
            run_pass(nbig * big, MOE_SB)

    @pl.when(e == N_EXPERTS - 1)
    def _():
        m = mod_ref[...]
        o_ref[...] = _ln(ALPHA * x_ref[...] + m[:, 5 * D:6 * D] * acc_scr[...], lg_ref[...], lb_ref[...])


def _moe_call(x, mod, p, seq_len, cst, tm=MOE_TM):
    n = x.shape[0]
    tm = min(tm, n)
    fw = D_FF_EXPERT
    tiles_per_mod = seq_len // tm if mod.shape[0] > 1 else n // tm
    full = lambda t: pl.BlockSpec(t.shape, lambda i, g: (0,) * t.ndim)
    tri = cst['tri_tok'][:tm, :tm]
    return pl.pallas_call(
        _moe_kernel, grid=(n // tm, N_EXPERTS),
        in_specs=[pl.BlockSpec((tm, D), lambda i, g: (i, 0)),
                  pl.BlockSpec((None, 1, 6 * D), lambda i, g: (i // tiles_per_mod, 0, 0)),
                  pl.BlockSpec((None, None, D, fw), lambda i, g: (p['index'], g, 0, 0)),
                  pl.BlockSpec((None, None, D, fw), lambda i, g: (p['index'], g, 0, 0)),
                  pl.BlockSpec((None, None, fw, D), lambda i, g: (p['index'], g, 0, 0)),
                  full(p['ln2_g']), full(p['ln2_b']), full(p['router']), full(tri)],
        out_specs=pl.BlockSpec((tm, D), lambda i, g: (i, 0)),
        out_shape=jax.ShapeDtypeStruct((n, D), f32),
        scratch_shapes=[pltpu.VMEM((tm, D), bf16), pltpu.VMEM((tm, D), f32),
                        pltpu.VMEM((tm, LANE), bf16), pltpu.VMEM((LANE, tm), f32)],
        compiler_params=_cparams(("arbitrary", "arbitrary")), name="moe_ffn",
    )(x, mod, p['w_gate'], p['w_up'], p['w_down'], p['ln2_g'], p['ln2_b'], p['router'], tri)


def _constants():
    hb = np.arange(GW) // DH
    mbd = (hb[:, None] == hb[None, :]).astype(np.float32)
    r = np.arange(CH)
    t = np.arange(MOE_TM)
    e = np.zeros((LANE, 3 * GW), np.float32)
    for h in range(NH):
        e[L_BETA + h, h * DH:(h + 1) * DH] = 1.0
        for d in range(2):
            e[L_DEC + NH * d + h, (1 + d) * GW + h * DH:(1 + d) * GW + (h + 1) * DH] = 1.0
    return dict(mbd=jnp.asarray(mbd, bf16), mbdf=jnp.asarray(mbd, f32),
                tril=jnp.asarray(r[:, None] >= r[None, :], bf16), triu=jnp.asarray(r[:, None] <= r[None, :], bf16),
                e_gdn=jnp.asarray(e, bf16), tri_tok=jnp.asarray(t[:, None] <= t[None, :], bf16))


_IN_SIZES = (GW, GW, GW, GW, GW, GW, 2 * GLA_LR, GW, GW, GW, GW, NH, 2 * NH, GW, GW, GW)


def _pack_w_in(w_in):
    offs = np.concatenate([[0], np.cumsum(_IN_SIZES)])
    big = [i for i, s in enumerate(_IN_SIZES) if s == GW]
    small = [i for i, s in enumerate(_IN_SIZES) if s != GW]
    parts = [w_in[..., int(offs[i]):int(offs[i + 1])] for i in big + small]
    pad = NU - int(offs[-1])
    parts.append(jnp.zeros(w_in.shape[:-1] + (pad,), w_in.dtype))
    return jnp.concatenate(parts, axis=-1).astype(bf16)


def _to_bd(s, transpose):
    if transpose:
        s = jnp.swapaxes(s, -1, -2)
    b = s.shape[0]
    z = jnp.zeros_like(s)
    rows = [jnp.concatenate([s[:, :, h] if g == h else z[:, :, h] for g in range(NH)], axis=-1) for h in range(NH)]
    return jnp.concatenate(rows, axis=-2).reshape(b, 2, GW, GW)


def _from_bd(s, transpose):
    out = jnp.stack([s[:, :, h * DH:(h + 1) * DH, h * DH:(h + 1) * DH] for h in range(NH)], axis=2)
    return jnp.swapaxes(out, -1, -2) if transpose else out


def _stream_layer(x, mod, nseq, seq_len, s0_gla, s0_gdn, p, cst, colmajor, moe):
    tpm_512 = max(seq_len // 512, 1) if mod.shape[0] > 1 else x.shape[0] // 512
    u, a, ysc = _in_conv_call(x, mod, p['w_in'], p['layer'], tpm_512, seq_len, p)
    ygla, sgla = _gla_call(u, nseq, seq_len, s0_gla, p, cst)
    o_f, o_b, sgdn = _gdn_call(u, nseq, seq_len, s0_gdn, p, cst, colmajor)
    x1 = _out_call(a, ygla, o_f, o_b, u, ysc, x, mod, p, cst, seq_len, colmajor)
    x2 = _moe_call(x1, mod, p['ffn'], seq_len, cst) if moe else _ffn_call(x1, mod, p['ffn'], seq_len)
    return x2, sgla, sgdn


def kernel(x_prompt, x_sample, c, state_gla, state_gdn, c_ctx, w_ada, b_ada, w_in, w_out, conv_w, conv_b, conv_ln_g, conv_ln_b, conv_pw, gla_w_lr, gla_b_lr, gla_norm_g, gdn_conv_w, gdn_a_log, gdn_dt_bias, gdn_norm_g, sc_conv_w, ln1_g, ln1_b, ln2_g, ln2_b, ffn_w_gate, ffn_w_up, ffn_w_down, moe_router, moe_w_gate, moe_w_up, moe_w_down):
    nb, seq, _ = x_prompt.shape
    db, dseq, _ = x_sample.shape
    depth = w_in.shape[0]
    cst = _constants()

    rows = 16
    cond = jnp.concatenate([c_ctx[None, :], c, jnp.zeros((rows - 1 - db, D), f32)], axis=0)
    mod = _mod_call(cond, w_ada, b_ada)

    w_in_p = _pack_w_in(w_in)
    w_out_b = w_out.astype(bf16)
    conv_pw_b = conv_pw.astype(bf16)
    wlr = jnp.zeros((depth, 2, LANE, GW), f32)
    wlr = wlr.at[:, 0, 0:GLA_LR].set(gla_w_lr[:, 0]).at[:, 1, GLA_LR:2 * GLA_LR].set(gla_w_lr[:, 1]).astype(bf16)
    rep = lambda t: jnp.repeat(t, DH, axis=-1)[:, :, None, :]
    row = lambda t: t[:, None, :]
    ffn_g, ffn_u, ffn_d = ffn_w_gate.astype(bf16), ffn_w_up.astype(bf16), ffn_w_down.astype(bf16)
    moe_g, moe_u, moe_d = moe_w_gate.astype(bf16), moe_w_up.astype(bf16), moe_w_down.astype(bf16)
    router = jnp.pad(moe_router, ((0, 0), (0, 0), (0, LANE - N_EXPERTS)))

    zero_bd = jnp.zeros((nb, 2, GW, GW), f32)
    y_p = x_prompt.reshape(nb * seq, D)
    y_s = x_sample.reshape(db * dseq, D)
    gla_states, gdn_states = [], []
    for l in range(depth):
        j = l // 2
        moe = l % 2 == 1
        if moe:
            ffn = dict(w_gate=moe_g, w_up=moe_u, w_down=moe_d, index=j, router=router[j],
                       ln2_g=row(ln2_g)[l], ln2_b=row(ln2_b)[l])
        else:
            ffn = dict(w_gate=ffn_g, w_up=ffn_u, w_down=ffn_d, index=j, ln2_g=row(ln2_g)[l], ln2_b=row(ln2_b)[l])
        p = dict(layer=l, w_in=w_in_p, w_out=w_out_b, conv_w=conv_w[l], conv_b=row(conv_b)[l],
                 conv_ln_g=row(conv_ln_g)[l], conv_ln_b=row(conv_ln_b)[l], conv_pw=conv_pw_b[l],
                 sc_conv_w=sc_conv_w[l], gla_wlr=wlr[l], gla_blr=gla_b_lr[l][:, None, :],
                 gla_norm_g=row(gla_norm_g)[l], gdn_conv_w=gdn_conv_w[l], gdn_alog=rep(gdn_a_log)[l],
                 gdn_dtb=rep(gdn_dt_bias)[l], gdn_norm_g=row(gdn_norm_g)[l],
                 ln1_g=row(ln1_g)[l], ln1_b=row(ln1_b)[l], ffn=ffn)
        mod_ctx = mod[l, 0:1][:, None, :]
        mod_lat = mod[l, 1:1 + db][:, None, :]
        y_p, s_gla, s_gdn = _stream_layer(y_p, mod_ctx, nb, seq, zero_bd, zero_bd, p, cst, False, moe)
        gla_states.append(_from_bd(s_gla, True))
        gdn_states.append(_from_bd(s_gdn, False))
        y_s, _, _ = _stream_layer(y_s, mod_lat, db, dseq, _to_bd(state_gla[:, l], True),
                                  _to_bd(state_gdn[:, l], False), p, cst, True, moe)
    return (y_p.reshape(nb, seq, D), y_s.reshape(db, dseq, D),
            jnp.stack(gla_states, axis=1), jnp.stack(gdn_states, axis=1))
```

```python
import functools
import math

import numpy as np
import jax
import jax.numpy as jnp
from jax import lax
from jax.experimental import pallas as pl
from jax.experimental.pallas import tpu as pltpu

f32 = jnp.float32
bf16 = jnp.bfloat16

D = 1024
GW = 256
NH = 4
DH = 64
CH = 64
GRID_W = 64
CONV_K = 31
GLA_LR = 16
GLA_TAU = 16.0
DEPTH = 4
N_EXPERTS = 8
D_FF = 2816
D_FF_EXPERT = 1408
ALPHA = (2 * DEPTH) ** 0.25
EPS = 1e-5
NU = 13 * GW + 128
SMALL_BLK = 13 * GW // 128
LANE = 128
C_AVAL, C_AGATE, C_GQ, C_GK, C_GV, C_GG, C_DQ, C_DK, C_DV, C_DGATE, C_SB, C_SC, C_SX = range(13)
L_BETA = 2 * GLA_LR
L_DEC = L_BETA + NH

VMEM_LIMIT = 56 * 1024 * 1024


def _cparams(sem):
    return pltpu.CompilerParams(dimension_semantics=sem, vmem_limit_bytes=VMEM_LIMIT)


def _dot(a, b):
    return jnp.dot(a.astype(bf16), b.astype(bf16), preferred_element_type=f32)


def _dot_nt(a, b):
    return lax.dot_general(a.astype(bf16), b.astype(bf16), (((1,), (1,)), ((), ())),
                           preferred_element_type=f32)


def _dot_tn(a, b):
    return lax.dot_general(a.astype(bf16), b.astype(bf16), (((0,), (0,)), ((), ())),
                           preferred_element_type=f32)


def _split2(x):
    hi = x.astype(bf16)
    return hi, x - hi.astype(f32)


def _split3(x):
    hi = x.astype(bf16)
    r = x - hi.astype(f32)
    mid = r.astype(bf16)
    lo = (r - mid.astype(f32)).astype(bf16)
    return hi, mid, lo


def _dot01_left(m01, x):
    hi, r = _split2(x)
    d = lambda t: jnp.dot(m01, t, preferred_element_type=f32)
    return d(hi) + d(r.astype(bf16))


def _dot01_right(x, m01):
    hi, r = _split2(x)
    d = lambda t: jnp.dot(t, m01, preferred_element_type=f32)
    return d(hi) + d(r.astype(bf16))


def _bd(x, mask):
    xb = x.astype(bf16)
    return jnp.concatenate([xb, xb, xb, xb], axis=0) * mask


def _dot3(a_parts, b_parts, mask):
    ah, ar = a_parts
    bh, br = b_parts
    al = ar.astype(bf16)
    bdh = _bd(bh, mask)
    bdl = _bd(br, mask)
    d = lambda x, y: jnp.dot(x, y, preferred_element_type=f32)
    return d(ah, bdh) + d(al, bdh) + d(ah, bdl)


def _silu(x):
    return x * jax.nn.sigmoid(x)


def _ln(x, g, b):
    xc = x - jnp.mean(x, axis=-1, keepdims=True)
    var = jnp.mean(xc * xc, axis=-1, keepdims=True)
    return xc * lax.rsqrt(var + EPS) * g + b


def _head_iotas():
    rowi = lax.broadcasted_iota(jnp.int32, (CH, GW), 0)
    coli = lax.broadcasted_iota(jnp.int32, (CH, GW), 1) & (DH - 1)
    return rowi, coli


def _mod_kernel(c_ref, w_ref, b_ref, o_ref):
    o_ref[...] = jnp.dot(_silu(c_ref[...]), w_ref[...], precision=lax.Precision.HIGHEST,
                         preferred_element_type=f32) + b_ref[...]


def _mod_call(cond, w_ada, b_ada):
    nl = w_ada.shape[0]
    rows = cond.shape[0]
    return pl.pallas_call(
        _mod_kernel, grid=(nl, 6),
        in_specs=[pl.BlockSpec((rows, D), lambda l, n: (0, 0)),
                  pl.BlockSpec((None, D, D), lambda l, n: (l, 0, n)),
                  pl.BlockSpec((None, 1, D), lambda l, n: (l, 0, n))],
        out_specs=pl.BlockSpec((None, rows, D), lambda l, n: (l, 0, n)),
        out_shape=jax.ShapeDtypeStruct((nl, rows, 6 * D), f32),
        compiler_params=_cparams(("arbitrary", "arbitrary")), name="adaln_mod",
    )(cond, w_ada, b_ada.reshape(nl, 1, 6 * D))


TM_IN = 512
CT = 256
HALO = 16
SHALO = 8


def _in_conv_kernel(x_ref, mod_ref, w_ref, cw, cb, lng, lnb, pw, scw, u_ref, a_out, ysc_out,
                    a_st, s_st, sb_st, tail_a, tail_s, pad_a, pad_s, win_scr, *, subs_per_seq):
    i = pl.program_id(0)
    slot = i % 2
    pslot = 1 - slot

    @pl.when(i == 0)
    def _():
        a_st[...] = jnp.zeros(a_st.shape, f32)
        s_st[...] = jnp.zeros(s_st.shape, f32)
        sb_st[...] = jnp.zeros(sb_st.shape, f32)

    tail_a[...] = a_st[slot, TM_IN - HALO:TM_IN, :]
    tail_s[...] = s_st[slot, TM_IN - SHALO:TM_IN, :]

    m = mod_ref[...]
    h = x_ref[...] * (1.0 + m[:, D:2 * D]) + m[:, 0:D]
    u = jnp.dot(h.astype(bf16), w_ref[...], preferred_element_type=f32)
    u_ref[...] = u
    slab = lambda c: u[:, c * GW:(c + 1) * GW]
    a_st[slot] = slab(C_AVAL) * jax.nn.sigmoid(slab(C_AGATE))
    s_st[slot] = slab(C_SC) * slab(C_SX)
    sb_st[slot] = slab(C_SB)

    base = HALO - CONV_K // 2
    span = 8 * ((CONV_K + 6) // 8)
    for hf in range(TM_IN // CT):
        g = (i - 1) * (TM_IN // CT) + hf
        first = g % subs_per_seq == 0
        last = g % subs_per_seq == subs_per_seq - 1
        lo, hi = hf * CT, (hf + 1) * CT
        if hf == 0:
            prev_a, prev_s = tail_a[...], tail_s[...]
        else:
            prev_a, prev_s = a_st[pslot, lo - HALO:lo, :], s_st[pslot, lo - SHALO:lo, :]
        if hi == TM_IN:
            next_a, next_s = a_st[slot, 0:HALO, :], s_st[slot, 0:SHALO, :]
        else:
            next_a, next_s = a_st[pslot, hi:hi + HALO, :], s_st[pslot, hi:hi + SHALO, :]
        pad_a[0:HALO, :] = jnp.where(first, 0.0, prev_a)
        pad_a[HALO:HALO + CT, :] = a_st[pslot, lo:hi, :]
        pad_a[HALO + CT:2 * HALO + CT, :] = jnp.where(last, 0.0, next_a)
        acc = jnp.zeros((CT, GW), f32)
        for ph in range(8):
            win_scr[...] = pad_a[ph:ph + CT + span - 8, :]
            for al in range(0, span, 8):
                k = al + ph - base
                if 0 <= k < CONV_K:
                    acc = acc + win_scr[al:al + CT, :] * cw[k:k + 1, :]
        a = _ln(acc + cb[...], lng[...], lnb[...])
        a_out[lo:hi, :] = _dot(_silu(a), pw[...]).astype(bf16)

        pad_s[0:SHALO, :] = jnp.where(first, 0.0, prev_s)
        pad_s[SHALO:SHALO + CT, :] = s_st[pslot, lo:hi, :]
        pad_s[SHALO + CT:2 * SHALO + CT, :] = jnp.where(last, 0.0, next_s)
        acc = jnp.zeros((CT, GW), f32)
        for k in range(3):
            o = SHALO - 1 + k
            acc = acc + pad_s[o:o + CT, :] * scw[k:k + 1, :]
        ysc_out[lo:hi, :] = (sb_st[pslot, lo:hi, :] * acc).astype(bf16)


def _in_conv_call(x, mod, w, li, tiles_per_mod, seq_len, p):
    n = x.shape[0]
    tm = TM_IN
    nt = n // tm
    cur = lambda i: jnp.minimum(i, nt - 1)
    prv = lambda i: jnp.maximum(i - 1, 0)
    full = lambda a: pl.BlockSpec(a.shape, lambda i: (0,) * a.ndim)
    consts = (p['conv_w'], p['conv_b'], p['conv_ln_g'], p['conv_ln_b'], p['conv_pw'], p['sc_conv_w'])
    return pl.pallas_call(
        functools.partial(_in_conv_kernel, subs_per_seq=seq_len // CT), grid=(nt + 1,),
        in_specs=[pl.BlockSpec((tm, D), lambda i: (cur(i), 0)),
                  pl.BlockSpec((None, 1, 6 * D), lambda i: (cur(i) // tiles_per_mod, 0, 0)),
                  pl.BlockSpec((None, D, NU), lambda i: (li, 0, 0))] + [full(a) for a in consts],
        out_specs=[pl.BlockSpec((tm, NU), lambda i: (cur(i), 0)),
                   pl.BlockSpec((tm, GW), lambda i: (prv(i), 0)),
                   pl.BlockSpec((tm, GW), lambda i: (prv(i), 0))],
        out_shape=[jax.ShapeDtypeStruct((n, NU), f32), jax.ShapeDtypeStruct((n, GW), bf16),
                   jax.ShapeDtypeStruct((n, GW), bf16)],
        scratch_shapes=[pltpu.VMEM((2, tm, GW), f32), pltpu.VMEM((2, tm, GW), f32), pltpu.VMEM((2, tm, GW), f32),
                        pltpu.VMEM((HALO, GW), f32), pltpu.VMEM((SHALO, GW), f32),
                        pltpu.VMEM((CT + 2 * HALO, GW), f32), pltpu.VMEM((CT + 2 * SHALO, GW), f32),
                        pltpu.VMEM((CT + 8 * ((CONV_K + 6) // 8) - 8, GW), f32)],
        compiler_params=_cparams(("arbitrary",)), name="in_proj_conv",
    )(x, mod, w, *consts)


GLA_LOCK = 16


def _gla_kernel(q_ref, k_ref, v_ref, g_ref, sm_ref, s0_ref, wlr_ref, blr_ref, ng_ref,
                mbd_ref, mbdf_ref, tril_ref, triu_ref, y_ref, sfin_ref, st_scr, of_scr, *, tt, ntiles):
    p = pl.program_id(1)
    nch = tt // CH
    nlock = min(GLA_LOCK, nch)
    rowi, coli = _head_iotas()

    @pl.when(p == 0)
    def _():
        st_scr[...] = s0_ref[0]

    @pl.when(p == ntiles)
    def _():
        sfin_ref[0] = st_scr[...]
        st_scr[...] = s0_ref[1]

    def run(dirn):
        tri = tril_ref[...] if dirn == 0 else triu_ref[...]
        keep = (rowi >= coli) if dirn == 0 else (rowi <= coli)
        wlr = wlr_ref[dirn]
        blr = blr_ref[dirn]
        tile = p if dirn == 0 else 2 * ntiles - 1 - p

        def body(it, carry):
            cis = [it * nlock + c for c in range(nlock)]
            if dirn == 1:
                cis = [nch - 1 - ci for ci in cis]
            rows = [pl.ds(pl.multiple_of(ci * CH, CH), CH) for ci in cis]
            grows = [pl.ds(pl.multiple_of(tile * tt + ci * CH, CH), CH) for ci in cis]
            mbd = mbd_ref[...]
            zs = [jnp.dot(sm_ref[r, :].astype(bf16), wlr, preferred_element_type=f32) + blr for r in rows]
            cums = [_dot01_left(tri, jax.nn.log_sigmoid(z) * (1.0 / GLA_TAU)) for z in zs]
            tots = [cum[CH - 1:CH, :] if dirn == 0 else cum[0:1, :] for cum in cums]
            vs = [v_ref[r, :] for r in rows]
            qes = [q_ref[r, :] * (DH ** -0.5) * jnp.exp(cum) for r, cum in zip(rows, cums)]
            kes = [k_ref[r, :] * jnp.exp(-cum) for r, cum in zip(rows, cums)]
            kds = [k_ref[r, :] * jnp.exp(tot - cum) for r, cum, tot in zip(rows, cums, tots)]
            atts = [jnp.where(keep, _dot_nt(qe, _bd(ke, mbd)), 0.0) for qe, ke in zip(qes, kes)]
            ols = [_dot(att, _bd(v, mbd)) for att, v in zip(atts, vs)]
            upds = [_dot_tn(v, kd) for v, kd in zip(vs, kds)]
            st = st_scr[...]
            os_ = []
            for c in range(nlock):
                os_.append(ols[c] + _dot_nt(qes[c], st))
                st = st * jnp.exp(tots[c]) + mbdf_ref[...] * upds[c]
            st_scr[...] = st
            if dirn == 0:
                for c in range(nlock):
                    of_scr[grows[c], :] = os_[c]
            else:
                os_ = [of_scr[grows[c], :] + os_[c] for c in range(nlock)]
                mss = [_dot(o * o, mbd) * (1.0 / DH) for o in os_]
                for c in range(nlock):
                    y = os_[c] * lax.rsqrt(mss[c] + EPS) * ng_ref[...] * _silu(g_ref[rows[c], :])
                    y_ref[rows[c], :] = y.astype(bf16)
            return carry

        lax.fori_loop(0, nch // nlock, body, 0)

    @pl.when(p < ntiles)
    def _():
        run(0)

    @pl.when(p >= ntiles)
    def _():
        run(1)

    @pl.when(p == 2 * ntiles - 1)
    def _():
        sfin_ref[1] = st_scr[...]


def _gla_call(u, nseq, seq_len, s0, p, cst):
    tt = min(seq_len, 1024)
    ntiles = seq_len // tt
    tmap = lambda q: jnp.where(q < ntiles, q, 2 * ntiles - 1 - q)
    main = lambda c: pl.BlockSpec((tt, GW), lambda s, q: (s * ntiles + tmap(q), c))
    full = lambda a: pl.BlockSpec(a.shape, lambda s, q: (0,) * a.ndim)
    consts = (p['gla_wlr'], p['gla_blr'], p['gla_norm_g'], cst['mbd'], cst['mbdf'], cst['tril'], cst['triu'])
    return pl.pallas_call(
        functools.partial(_gla_kernel, tt=tt, ntiles=ntiles), grid=(nseq, 2 * ntiles),
        in_specs=[main(C_GQ), main(C_GK), main(C_GV), main(C_GG),
                  pl.BlockSpec((tt, LANE), lambda s, q: (s * ntiles + tmap(q), SMALL_BLK)),
                  pl.BlockSpec((None, 2, GW, GW), lambda s, q: (s, 0, 0, 0))] + [full(a) for a in consts],
        out_specs=[pl.BlockSpec((tt, GW), lambda s, q: (s * ntiles + jnp.where(q < ntiles, ntiles - 1, 2 * ntiles - 1 - q), 0)),
                   pl.BlockSpec((None, 2, GW, GW), lambda s, q: (s, 0, 0, 0))],
        out_shape=[jax.ShapeDtypeStruct((nseq * seq_len, GW), bf16),
                   jax.ShapeDtypeStruct((nseq, 2, GW, GW), f32)],
        scratch_shapes=[pltpu.VMEM((GW, GW), f32), pltpu.VMEM((seq_len, GW), f32)],
        compiler_params=_cparams(("arbitrary", "arbitrary")), name="gla",
    )(u, u, u, u, u, s0, *consts)


GDN_G = 8
HR = 8
NLOCK = 4
INV_LEVELS = 6
INV_HI_LEVELS = 6


def _gdn_prep_kernel(*refs, g, ngroups, colmajor):
    if ngroups > 1:
        (q_ref, k_ref, v_ref, sm_ref, qp, kp, vp, qn, kn, vnx, cw_ref, alog_ref, dtb_ref, e_ref, mbd_ref,
         tril_ref, triu_ref, u_out, wq_out, qkd_out, kd_out, egl_out) = refs
    else:
        (q_ref, k_ref, v_ref, sm_ref, cw_ref, alog_ref, dtb_ref, e_ref, mbd_ref,
         tril_ref, triu_ref, u_out, wq_out, qkd_out, kd_out, egl_out) = refs
        qp = kp = vp = qn = kn = vnx = None
    gi = pl.program_id(1)
    rowi, coli = _head_iotas()
    row_first = rowi == 0
    row_last = rowi == CH - 1
    icat = jnp.where(rowi == coli, 1.0, 0.0)

    if colmajor:
        get = lambda ref, j: ref[:, j, :]
        last_row = lambda ref, j: ref[CH - 1, pl.ds(j, 1), :]
        first_row = lambda ref, j: ref[0, pl.ds(j, 1), :]
    else:
        get = lambda ref, j: ref[j]
        last_row = lambda ref, j: ref[j, CH - 1:CH, :]
        first_row = lambda ref, j: ref[j, 0:1, :]

    def shared(jj):
        jm = jnp.maximum(jj - 1, 0)
        jp = jnp.minimum(jj + 1, g - 1)

        def conv(ref, pref, nref, c0):
            x = get(ref, jj)
            if pref is None:
                pr_out = 0.0
                nx_out = 0.0
            else:
                pr_out = jnp.where(gi > 0, pref[HR - 1, HR - 1:HR, :], 0.0)
                nx_out = jnp.where(gi < ngroups - 1, nref[0, 0:1, :], 0.0)
            pr = jnp.where(jj > 0, last_row(ref, jm), pr_out)
            nx = jnp.where(jj < g - 1, first_row(ref, jp), nx_out)
            xd = jnp.where(row_first, pr, pltpu.roll(x, 1, 0))
            xu = jnp.where(row_last, nx, pltpu.roll(x, CH - 1, 0))
            return _silu(xd * cw_ref[0:1, c0:c0 + GW] + x * cw_ref[1:2, c0:c0 + GW] + xu * cw_ref[2:3, c0:c0 + GW])

        mbd = mbd_ref[...]
        q = conv(q_ref, qp, qn, 0)
        k = conv(k_ref, kp, kn, GW)
        v = conv(v_ref, vp, vnx, 2 * GW)
        q = q * lax.rsqrt(_dot(q * q, mbd) + 1e-6) * (DH ** -0.5)
        k = k * lax.rsqrt(_dot(k * k, mbd) + 1e-6)
        ex = _dot01_right(get(sm_ref, jj), e_ref[...])
        beta = jax.nn.sigmoid(ex[:, :GW])
        kb = k * beta
        gm = _dot_nt(jnp.concatenate([kb, q], axis=0), _bd(k, mbd))
        return q, k, kb, v * beta, ex, gm

    def gates(sh, dirn):
        q, k, kb, vb, ex, gm = sh
        la = -jnp.exp(alog_ref[dirn]) * jax.nn.softplus(ex[:, (1 + dirn) * GW:(2 + dirn) * GW] + dtb_ref[dirn])
        if dirn == 0:
            tri, keep, strict, keep_t = tril_ref[...], rowi >= coli, rowi > coli, rowi <= coli
        else:
            tri, keep, strict, keep_t = triu_ref[...], rowi <= coli, rowi < coli, rowi >= coli
        gx = _dot01_left(tri, la)
        grow = jnp.sum(jnp.where(keep_t, la, 0.0), axis=0, keepdims=True)
        decay = jnp.where(keep, jnp.exp(jnp.where(keep, gx - grow, 0.0)), 0.0)
        glast = gx[CH - 1:CH, :] if dirn == 0 else gx[0:1, :]
        nm = jnp.where(strict, -(gm[:CH] * decay), 0.0)
        return gx, glast, nm, gm[CH:] * decay

    def body(it, carry):
        jjs = [it * NLOCK + c for c in range(NLOCK)]
        shs = [shared(jj) for jj in jjs]
        chains = [(c, dirn) for c in range(NLOCK) for dirn in range(2)]
        gs = [gates(shs[c], dirn) for c, dirn in chains]
        mbd = mbd_ref[...]
        nms = [gt[2] for gt in gs]
        cat = lambda x, y: jnp.concatenate([x, y], axis=0)
        nsp = [_split2(nm) for nm in nms]
        pws = [_dot3(sp, sp, mbd) for sp in nsp]
        tinvs = [icat + nm for nm in nms]
        for lvl in range(1, INV_LEVELS):
            last = lvl == INV_LEVELS - 1
            lhs = (lambda t, q: t) if last else cat
            if lvl < INV_HI_LEVELS:
                psp = [_split2(pw) for pw in pws]
                tsp = [_split2(ti) for ti in tinvs]
                rs = [_dot3((lhs(th, ph), lhs(tr, pr)), (ph, pr), mbd) for (th, tr), (ph, pr) in zip(tsp, psp)]
            else:
                rs = [_dot(lhs(ti, pw), _bd(pw, mbd)) for ti, pw in zip(tinvs, pws)]
            tinvs = [ti + r[:CH] for ti, r in zip(tinvs, rs)]
            pws = [r[CH:] for r in rs]
        egs = [jnp.exp(gt[0]) for gt in gs]
        us = [_dot(ti, _bd(shs[c][3], mbd)) for ti, (c, dirn) in zip(tinvs, chains)]
        ws = [_dot(ti, _bd(shs[c][2] * eg, mbd)) for ti, eg, (c, dirn) in zip(tinvs, egs, chains)]
        for i, (c, dirn) in enumerate(chains):
            q, k = shs[c][0], shs[c][1]
            gx, glast, _, qkd = gs[i]
            jj = jjs[c]
            u_out[dirn, jj] = us[i]
            wq_out[dirn, jj] = jnp.concatenate([ws[i], q * egs[i]], axis=0).astype(bf16)
            qkd_out[dirn, jj] = qkd.astype(bf16)
            kd_out[dirn, jj] = (k * jnp.exp(glast - gx)).astype(bf16)
            egl_out[dirn, jj] = jnp.exp(glast)
        return carry

    lax.fori_loop(0, g // NLOCK, body, 0)


def _gdn_scan_kernel(uf, wqf, qkf, kdf, egf, ub, wqb, qkb, kdb, egb, s0_ref, mbd_ref, mbdf_ref,
                     of_out, ob_out, sfin_ref, s_scr, *, sg, cg, nt):
    t = pl.program_id(1)

    @pl.when(t == 0)
    def _():
        s_scr[...] = s0_ref[...]

    def body(c, carry):
        chains = [(sq, dirn) for sq in range(sg) for dirn in range(2)]
        src = lambda dirn: (uf, wqf, qkf, kdf, egf, of_out, c) if dirn == 0 else (ub, wqb, qkb, kdb, egb, ob_out, cg - 1 - c)
        mbd = mbd_ref[...]
        ss = [s_scr[sq, dirn] for sq, dirn in chains]
        wss = [jnp.dot(src(dirn)[1][sq, src(dirn)[6]], st.astype(bf16), preferred_element_type=f32)
               for st, (sq, dirn) in zip(ss, chains)]
        vns = [src(dirn)[0][sq, src(dirn)[6]] - ws[:CH] for ws, (sq, dirn) in zip(wss, chains)]
        os_ = [ws[CH:] + jnp.dot(src(dirn)[2][sq, src(dirn)[6]], _bd(vn, mbd), preferred_element_type=f32)
               for ws, vn, (sq, dirn) in zip(wss, vns, chains)]
        upd = [_dot_tn(src(dirn)[3][sq, src(dirn)[6]], vn) for vn, (sq, dirn) in zip(vns, chains)]
        for st, o, up, (sq, dirn) in zip(ss, os_, upd, chains):
            _, _, _, _, eg, o_out, cc = src(dirn)
            o_out[sq, cc] = o
            s_scr[sq, dirn] = st * eg[sq, cc] + mbdf_ref[...] * up
        return carry

    lax.fori_loop(0, cg, body, 0)

    @pl.when(t == nt - 1)
    def _():
        sfin_ref[...] = s_scr[...]


def _gdn_call(u, nseq, seq_len, s0, p, cst, colmajor):
    n = seq_len // CH
    if colmajor:
        assert seq_len == GRID_W * CH
        g, a = GDN_G, GRID_W
        blk = lambda w: (None, a, g, w)
        imap = lambda c: (lambda s, i: (s, 0, i, c))
    else:
        g, a = n, n
        blk = lambda w: (None, g, CH, w)
        imap = lambda c: (lambda s, i: (s, i, 0, c))
    ngroups = n // g
    u4 = u.reshape(nseq, a, CH, NU)
    main = lambda c: pl.BlockSpec(blk(GW), imap(c))
    full2 = lambda x: pl.BlockSpec(x.shape, lambda s, i: (0,) * x.ndim)
    halos, halo_specs = (), []
    if ngroups > 1:
        per = g // HR
        prev = lambda c: pl.BlockSpec((None, HR, HR, GW), lambda s, i: (s, a // HR - 1, jnp.maximum(i * per - 1, 0), c))
        nxt = lambda c: pl.BlockSpec((None, HR, HR, GW), lambda s, i: (s, 0, jnp.minimum((i + 1) * per, a // HR - 1), c))
        halo_specs = [prev(C_DQ), prev(C_DK), prev(C_DV), nxt(C_DQ), nxt(C_DK), nxt(C_DV)]
        halos = (u4,) * 6
    consts = (p['gdn_conv_w'], p['gdn_alog'], p['gdn_dtb'], cst['e_gdn'], cst['mbd'], cst['tril'], cst['triu'])
    per_chunk = lambda rows, dt: jax.ShapeDtypeStruct((nseq, 2, n, rows, GW), dt)
    per_chunk_spec = lambda rows: pl.BlockSpec((None, 2, g, rows, GW), lambda s, i: (s, 0, i, 0, 0))
    uu, wq, qkd, kd, egl = pl.pallas_call(
        functools.partial(_gdn_prep_kernel, g=g, ngroups=ngroups, colmajor=colmajor), grid=(nseq, ngroups),
        in_specs=[main(C_DQ), main(C_DK), main(C_DV), pl.BlockSpec(blk(LANE), imap(SMALL_BLK))] + halo_specs
                 + [full2(x) for x in consts],
        out_specs=[per_chunk_spec(CH), per_chunk_spec(2 * CH), per_chunk_spec(CH), per_chunk_spec(CH), per_chunk_spec(1)],
        out_shape=[per_chunk(CH, f32), per_chunk(2 * CH, bf16), per_chunk(CH, bf16), per_chunk(CH, bf16), per_chunk(1, f32)],
        compiler_params=_cparams(("arbitrary", "arbitrary")), name="gdn_prep",
    )(u4, u4, u4, u4, *halos, *consts)

    sg = math.gcd(nseq, 8)
    cg = min(n, 4)
    nt = n // cg
    fwd = lambda rows: pl.BlockSpec((sg, None, cg, rows, GW), lambda s, t: (s, 0, t, 0, 0))
    bwd = lambda rows: pl.BlockSpec((sg, None, cg, rows, GW), lambda s, t: (s, 1, nt - 1 - t, 0, 0))
    st_spec = pl.BlockSpec((sg, 2, GW, GW), lambda s, t: (s, 0, 0, 0))
    o_shape = jax.ShapeDtypeStruct((nseq, n, CH, GW), f32)
    o_f, o_b, sfin = pl.pallas_call(
        functools.partial(_gdn_scan_kernel, sg=sg, cg=cg, nt=nt), grid=(nseq // sg, nt),
        in_specs=[fwd(CH), fwd(2 * CH), fwd(CH), fwd(CH), fwd(1), bwd(CH), bwd(2 * CH), bwd(CH), bwd(CH), bwd(1),
                  st_spec, full2(cst['mbd']), full2(cst['mbdf'])],
        out_specs=[pl.BlockSpec((sg, cg, CH, GW), lambda s, t: (s, t, 0, 0)),
                   pl.BlockSpec((sg, cg, CH, GW), lambda s, t: (s, nt - 1 - t, 0, 0)), st_spec],
        out_shape=[o_shape, o_shape, jax.ShapeDtypeStruct((nseq, 2, GW, GW), f32)],
        scratch_shapes=[pltpu.VMEM((sg, 2, GW, GW), f32)],
        compiler_params=_cparams(("arbitrary", "arbitrary")), name="gdn_scan",
    )(uu, wq, qkd, kd, egl, uu, wq, qkd, kd, egl, s0, cst['mbd'], cst['mbdf'])

    return o_f, o_b, sfin


OUT_SPLIT = 2


def _out_kernel(a_ref, gl_ref, of_ref, ob_ref, gate_ref, sc_ref, x_ref, mod_ref, w_ref, lg_ref, lb_ref,
                ng_ref, mbd_ref, o_ref, *, colmajor):
    tm = x_ref.shape[0]
    hs = tm // OUT_SPLIT
    rows = [slice(h * hs, (h + 1) * hs) for h in range(OUT_SPLIT)]
    if colmajor:
        per = hs // GRID_W
        os_ = [jnp.concatenate([of_ref[:, rr, :] + ob_ref[:, rr, :] for rr in range(h * per, (h + 1) * per)], axis=0)
               for h in range(OUT_SPLIT)]
    else:
        os_ = [of_ref[r, :] + ob_ref[r, :] for r in rows]
    mss = [_dot(o * o, mbd_ref[...]) * (1.0 / DH) for o in os_]
    gds = [o * lax.rsqrt(ms + EPS) * ng_ref[...] * _silu(gate_ref[r, :]) for o, ms, r in zip(os_, mss, rows)]
    mixes = [jnp.dot(jnp.concatenate([a_ref[r, :], gl_ref[r, :], gd.astype(bf16), sc_ref[r, :]], axis=1), w_ref[...],
                     preferred_element_type=f32) for gd, r in zip(gds, rows)]
    m = mod_ref[...]
    for mix, r in zip(mixes, rows):
        o_ref[r, :] = _ln(ALPHA * x_ref[r, :] + m[:, 2 * D:3 * D] * mix, lg_ref[...], lb_ref[...])


def _out_call(a, gl, o_f, o_b, u, sc, x, mod, p, cst, seq_len, colmajor, tm=512):
    n = x.shape[0]
    tm = min(tm, n)
    tiles_per_mod = seq_len // tm if mod.shape[0] > 1 else n // tm
    part = pl.BlockSpec((tm, GW), lambda i: (i, 0))
    full = lambda t: pl.BlockSpec(t.shape, lambda i: (0,) * t.ndim)
    if colmajor:
        rows = tm // GRID_W
        per_seq = seq_len // tm
        o_spec = pl.BlockSpec((None, GRID_W, rows, GW), lambda i: (i // per_seq, 0, i % per_seq, 0))
    else:
        o_f, o_b = o_f.reshape(n, GW), o_b.reshape(n, GW)
        o_spec = part
    return pl.pallas_call(
        functools.partial(_out_kernel, colmajor=colmajor), grid=(n // tm,),
        in_specs=[part, part, o_spec, o_spec, pl.BlockSpec((tm, GW), lambda i: (i, C_DGATE)), part,
                  pl.BlockSpec((tm, D), lambda i: (i, 0)),
                  pl.BlockSpec((None, 1, 6 * D), lambda i: (i // tiles_per_mod, 0, 0)),
                  pl.BlockSpec((None, D, D), lambda i: (p['layer'], 0, 0)),
                  full(p['ln1_g']), full(p['ln1_b']), full(p['gdn_norm_g']), full(cst['mbd'])],
        out_specs=pl.BlockSpec((tm, D), lambda i: (i, 0)),
        out_shape=jax.ShapeDtypeStruct((n, D), f32),
        compiler_params=_cparams(("arbitrary",)), name="out_proj_ln",
    )(a, gl, o_f, o_b, u, sc, x, mod, p['w_out'], p['ln1_g'], p['ln1_b'], p['gdn_norm_g'], cst['mbd'])


def _ffn_kernel(x_ref, mod_ref, wg_ref, wu_ref, wd_ref, lg_ref, lb_ref, o_ref):
    m = mod_ref[...]
    x = x_ref[...]
    hb = (x * (1.0 + m[:, 4 * D:5 * D]) + m[:, 3 * D:4 * D]).astype(bf16)
    t = _silu(jnp.dot(hb, wg_ref[...], preferred_element_type=f32)) * jnp.dot(hb, wu_ref[...], preferred_element_type=f32)
    f = jnp.dot(t.astype(bf16), wd_ref[...], preferred_element_type=f32)
    o_ref[...] = _ln(ALPHA * x + m[:, 5 * D:6 * D] * f, lg_ref[...], lb_ref[...])


def _ffn_call(x, mod, p, seq_len, tm=512):
    n = x.shape[0]
    tiles_per_mod = seq_len // tm if mod.shape[0] > 1 else n // tm
    once = lambda t: pl.BlockSpec(t.shape, lambda i: (0,) * t.ndim, pipeline_mode=pl.Buffered(1))
    wspec = lambda t: pl.BlockSpec((None,) + t.shape[1:], lambda i: (p['index'], 0, 0), pipeline_mode=pl.Buffered(1))
    return pl.pallas_call(
        _ffn_kernel, grid=(n // tm,),
        in_specs=[pl.BlockSpec((tm, D), lambda i: (i, 0)),
                  pl.BlockSpec((None, 1, 6 * D), lambda i: (i // tiles_per_mod, 0, 0)),
                  wspec(p['w_gate']), wspec(p['w_up']), wspec(p['w_down']), once(p['ln2_g']), once(p['ln2_b'])],
        out_specs=pl.BlockSpec((tm, D), lambda i: (i, 0)),
        out_shape=jax.ShapeDtypeStruct((n, D), f32),
        compiler_params=_cparams(("arbitrary",)), name="dense_ffn",
    )(x, mod, p['w_gate'], p['w_up'], p['w_down'], p['ln2_g'], p['ln2_b'])


MOE_TM = 1024
MOE_SB = 128
MOE_MID = 192
MOE_WIDE = 320


def _moe_kernel(x_ref, mod_ref, wg_ref, wu_ref, wd_ref, lg_ref, lb_ref, rt_ref, tri_ref, o_ref,
                h_scr, acc_scr, comb_scr, pos_scr):
    e = pl.program_id(1)
    tm = x_ref.shape[0]

    @pl.when(e == 0)
    def _():
        m = mod_ref[...]
        h = x_ref[...] * (1.0 + m[:, 4 * D:5 * D]) + m[:, 3 * D:4 * D]
        h_scr[...] = h.astype(bf16)
        acc_scr[...] = jnp.zeros(acc_scr.shape, f32)
        hh, hr = _split2(h)
        rh, rr = _split2(rt_ref[...])
        d = lambda x, y: jnp.dot(x, y, preferred_element_type=f32)
        logits = d(hh, rh) + d(hr.astype(bf16), rh) + d(hh, rr.astype(bf16))
        lane = lax.broadcasted_iota(jnp.int32, logits.shape, 1).astype(f32)
        neg = jnp.float32(-jnp.inf)
        logits = jnp.where(lane < N_EXPERTS, logits, neg)
        m1 = jnp.max(logits, axis=-1, keepdims=True)
        i1 = jnp.min(jnp.where(logits == m1, lane, float(LANE)), axis=-1, keepdims=True)
        rest = jnp.where(lane == i1, neg, logits)
        m2 = jnp.max(rest, axis=-1, keepdims=True)
        i2 = jnp.min(jnp.where(rest == m2, lane, float(LANE)), axis=-1, keepdims=True)
        e2 = jnp.exp(m2 - m1)
        p1 = 1.0 / (1.0 + e2)
        p2 = e2 / (1.0 + e2)
        sel1 = lane == i1
        sel2 = lane == i2
        comb = jnp.where(sel1, p1, 0.0) + jnp.where(sel2, p2, 0.0)
        c_hi, c_mid, c_lo = _split3(comb)
        comb_scr[...] = jnp.where(lane < N_EXPERTS, c_hi.astype(f32),
                                  jnp.where(lane < 2 * N_EXPERTS, pltpu.roll(c_mid.astype(f32), N_EXPERTS, 1),
                                            pltpu.roll(c_lo.astype(f32), 2 * N_EXPERTS, 1))).astype(bf16)
        routed = jnp.where(sel1 | sel2, 1.0, 0.0).astype(bf16)
        incl = lax.dot_general(routed, tri_ref[...], (((0,), (0,)), ((), ())), preferred_element_type=f32)
        tok = lax.broadcasted_iota(jnp.int32, incl.shape, 1)
        excl = jnp.where(tok == 0, 0.0, pltpu.roll(incl, 1, 1))
        pos_scr[...] = jnp.where(incl > excl, excl, -1.0)

    pos_row = pos_scr[pl.ds(e, 1), :]
    nrows = (jnp.max(pos_row) + 1.0).astype(jnp.int32)

    def run_pass(base, sb):
        slot = lax.broadcasted_iota(jnp.int32, (sb, tm), 0).astype(f32) + base.astype(f32)
        lane_sb = lax.broadcasted_iota(jnp.int32, (sb, LANE), 1)
        sel = jnp.where(pos_row == slot, 1.0, 0.0).astype(bf16)
        xs = jnp.dot(sel, h_scr[...], preferred_element_type=f32).astype(bf16)
        t = _silu(jnp.dot(xs, wg_ref[...], preferred_element_type=f32)) * jnp.dot(xs, wu_ref[...], preferred_element_type=f32)
        y = jnp.dot(t.astype(bf16), wd_ref[...], preferred_element_type=f32)
        cw3 = jnp.dot(sel, comb_scr[...], preferred_element_type=f32)
        cw = jnp.sum(jnp.where((lane_sb & (N_EXPERTS - 1)) == e, cw3, 0.0), axis=-1, keepdims=True)
        acc_scr[...] += lax.dot_general(sel, (y * cw).astype(bf16), (((0,), (0,)), ((), ())), preferred_element_type=f32)

    big = 2 * MOE_SB
    one_wide = (nrows > big) & (nrows <= MOE_WIDE)

    @pl.when(one_wide)
    def _():
        run_pass(jnp.int32(0), MOE_WIDE)

    @pl.when(jnp.logical_not(one_wide))
    def _():
        nbig = (nrows + big - MOE_MID - 1) // big
        rem = nrows - nbig * big

        def big_pass(i, carry):
            run_pass(i * big, big)
            return carry

        lax.fori_loop(0, nbig, big_pass, 0)

        @pl.when(rem > MOE_SB)
        def _():
            run_pass(nbig * big, MOE_MID)

        @pl.when((rem > 0) & (rem <= MOE_SB))
        def _():
            run_pass(nbig * big, MOE_SB)

    @pl.when(e == N_EXPERTS - 1)
    def _():
        m = mod_ref[...]
        o_ref[...] = _ln(ALPHA * x_ref[...] + m[:, 5 * D:6 * D] * acc_scr[...], lg_ref[...], lb_ref[...])


def _moe_call(x, mod, p, seq_len, cst, tm=MOE_TM):
    n = x.shape[0]
    tm = min(tm, n)
    fw = D_FF_EXPERT
    tiles_per_mod = seq_len // tm if mod.shape[0] > 1 else n // tm
    full = lambda t: pl.BlockSpec(t.shape, lambda i, g: (0,) * t.ndim)
    tri = cst['tri_tok'][:tm, :tm]
    return pl.pallas_call(
        _moe_kernel, grid=(n // tm, N_EXPERTS),
        in_specs=[pl.BlockSpec((tm, D), lambda i, g: (i, 0)),
                  pl.BlockSpec((None, 1, 6 * D), lambda i, g: (i // tiles_per_mod, 0, 0)),
                  pl.BlockSpec((None, None, D, fw), lambda i, g: (p['index'], g, 0, 0)),
                  pl.BlockSpec((None, None, D, fw), lambda i, g: (p['index'], g, 0, 0)),
                  pl.BlockSpec((None, None, fw, D), lambda i, g: (p['index'], g, 0, 0)),
                  full(p['ln2_g']), full(p['ln2_b']), full(p['router']), full(tri)],
        out_specs=pl.BlockSpec((tm, D), lambda i, g: (i, 0)),
        out_shape=jax.ShapeDtypeStruct((n, D), f32),
        scratch_shapes=[pltpu.VMEM((tm, D), bf16), pltpu.VMEM((tm, D), f32),
                        pltpu.VMEM((tm, LANE), bf16), pltpu.VMEM((LANE, tm), f32)],
        compiler_params=_cparams(("arbitrary", "arbitrary")), name="moe_ffn",
    )(x, mod, p['w_gate'], p['w_up'], p['w_down'], p['ln2_g'], p['ln2_b'], p['router'], tri)


def _constants():
    hb = np.arange(GW) // DH
    mbd = (hb[:, None] == hb[None, :]).astype(np.float32)
    r = np.arange(CH)
    t = np.arange(MOE_TM)
    e = np.zeros((LANE, 3 * GW), np.float32)
    for h in range(NH):
        e[L_BETA + h, h * DH:(h + 1) * DH] = 1.0
        for d in range(2):
            e[L_DEC + NH * d + h, (1 + d) * GW + h * DH:(1 + d) * GW + (h + 1) * DH] = 1.0
    return dict(mbd=jnp.asarray(mbd, bf16), mbdf=jnp.asarray(mbd, f32),
                tril=jnp.asarray(r[:, None] >= r[None, :], bf16), triu=jnp.asarray(r[:, None] <= r[None, :], bf16),
                e_gdn=jnp.asarray(e, bf16), tri_tok=jnp.asarray(t[:, None] <= t[None, :], bf16))


_IN_SIZES = (GW, GW, GW, GW, GW, GW, 2 * GLA_LR, GW, GW, GW, GW, NH, 2 * NH, GW, GW, GW)


def _pack_w_in(w_in):
    offs = np.concatenate([[0], np.cumsum(_IN_SIZES)])
    big = [i for i, s in enumerate(_IN_SIZES) if s == GW]
    small = [i for i, s in enumerate(_IN_SIZES) if s != GW]
    parts = [w_in[..., int(offs[i]):int(offs[i + 1])] for i in big + small]
    pad = NU - int(offs[-1])
    parts.append(jnp.zeros(w_in.shape[:-1] + (pad,), w_in.dtype))
    return jnp.concatenate(parts, axis=-1).astype(bf16)


def _to_bd(s, transpose):
    if transpose:
        s = jnp.swapaxes(s, -1, -2)
    b = s.shape[0]
    z = jnp.zeros_like(s)
    rows = [jnp.concatenate([s[:, :, h] if g == h else z[:, :, h] for g in range(NH)], axis=-1) for h in range(NH)]
    return jnp.concatenate(rows, axis=-2).reshape(b, 2, GW, GW)


def _from_bd(s, transpose):
    out = jnp.stack([s[:, :, h * DH:(h + 1) * DH, h * DH:(h + 1) * DH] for h in range(NH)], axis=2)
    return jnp.swapaxes(out, -1, -2) if transpose else out


def _stream_layer(x, mod, nseq, seq_len, s0_gla, s0_gdn, p, cst, colmajor, moe):
    tpm_512 = max(seq_len // 512, 1) if mod.shape[0] > 1 else x.shape[0] // 512
    u, a, ysc = _in_conv_call(x, mod, p['w_in'], p['layer'], tpm_512, seq_len, p)
    ygla, sgla = _gla_call(u, nseq, seq_len, s0_gla, p, cst)
    o_f, o_b, sgdn = _gdn_call(u, nseq, seq_len, s0_gdn, p, cst, colmajor)
    x1 = _out_call(a, ygla, o_f, o_b, u, ysc, x, mod, p, cst, seq_len, colmajor)
    x2 = _moe_call(x1, mod, p['ffn'], seq_len, cst) if moe else _ffn_call(x1, mod, p['ffn'], seq_len)
    return x2, sgla, sgdn


def kernel(x_prompt, x_sample, c, state_gla, state_gdn, c_ctx, w_ada, b_ada, w_in, w_out, conv_w, conv_b, conv_ln_g, conv_ln_b, conv_pw, gla_w_lr, gla_b_lr, gla_norm_g, gdn_conv_w, gdn_a_log, gdn_dt_bias, gdn_norm_g, sc_conv_w, ln1_g, ln1_b, ln2_g, ln2_b, ffn_w_gate, ffn_w_up, ffn_w_down, moe_router, moe_w_gate, moe_w_up, moe_w_down):
    nb, seq, _ = x_prompt.shape
    db, dseq, _ = x_sample.shape
    depth = w_in.shape[0]
    cst = _constants()

    rows = 16
    cond = jnp.concatenate([c_ctx[None, :], c, jnp.zeros((rows - 1 - db, D), f32)], axis=0)
    mod = _mod_call(cond, w_ada, b_ada)

    w_in_p = _pack_w_in(w_in)
    w_out_b = w_out.astype(bf16)
    conv_pw_b = conv_pw.astype(bf16)
    wlr = jnp.zeros((depth, 2, LANE, GW), f32)
    wlr = wlr.at[:, 0, 0:GLA_LR].set(gla_w_lr[:, 0]).at[:, 1, GLA_LR:2 * GLA_LR].set(gla_w_lr[:, 1]).astype(bf16)
    rep = lambda t: jnp.repeat(t, DH, axis=-1)[:, :, None, :]
    row = lambda t: t[:, None, :]
    ffn_g, ffn_u, ffn_d = ffn_w_gate.astype(bf16), ffn_w_up.astype(bf16), ffn_w_down.astype(bf16)
    moe_g, moe_u, moe_d = moe_w_gate.astype(bf16), moe_w_up.astype(bf16), moe_w_down.astype(bf16)
    router = jnp.pad(moe_router, ((0, 0), (0, 0), (0, LANE - N_EXPERTS)))

    zero_bd = jnp.zeros((nb, 2, GW, GW), f32)
    y_p = x_prompt.reshape(nb * seq, D)
    y_s = x_sample.reshape(db * dseq, D)
    gla_states, gdn_states = [], []
    for l in range(depth):
        j = l // 2
        moe = l % 2 == 1
        if moe:
            ffn = dict(w_gate=moe_g, w_up=moe_u, w_down=moe_d, index=j, router=router[j],
                       ln2_g=row(ln2_g)[l], ln2_b=row(ln2_b)[l])
        else:
            ffn = dict(w_gate=ffn_g, w_up=ffn_u, w_down=ffn_d, index=j, ln2_g=row(ln2_g)[l], ln2_b=row(ln2_b)[l])
        p = dict(layer=l, w_in=w_in_p, w_out=w_out_b, conv_w=conv_w[l], conv_b=row(conv_b)[l],
                 conv_ln_g=row(conv_ln_g)[l], conv_ln_b=row(conv_ln_b)[l], conv_pw=conv_pw_b[l],
                 sc_conv_w=sc_conv_w[l], gla_wlr=wlr[l], gla_blr=gla_b_lr[l][:, None, :],
                 gla_norm_g=row(gla_norm_g)[l], gdn_conv_w=gdn_conv_w[l], gdn_alog=rep(gdn_a_log)[l],
                 gdn_dtb=rep(gdn_dt_bias)[l], gdn_norm_g=row(gdn_norm_g)[l],
                 ln1_g=row(ln1_g)[l], ln1_b=row(ln1_b)[l], ffn=ffn)
        mod_ctx = mod[l, 0:1][:, None, :]
        mod_lat = mod[l, 1:1 + db][:, None, :]
        y_p, s_gla, s_gdn = _stream_layer(y_p, mod_ctx, nb, seq, zero_bd, zero_bd, p, cst, False, moe)
        gla_states.append(_from_bd(s_gla, True))
        gdn_states.append(_from_bd(s_gdn, False))
        y_s, _, _ = _stream_layer(y_s, mod_lat, db, dseq, _to_bd(state_gla[:, l], True),
                                  _to_bd(state_gdn[:, l], False), p, cst, True, moe)
    return (y_p.reshape(nb, seq, D), y_s.reshape(db, dseq, D),
            jnp.stack(gla_states, axis=1), jnp.stack(gdn_states, axis=1))
```

```python
import functools
import math

import numpy as np
import jax
import jax.numpy as jnp
from jax import lax
from jax.experimental import pallas as pl
from jax.experimental.pallas import tpu as pltpu

f32 = jnp.float32
bf16 = jnp.bfloat16

D = 1024
GW = 256
NH = 4
DH = 64
CH = 64
GRID_W = 64
CONV_K = 31
GLA_LR = 16
GLA_TAU = 16.0
DEPTH = 4
N_EXPERTS = 8
D_FF = 2816
D_FF_EXPERT = 1408
ALPHA = (2 * DEPTH) ** 0.25
EPS = 1e-5
NU = 13 * GW + 128
SMALL_BLK = 13 * GW // 128
LANE = 128
C_AVAL, C_AGATE, C_GQ, C_GK, C_GV, C_GG, C_DQ, C_DK, C_DV, C_DGATE, C_SB, C_SC, C_SX = range(13)
L_BETA = 2 * GLA_LR
L_DEC = L_BETA + NH

VMEM_LIMIT = 56 * 1024 * 1024


def _cparams(sem):
    return pltpu.CompilerParams(dimension_semantics=sem, vmem_limit_bytes=VMEM_LIMIT)


def _dot(a, b):
    return jnp.dot(a.astype(bf16), b.astype(bf16), preferred_element_type=f32)


def _dot_nt(a, b):
    return lax.dot_general(a.astype(bf16), b.astype(bf16), (((1,), (1,)), ((), ())),
                           preferred_element_type=f32)


def _dot_tn(a, b):
    return lax.dot_general(a.astype(bf16), b.astype(bf16), (((0,), (0,)), ((), ())),
                           preferred_element_type=f32)


def _split2(x):
    hi = x.astype(bf16)
    return hi, x - hi.astype(f32)


def _split3(x):
    hi = x.astype(bf16)
    r = x - hi.astype(f32)
    mid = r.astype(bf16)
    lo = (r - mid.astype(f32)).astype(bf16)
    return hi, mid, lo


def _dot01_left(m01, x):
    hi, r = _split2(x)
    d = lambda t: jnp.dot(m01, t, preferred_element_type=f32)
    return d(hi) + d(r.astype(bf16))


def _dot01_right(x, m01):
    hi, r = _split2(x)
    d = lambda t: jnp.dot(t, m01, preferred_element_type=f32)
    return d(hi) + d(r.astype(bf16))


def _bd(x, mask):
    xb = x.astype(bf16)
    return jnp.concatenate([xb, xb, xb, xb], axis=0) * mask


def _dot3(a_parts, b_parts, mask):
    ah, ar = a_parts
    bh, br = b_parts
    al = ar.astype(bf16)
    bdh = _bd(bh, mask)
    bdl = _bd(br, mask)
    d = lambda x, y: jnp.dot(x, y, preferred_element_type=f32)
    return d(ah, bdh) + d(al, bdh) + d(ah, bdl)


def _silu(x):
    return x * jax.nn.sigmoid(x)


def _ln(x, g, b):
    xc = x - jnp.mean(x, axis=-1, keepdims=True)
    var = jnp.mean(xc * xc, axis=-1, keepdims=True)
    return xc * lax.rsqrt(var + EPS) * g + b


def _head_iotas():
    rowi = lax.broadcasted_iota(jnp.int32, (CH, GW), 0)
    coli = lax.broadcasted_iota(jnp.int32, (CH, GW), 1) & (DH - 1)
    return rowi, coli


def _mod_kernel(c_ref, w_ref, b_ref, o_ref):
    o_ref[...] = jnp.dot(_silu(c_ref[...]), w_ref[...], precision=lax.Precision.HIGHEST,
                         preferred_element_type=f32) + b_ref[...]


def _mod_call(cond, w_ada, b_ada):
    nl = w_ada.shape[0]
    rows = cond.shape[0]
    return pl.pallas_call(
        _mod_kernel, grid=(nl, 6),
        in_specs=[pl.BlockSpec((rows, D), lambda l, n: (0, 0)),
                  pl.BlockSpec((None, D, D), lambda l, n: (l, 0, n)),
                  pl.BlockSpec((None, 1, D), lambda l, n: (l, 0, n))],
        out_specs=pl.BlockSpec((None, rows, D), lambda l, n: (l, 0, n)),
        out_shape=jax.ShapeDtypeStruct((nl, rows, 6 * D), f32),
        compiler_params=_cparams(("arbitrary", "arbitrary")), name="adaln_mod",
    )(cond, w_ada, b_ada.reshape(nl, 1, 6 * D))


TM_IN = 512
CT = 256
HALO = 16
SHALO = 8


def _in_conv_kernel(x_ref, mod_ref, w_ref, cw, cb, lng, lnb, pw, scw, u_ref, a_out, ysc_out,
                    a_st, s_st, sb_st, tail_a, tail_s, pad_a, pad_s, win_scr, *, subs_per_seq):
    i = pl.program_id(0)
    slot = i % 2
    pslot = 1 - slot

    @pl.when(i == 0)
    def _():
        a_st[...] = jnp.zeros(a_st.shape, f32)
        s_st[...] = jnp.zeros(s_st.shape, f32)
        sb_st[...] = jnp.zeros(sb_st.shape, f32)

    tail_a[...] = a_st[slot, TM_IN - HALO:TM_IN, :]
    tail_s[...] = s_st[slot, TM_IN - SHALO:TM_IN, :]

    m = mod_ref[...]
    h = x_ref[...] * (1.0 + m[:, D:2 * D]) + m[:, 0:D]
    u = jnp.dot(h.astype(bf16), w_ref[...], preferred_element_type=f32)
    u_ref[...] = u
    slab = lambda c: u[:, c * GW:(c + 1) * GW]
    a_st[slot] = slab(C_AVAL) * jax.nn.sigmoid(slab(C_AGATE))
    s_st[slot] = slab(C_SC) * slab(C_SX)
    sb_st[slot] = slab(C_SB)

    base = HALO - CONV_K // 2
    span = 8 * ((CONV_K + 6) // 8)
    for hf in range(TM_IN // CT):
        g = (i - 1) * (TM_IN // CT) + hf
        first = g % subs_per_seq == 0
        last = g % subs_per_seq == subs_per_seq - 1
        lo, hi = hf * CT, (hf + 1) * CT
        if hf == 0:
            prev_a, prev_s = tail_a[...], tail_s[...]
        else:
            prev_a, prev_s = a_st[pslot, lo - HALO:lo, :], s_st[pslot, lo - SHALO:lo, :]
        if hi == TM_IN:
            next_a, next_s = a_st[slot, 0:HALO, :], s_st[slot, 0:SHALO, :]
        else:
            next_a, next_s = a_st[pslot, hi:hi + HALO, :], s_st[pslot, hi:hi + SHALO, :]
        pad_a[0:HALO, :] = jnp.where(first, 0.0, prev_a)
        pad_a[HALO:HALO + CT, :] = a_st[pslot, lo:hi, :]
        pad_a[HALO + CT:2 * HALO + CT, :] = jnp.where(last, 0.0, next_a)
        acc = jnp.zeros((CT, GW), f32)
        for ph in range(8):
            win_scr[...] = pad_a[ph:ph + CT + span - 8, :]
            for al in range(0, span, 8):
                k = al + ph - base
                if 0 <= k < CONV_K:
                    acc = acc + win_scr[al:al + CT, :] * cw[k:k + 1, :]
        a = _ln(acc + cb[...], lng[...], lnb[...])
        a_out[lo:hi, :] = _dot(_silu(a), pw[...]).astype(bf16)

        pad_s[0:SHALO, :] = jnp.where(first, 0.0, prev_s)
        pad_s[SHALO:SHALO + CT, :] = s_st[pslot, lo:hi, :]
        pad_s[SHALO + CT:2 * SHALO + CT, :] = jnp.where(last, 0.0, next_s)
        acc = jnp.zeros((CT, GW), f32)
        for k in range(3):
            o = SHALO - 1 + k
            acc = acc + pad_s[o:o + CT, :] * scw[k:k + 1, :]
        ysc_out[lo:hi, :] = (sb_st[pslot, lo:hi, :] * acc).astype(bf16)


def _in_conv_call(x, mod, w, li, tiles_per_mod, seq_len, p):
    n = x.shape[0]
    tm = TM_IN
    nt = n // tm
    cur = lambda i: jnp.minimum(i, nt - 1)
    prv = lambda i: jnp.maximum(i - 1, 0)
    full = lambda a: pl.BlockSpec(a.shape, lambda i: (0,) * a.ndim)
    consts = (p['conv_w'], p['conv_b'], p['conv_ln_g'], p['conv_ln_b'], p['conv_pw'], p['sc_conv_w'])
    return pl.pallas_call(
        functools.partial(_in_conv_kernel, subs_per_seq=seq_len // CT), grid=(nt + 1,),
        in_specs=[pl.BlockSpec((tm, D), lambda i: (cur(i), 0)),
                  pl.BlockSpec((None, 1, 6 * D), lambda i: (cur(i) // tiles_per_mod, 0, 0)),
                  pl.BlockSpec((None, D, NU), lambda i: (li, 0, 0))] + [full(a) for a in consts],
        out_specs=[pl.BlockSpec((tm, NU), lambda i: (cur(i), 0)),
                   pl.BlockSpec((tm, GW), lambda i: (prv(i), 0)),
                   pl.BlockSpec((tm, GW), lambda i: (prv(i), 0))],
        out_shape=[jax.ShapeDtypeStruct((n, NU), f32), jax.ShapeDtypeStruct((n, GW), bf16),
                   jax.ShapeDtypeStruct((n, GW), bf16)],
        scratch_shapes=[pltpu.VMEM((2, tm, GW), f32), pltpu.VMEM((2, tm, GW), f32), pltpu.VMEM((2, tm, GW), f32),
                        pltpu.VMEM((HALO, GW), f32), pltpu.VMEM((SHALO, GW), f32),
                        pltpu.VMEM((CT + 2 * HALO, GW), f32), pltpu.VMEM((CT + 2 * SHALO, GW), f32),
                        pltpu.VMEM((CT + 8 * ((CONV_K + 6) // 8) - 8, GW), f32)],
        compiler_params=_cparams(("arbitrary",)), name="in_proj_conv",
    )(x, mod, w, *consts)


GLA_LOCK = 16


def _gla_kernel(q_ref, k_ref, v_ref, g_ref, sm_ref, s0_ref, wlr_ref, blr_ref, ng_ref,
                mbd_ref, mbdf_ref, tril_ref, triu_ref, y_ref, sfin_ref, st_scr, of_scr, *, tt, ntiles):
    p = pl.program_id(1)
    nch = tt // CH
    nlock = min(GLA_LOCK, nch)
    rowi, coli = _head_iotas()

    @pl.when(p == 0)
    def _():
        st_scr[...] = s0_ref[0]

    @pl.when(p == ntiles)
    def _():
        sfin_ref[0] = st_scr[...]
        st_scr[...] = s0_ref[1]

    def run(dirn):
        tri = tril_ref[...] if dirn == 0 else triu_ref[...]
        keep = (rowi >= coli) if dirn == 0 else (rowi <= coli)
        wlr = wlr_ref[dirn]
        blr = blr_ref[dirn]
        tile = p if dirn == 0 else 2 * ntiles - 1 - p

        def body(it, carry):
            cis = [it * nlock + c for c in range(nlock)]
            if dirn == 1:
                cis = [nch - 1 - ci for ci in cis]
            rows = [pl.ds(pl.multiple_of(ci * CH, CH), CH) for ci in cis]
            grows = [pl.ds(pl.multiple_of(tile * tt + ci * CH, CH), CH) for ci in cis]
            mbd = mbd_ref[...]
            zs = [jnp.dot(sm_ref[r, :].astype(bf16), wlr, preferred_element_type=f32) + blr for r in rows]
            cums = [_dot01_left(tri, jax.nn.log_sigmoid(z) * (1.0 / GLA_TAU)) for z in zs]
            tots = [cum[CH - 1:CH, :] if dirn == 0 else cum[0:1, :] for cum in cums]
            vs = [v_ref[r, :] for r in rows]
            qes = [q_ref[r, :] * (DH ** -0.5) * jnp.exp(cum) for r, cum in zip(rows, cums)]
            kes = [k_ref[r, :] * jnp.exp(-cum) for r, cum in zip(rows, cums)]
            kds = [k_ref[r, :] * jnp.exp(tot - cum) for r, cum, tot in zip(rows, cums, tots)]
            atts = [jnp.where(keep, _dot_nt(qe, _bd(ke, mbd)), 0.0) for qe, ke in zip(qes, kes)]
            ols = [_dot(att, _bd(v, mbd)) for att, v in zip(atts, vs)]
            upds = [_dot_tn(v, kd) for v, kd in zip(vs, kds)]
            st = st_scr[...]
            os_ = []
            for c in range(nlock):
                os_.append(ols[c] + _dot_nt(qes[c], st))
                st = st * jnp.exp(tots[c]) + mbdf_ref[...] * upds[c]
            st_scr[...] = st
            if dirn == 0:
                for c in range(nlock):
                    of_scr[grows[c], :] = os_[c]
            else:
                os_ = [of_scr[grows[c], :] + os_[c] for c in range(nlock)]
                mss = [_dot(o * o, mbd) * (1.0 / DH) for o in os_]
                for c in range(nlock):
                    y = os_[c] * lax.rsqrt(mss[c] + EPS) * ng_ref[...] * _silu(g_ref[rows[c], :])
                    y_ref[rows[c], :] = y.astype(bf16)
            return carry

        lax.fori_loop(0, nch // nlock, body, 0)

    @pl.when(p < ntiles)
    def _():
        run(0)

    @pl.when(p >= ntiles)
    def _():
        run(1)

    @pl.when(p == 2 * ntiles - 1)
    def _():
        sfin_ref[1] = st_scr[...]


def _gla_call(u, nseq, seq_len, s0, p, cst):
    tt = min(seq_len, 1024)
    ntiles = seq_len // tt
    tmap = lambda q: jnp.where(q < ntiles, q, 2 * ntiles - 1 - q)
    main = lambda c: pl.BlockSpec((tt, GW), lambda s, q: (s * ntiles + tmap(q), c))
    full = lambda a: pl.BlockSpec(a.shape, lambda s, q: (0,) * a.ndim)
    consts = (p['gla_wlr'], p['gla_blr'], p['gla_norm_g'], cst['mbd'], cst['mbdf'], cst['tril'], cst['triu'])
    return pl.pallas_call(
        functools.partial(_gla_kernel, tt=tt, ntiles=ntiles), grid=(nseq, 2 * ntiles),
        in_specs=[main(C_GQ), main(C_GK), main(C_GV), main(C_GG),
                  pl.BlockSpec((tt, LANE), lambda s, q: (s * ntiles + tmap(q), SMALL_BLK)),
                  pl.BlockSpec((None, 2, GW, GW), lambda s, q: (s, 0, 0, 0))] + [full(a) for a in consts],
        out_specs=[pl.BlockSpec((tt, GW), lambda s, q: (s * ntiles + jnp.where(q < ntiles, ntiles - 1, 2 * ntiles - 1 - q), 0)),
                   pl.BlockSpec((None, 2, GW, GW), lambda s, q: (s, 0, 0, 0))],
        out_shape=[jax.ShapeDtypeStruct((nseq * seq_len, GW), bf16),
                   jax.ShapeDtypeStruct((nseq, 2, GW, GW), f32)],
        scratch_shapes=[pltpu.VMEM((GW, GW), f32), pltpu.VMEM((seq_len, GW), f32)],
        compiler_params=_cparams(("arbitrary", "arbitrary")), name="gla",
    )(u, u, u, u, u, s0, *consts)


GDN_G = 8
HR = 8
NLOCK = 4
INV_LEVELS = 6
INV_HI_LEVELS = 6


def _gdn_prep_kernel(*refs, g, ngroups, colmajor):
    if ngroups > 1:
        (q_ref, k_ref, v_ref, sm_ref, qp, kp, vp, qn, kn, vnx, cw_ref, alog_ref, dtb_ref, e_ref, mbd_ref,
         tril_ref, triu_ref, u_out, wq_out, qkd_out, kd_out, egl_out) = refs
    else:
        (q_ref, k_ref, v_ref, sm_ref, cw_ref, alog_ref, dtb_ref, e_ref, mbd_ref,
         tril_ref, triu_ref, u_out, wq_out, qkd_out, kd_out, egl_out) = refs
        qp = kp = vp = qn = kn = vnx = None
    gi = pl.program_id(1)
    rowi, coli = _head_iotas()
    row_first = rowi == 0
    row_last = rowi == CH - 1
    icat = jnp.where(rowi == coli, 1.0, 0.0)

    if colmajor:
        get = lambda ref, j: ref[:, j, :]
        last_row = lambda ref, j: ref[CH - 1, pl.ds(j, 1), :]
        first_row = lambda ref, j: ref[0, pl.ds(j, 1), :]
    else:
        get = lambda ref, j: ref[j]
        last_row = lambda ref, j: ref[j, CH - 1:CH, :]
        first_row = lambda ref, j: ref[j, 0:1, :]

    def shared(jj):
        jm = jnp.maximum(jj - 1, 0)
        jp = jnp.minimum(jj + 1, g - 1)

        def conv(ref, pref, nref, c0):
            x = get(ref, jj)
            if pref is None:
                pr_out = 0.0
                nx_out = 0.0
            else:
                pr_out = jnp.where(gi > 0, pref[HR - 1, HR - 1:HR, :], 0.0)
                nx_out = jnp.where(gi < ngroups - 1, nref[0, 0:1, :], 0.0)
            pr = jnp.where(jj > 0, last_row(ref, jm), pr_out)
            nx = jnp.where(jj < g - 1, first_row(ref, jp), nx_out)
            xd = jnp.where(row_first, pr, pltpu.roll(x, 1, 0))
            xu = jnp.where(row_last, nx, pltpu.roll(x, CH - 1, 0))
            return _silu(xd * cw_ref[0:1, c0:c0 + GW] + x * cw_ref[1:2, c0:c0 + GW] + xu * cw_ref[2:3, c0:c0 + GW])

        mbd = mbd_ref[...]
        q = conv(q_ref, qp, qn, 0)
        k = conv(k_ref, kp, kn, GW)
        v = conv(v_ref, vp, vnx, 2 * GW)
        q = q * lax.rsqrt(_dot(q * q, mbd) + 1e-6) * (DH ** -0.5)
        k = k * lax.rsqrt(_dot(k * k, mbd) + 1e-6)
        ex = _dot01_right(get(sm_ref, jj), e_ref[...])
        beta = jax.nn.sigmoid(ex[:, :GW])
        kb = k * beta
        gm = _dot_nt(jnp.concatenate([kb, q], axis=0), _bd(k, mbd))
        return q, k, kb, v * beta, ex, gm

    def gates(sh, dirn):
        q, k, kb, vb, ex, gm = sh
        la = -jnp.exp(alog_ref[dirn]) * jax.nn.softplus(ex[:, (1 + dirn) * GW:(2 + dirn) * GW] + dtb_ref[dirn])
        if dirn == 0:
            tri, keep, strict, keep_t = tril_ref[...], rowi >= coli, rowi > coli, rowi <= coli
        else:
            tri, keep, strict, keep_t = triu_ref[...], rowi <= coli, rowi < coli, rowi >= coli
        gx = _dot01_left(tri, la)
        grow = jnp.sum(jnp.where(keep_t, la, 0.0), axis=0, keepdims=True)
        decay = jnp.where(keep, jnp.exp(jnp.where(keep, gx - grow, 0.0)), 0.0)
        glast = gx[CH - 1:CH, :] if dirn == 0 else gx[0:1, :]
        nm = jnp.where(strict, -(gm[:CH] * decay), 0.0)
        return gx, glast, nm, gm[CH:] * decay

    def body(it, carry):
        jjs = [it * NLOCK + c for c in range(NLOCK)]
        shs = [shared(jj) for jj in jjs]
        chains = [(c, dirn) for c in range(NLOCK) for dirn in range(2)]
        gs = [gates(shs[c], dirn) for c, dirn in chains]
        mbd = mbd_ref[...]
        nms = [gt[2] for gt in gs]
        cat = lambda x, y: jnp.concatenate([x, y], axis=0)
        nsp = [_split2(nm) for nm in nms]
        pws = [_dot3(sp, sp, mbd) for sp in nsp]
        tinvs = [icat + nm for nm in nms]
        for lvl in range(1, INV_LEVELS):
            last = lvl == INV_LEVELS - 1
            lhs = (lambda t, q: t) if last else cat
            if lvl < INV_HI_LEVELS:
                psp = [_split2(pw) for pw in pws]
                tsp = [_split2(ti) for ti in tinvs]
                rs = [_dot3((lhs(th, ph), lhs(tr, pr)), (ph, pr), mbd) for (th, tr), (ph, pr) in zip(tsp, psp)]
            else:
                rs = [_dot(lhs(ti, pw), _bd(pw, mbd)) for ti, pw in zip(tinvs, pws)]
            tinvs = [ti + r[:CH] for ti, r in zip(tinvs, rs)]
            pws = [r[CH:] for r in rs]
        egs = [jnp.exp(gt[0]) for gt in gs]
        us = [_dot(ti, _bd(shs[c][3], mbd)) for ti, (c, dirn) in zip(tinvs, chains)]
        ws = [_dot(ti, _bd(shs[c][2] * eg, mbd)) for ti, eg, (c, dirn) in zip(tinvs, egs, chains)]
        for i, (c, dirn) in enumerate(chains):
            q, k = shs[c][0], shs[c][1]
            gx, glast, _, qkd = gs[i]
            jj = jjs[c]
            u_out[dirn, jj] = us[i]
            wq_out[dirn, jj] = jnp.concatenate([ws[i], q * egs[i]], axis=0).astype(bf16)
            qkd_out[dirn, jj] = qkd.astype(bf16)
            kd_out[dirn, jj] = (k * jnp.exp(glast - gx)).astype(bf16)
            egl_out[dirn, jj] = jnp.exp(glast)
        return carry

    lax.fori_loop(0, g // NLOCK, body, 0)


def _gdn_scan_kernel(uf, wqf, qkf, kdf, egf, ub, wqb, qkb, kdb, egb, s0_ref, mbd_ref, mbdf_ref,
                     of_out, ob_out, sfin_ref, s_scr, *, sg, cg, nt):
    t = pl.program_id(1)

    @pl.when(t == 0)
    def _():
        s_scr[...] = s0_ref[...]

    def body(c, carry):
        chains = [(sq, dirn) for sq in range(sg) for dirn in range(2)]
        src = lambda dirn: (uf, wqf, qkf, kdf, egf, of_out, c) if dirn == 0 else (ub, wqb, qkb, kdb, egb, ob_out, cg - 1 - c)
        mbd = mbd_ref[...]
        ss = [s_scr[sq, dirn] for sq, dirn in chains]
        wss = [jnp.dot(src(dirn)[1][sq, src(dirn)[6]], st.astype(bf16), preferred_element_type=f32)
               for st, (sq, dirn) in zip(ss, chains)]
        vns = [src(dirn)[0][sq, src(dirn)[6]] - ws[:CH] for ws, (sq, dirn) in zip(wss, chains)]
        os_ = [ws[CH:] + jnp.dot(src(dirn)[2][sq, src(dirn)[6]], _bd(vn, mbd), preferred_element_type=f32)
               for ws, vn, (sq, dirn) in zip(wss, vns, chains)]
        upd = [_dot_tn(src(dirn)[3][sq, src(dirn)[6]], vn) for vn, (sq, dirn) in zip(vns, chains)]
        for st, o, up, (sq, dirn) in zip(ss, os_, upd, chains):
            _, _, _, _, eg, o_out, cc = src(dirn)
            o_out[sq, cc] = o
            s_scr[sq, dirn] = st * eg[sq, cc] + mbdf_ref[...] * up
        return carry

    lax.fori_loop(0, cg, body, 0)

    @pl.when(t == nt - 1)
    def _():
        sfin_ref[...] = s_scr[...]


def _gdn_call(u, nseq, seq_len, s0, p, cst, colmajor):
    n = seq_len // CH
    if colmajor:
        assert seq_len == GRID_W * CH
        g, a = GDN_G, GRID_W
        blk = lambda w: (None, a, g, w)
        imap = lambda c: (lambda s, i: (s, 0, i, c))
    else:
        g, a = n, n
        blk = lambda w: (None, g, CH, w)
        imap = lambda c: (lambda s, i: (s, i, 0, c))
    ngroups = n // g
    u4 = u.reshape(nseq, a, CH, NU)
    main = lambda c: pl.BlockSpec(blk(GW), imap(c))
    full2 = lambda x: pl.BlockSpec(x.shape, lambda s, i: (0,) * x.ndim)
    halos, halo_specs = (), []
    if ngroups > 1:
        per = g // HR
        prev = lambda c: pl.BlockSpec((None, HR, HR, GW), lambda s, i: (s, a // HR - 1, jnp.maximum(i * per - 1, 0), c))
        nxt = lambda c: pl.BlockSpec((None, HR, HR, GW), lambda s, i: (s, 0, jnp.minimum((i + 1) * per, a // HR - 1), c))
        halo_specs = [prev(C_DQ), prev(C_DK), prev(C_DV), nxt(C_DQ), nxt(C_DK), nxt(C_DV)]
        halos = (u4,) * 6
    consts = (p['gdn_conv_w'], p['gdn_alog'], p['gdn_dtb'], cst['e_gdn'], cst['mbd'], cst['tril'], cst['triu'])
    per_chunk = lambda rows, dt: jax.ShapeDtypeStruct((nseq, 2, n, rows, GW), dt)
    per_chunk_spec = lambda rows: pl.BlockSpec((None, 2, g, rows, GW), lambda s, i: (s, 0, i, 0, 0))
    uu, wq, qkd, kd, egl = pl.pallas_call(
        functools.partial(_gdn_prep_kernel, g=g, ngroups=ngroups, colmajor=colmajor), grid=(nseq, ngroups),
        in_specs=[main(C_DQ), main(C_DK), main(C_DV), pl.BlockSpec(blk(LANE), imap(SMALL_BLK))] + halo_specs
                 + [full2(x) for x in consts],
        out_specs=[per_chunk_spec(CH), per_chunk_spec(2 * CH), per_chunk_spec(CH), per_chunk_spec(CH), per_chunk_spec(1)],
        out_shape=[per_chunk(CH, f32), per_chunk(2 * CH, bf16), per_chunk(CH, bf16), per_chunk(CH, bf16), per_chunk(1, f32)],
        compiler_params=_cparams(("arbitrary", "arbitrary")), name="gdn_prep",
    )(u4, u4, u4, u4, *halos, *consts)

    sg = math.gcd(nseq, 8)
    cg = min(n, 4)
    nt = n // cg
    fwd = lambda rows: pl.BlockSpec((sg, None, cg, rows, GW), lambda s, t: (s, 0, t, 0, 0))
    bwd = lambda rows: pl.BlockSpec((sg, None, cg, rows, GW), lambda s, t: (s, 1, nt - 1 - t, 0, 0))
    st_spec = pl.BlockSpec((sg, 2, GW, GW), lambda s, t: (s, 0, 0, 0))
    o_shape = jax.ShapeDtypeStruct((nseq, n, CH, GW), f32)
    o_f, o_b, sfin = pl.pallas_call(
        functools.partial(_gdn_scan_kernel, sg=sg, cg=cg, nt=nt), grid=(nseq // sg, nt),
        in_specs=[fwd(CH), fwd(2 * CH), fwd(CH), fwd(CH), fwd(1), bwd(CH), bwd(2 * CH), bwd(CH), bwd(CH), bwd(1),
                  st_spec, full2(cst['mbd']), full2(cst['mbdf'])],
        out_specs=[pl.BlockSpec((sg, cg, CH, GW), lambda s, t: (s, t, 0, 0)),
                   pl.BlockSpec((sg, cg, CH, GW), lambda s, t: (s, nt - 1 - t, 0, 0)), st_spec],
        out_shape=[o_shape, o_shape, jax.ShapeDtypeStruct((nseq, 2, GW, GW), f32)],
        scratch_shapes=[pltpu.VMEM((sg, 2, GW, GW), f32)],
        compiler_params=_cparams(("arbitrary", "arbitrary")), name="gdn_scan",
    )(uu, wq, qkd, kd, egl, uu, wq, qkd, kd, egl, s0, cst['mbd'], cst['mbdf'])

    return o_f, o_b, sfin


OUT_SPLIT = 2


def _out_kernel(a_ref, gl_ref, of_ref, ob_ref, gate_ref, sc_ref, x_ref, mod_ref, w_ref, lg_ref, lb_ref,
                ng_ref, mbd_ref, o_ref, *, colmajor):
    tm = x_ref.shape[0]
    hs = tm // OUT_SPLIT
    rows = [slice(h * hs, (h + 1) * hs) for h in range(OUT_SPLIT)]
    if colmajor:
        per = hs // GRID_W
        os_ = [jnp.concatenate([of_ref[:, rr, :] + ob_ref[:, rr, :] for rr in range(h * per, (h + 1) * per)], axis=0)
               for h in range(OUT_SPLIT)]
    else:
        os_ = [of_ref[r, :] + ob_ref[r, :] for r in rows]
    mss = [_dot(o * o, mbd_ref[...]) * (1.0 / DH) for o in os_]
    gds = [o * lax.rsqrt(ms + EPS) * ng_ref[...] * _silu(gate_ref[r, :]) for o, ms, r in zip(os_, mss, rows)]
    mixes = [jnp.dot(jnp.concatenate([a_ref[r, :], gl_ref[r, :], gd.astype(bf16), sc_ref[r, :]], axis=1), w_ref[...],
                     preferred_element_type=f32) for gd, r in zip(gds, rows)]
    m = mod_ref[...]
    for mix, r in zip(mixes, rows):
        o_ref[r, :] = _ln(ALPHA * x_ref[r, :] + m[:, 2 * D:3 * D] * mix, lg_ref[...], lb_ref[...])


def _out_call(a, gl, o_f, o_b, u, sc, x, mod, p, cst, seq_len, colmajor, tm=512):
    n = x.shape[0]
    tm = min(tm, n)
    tiles_per_mod = seq_len // tm if mod.shape[0] > 1 else n // tm
    part = pl.BlockSpec((tm, GW), lambda i: (i, 0))
    full = lambda t: pl.BlockSpec(t.shape, lambda i: (0,) * t.ndim)
    if colmajor:
        rows = tm // GRID_W
        per_seq = seq_len // tm
        o_spec = pl.BlockSpec((None, GRID_W, rows, GW), lambda i: (i // per_seq, 0, i % per_seq, 0))
    else:
        o_f, o_b = o_f.reshape(n, GW), o_b.reshape(n, GW)
        o_spec = part
    return pl.pallas_call(
        functools.partial(_out_kernel, colmajor=colmajor), grid=(n // tm,),
        in_specs=[part, part, o_spec, o_spec, pl.BlockSpec((tm, GW), lambda i: (i, C_DGATE)), part,
                  pl.BlockSpec((tm, D), lambda i: (i, 0)),
                  pl.BlockSpec((None, 1, 6 * D), lambda i: (i // tiles_per_mod, 0, 0)),
                  pl.BlockSpec((None, D, D), lambda i: (p['layer'], 0, 0)),
                  full(p['ln1_g']), full(p['ln1_b']), full(p['gdn_norm_g']), full(cst['mbd'])],
        out_specs=pl.BlockSpec((tm, D), lambda i: (i, 0)),
        out_shape=jax.ShapeDtypeStruct((n, D), f32),
        compiler_params=_cparams(("arbitrary",)), name="out_proj_ln",
    )(a, gl, o_f, o_b, u, sc, x, mod, p['w_out'], p['ln1_g'], p['ln1_b'], p['gdn_norm_g'], cst['mbd'])


def _ffn_kernel(x_ref, mod_ref, wg_ref, wu_ref, wd_ref, lg_ref, lb_ref, o_ref):
    m = mod_ref[...]
    x = x_ref[...]
    hb = (x * (1.0 + m[:, 4 * D:5 * D]) + m[:, 3 * D:4 * D]).astype(bf16)
    t = _silu(jnp.dot(hb, wg_ref[...], preferred_element_type=f32)) * jnp.dot(hb, wu_ref[...], preferred_element_type=f32)
    f = jnp.dot(t.astype(bf16), wd_ref[...], preferred_element_type=f32)
    o_ref[...] = _ln(ALPHA * x + m[:, 5 * D:6 * D] * f, lg_ref[...], lb_ref[...])


def _ffn_call(x, mod, p, seq_len, tm=512):
    n = x.shape[0]
    tiles_per_mod = seq_len // tm if mod.shape[0] > 1 else n // tm
    once = lambda t: pl.BlockSpec(t.shape, lambda i: (0,) * t.ndim, pipeline_mode=pl.Buffered(1))
    wspec = lambda t: pl.BlockSpec((None,) + t.shape[1:], lambda i: (p['index'], 0, 0), pipeline_mode=pl.Buffered(1))
    return pl.pallas_call(
        _ffn_kernel, grid=(n // tm,),
        in_specs=[pl.BlockSpec((tm, D), lambda i: (i, 0)),
                  pl.BlockSpec((None, 1, 6 * D), lambda i: (i // tiles_per_mod, 0, 0)),
                  wspec(p['w_gate']), wspec(p['w_up']), wspec(p['w_down']), once(p['ln2_g']), once(p['ln2_b'])],
        out_specs=pl.BlockSpec((tm, D), lambda i: (i, 0)),
        out_shape=jax.ShapeDtypeStruct((n, D), f32),
        compiler_params=_cparams(("arbitrary",)), name="dense_ffn",
    )(x, mod, p['w_gate'], p['w_up'], p['w_down'], p['ln2_g'], p['ln2_b'])


def _out_ffn_kernel(a_ref, gl_ref, of_ref, ob_ref, gate_ref, sc_ref, x_ref, mod_ref, w_ref, lg_ref, lb_ref,
                    ng_ref, mbd_ref, wg_ref, wu_ref, wd_ref, l2g_ref, l2b_ref, o_ref, x1_scr, *, colmajor):
    _out_kernel(a_ref, gl_ref, of_ref, ob_ref, gate_ref, sc_ref, x_ref, mod_ref, w_ref, lg_ref, lb_ref,
                ng_ref, mbd_ref, x1_scr, colmajor=colmajor)
    _ffn_kernel(x1_scr, mod_ref, wg_ref, wu_ref, wd_ref, l2g_ref, l2b_ref, o_ref)


def _out_ffn_call(a, gl, o_f, o_b, u, sc, x, mod, p, cst, seq_len, colmajor, tm=512):
    n = x.shape[0]
    tm = min(tm, n)
    f = p['ffn']
    tiles_per_mod = seq_len // tm if mod.shape[0] > 1 else n // tm
    part = pl.BlockSpec((tm, GW), lambda i: (i, 0))
    once = lambda t: pl.BlockSpec(t.shape, lambda i: (0,) * t.ndim, pipeline_mode=pl.Buffered(1))
    wspec = lambda t, k: pl.BlockSpec((None,) + t.shape[1:], lambda i: (k, 0, 0), pipeline_mode=pl.Buffered(1))
    if colmajor:
        rows = tm // GRID_W
        per_seq = seq_len // tm
        o_spec = pl.BlockSpec((None, GRID_W, rows, GW), lambda i: (i // per_seq, 0, i % per_seq, 0))
    else:
        o_f, o_b = o_f.reshape(n, GW), o_b.reshape(n, GW)
        o_spec = part
    return pl.pallas_call(
        functools.partial(_out_ffn_kernel, colmajor=colmajor), grid=(n // tm,),
        in_specs=[part, part, o_spec, o_spec, pl.BlockSpec((tm, GW), lambda i: (i, C_DGATE)), part,
                  pl.BlockSpec((tm, D), lambda i: (i, 0)),
                  pl.BlockSpec((None, 1, 6 * D), lambda i: (i // tiles_per_mod, 0, 0)),
                  wspec(p['w_out'], p['layer']), once(p['ln1_g']), once(p['ln1_b']), once(p['gdn_norm_g']),
                  once(cst['mbd']), wspec(f['w_gate'], f['index']), wspec(f['w_up'], f['index']),
                  wspec(f['w_down'], f['index']), once(f['ln2_g']), once(f['ln2_b'])],
        out_specs=pl.BlockSpec((tm, D), lambda i: (i, 0)),
        out_shape=jax.ShapeDtypeStruct((n, D), f32),
        scratch_shapes=[pltpu.VMEM((tm, D), f32)],
        compiler_params=_cparams(("arbitrary",)), name="out_proj_dense_ffn",
    )(a, gl, o_f, o_b, u, sc, x, mod, p['w_out'], p['ln1_g'], p['ln1_b'], p['gdn_norm_g'], cst['mbd'],
      f['w_gate'], f['w_up'], f['w_down'], f['ln2_g'], f['ln2_b'])


MOE_TM = 1024
MOE_SB = 128
MOE_MID = 192
MOE_WIDE = 320


def _moe_kernel(x_ref, mod_ref, wg_ref, wu_ref, wd_ref, lg_ref, lb_ref, rt_ref, tri_ref, o_ref,
                h_scr, acc_scr, comb_scr, pos_scr):
    e = pl.program_id(1)
    tm = x_ref.shape[0]

    @pl.when(e == 0)
    def _():
        m = mod_ref[...]
        h = x_ref[...] * (1.0 + m[:, 4 * D:5 * D]) + m[:, 3 * D:4 * D]
        h_scr[...] = h.astype(bf16)
        acc_scr[...] = jnp.zeros(acc_scr.shape, f32)
        hh, hr = _split2(h)
        rh, rr = _split2(rt_ref[...])
        d = lambda x, y: jnp.dot(x, y, preferred_element_type=f32)
        logits = d(hh, rh) + d(hr.astype(bf16), rh) + d(hh, rr.astype(bf16))
        lane = lax.broadcasted_iota(jnp.int32, logits.shape, 1).astype(f32)
        neg = jnp.float32(-jnp.inf)
        logits = jnp.where(lane < N_EXPERTS, logits, neg)
        m1 = jnp.max(logits, axis=-1, keepdims=True)
        i1 = jnp.min(jnp.where(logits == m1, lane, float(LANE)), axis=-1, keepdims=True)
        rest = jnp.where(lane == i1, neg, logits)
        m2 = jnp.max(rest, axis=-1, keepdims=True)
        i2 = jnp.min(jnp.where(rest == m2, lane, float(LANE)), axis=-1, keepdims=True)
        e2 = jnp.exp(m2 - m1)
        p1 = 1.0 / (1.0 + e2)
        p2 = e2 / (1.0 + e2)
        sel1 = lane == i1
        sel2 = lane == i2
        comb = jnp.where(sel1, p1, 0.0) + jnp.where(sel2, p2, 0.0)
        c_hi, c_mid, c_lo = _split3(comb)
        comb_scr[...] = jnp.where(lane < N_EXPERTS, c_hi.astype(f32),
                                  jnp.where(lane < 2 * N_EXPERTS, pltpu.roll(c_mid.astype(f32), N_EXPERTS, 1),
                                            pltpu.roll(c_lo.astype(f32), 2 * N_EXPERTS, 1))).astype(bf16)
        routed = jnp.where(sel1 | sel2, 1.0, 0.0).astype(bf16)
        incl = lax.dot_general(routed, tri_ref[...], (((0,), (0,)), ((), ())), preferred_element_type=f32)
        tok = lax.broadcasted_iota(jnp.int32, incl.shape, 1)
        excl = jnp.where(tok == 0, 0.0, pltpu.roll(incl, 1, 1))
        pos_scr[...] = jnp.where(incl > excl, excl, -1.0)

    pos_row = pos_scr[pl.ds(e, 1), :]
    nrows = (jnp.max(pos_row) + 1.0).astype(jnp.int32)

    def run_pass(base, sb):
        slot = lax.broadcasted_iota(jnp.int32, (sb, tm), 0).astype(f32) + base.astype(f32)
        lane_sb = lax.broadcasted_iota(jnp.int32, (sb, LANE), 1)
        sel = jnp.where(pos_row == slot, 1.0, 0.0).astype(bf16)
        xs = jnp.dot(sel, h_scr[...], preferred_element_type=f32).astype(bf16)
        t = _silu(jnp.dot(xs, wg_ref[...], preferred_element_type=f32)) * jnp.dot(xs, wu_ref[...], preferred_element_type=f32)
        y = jnp.dot(t.astype(bf16), wd_ref[...], preferred_element_type=f32)
        cw3 = jnp.dot(sel, comb_scr[...], preferred_element_type=f32)
        cw = jnp.sum(jnp.where((lane_sb & (N_EXPERTS - 1)) == e, cw3, 0.0), axis=-1, keepdims=True)
        acc_scr[...] += lax.dot_general(sel, (y * cw).astype(bf16), (((0,), (0,)), ((), ())), preferred_element_type=f32)

    big = 2 * MOE_SB
    one_wide = (nrows > big) & (nrows <= MOE_WIDE)

    @pl.when(one_wide)
    def _():
        run_pass(jnp.int32(0), MOE_WIDE)

    @pl.when(jnp.logical_not(one_wide))
    def _():
        nbig = (nrows + big - MOE_MID - 1) // big
        rem = nrows - nbig * big

        def big_pass(i, carry):
            run_pass(i * big, big)
            return carry

        lax.fori_loop(0, nbig, big_pass, 0)

        @pl.when(rem > MOE_SB)
        def _():
            run_pass(nbig * big, MOE_MID)

        @pl.when((rem > 0) & (rem <= MOE_SB))
        def _():
            run_pass(nbig * big, MOE_SB)

    @pl.when(e == N_EXPERTS - 1)
    def _():
        m = mod_ref[...]
        o_ref[...] = _ln(ALPHA * x_ref[...] + m[:, 5 * D:6 * D] * acc_scr[...], lg_ref[...], lb_ref[...])


def _moe_call(x, mod, p, seq_len, cst, tm=MOE_TM):
    n = x.shape[0]
    tm = min(tm, n)
    fw = D_FF_EXPERT
    tiles_per_mod = seq_len // tm if mod.shape[0] > 1 else n // tm
    full = lambda t: pl.BlockSpec(t.shape, lambda i, g: (0,) * t.ndim)
    tri = cst['tri_tok'][:tm, :tm]
    return pl.pallas_call(
        _moe_kernel, grid=(n // tm, N_EXPERTS),
        in_specs=[pl.BlockSpec((tm, D), lambda i, g: (i, 0)),
                  pl.BlockSpec((None, 1, 6 * D), lambda i, g: (i // tiles_per_mod, 0, 0)),
                  pl.BlockSpec((None, None, D, fw), lambda i, g: (p['index'], g, 0, 0)),
                  pl.BlockSpec((None, None, D, fw), lambda i, g: (p['index'], g, 0, 0)),
                  pl.BlockSpec((None, None, fw, D), lambda i, g: (p['index'], g, 0, 0)),
                  full(p['ln2_g']), full(p['ln2_b']), full(p['router']), full(tri)],
        out_specs=pl.BlockSpec((tm, D), lambda i, g: (i, 0)),
        out_shape=jax.ShapeDtypeStruct((n, D), f32),
        scratch_shapes=[pltpu.VMEM((tm, D), bf16), pltpu.VMEM((tm, D), f32),
                        pltpu.VMEM((tm, LANE), bf16), pltpu.VMEM((LANE, tm), f32)],
        compiler_params=_cparams(("arbitrary", "arbitrary")), name="moe_ffn",
    )(x, mod, p['w_gate'], p['w_up'], p['w_down'], p['ln2_g'], p['ln2_b'], p['router'], tri)


def _constants():
    hb = np.arange(GW) // DH
    mbd = (hb[:, None] == hb[None, :]).astype(np.float32)
    r = np.arange(CH)
    t = np.arange(MOE_TM)
    e = np.zeros((LANE, 3 * GW), np.float32)
    for h in range(NH):
        e[L_BETA + h, h * DH:(h + 1) * DH] = 1.0
        for d in range(2):
            e[L_DEC + NH * d + h, (1 + d) * GW + h * DH:(1 + d) * GW + (h + 1) * DH] = 1.0
    return dict(mbd=jnp.asarray(mbd, bf16), mbdf=jnp.asarray(mbd, f32),
                tril=jnp.asarray(r[:, None] >= r[None, :], bf16), triu=jnp.asarray(r[:, None] <= r[None, :], bf16),
                e_gdn=jnp.asarray(e, bf16), tri_tok=jnp.asarray(t[:, None] <= t[None, :], bf16))


_IN_SIZES = (GW, GW, GW, GW, GW, GW, 2 * GLA_LR, GW, GW, GW, GW, NH, 2 * NH, GW, GW, GW)


def _pack_w_in(w_in):
    offs = np.concatenate([[0], np.cumsum(_IN_SIZES)])
    big = [i for i, s in enumerate(_IN_SIZES) if s == GW]
    small = [i for i, s in enumerate(_IN_SIZES) if s != GW]
    parts = [w_in[..., int(offs[i]):int(offs[i + 1])] for i in big + small]
    pad = NU - int(offs[-1])
    parts.append(jnp.zeros(w_in.shape[:-1] + (pad,), w_in.dtype))
    return jnp.concatenate(parts, axis=-1).astype(bf16)


def _to_bd(s, transpose):
    if transpose:
        s = jnp.swapaxes(s, -1, -2)
    b = s.shape[0]
    z = jnp.zeros_like(s)
    rows = [jnp.concatenate([s[:, :, h] if g == h else z[:, :, h] for g in range(NH)], axis=-1) for h in range(NH)]
    return jnp.concatenate(rows, axis=-2).reshape(b, 2, GW, GW)


def _from_bd(s, transpose):
    out = jnp.stack([s[:, :, h * DH:(h + 1) * DH, h * DH:(h + 1) * DH] for h in range(NH)], axis=2)
    return jnp.swapaxes(out, -1, -2) if transpose else out


def _stream_layer(x, mod, nseq, seq_len, s0_gla, s0_gdn, p, cst, colmajor, moe):
    tpm_512 = max(seq_len // 512, 1) if mod.shape[0] > 1 else x.shape[0] // 512
    u, a, ysc = _in_conv_call(x, mod, p['w_in'], p['layer'], tpm_512, seq_len, p)
    ygla, sgla = _gla_call(u, nseq, seq_len, s0_gla, p, cst)
    o_f, o_b, sgdn = _gdn_call(u, nseq, seq_len, s0_gdn, p, cst, colmajor)
    if moe:
        x1 = _out_call(a, ygla, o_f, o_b, u, ysc, x, mod, p, cst, seq_len, colmajor)
        x2 = _moe_call(x1, mod, p['ffn'], seq_len, cst)
    else:
        x2 = _out_ffn_call(a, ygla, o_f, o_b, u, ysc, x, mod, p, cst, seq_len, colmajor)
    return x2, sgla, sgdn


def kernel(x_prompt, x_sample, c, state_gla, state_gdn, c_ctx, w_ada, b_ada, w_in, w_out, conv_w, conv_b, conv_ln_g, conv_ln_b, conv_pw, gla_w_lr, gla_b_lr, gla_norm_g, gdn_conv_w, gdn_a_log, gdn_dt_bias, gdn_norm_g, sc_conv_w, ln1_g, ln1_b, ln2_g, ln2_b, ffn_w_gate, ffn_w_up, ffn_w_down, moe_router, moe_w_gate, moe_w_up, moe_w_down):
    nb, seq, _ = x_prompt.shape
    db, dseq, _ = x_sample.shape
    depth = w_in.shape[0]
    cst = _constants()

    rows = 16
    cond = jnp.concatenate([c_ctx[None, :], c, jnp.zeros((rows - 1 - db, D), f32)], axis=0)
    mod = _mod_call(cond, w_ada, b_ada)

    w_in_p = _pack_w_in(w_in)
    w_out_b = w_out.astype(bf16)
    conv_pw_b = conv_pw.astype(bf16)
    wlr = jnp.zeros((depth, 2, LANE, GW), f32)
    wlr = wlr.at[:, 0, 0:GLA_LR].set(gla_w_lr[:, 0]).at[:, 1, GLA_LR:2 * GLA_LR].set(gla_w_lr[:, 1]).astype(bf16)
    rep = lambda t: jnp.repeat(t, DH, axis=-1)[:, :, None, :]
    row = lambda t: t[:, None, :]
    ffn_g, ffn_u, ffn_d = ffn_w_gate.astype(bf16), ffn_w_up.astype(bf16), ffn_w_down.astype(bf16)
    moe_g, moe_u, moe_d = moe_w_gate.astype(bf16), moe_w_up.astype(bf16), moe_w_down.astype(bf16)
    router = jnp.pad(moe_router, ((0, 0), (0, 0), (0, LANE - N_EXPERTS)))

    zero_bd = jnp.zeros((nb, 2, GW, GW), f32)
    y_p = x_prompt.reshape(nb * seq, D)
    y_s = x_sample.reshape(db * dseq, D)
    gla_states, gdn_states = [], []
    for l in range(depth):
        j = l // 2
        moe = l % 2 == 1
        if moe:
            ffn = dict(w_gate=moe_g, w_up=moe_u, w_down=moe_d, index=j, router=router[j],
                       ln2_g=row(ln2_g)[l], ln2_b=row(ln2_b)[l])
        else:
            ffn = dict(w_gate=ffn_g, w_up=ffn_u, w_down=ffn_d, index=j, ln2_g=row(ln2_g)[l], ln2_b=row(ln2_b)[l])
        p = dict(layer=l, w_in=w_in_p, w_out=w_out_b, conv_w=conv_w[l], conv_b=row(conv_b)[l],
                 conv_ln_g=row(conv_ln_g)[l], conv_ln_b=row(conv_ln_b)[l], conv_pw=conv_pw_b[l],
                 sc_conv_w=sc_conv_w[l], gla_wlr=wlr[l], gla_blr=gla_b_lr[l][:, None, :],
                 gla_norm_g=row(gla_norm_g)[l], gdn_conv_w=gdn_conv_w[l], gdn_alog=rep(gdn_a_log)[l],
                 gdn_dtb=rep(gdn_dt_bias)[l], gdn_norm_g=row(gdn_norm_g)[l],
                 ln1_g=row(ln1_g)[l], ln1_b=row(ln1_b)[l], ffn=ffn)
        mod_ctx = mod[l, 0:1][:, None, :]
        mod_lat = mod[l, 1:1 + db][:, None, :]
        y_p, s_gla, s_gdn = _stream_layer(y_p, mod_ctx, nb, seq, zero_bd, zero_bd, p, cst, False, moe)
        gla_states.append(_from_bd(s_gla, True))
        gdn_states.append(_from_bd(s_gdn, False))
        y_s, _, _ = _stream_layer(y_s, mod_lat, db, dseq, _to_bd(state_gla[:, l], True),
                                  _to_bd(state_gdn[:, l], False), p, cst, True, moe)
    return (y_p.reshape(nb, seq, D), y_s.reshape(db, dseq, D),
            jnp.stack(gla_states, axis=1), jnp.stack(gdn_states, axis=1))
```

```python
import functools
import math

import numpy as np
import jax
import jax.numpy as jnp
from jax import lax
from jax.experimental import pallas as pl
from jax.experimental.pallas import tpu as pltpu

f32 = jnp.float32
bf16 = jnp.bfloat16

D = 1024
GW = 256
NH = 4
DH = 64
CH = 64
GRID_W = 64
CONV_K = 31
GLA_LR = 16
GLA_TAU = 16.0
DEPTH = 4
N_EXPERTS = 8
D_FF = 2816
D_FF_EXPERT = 1408
ALPHA = (2 * DEPTH) ** 0.25
EPS = 1e-5
NU = 13 * GW + 128
SMALL_BLK = 13 * GW // 128
LANE = 128
C_AVAL, C_AGATE, C_GQ, C_GK, C_GV, C_GG, C_DQ, C_DK, C_DV, C_DGATE, C_SB, C_SC, C_SX = range(13)
L_BETA = 2 * GLA_LR
L_DEC = L_BETA + NH

VMEM_LIMIT = 56 * 1024 * 1024


def _cparams(sem):
    return pltpu.CompilerParams(dimension_semantics=sem, vmem_limit_bytes=VMEM_LIMIT)


def _dot(a, b):
    return jnp.dot(a.astype(bf16), b.astype(bf16), preferred_element_type=f32)


def _dot_nt(a, b):
    return lax.dot_general(a.astype(bf16), b.astype(bf16), (((1,), (1,)), ((), ())),
                           preferred_element_type=f32)


def _dot_tn(a, b):
    return lax.dot_general(a.astype(bf16), b.astype(bf16), (((0,), (0,)), ((), ())),
                           preferred_element_type=f32)


def _split2(x):
    hi = x.astype(bf16)
    return hi, x - hi.astype(f32)


def _split3(x):
    hi = x.astype(bf16)
    r = x - hi.astype(f32)
    mid = r.astype(bf16)
    lo = (r - mid.astype(f32)).astype(bf16)
    return hi, mid, lo


def _dot01_left(m01, x):
    hi, r = _split2(x)
    d = lambda t: jnp.dot(m01, t, preferred_element_type=f32)
    return d(hi) + d(r.astype(bf16))


def _dot01_right(x, m01):
    hi, r = _split2(x)
    d = lambda t: jnp.dot(t, m01, preferred_element_type=f32)
    return d(hi) + d(r.astype(bf16))


def _bd(x, mask):
    xb = x.astype(bf16)
    return jnp.concatenate([xb, xb, xb, xb], axis=0) * mask


def _dot3(a_parts, b_parts, mask):
    ah, ar = a_parts
    bh, br = b_parts
    al = ar.astype(bf16)
    bdh = _bd(bh, mask)
    bdl = _bd(br, mask)
    d = lambda x, y: jnp.dot(x, y, preferred_element_type=f32)
    return d(ah, bdh) + d(al, bdh) + d(ah, bdl)


def _silu(x):
    return x * jax.nn.sigmoid(x)


def _ln(x, g, b):
    xc = x - jnp.mean(x, axis=-1, keepdims=True)
    var = jnp.mean(xc * xc, axis=-1, keepdims=True)
    return xc * lax.rsqrt(var + EPS) * g + b


def _head_iotas():
    rowi = lax.broadcasted_iota(jnp.int32, (CH, GW), 0)
    coli = lax.broadcasted_iota(jnp.int32, (CH, GW), 1) & (DH - 1)
    return rowi, coli


def _mod_kernel(c_ref, w_ref, b_ref, o_ref):
    o_ref[...] = jnp.dot(_silu(c_ref[...]), w_ref[...], precision=lax.Precision.HIGHEST,
                         preferred_element_type=f32) + b_ref[...]


def _mod_call(cond, w_ada, b_ada):
    nl = w_ada.shape[0]
    rows = cond.shape[0]
    return pl.pallas_call(
        _mod_kernel, grid=(nl, 6),
        in_specs=[pl.BlockSpec((rows, D), lambda l, n: (0, 0)),
                  pl.BlockSpec((None, D, D), lambda l, n: (l, 0, n)),
                  pl.BlockSpec((None, 1, D), lambda l, n: (l, 0, n))],
        out_specs=pl.BlockSpec((None, rows, D), lambda l, n: (l, 0, n)),
        out_shape=jax.ShapeDtypeStruct((nl, rows, 6 * D), f32),
        compiler_params=_cparams(("arbitrary", "arbitrary")), name="adaln_mod",
    )(cond, w_ada, b_ada.reshape(nl, 1, 6 * D))


TM_IN = 512
CT = 256
HALO = 16
SHALO = 8


def _in_conv_kernel(x_ref, mod_ref, w_ref, cw, cb, lng, lnb, pw, scw, u_ref, a_out, ysc_out,
                    a_st, s_st, sb_st, tail_a, tail_s, pad_a, pad_s, win_scr, *, subs_per_seq):
    i = pl.program_id(0)
    slot = i % 2
    pslot = 1 - slot

    @pl.when(i == 0)
    def _():
        a_st[...] = jnp.zeros(a_st.shape, f32)
        s_st[...] = jnp.zeros(s_st.shape, f32)
        sb_st[...] = jnp.zeros(sb_st.shape, f32)

    tail_a[...] = a_st[slot, TM_IN - HALO:TM_IN, :]
    tail_s[...] = s_st[slot, TM_IN - SHALO:TM_IN, :]

    m = mod_ref[...]
    h = x_ref[...] * (1.0 + m[:, D:2 * D]) + m[:, 0:D]
    u = jnp.dot(h.astype(bf16), w_ref[...], preferred_element_type=f32)
    u_ref[...] = u
    slab = lambda c: u[:, c * GW:(c + 1) * GW]
    a_st[slot] = slab(C_AVAL) * jax.nn.sigmoid(slab(C_AGATE))
    s_st[slot] = slab(C_SC) * slab(C_SX)
    sb_st[slot] = slab(C_SB)

    base = HALO - CONV_K // 2
    span = 8 * ((CONV_K + 6) // 8)
    for hf in range(TM_IN // CT):
        g = (i - 1) * (TM_IN // CT) + hf
        first = g % subs_per_seq == 0
        last = g % subs_per_seq == subs_per_seq - 1
        lo, hi = hf * CT, (hf + 1) * CT
        if hf == 0:
            prev_a, prev_s = tail_a[...], tail_s[...]
        else:
            prev_a, prev_s = a_st[pslot, lo - HALO:lo, :], s_st[pslot, lo - SHALO:lo, :]
        if hi == TM_IN:
            next_a, next_s = a_st[slot, 0:HALO, :], s_st[slot, 0:SHALO, :]
        else:
            next_a, next_s = a_st[pslot, hi:hi + HALO, :], s_st[pslot, hi:hi + SHALO, :]
        pad_a[0:HALO, :] = jnp.where(first, 0.0, prev_a)
        pad_a[HALO:HALO + CT, :] = a_st[pslot, lo:hi, :]
        pad_a[HALO + CT:2 * HALO + CT, :] = jnp.where(last, 0.0, next_a)
        acc = jnp.zeros((CT, GW), f32)
        for ph in range(8):
            win_scr[...] = pad_a[ph:ph + CT + span - 8, :]
            for al in range(0, span, 8):
                k = al + ph - base
                if 0 <= k < CONV_K:
                    acc = acc + win_scr[al:al + CT, :] * cw[k:k + 1, :]
        a = _ln(acc + cb[...], lng[...], lnb[...])
        a_out[lo:hi, :] = _dot(_silu(a), pw[...]).astype(bf16)

        pad_s[0:SHALO, :] = jnp.where(first, 0.0, prev_s)
        pad_s[SHALO:SHALO + CT, :] = s_st[pslot, lo:hi, :]
        pad_s[SHALO + CT:2 * SHALO + CT, :] = jnp.where(last, 0.0, next_s)
        acc = jnp.zeros((CT, GW), f32)
        for k in range(3):
            o = SHALO - 1 + k
            acc = acc + pad_s[o:o + CT, :] * scw[k:k + 1, :]
        ysc_out[lo:hi, :] = (sb_st[pslot, lo:hi, :] * acc).astype(bf16)


def _in_conv_call(x, mod, w, li, tiles_per_mod, seq_len, p):
    n = x.shape[0]
    tm = TM_IN
    nt = n // tm
    cur = lambda i: jnp.minimum(i, nt - 1)
    prv = lambda i: jnp.maximum(i - 1, 0)
    full = lambda a: pl.BlockSpec(a.shape, lambda i: (0,) * a.ndim)
    consts = (p['conv_w'], p['conv_b'], p['conv_ln_g'], p['conv_ln_b'], p['conv_pw'], p['sc_conv_w'])
    return pl.pallas_call(
        functools.partial(_in_conv_kernel, subs_per_seq=seq_len // CT), grid=(nt + 1,),
        in_specs=[pl.BlockSpec((tm, D), lambda i: (cur(i), 0)),
                  pl.BlockSpec((None, 1, 6 * D), lambda i: (cur(i) // tiles_per_mod, 0, 0)),
                  pl.BlockSpec((None, D, NU), lambda i: (li, 0, 0))] + [full(a) for a in consts],
        out_specs=[pl.BlockSpec((tm, NU), lambda i: (cur(i), 0)),
                   pl.BlockSpec((tm, GW), lambda i: (prv(i), 0)),
                   pl.BlockSpec((tm, GW), lambda i: (prv(i), 0))],
        out_shape=[jax.ShapeDtypeStruct((n, NU), f32), jax.ShapeDtypeStruct((n, GW), bf16),
                   jax.ShapeDtypeStruct((n, GW), bf16)],
        scratch_shapes=[pltpu.VMEM((2, tm, GW), f32), pltpu.VMEM((2, tm, GW), f32), pltpu.VMEM((2, tm, GW), f32),
                        pltpu.VMEM((HALO, GW), f32), pltpu.VMEM((SHALO, GW), f32),
                        pltpu.VMEM((CT + 2 * HALO, GW), f32), pltpu.VMEM((CT + 2 * SHALO, GW), f32),
                        pltpu.VMEM((CT + 8 * ((CONV_K + 6) // 8) - 8, GW), f32)],
        compiler_params=_cparams(("arbitrary",)), name="in_proj_conv",
    )(x, mod, w, *consts)


GLA_LOCK = 16


def _gla_kernel(q_ref, k_ref, v_ref, g_ref, sm_ref, s0_ref, wlr_ref, blr_ref, ng_ref,
                mbd_ref, mbdf_ref, tril_ref, triu_ref, y_ref, sfin_ref, st_scr, of_scr, *, tt, ntiles):
    p = pl.program_id(1)
    nch = tt // CH
    nlock = min(GLA_LOCK, nch)
    rowi, coli = _head_iotas()

    @pl.when(p == 0)
    def _():
        st_scr[...] = s0_ref[0]

    @pl.when(p == ntiles)
    def _():
        sfin_ref[0] = st_scr[...]
        st_scr[...] = s0_ref[1]

    def run(dirn):
        tri = tril_ref[...] if dirn == 0 else triu_ref[...]
        keep = (rowi >= coli) if dirn == 0 else (rowi <= coli)
        wlr = wlr_ref[dirn]
        blr = blr_ref[dirn]
        tile = p if dirn == 0 else 2 * ntiles - 1 - p

        def body(it, carry):
            cis = [it * nlock + c for c in range(nlock)]
            if dirn == 1:
                cis = [nch - 1 - ci for ci in cis]
            rows = [pl.ds(pl.multiple_of(ci * CH, CH), CH) for ci in cis]
            grows = [pl.ds(pl.multiple_of(tile * tt + ci * CH, CH), CH) for ci in cis]
            mbd = mbd_ref[...]
            zs = [jnp.dot(sm_ref[r, :].astype(bf16), wlr, preferred_element_type=f32) + blr for r in rows]
            cums = [_dot01_left(tri, jax.nn.log_sigmoid(z) * (1.0 / GLA_TAU)) for z in zs]
            tots = [cum[CH - 1:CH, :] if dirn == 0 else cum[0:1, :] for cum in cums]
            vs = [v_ref[r, :] for r in rows]
            qes = [q_ref[r, :] * (DH ** -0.5) * jnp.exp(cum) for r, cum in zip(rows, cums)]
            kes = [k_ref[r, :] * jnp.exp(-cum) for r, cum in zip(rows, cums)]
            kds = [k_ref[r, :] * jnp.exp(tot - cum) for r, cum, tot in zip(rows, cums, tots)]
            atts = [jnp.where(keep, _dot_nt(qe, _bd(ke, mbd)), 0.0) for qe, ke in zip(qes, kes)]
            ols = [_dot(att, _bd(v, mbd)) for att, v in zip(atts, vs)]
            upds = [_dot_tn(v, kd) for v, kd in zip(vs, kds)]
            st = st_scr[...]
            os_ = []
            for c in range(nlock):
                os_.append(ols[c] + _dot_nt(qes[c], st))
                st = st * jnp.exp(tots[c]) + mbdf_ref[...] * upds[c]
            st_scr[...] = st
            if dirn == 0:
                for c in range(nlock):
                    of_scr[grows[c], :] = os_[c]
            else:
                os_ = [of_scr[grows[c], :] + os_[c] for c in range(nlock)]
                mss = [_dot(o * o, mbd) * (1.0 / DH) for o in os_]
                for c in range(nlock):
                    y = os_[c] * lax.rsqrt(mss[c] + EPS) * ng_ref[...] * _silu(g_ref[rows[c], :])
                    y_ref[rows[c], :] = y.astype(bf16)
            return carry

        lax.fori_loop(0, nch // nlock, body, 0)

    @pl.when(p < ntiles)
    def _():
        run(0)

    @pl.when(p >= ntiles)
    def _():
        run(1)

    @pl.when(p == 2 * ntiles - 1)
    def _():
        sfin_ref[1] = st_scr[...]


def _gla_call(u, nseq, seq_len, s0, p, cst):
    tt = min(seq_len, 1024)
    ntiles = seq_len // tt
    tmap = lambda q: jnp.where(q < ntiles, q, 2 * ntiles - 1 - q)
    main = lambda c: pl.BlockSpec((tt, GW), lambda s, q: (s * ntiles + tmap(q), c))
    full = lambda a: pl.BlockSpec(a.shape, lambda s, q: (0,) * a.ndim)
    consts = (p['gla_wlr'], p['gla_blr'], p['gla_norm_g'], cst['mbd'], cst['mbdf'], cst['tril'], cst['triu'])
    return pl.pallas_call(
        functools.partial(_gla_kernel, tt=tt, ntiles=ntiles), grid=(nseq, 2 * ntiles),
        in_specs=[main(C_GQ), main(C_GK), main(C_GV), main(C_GG),
                  pl.BlockSpec((tt, LANE), lambda s, q: (s * ntiles + tmap(q), SMALL_BLK)),
                  pl.BlockSpec((None, 2, GW, GW), lambda s, q: (s, 0, 0, 0))] + [full(a) for a in consts],
        out_specs=[pl.BlockSpec((tt, GW), lambda s, q: (s * ntiles + jnp.where(q < ntiles, ntiles - 1, 2 * ntiles - 1 - q), 0)),
                   pl.BlockSpec((None, 2, GW, GW), lambda s, q: (s, 0, 0, 0))],
        out_shape=[jax.ShapeDtypeStruct((nseq * seq_len, GW), bf16),
                   jax.ShapeDtypeStruct((nseq, 2, GW, GW), f32)],
        scratch_shapes=[pltpu.VMEM((GW, GW), f32), pltpu.VMEM((seq_len, GW), f32)],
        compiler_params=_cparams(("arbitrary", "arbitrary")), name="gla",
    )(u, u, u, u, u, s0, *consts)


GDN_G = 8
HR = 8
NLOCK = 4
INV_LEVELS = 6
INV_HI_LEVELS = 6


def _gdn_prep_kernel(*refs, g, ngroups, colmajor):
    if ngroups > 1:
        (q_ref, k_ref, v_ref, sm_ref, qp, kp, vp, qn, kn, vnx, cw_ref, alog_ref, dtb_ref, e_ref, mbd_ref,
         tril_ref, triu_ref, u_out, wq_out, qkd_out, kd_out, egl_out) = refs
    else:
        (q_ref, k_ref, v_ref, sm_ref, cw_ref, alog_ref, dtb_ref, e_ref, mbd_ref,
         tril_ref, triu_ref, u_out, wq_out, qkd_out, kd_out, egl_out) = refs
        qp = kp = vp = qn = kn = vnx = None
    gi = pl.program_id(1)
    rowi, coli = _head_iotas()
    row_first = rowi == 0
    row_last = rowi == CH - 1
    icat = jnp.where(rowi == coli, 1.0, 0.0)

    if colmajor:
        get = lambda ref, j: ref[:, j, :]
        last_row = lambda ref, j: ref[CH - 1, pl.ds(j, 1), :]
        first_row = lambda ref, j: ref[0, pl.ds(j, 1), :]
    else:
        get = lambda ref, j: ref[j]
        last_row = lambda ref, j: ref[j, CH - 1:CH, :]
        first_row = lambda ref, j: ref[j, 0:1, :]

    def shared(jj):
        jm = jnp.maximum(jj - 1, 0)
        jp = jnp.minimum(jj + 1, g - 1)

        def conv(ref, pref, nref, c0):
            x = get(ref, jj)
            if pref is None:
                pr_out = 0.0
                nx_out = 0.0
            else:
                pr_out = jnp.where(gi > 0, pref[HR - 1, HR - 1:HR, :], 0.0)
                nx_out = jnp.where(gi < ngroups - 1, nref[0, 0:1, :], 0.0)
            pr = jnp.where(jj > 0, last_row(ref, jm), pr_out)
            nx = jnp.where(jj < g - 1, first_row(ref, jp), nx_out)
            xd = jnp.where(row_first, pr, pltpu.roll(x, 1, 0))
            xu = jnp.where(row_last, nx, pltpu.roll(x, CH - 1, 0))
            return _silu(xd * cw_ref[0:1, c0:c0 + GW] + x * cw_ref[1:2, c0:c0 + GW] + xu * cw_ref[2:3, c0:c0 + GW])

        mbd = mbd_ref[...]
        q = conv(q_ref, qp, qn, 0)
        k = conv(k_ref, kp, kn, GW)
        v = conv(v_ref, vp, vnx, 2 * GW)
        q = q * lax.rsqrt(_dot(q * q, mbd) + 1e-6) * (DH ** -0.5)
        k = k * lax.rsqrt(_dot(k * k, mbd) + 1e-6)
        ex = _dot01_right(get(sm_ref, jj), e_ref[...])
        beta = jax.nn.sigmoid(ex[:, :GW])
        kb = k * beta
        gm = _dot_nt(jnp.concatenate([kb, q], axis=0), _bd(k, mbd))
        return q, k, kb, v * beta, ex, gm

    def gates(sh, dirn):
        q, k, kb, vb, ex, gm = sh
        la = -jnp.exp(alog_ref[dirn]) * jax.nn.softplus(ex[:, (1 + dirn) * GW:(2 + dirn) * GW] + dtb_ref[dirn])
        if dirn == 0:
            tri, keep, strict, keep_t = tril_ref[...], rowi >= coli, rowi > coli, rowi <= coli
        else:
            tri, keep, strict, keep_t = triu_ref[...], rowi <= coli, rowi < coli, rowi >= coli
        gx = _dot01_left(tri, la)
        grow = jnp.sum(jnp.where(keep_t, la, 0.0), axis=0, keepdims=True)
        decay = jnp.where(keep, jnp.exp(jnp.where(keep, gx - grow, 0.0)), 0.0)
        glast = gx[CH - 1:CH, :] if dirn == 0 else gx[0:1, :]
        nm = jnp.where(strict, -(gm[:CH] * decay), 0.0)
        return gx, glast, nm, gm[CH:] * decay

    def body(it, carry):
        jjs = [it * NLOCK + c for c in range(NLOCK)]
        shs = [shared(jj) for jj in jjs]
        chains = [(c, dirn) for c in range(NLOCK) for dirn in range(2)]
        gs = [gates(shs[c], dirn) for c, dirn in chains]
        mbd = mbd_ref[...]
        nms = [gt[2] for gt in gs]
        cat = lambda x, y: jnp.concatenate([x, y], axis=0)
        nsp = [_split2(nm) for nm in nms]
        pws = [_dot3(sp, sp, mbd) for sp in nsp]
        tinvs = [icat + nm for nm in nms]
        for lvl in range(1, INV_LEVELS):
            last = lvl == INV_LEVELS - 1
            lhs = (lambda t, q: t) if last else cat
            if lvl < INV_HI_LEVELS:
                psp = [_split2(pw) for pw in pws]
                tsp = [_split2(ti) for ti in tinvs]
                rs = [_dot3((lhs(th, ph), lhs(tr, pr)), (ph, pr), mbd) for (th, tr), (ph, pr) in zip(tsp, psp)]
            else:
                rs = [_dot(lhs(ti, pw), _bd(pw, mbd)) for ti, pw in zip(tinvs, pws)]
            tinvs = [ti + r[:CH] for ti, r in zip(tinvs, rs)]
            pws = [r[CH:] for r in rs]
        egs = [jnp.exp(gt[0]) for gt in gs]
        us = [_dot(ti, _bd(shs[c][3], mbd)) for ti, (c, dirn) in zip(tinvs, chains)]
        ws = [_dot(ti, _bd(shs[c][2] * eg, mbd)) for ti, eg, (c, dirn) in zip(tinvs, egs, chains)]
        for i, (c, dirn) in enumerate(chains):
            q, k = shs[c][0], shs[c][1]
            gx, glast, _, qkd = gs[i]
            jj = jjs[c]
            u_out[dirn, jj] = us[i]
            wq_out[dirn, jj] = jnp.concatenate([ws[i], q * egs[i]], axis=0).astype(bf16)
            qkd_out[dirn, jj] = qkd.astype(bf16)
            kd_out[dirn, jj] = (k * jnp.exp(glast - gx)).astype(bf16)
            egl_out[dirn, jj] = jnp.exp(glast)
        return carry

    lax.fori_loop(0, g // NLOCK, body, 0)


def _gdn_scan_kernel(uf, wqf, qkf, kdf, egf, ub, wqb, qkb, kdb, egb, s0_ref, mbd_ref, mbdf_ref,
                     of_out, ob_out, sfin_ref, s_scr, *, sg, cg, nt):
    t = pl.program_id(1)

    @pl.when(t == 0)
    def _():
        s_scr[...] = s0_ref[...]

    def body(c, carry):
        chains = [(sq, dirn) for sq in range(sg) for dirn in range(2)]
        src = lambda dirn: (uf, wqf, qkf, kdf, egf, of_out, c) if dirn == 0 else (ub, wqb, qkb, kdb, egb, ob_out, cg - 1 - c)
        mbd = mbd_ref[...]
        ss = [s_scr[sq, dirn] for sq, dirn in chains]
        wss = [jnp.dot(src(dirn)[1][sq, src(dirn)[6]], st.astype(bf16), preferred_element_type=f32)
               for st, (sq, dirn) in zip(ss, chains)]
        vns = [src(dirn)[0][sq, src(dirn)[6]] - ws[:CH] for ws, (sq, dirn) in zip(wss, chains)]
        os_ = [ws[CH:] + jnp.dot(src(dirn)[2][sq, src(dirn)[6]], _bd(vn, mbd), preferred_element_type=f32)
               for ws, vn, (sq, dirn) in zip(wss, vns, chains)]
        upd = [_dot_tn(src(dirn)[3][sq, src(dirn)[6]], vn) for vn, (sq, dirn) in zip(vns, chains)]
        for st, o, up, (sq, dirn) in zip(ss, os_, upd, chains):
            _, _, _, _, eg, o_out, cc = src(dirn)
            o_out[sq, cc] = o
            s_scr[sq, dirn] = st * eg[sq, cc] + mbdf_ref[...] * up
        return carry

    lax.fori_loop(0, cg, body, 0)

    @pl.when(t == nt - 1)
    def _():
        sfin_ref[...] = s_scr[...]


def _gdn_call(u, nseq, seq_len, s0, p, cst, colmajor):
    n = seq_len // CH
    if colmajor:
        assert seq_len == GRID_W * CH
        g, a = GDN_G, GRID_W
        blk = lambda w: (None, a, g, w)
        imap = lambda c: (lambda s, i: (s, 0, i, c))
    else:
        g, a = n, n
        blk = lambda w: (None, g, CH, w)
        imap = lambda c: (lambda s, i: (s, i, 0, c))
    ngroups = n // g
    u4 = u.reshape(nseq, a, CH, NU)
    main = lambda c: pl.BlockSpec(blk(GW), imap(c))
    full2 = lambda x: pl.BlockSpec(x.shape, lambda s, i: (0,) * x.ndim)
    halos, halo_specs = (), []
    if ngroups > 1:
        per = g // HR
        prev = lambda c: pl.BlockSpec((None, HR, HR, GW), lambda s, i: (s, a // HR - 1, jnp.maximum(i * per - 1, 0), c))
        nxt = lambda c: pl.BlockSpec((None, HR, HR, GW), lambda s, i: (s, 0, jnp.minimum((i + 1) * per, a // HR - 1), c))
        halo_specs = [prev(C_DQ), prev(C_DK), prev(C_DV), nxt(C_DQ), nxt(C_DK), nxt(C_DV)]
        halos = (u4,) * 6
    consts = (p['gdn_conv_w'], p['gdn_alog'], p['gdn_dtb'], cst['e_gdn'], cst['mbd'], cst['tril'], cst['triu'])
    per_chunk = lambda rows, dt: jax.ShapeDtypeStruct((nseq, 2, n, rows, GW), dt)
    per_chunk_spec = lambda rows: pl.BlockSpec((None, 2, g, rows, GW), lambda s, i: (s, 0, i, 0, 0))
    uu, wq, qkd, kd, egl = pl.pallas_call(
        functools.partial(_gdn_prep_kernel, g=g, ngroups=ngroups, colmajor=colmajor), grid=(nseq, ngroups),
        in_specs=[main(C_DQ), main(C_DK), main(C_DV), pl.BlockSpec(blk(LANE), imap(SMALL_BLK))] + halo_specs
                 + [full2(x) for x in consts],
        out_specs=[per_chunk_spec(CH), per_chunk_spec(2 * CH), per_chunk_spec(CH), per_chunk_spec(CH), per_chunk_spec(1)],
        out_shape=[per_chunk(CH, f32), per_chunk(2 * CH, bf16), per_chunk(CH, bf16), per_chunk(CH, bf16), per_chunk(1, f32)],
        compiler_params=_cparams(("arbitrary", "arbitrary")), name="gdn_prep",
    )(u4, u4, u4, u4, *halos, *consts)

    sg = math.gcd(nseq, 8)
    cg = min(n, 4)
    nt = n // cg
    fwd = lambda rows: pl.BlockSpec((sg, None, cg, rows, GW), lambda s, t: (s, 0, t, 0, 0))
    bwd = lambda rows: pl.BlockSpec((sg, None, cg, rows, GW), lambda s, t: (s, 1, nt - 1 - t, 0, 0))
    st_spec = pl.BlockSpec((sg, 2, GW, GW), lambda s, t: (s, 0, 0, 0))
    o_shape = jax.ShapeDtypeStruct((nseq, n, CH, GW), f32)
    o_f, o_b, sfin = pl.pallas_call(
        functools.partial(_gdn_scan_kernel, sg=sg, cg=cg, nt=nt), grid=(nseq // sg, nt),
        in_specs=[fwd(CH), fwd(2 * CH), fwd(CH), fwd(CH), fwd(1), bwd(CH), bwd(2 * CH), bwd(CH), bwd(CH), bwd(1),
                  st_spec, full2(cst['mbd']), full2(cst['mbdf'])],
        out_specs=[pl.BlockSpec((sg, cg, CH, GW), lambda s, t: (s, t, 0, 0)),
                   pl.BlockSpec((sg, cg, CH, GW), lambda s, t: (s, nt - 1 - t, 0, 0)), st_spec],
        out_shape=[o_shape, o_shape, jax.ShapeDtypeStruct((nseq, 2, GW, GW), f32)],
        scratch_shapes=[pltpu.VMEM((sg, 2, GW, GW), f32)],
        compiler_params=_cparams(("arbitrary", "arbitrary")), name="gdn_scan",
    )(uu, wq, qkd, kd, egl, uu, wq, qkd, kd, egl, s0, cst['mbd'], cst['mbdf'])

    return o_f, o_b, sfin


OUT_SPLIT = 2


def _out_kernel(a_ref, gl_ref, of_ref, ob_ref, gate_ref, sc_ref, x_ref, mod_ref, w_ref, lg_ref, lb_ref,
                ng_ref, mbd_ref, o_ref, *, colmajor):
    tm = x_ref.shape[0]
    hs = tm // OUT_SPLIT
    rows = [slice(h * hs, (h + 1) * hs) for h in range(OUT_SPLIT)]
    if colmajor:
        per = hs // GRID_W
        os_ = [jnp.concatenate([of_ref[:, rr, :] + ob_ref[:, rr, :] for rr in range(h * per, (h + 1) * per)], axis=0)
               for h in range(OUT_SPLIT)]
    else:
        os_ = [of_ref[r, :] + ob_ref[r, :] for r in rows]
    mss = [_dot(o * o, mbd_ref[...]) * (1.0 / DH) for o in os_]
    gds = [o * lax.rsqrt(ms + EPS) * ng_ref[...] * _silu(gate_ref[r, :]) for o, ms, r in zip(os_, mss, rows)]
    mixes = [jnp.dot(jnp.concatenate([a_ref[r, :], gl_ref[r, :], gd.astype(bf16), sc_ref[r, :]], axis=1), w_ref[...],
                     preferred_element_type=f32) for gd, r in zip(gds, rows)]
    m = mod_ref[...]
    for mix, r in zip(mixes, rows):
        o_ref[r, :] = _ln(ALPHA * x_ref[r, :] + m[:, 2 * D:3 * D] * mix, lg_ref[...], lb_ref[...])


def _out_call(a, gl, o_f, o_b, u, sc, x, mod, p, cst, seq_len, colmajor, tm=512):
    n = x.shape[0]
    tm = min(tm, n)
    tiles_per_mod = seq_len // tm if mod.shape[0] > 1 else n // tm
    part = pl.BlockSpec((tm, GW), lambda i: (i, 0))
    full = lambda t: pl.BlockSpec(t.shape, lambda i: (0,) * t.ndim)
    if colmajor:
        rows = tm // GRID_W
        per_seq = seq_len // tm
        o_spec = pl.BlockSpec((None, GRID_W, rows, GW), lambda i: (i // per_seq, 0, i % per_seq, 0))
    else:
        o_f, o_b = o_f.reshape(n, GW), o_b.reshape(n, GW)
        o_spec = part
    return pl.pallas_call(
        functools.partial(_out_kernel, colmajor=colmajor), grid=(n // tm,),
        in_specs=[part, part, o_spec, o_spec, pl.BlockSpec((tm, GW), lambda i: (i, C_DGATE)), part,
                  pl.BlockSpec((tm, D), lambda i: (i, 0)),
                  pl.BlockSpec((None, 1, 6 * D), lambda i: (i // tiles_per_mod, 0, 0)),
                  pl.BlockSpec((None, D, D), lambda i: (p['layer'], 0, 0)),
                  full(p['ln1_g']), full(p['ln1_b']), full(p['gdn_norm_g']), full(cst['mbd'])],
        out_specs=pl.BlockSpec((tm, D), lambda i: (i, 0)),
        out_shape=jax.ShapeDtypeStruct((n, D), f32),
        compiler_params=_cparams(("arbitrary",)), name="out_proj_ln",
    )(a, gl, o_f, o_b, u, sc, x, mod, p['w_out'], p['ln1_g'], p['ln1_b'], p['gdn_norm_g'], cst['mbd'])


def _ffn_kernel(x_ref, mod_ref, wg_ref, wu_ref, wd_ref, lg_ref, lb_ref, o_ref):
    m = mod_ref[...]
    x = x_ref[...]
    hb = (x * (1.0 + m[:, 4 * D:5 * D]) + m[:, 3 * D:4 * D]).astype(bf16)
    t = _silu(jnp.dot(hb, wg_ref[...], preferred_element_type=f32)) * jnp.dot(hb, wu_ref[...], preferred_element_type=f32)
    f = jnp.dot(t.astype(bf16), wd_ref[...], preferred_element_type=f32)
    o_ref[...] = _ln(ALPHA * x + m[:, 5 * D:6 * D] * f, lg_ref[...], lb_ref[...])


def _ffn_call(x, mod, p, seq_len, tm=512):
    n = x.shape[0]
    tiles_per_mod = seq_len // tm if mod.shape[0] > 1 else n // tm
    once = lambda t: pl.BlockSpec(t.shape, lambda i: (0,) * t.ndim, pipeline_mode=pl.Buffered(1))
    wspec = lambda t: pl.BlockSpec((None,) + t.shape[1:], lambda i: (p['index'], 0, 0), pipeline_mode=pl.Buffered(1))
    return pl.pallas_call(
        _ffn_kernel, grid=(n // tm,),
        in_specs=[pl.BlockSpec((tm, D), lambda i: (i, 0)),
                  pl.BlockSpec((None, 1, 6 * D), lambda i: (i // tiles_per_mod, 0, 0)),
                  wspec(p['w_gate']), wspec(p['w_up']), wspec(p['w_down']), once(p['ln2_g']), once(p['ln2_b'])],
        out_specs=pl.BlockSpec((tm, D), lambda i: (i, 0)),
        out_shape=jax.ShapeDtypeStruct((n, D), f32),
        compiler_params=_cparams(("arbitrary",)), name="dense_ffn",
    )(x, mod, p['w_gate'], p['w_up'], p['w_down'], p['ln2_g'], p['ln2_b'])


FUSE_SPLIT = 2


def _out_ffn_kernel(a_ref, gl_ref, of_ref, ob_ref, gate_ref, sc_ref, x_ref, mod_ref, w_ref, lg_ref, lb_ref,
                    ng_ref, mbd_ref, wg_ref, wu_ref, wd_ref, l2g_ref, l2b_ref, o_ref, x1_scr, *, colmajor):
    tm = x_ref.shape[0]
    hs = tm // FUSE_SPLIT
    for h in range(FUSE_SPLIT):
        r = pl.ds(h * hs, hs)
        if colmajor:
            per = hs // GRID_W
            of_h, ob_h = of_ref.at[:, pl.ds(h * per, per), :], ob_ref.at[:, pl.ds(h * per, per), :]
        else:
            of_h, ob_h = of_ref.at[r], ob_ref.at[r]
        _out_kernel(a_ref.at[r], gl_ref.at[r], of_h, ob_h, gate_ref.at[r], sc_ref.at[r], x_ref.at[r], mod_ref, w_ref,
                    lg_ref, lb_ref, ng_ref, mbd_ref, x1_scr.at[r], colmajor=colmajor)
    for h in range(FUSE_SPLIT):
        r = pl.ds(h * hs, hs)
        _ffn_kernel(x1_scr.at[r], mod_ref, wg_ref, wu_ref, wd_ref, l2g_ref, l2b_ref, o_ref.at[r])


def _out_ffn_call(a, gl, o_f, o_b, u, sc, x, mod, p, cst, seq_len, colmajor, tm=512):
    n = x.shape[0]
    tm = min(tm, n)
    f = p['ffn']
    tiles_per_mod = seq_len // tm if mod.shape[0] > 1 else n // tm
    part = pl.BlockSpec((tm, GW), lambda i: (i, 0))
    once = lambda t: pl.BlockSpec(t.shape, lambda i: (0,) * t.ndim, pipeline_mode=pl.Buffered(1))
    wspec = lambda t, k: pl.BlockSpec((None,) + t.shape[1:], lambda i: (k, 0, 0), pipeline_mode=pl.Buffered(1))
    if colmajor:
        rows = tm // GRID_W
        per_seq = seq_len // tm
        o_spec = pl.BlockSpec((None, GRID_W, rows, GW), lambda i: (i // per_seq, 0, i % per_seq, 0))
    else:
        o_f, o_b = o_f.reshape(n, GW), o_b.reshape(n, GW)
        o_spec = part
    return pl.pallas_call(
        functools.partial(_out_ffn_kernel, colmajor=colmajor), grid=(n // tm,),
        in_specs=[part, part, o_spec, o_spec, pl.BlockSpec((tm, GW), lambda i: (i, C_DGATE)), part,
                  pl.BlockSpec((tm, D), lambda i: (i, 0)),
                  pl.BlockSpec((None, 1, 6 * D), lambda i: (i // tiles_per_mod, 0, 0)),
                  wspec(p['w_out'], p['layer']), once(p['ln1_g']), once(p['ln1_b']), once(p['gdn_norm_g']),
                  once(cst['mbd']), wspec(f['w_gate'], f['index']), wspec(f['w_up'], f['index']),
                  wspec(f['w_down'], f['index']), once(f['ln2_g']), once(f['ln2_b'])],
        out_specs=pl.BlockSpec((tm, D), lambda i: (i, 0)),
        out_shape=jax.ShapeDtypeStruct((n, D), f32),
        scratch_shapes=[pltpu.VMEM((tm, D), f32)],
        compiler_params=_cparams(("arbitrary",)), name="out_proj_dense_ffn",
    )(a, gl, o_f, o_b, u, sc, x, mod, p['w_out'], p['ln1_g'], p['ln1_b'], p['gdn_norm_g'], cst['mbd'],
      f['w_gate'], f['w_up'], f['w_down'], f['ln2_g'], f['ln2_b'])


MOE_TM = 1024
MOE_SB = 128
MOE_MID = 192
MOE_WIDE = 320


def _moe_kernel(x_ref, mod_ref, wg_ref, wu_ref, wd_ref, lg_ref, lb_ref, rt_ref, tri_ref, o_ref,
                h_scr, acc_scr, comb_scr, pos_scr):
    e = pl.program_id(1)
    tm = x_ref.shape[0]

    @pl.when(e == 0)
    def _():
        m = mod_ref[...]
        h = x_ref[...] * (1.0 + m[:, 4 * D:5 * D]) + m[:, 3 * D:4 * D]
        h_scr[...] = h.astype(bf16)
        acc_scr[...] = jnp.zeros(acc_scr.shape, f32)
        hh, hr = _split2(h)
        rh, rr = _split2(rt_ref[...])
        d = lambda x, y: jnp.dot(x, y, preferred_element_type=f32)
        logits = d(hh, rh) + d(hr.astype(bf16), rh) + d(hh, rr.astype(bf16))
        lane = lax.broadcasted_iota(jnp.int32, logits.shape, 1).astype(f32)
        neg = jnp.float32(-jnp.inf)
        logits = jnp.where(lane < N_EXPERTS, logits, neg)
        m1 = jnp.max(logits, axis=-1, keepdims=True)
        i1 = jnp.min(jnp.where(logits == m1, lane, float(LANE)), axis=-1, keepdims=True)
        rest = jnp.where(lane == i1, neg, logits)
        m2 = jnp.max(rest, axis=-1, keepdims=True)
        i2 = jnp.min(jnp.where(rest == m2, lane, float(LANE)), axis=-1, keepdims=True)
        e2 = jnp.exp(m2 - m1)
        p1 = 1.0 / (1.0 + e2)
        p2 = e2 / (1.0 + e2)
        sel1 = lane == i1
        sel2 = lane == i2
        comb = jnp.where(sel1, p1, 0.0) + jnp.where(sel2, p2, 0.0)
        c_hi, c_mid, c_lo = _split3(comb)
        comb_scr[...] = jnp.where(lane < N_EXPERTS, c_hi.astype(f32),
                                  jnp.where(lane < 2 * N_EXPERTS, pltpu.roll(c_mid.astype(f32), N_EXPERTS, 1),
                                            pltpu.roll(c_lo.astype(f32), 2 * N_EXPERTS, 1))).astype(bf16)
        routed = jnp.where(sel1 | sel2, 1.0, 0.0).astype(bf16)
        incl = lax.dot_general(routed, tri_ref[...], (((0,), (0,)), ((), ())), preferred_element_type=f32)
        tok = lax.broadcasted_iota(jnp.int32, incl.shape, 1)
        excl = jnp.where(tok == 0, 0.0, pltpu.roll(incl, 1, 1))
        pos_scr[...] = jnp.where(incl > excl, excl, -1.0)

    pos_row = pos_scr[pl.ds(e, 1), :]
    nrows = (jnp.max(pos_row) + 1.0).astype(jnp.int32)

    def run_pass(base, sb):
        slot = lax.broadcasted_iota(jnp.int32, (sb, tm), 0).astype(f32) + base.astype(f32)
        lane_sb = lax.broadcasted_iota(jnp.int32, (sb, LANE), 1)
        sel = jnp.where(pos_row == slot, 1.0, 0.0).astype(bf16)
        xs = jnp.dot(sel, h_scr[...], preferred_element_type=f32).astype(bf16)
        t = _silu(jnp.dot(xs, wg_ref[...], preferred_element_type=f32)) * jnp.dot(xs, wu_ref[...], preferred_element_type=f32)
        y = jnp.dot(t.astype(bf16), wd_ref[...], preferred_element_type=f32)
        cw3 = jnp.dot(sel, comb_scr[...], preferred_element_type=f32)
        cw = jnp.sum(jnp.where((lane_sb & (N_EXPERTS - 1)) == e, cw3, 0.0), axis=-1, keepdims=True)
        acc_scr[...] += lax.dot_general(sel, (y * cw).astype(bf16), (((0,), (0,)), ((), ())), preferred_element_type=f32)

    big = 2 * MOE_SB
    one_wide = (nrows > big) & (nrows <= MOE_WIDE)

    @pl.when(one_wide)
    def _():
        run_pass(jnp.int32(0), MOE_WIDE)

    @pl.when(jnp.logical_not(one_wide))
    def _():
        nbig = (nrows + big - MOE_MID - 1) // big
        rem = nrows - nbig * big

        def big_pass(i, carry):
            run_pass(i * big, big)
            return carry

        lax.fori_loop(0, nbig, big_pass, 0)

        @pl.when(rem > MOE_SB)
        def _():
            run_pass(nbig * big, MOE_MID)

        @pl.when((rem > 0) & (rem <= MOE_SB))
        def _():
            run_pass(nbig * big, MOE_SB)

    @pl.when(e == N_EXPERTS - 1)
    def _():
        m = mod_ref[...]
        o_ref[...] = _ln(ALPHA * x_ref[...] + m[:, 5 * D:6 * D] * acc_scr[...], lg_ref[...], lb_ref[...])


def _moe_call(x, mod, p, seq_len, cst, tm=MOE_TM):
    n = x.shape[0]
    tm = min(tm, n)
    fw = D_FF_EXPERT
    tiles_per_mod = seq_len // tm if mod.shape[0] > 1 else n // tm
    full = lambda t: pl.BlockSpec(t.shape, lambda i, g: (0,) * t.ndim)
    tri = cst['tri_tok'][:tm, :tm]
    return pl.pallas_call(
        _moe_kernel, grid=(n // tm, N_EXPERTS),
        in_specs=[pl.BlockSpec((tm, D), lambda i, g: (i, 0)),
                  pl.BlockSpec((None, 1, 6 * D), lambda i, g: (i // tiles_per_mod, 0, 0)),
                  pl.BlockSpec((None, None, D, fw), lambda i, g: (p['index'], g, 0, 0)),
                  pl.BlockSpec((None, None, D, fw), lambda i, g: (p['index'], g, 0, 0)),
                  pl.BlockSpec((None, None, fw, D), lambda i, g: (p['index'], g, 0, 0)),
                  full(p['ln2_g']), full(p['ln2_b']), full(p['router']), full(tri)],
        out_specs=pl.BlockSpec((tm, D), lambda i, g: (i, 0)),
        out_shape=jax.ShapeDtypeStruct((n, D), f32),
        scratch_shapes=[pltpu.VMEM((tm, D), bf16), pltpu.VMEM((tm, D), f32),
                        pltpu.VMEM((tm, LANE), bf16), pltpu.VMEM((LANE, tm), f32)],
        compiler_params=_cparams(("arbitrary", "arbitrary")), name="moe_ffn",
    )(x, mod, p['w_gate'], p['w_up'], p['w_down'], p['ln2_g'], p['ln2_b'], p['router'], tri)


def _constants():
    hb = np.arange(GW) // DH
    mbd = (hb[:, None] == hb[None, :]).astype(np.float32)
    r = np.arange(CH)
    t = np.arange(MOE_TM)
    e = np.zeros((LANE, 3 * GW), np.float32)
    for h in range(NH):
        e[L_BETA + h, h * DH:(h + 1) * DH] = 1.0
        for d in range(2):
            e[L_DEC + NH * d + h, (1 + d) * GW + h * DH:(1 + d) * GW + (h + 1) * DH] = 1.0
    return dict(mbd=jnp.asarray(mbd, bf16), mbdf=jnp.asarray(mbd, f32),
                tril=jnp.asarray(r[:, None] >= r[None, :], bf16), triu=jnp.asarray(r[:, None] <= r[None, :], bf16),
                e_gdn=jnp.asarray(e, bf16), tri_tok=jnp.asarray(t[:, None] <= t[None, :], bf16))


_IN_SIZES = (GW, GW, GW, GW, GW, GW, 2 * GLA_LR, GW, GW, GW, GW, NH, 2 * NH, GW, GW, GW)


def _pack_w_in(w_in):
    offs = np.concatenate([[0], np.cumsum(_IN_SIZES)])
    big = [i for i, s in enumerate(_IN_SIZES) if s == GW]
    small = [i for i, s in enumerate(_IN_SIZES) if s != GW]
    parts = [w_in[..., int(offs[i]):int(offs[i + 1])] for i in big + small]
    pad = NU - int(offs[-1])
    parts.append(jnp.zeros(w_in.shape[:-1] + (pad,), w_in.dtype))
    return jnp.concatenate(parts, axis=-1).astype(bf16)


def _to_bd(s, transpose):
    if transpose:
        s = jnp.swapaxes(s, -1, -2)
    b = s.shape[0]
    z = jnp.zeros_like(s)
    rows = [jnp.concatenate([s[:, :, h] if g == h else z[:, :, h] for g in range(NH)], axis=-1) for h in range(NH)]
    return jnp.concatenate(rows, axis=-2).reshape(b, 2, GW, GW)


def _from_bd(s, transpose):
    out = jnp.stack([s[:, :, h * DH:(h + 1) * DH, h * DH:(h + 1) * DH] for h in range(NH)], axis=2)
    return jnp.swapaxes(out, -1, -2) if transpose else out


def _stream_layer(x, mod, nseq, seq_len, s0_gla, s0_gdn, p, cst, colmajor, moe):
    tpm_512 = max(seq_len // 512, 1) if mod.shape[0] > 1 else x.shape[0] // 512
    u, a, ysc = _in_conv_call(x, mod, p['w_in'], p['layer'], tpm_512, seq_len, p)
    ygla, sgla = _gla_call(u, nseq, seq_len, s0_gla, p, cst)
    o_f, o_b, sgdn = _gdn_call(u, nseq, seq_len, s0_gdn, p, cst, colmajor)
    if moe:
        x1 = _out_call(a, ygla, o_f, o_b, u, ysc, x, mod, p, cst, seq_len, colmajor)
        x2 = _moe_call(x1, mod, p['ffn'], seq_len, cst)
    else:
        x2 = _out_ffn_call(a, ygla, o_f, o_b, u, ysc, x, mod, p, cst, seq_len, colmajor)
    return x2, sgla, sgdn


def kernel(x_prompt, x_sample, c, state_gla, state_gdn, c_ctx, w_ada, b_ada, w_in, w_out, conv_w, conv_b, conv_ln_g, conv_ln_b, conv_pw, gla_w_lr, gla_b_lr, gla_norm_g, gdn_conv_w, gdn_a_log, gdn_dt_bias, gdn_norm_g, sc_conv_w, ln1_g, ln1_b, ln2_g, ln2_b, ffn_w_gate, ffn_w_up, ffn_w_down, moe_router, moe_w_gate, moe_w_up, moe_w_down):
    nb, seq, _ = x_prompt.shape
    db, dseq, _ = x_sample.shape
    depth = w_in.shape[0]
    cst = _constants()

    rows = 16
    cond = jnp.concatenate([c_ctx[None, :], c, jnp.zeros((rows - 1 - db, D), f32)], axis=0)
    mod = _mod_call(cond, w_ada, b_ada)

    w_in_p = _pack_w_in(w_in)
    w_out_b = w_out.astype(bf16)
    conv_pw_b = conv_pw.astype(bf16)
    wlr = jnp.zeros((depth, 2, LANE, GW), f32)
    wlr = wlr.at[:, 0, 0:GLA_LR].set(gla_w_lr[:, 0]).at[:, 1, GLA_LR:2 * GLA_LR].set(gla_w_lr[:, 1]).astype(bf16)
    rep = lambda t: jnp.repeat(t, DH, axis=-1)[:, :, None, :]
    row = lambda t: t[:, None, :]
    ffn_g, ffn_u, ffn_d = ffn_w_gate.astype(bf16), ffn_w_up.astype(bf16), ffn_w_down.astype(bf16)
    moe_g, moe_u, moe_d = moe_w_gate.astype(bf16), moe_w_up.astype(bf16), moe_w_down.astype(bf16)
    router = jnp.pad(moe_router, ((0, 0), (0, 0), (0, LANE - N_EXPERTS)))

    zero_bd = jnp.zeros((nb, 2, GW, GW), f32)
    y_p = x_prompt.reshape(nb * seq, D)
    y_s = x_sample.reshape(db * dseq, D)
    gla_states, gdn_states = [], []
    for l in range(depth):
        j = l // 2
        moe = l % 2 == 1
        if moe:
            ffn = dict(w_gate=moe_g, w_up=moe_u, w_down=moe_d, index=j, router=router[j],
                       ln2_g=row(ln2_g)[l], ln2_b=row(ln2_b)[l])
        else:
            ffn = dict(w_gate=ffn_g, w_up=ffn_u, w_down=ffn_d, index=j, ln2_g=row(ln2_g)[l], ln2_b=row(ln2_b)[l])
        p = dict(layer=l, w_in=w_in_p, w_out=w_out_b, conv_w=conv_w[l], conv_b=row(conv_b)[l],
                 conv_ln_g=row(conv_ln_g)[l], conv_ln_b=row(conv_ln_b)[l], conv_pw=conv_pw_b[l],
                 sc_conv_w=sc_conv_w[l], gla_wlr=wlr[l], gla_blr=gla_b_lr[l][:, None, :],
                 gla_norm_g=row(gla_norm_g)[l], gdn_conv_w=gdn_conv_w[l], gdn_alog=rep(gdn_a_log)[l],
                 gdn_dtb=rep(gdn_dt_bias)[l], gdn_norm_g=row(gdn_norm_g)[l],
                 ln1_g=row(ln1_g)[l], ln1_b=row(ln1_b)[l], ffn=ffn)
        mod_ctx = mod[l, 0:1][:, None, :]
        mod_lat = mod[l, 1:1 + db][:, None, :]
        y_p, s_gla, s_gdn = _stream_layer(y_p, mod_ctx, nb, seq, zero_bd, zero_bd, p, cst, False, moe)
        gla_states.append(_from_bd(s_gla, True))
        gdn_states.append(_from_bd(s_gdn, False))
        y_s, _, _ = _stream_layer(y_s, mod_lat, db, dseq, _to_bd(state_gla[:, l], True),
                                  _to_bd(state_gdn[:, l], False), p, cst, True, moe)
    return (y_p.reshape(nb, seq, D), y_s.reshape(db, dseq, D),
            jnp.stack(gla_states, axis=1), jnp.stack(gdn_states, axis=1))
```
